```python
import math
import jax, jax.numpy as jnp
from jax import lax
import numpy as np

D_MODEL = 2048
BATCH = 16
SEQ = 2048
DEPTH = 2

SSD_HEADS = 32
SSD_HEAD_DIM = 64
SSD_INNER = SSD_HEADS * SSD_HEAD_DIM
SSD_GROUPS = 8
SSD_STATE = 128
SSD_CHUNK = 128
CONV_WIDTH = 5
XBC_WIDTH = SSD_INNER + 2 * SSD_GROUPS * SSD_STATE
DT_MIN = 0.001
DT_MAX = 0.1
HEAD_DIM = 128
ROPE_DIM = HEAD_DIM // 4
ROPE_THETA = 500000.0
DIL_PATTERNS = ((128, 1), (512, 4), (2048, 16))
DIL_GROUPS = len(DIL_PATTERNS)
DIL_HEADS = 8
DIL_WIDTH = DIL_HEADS * HEAD_DIM
WIN_Q_HEADS = 16
WIN_KV_HEADS = 4
WIN_HALF = 128
D_FF = 4 * D_MODEL
N_BRANCH = 3
EPS = 1e-6
NEG_INF = -1e30

SPLITS = (SSD_INNER,
          XBC_WIDTH,
          2 * SSD_HEADS,
          3 * DIL_GROUPS * DIL_WIDTH,
          WIN_Q_HEADS * HEAD_DIM,
          WIN_KV_HEADS * HEAD_DIM,
          WIN_KV_HEADS * HEAD_DIM,
          N_BRANCH * D_MODEL)
N_IN = sum(SPLITS)
SPLIT_POINTS = tuple(int(v) for v in np.cumsum(SPLITS)[:-1])

kernel_name = "gated_parallel_ssd_dilated_window_hybrid"


def rms_norm(x, g):
    xf = x.astype(jnp.float32)
    y = xf * lax.rsqrt(jnp.mean(xf * xf, axis=-1, keepdims=True) + EPS)
    return (y * g.astype(jnp.float32)).astype(x.dtype)


def rope_tables(seq):
    inv = ROPE_THETA ** (-jnp.arange(0, ROPE_DIM, 2, dtype=jnp.float32) / ROPE_DIM)
    ang = jnp.arange(seq, dtype=jnp.float32)[:, None] * inv[None, :]
    return jnp.cos(ang), jnp.sin(ang)


def partial_rope(t, cos, sin):
    shp = (1, cos.shape[0]) + (1,) * (t.ndim - 3) + (cos.shape[1],)
    c, s = cos.reshape(shp), sin.reshape(shp)
    t1, t2 = jnp.split(t[..., :ROPE_DIM].astype(jnp.float32), 2, axis=-1)
    rot = jnp.concatenate([t1 * c - t2 * s, t1 * s + t2 * c], axis=-1).astype(t.dtype)
    return jnp.concatenate([rot, t[..., ROPE_DIM:]], axis=-1)


def banded_attention(q, k, v, half_window, sink_logits=None):
    b, l, hq, dh = q.shape
    hkv = k.shape[2]
    rep = hq // hkv
    blk = half_window
    nb = -(-l // blk)
    lp = nb * blk
    qb = jnp.pad(q, ((0, 0), (0, lp - l), (0, 0), (0, 0))).reshape(b, nb, blk, hkv, rep, dh)

    def windows(t):
        tp = jnp.pad(t, ((0, 0), (blk, lp - l + blk), (0, 0), (0, 0))).reshape(b, nb + 2, blk, hkv, dh)
        return jnp.concatenate([tp[:, :-2], tp[:, 1:-1], tp[:, 2:]], axis=2)

    kw, vw = windows(k), windows(v)
    scores = jnp.einsum('bnqgrd,bnkgd->bngrqk', qb, kw,
                        preferred_element_type=jnp.float32) * (dh ** -0.5)
    qpos = jnp.arange(lp).reshape(nb, blk)
    kpos = jnp.arange(nb)[:, None] * blk + jnp.arange(3 * blk)[None, :] - blk
    valid = ((jnp.abs(qpos[:, :, None] - kpos[:, None, :]) <= half_window)
             & (kpos[:, None, :] >= 0) & (kpos[:, None, :] < l))
    scores = jnp.where(valid[None, :, None, None], scores, NEG_INF)
    lse = jax.nn.logsumexp(scores, axis=-1)
    if sink_logits is not None:
        lse = jnp.logaddexp(lse, sink_logits.astype(jnp.float32).reshape(1, 1, hkv, rep, 1))
    p = jnp.exp(scores - lse[..., None])
    out = jnp.einsum('bngrqk,bnkgd->bnqgrd', p.astype(v.dtype), vw).reshape(b, lp, hq, dh)[:, :l]
    lse = lse.transpose(0, 1, 4, 2, 3).reshape(b, lp, hq)[:, :l]
    return out, lse


def dilated_mixture_attention(q, k, v):
    b, s = q.shape[:2]
    outs, lses = [], []
    for gi, (window, dil) in enumerate(DIL_PATTERNS):
        n_sub = s // dil

        def by_stride(t):
            t = t.reshape(b, n_sub, dil, DIL_HEADS, HEAD_DIM).transpose(0, 2, 1, 3, 4)
            return t.reshape(b * dil, n_sub, DIL_HEADS, HEAD_DIM)

        o, lse = banded_attention(by_stride(q[:, :, gi]), by_stride(k[:, :, gi]),
                                  by_stride(v[:, :, gi]), window // (2 * dil))
        outs.append(o.reshape(b, dil, n_sub, DIL_HEADS, HEAD_DIM)
                    .transpose(0, 2, 1, 3, 4).reshape(b, s, DIL_HEADS, HEAD_DIM))
        lses.append(lse.reshape(b, dil, n_sub, DIL_HEADS).transpose(0, 2, 1, 3).reshape(b, s, DIL_HEADS))
    weights = jax.nn.softmax(jnp.stack(lses, axis=0), axis=0)
    out = jnp.einsum('gbsh,gbshd->bshd', weights, jnp.stack(outs, axis=0).astype(jnp.float32))
    return out.astype(q.dtype).reshape(b, s, DIL_WIDTH)


def ssd_scan(x, dt, a, bm, cm):
    b, l, h, p = x.shape
    g, n = bm.shape[2], bm.shape[3]
    hg = h // g
    t = SSD_CHUNK
    c = l // t
    x = x.reshape(b, c, t, g, hg, p)
    dt = dt.reshape(b, c, t, g, hg)
    bm = bm.reshape(b, c, t, g, n)
    cm = cm.reshape(b, c, t, g, n)
    cs = jnp.cumsum(dt * a.reshape(g, hg), axis=2)
    xdt = x * dt[..., None]
    lower = jnp.tril(jnp.ones((t, t), dtype=bool))[None, None, :, :, None, None]
    seg = cs[:, :, :, None] - cs[:, :, None, :]
    decay = jnp.exp(jnp.where(lower, seg, -jnp.inf))
    cb = jnp.einsum('bclgn,bcsgn->bclsg', cm, bm)
    y_diag = jnp.einsum('bclsg,bclsgh,bcsghp->bclghp', cb, decay, xdt)
    decay_states = jnp.exp(cs[:, :, -1:] - cs)
    states = jnp.einsum('bctgn,bctgh,bctghp->bcghpn', bm, decay_states, xdt)
    chunk_decay = jnp.exp(cs[:, :, -1])

    def step(carry, inp):
        st, dec = inp
        return carry * dec[..., None, None] + st, carry

    init = jnp.zeros((b, g, hg, p, n), dtype=x.dtype)
    _, prev = lax.scan(step, init, (states.transpose(1, 0, 2, 3, 4, 5), chunk_decay.transpose(1, 0, 2, 3)))
    prev = prev.transpose(1, 0, 2, 3, 4, 5)
    y_off = jnp.einsum('bctgn,bcghpn,bctgh->bctghp', cm, prev, jnp.exp(cs))
    return (y_diag + y_off).reshape(b, l, h, p)


def ssd_mixer(z, xbc, dt_raw, conv_w, conv_b, dt_bias, a_log, d_skip, ssd_norm):
    b, s, _ = xbc.shape
    pad = (CONV_WIDTH - 1) // 2
    xbc = lax.conv_general_dilated(xbc, conv_w[:, None, :].astype(xbc.dtype), window_strides=(1,),
                                   padding=[(pad, pad)], dimension_numbers=('NWC', 'WIO', 'NWC'),
                                   feature_group_count=XBC_WIDTH) + conv_b
    xbc = jax.nn.silu(xbc).astype(jnp.float32)
    xs, bm, cm = jnp.split(xbc, [SSD_INNER, SSD_INNER + SSD_GROUPS * SSD_STATE], axis=-1)
    xs = xs.reshape(b, s, SSD_HEADS, SSD_HEAD_DIM)
    bm = bm.reshape(b, s, SSD_GROUPS, SSD_STATE)
    cm = cm.reshape(b, s, SSD_GROUPS, SSD_STATE)
    a = -jnp.exp(a_log.astype(jnp.float32))
    dt = jax.nn.softplus(dt_raw.astype(jnp.float32).reshape(b, s, 2, SSD_HEADS)
                         + dt_bias.astype(jnp.float32))
    flip = lambda t: jnp.flip(t, axis=1)
    y_fwd = ssd_scan(xs, dt[:, :, 0], a[0], bm, cm)
    y_bwd = flip(ssd_scan(flip(xs), flip(dt[:, :, 1]), a[1], flip(bm), flip(cm)))
    y = y_fwd + y_bwd + d_skip.astype(jnp.float32)[:, None] * xs
    y = y.reshape(b, s, SSD_INNER) * jax.nn.silu(z.astype(jnp.float32))
    return rms_norm(y, ssd_norm).astype(z.dtype)


def hybrid_layer(x, cos, sin, g_mix, w_in, conv_w, conv_b, dt_bias, a_log, d_skip, ssd_norm,
                 w_a, w_b, w_c, sink, w_out, g_mlp, w_up, w_down):
    b, s, _ = x.shape
    h = rms_norm(x, g_mix)
    proj = h @ w_in
    z, xbc, dt_raw, qkv_d, q_w, k_w, v_w, gate_logits = jnp.split(proj, SPLIT_POINTS, axis=-1)
    y_a = ssd_mixer(z, xbc, dt_raw, conv_w, conv_b, dt_bias, a_log, d_skip, ssd_norm)
    qkv_d = partial_rope(qkv_d.reshape(b, s, 3 * DIL_GROUPS * DIL_HEADS, HEAD_DIM), cos, sin)
    qkv_d = qkv_d.reshape(b, s, 3, DIL_GROUPS, DIL_HEADS, HEAD_DIM)
    y_b = dilated_mixture_attention(qkv_d[:, :, 0], qkv_d[:, :, 1], qkv_d[:, :, 2])
    q_w = partial_rope(q_w.reshape(b, s, WIN_Q_HEADS, HEAD_DIM), cos, sin)
    k_w = partial_rope(k_w.reshape(b, s, WIN_KV_HEADS, HEAD_DIM), cos, sin)
    v_w = v_w.reshape(b, s, WIN_KV_HEADS, HEAD_DIM)
    y_c, _ = banded_attention(q_w, k_w, v_w, WIN_HALF, sink)
    y_c = y_c.reshape(b, s, WIN_Q_HEADS * HEAD_DIM)
    gates = jax.nn.sigmoid(gate_logits.astype(jnp.float32)).astype(x.dtype).reshape(b, s, N_BRANCH, D_MODEL)
    merged = gates[:, :, 0] * (y_a @ w_a) + gates[:, :, 1] * (y_b @ w_b) + gates[:, :, 2] * (y_c @ w_c)
    x = x + merged @ w_out
    hm = rms_norm(x, g_mlp)
    return x + jnp.square(jax.nn.relu(hm @ w_up)) @ w_down


def _fwd_setup_inputs(seed: int = 0) -> dict:
    key = jax.random.key(seed)
    ks = jax.random.split(key, 18)
    f32 = jnp.float32

    def normal(k, shape, scale):
        return jax.random.normal(k, shape, f32) * scale

    x = normal(ks[0], (BATCH, SEQ, D_MODEL), 1.0)
    g_mix = 1.0 + normal(ks[1], (DEPTH, D_MODEL), 0.02)
    w_in = normal(ks[2], (DEPTH, D_MODEL, N_IN), D_MODEL ** -0.5)
    conv_w = normal(ks[3], (DEPTH, CONV_WIDTH, XBC_WIDTH), CONV_WIDTH ** -0.5)
    conv_b = normal(ks[4], (DEPTH, XBC_WIDTH), 0.01)
    dt0 = jnp.exp(jax.random.uniform(ks[5], (DEPTH, 2, SSD_HEADS), f32, math.log(DT_MIN), math.log(DT_MAX)))
    dt_bias = dt0 + jnp.log(-jnp.expm1(-dt0))
    a_log = jnp.log(jax.random.uniform(ks[6], (DEPTH, 2, SSD_HEADS), f32, 1.0, 16.0))
    d_skip = 1.0 + normal(ks[7], (DEPTH, SSD_HEADS), 0.1)
    ssd_norm = 1.0 + normal(ks[8], (DEPTH, SSD_INNER), 0.02)
    w_a = normal(ks[9], (DEPTH, SSD_INNER, D_MODEL), SSD_INNER ** -0.5)
    w_b = normal(ks[10], (DEPTH, DIL_WIDTH, D_MODEL), DIL_WIDTH ** -0.5)
    w_c = normal(ks[11], (DEPTH, WIN_Q_HEADS * HEAD_DIM, D_MODEL), (WIN_Q_HEADS * HEAD_DIM) ** -0.5)
    sink = normal(ks[12], (DEPTH, WIN_Q_HEADS), 0.5)
    w_out = normal(ks[13], (DEPTH, D_MODEL, D_MODEL), D_MODEL ** -0.5)
    g_mlp = 1.0 + normal(ks[14], (DEPTH, D_MODEL), 0.02)
    w_up = normal(ks[15], (DEPTH, D_MODEL, D_FF), D_MODEL ** -0.5)
    w_down = normal(ks[16], (DEPTH, D_FF, D_MODEL), D_FF ** -0.5)
    g_final = 1.0 + normal(ks[17], (D_MODEL,), 0.02)
    return {"x": x, "g_mix": g_mix, "w_in": w_in, "conv_w": conv_w, "conv_b": conv_b,
            "dt_bias": dt_bias, "a_log": a_log, "d_skip": d_skip, "ssd_norm": ssd_norm,
            "w_a": w_a, "w_b": w_b, "w_c": w_c, "sink": sink, "w_out": w_out,
            "g_mlp": g_mlp, "w_up": w_up, "w_down": w_down, "g_final": g_final}


def _fwd_reference(x, g_mix, w_in, conv_w, conv_b, dt_bias, a_log, d_skip, ssd_norm,
              w_a, w_b, w_c, sink, w_out, g_mlp, w_up, w_down, g_final):
    cos, sin = rope_tables(x.shape[1])
    for i in range(DEPTH):
        x = hybrid_layer(x, cos, sin, g_mix[i], w_in[i], conv_w[i], conv_b[i], dt_bias[i], a_log[i],
                         d_skip[i], ssd_norm[i], w_a[i], w_b[i], w_c[i], sink[i], w_out[i],
                         g_mlp[i], w_up[i], w_down[i])
    return rms_norm(x, g_final)


import jax as _jax
import jax.numpy as _jnp

TWIN_FORMAT = 'train_step'
FWD_PARAMS = ['x', 'g_mix', 'w_in', 'conv_w', 'conv_b', 'dt_bias', 'a_log', 'd_skip', 'ssd_norm', 'w_a', 'w_b', 'w_c', 'sink', 'w_out', 'g_mlp', 'w_up', 'w_down', 'g_final']
TWIN_WEIGHTS = ['g_mix', 'w_in', 'conv_w', 'conv_b', 'dt_bias', 'a_log', 'd_skip', 'ssd_norm', 'w_a', 'w_b', 'w_c', 'sink', 'w_out', 'g_mlp', 'w_up', 'w_down', 'g_final']
TWIN_DIFF_INPUT = 'x'
TWIN_INPUTS = ['x', 'g_mix', 'w_in', 'conv_w', 'conv_b', 'dt_bias', 'a_log', 'd_skip', 'ssd_norm', 'w_a', 'w_b', 'w_c', 'sink', 'w_out', 'g_mlp', 'w_up', 'w_down', 'g_final', 'loss_target', 'm_g_mix', 'm_w_in', 'm_conv_w', 'm_conv_b', 'm_dt_bias', 'm_a_log', 'm_d_skip', 'm_ssd_norm', 'm_w_a', 'm_w_b', 'm_w_c', 'm_sink', 'm_w_out', 'm_g_mlp', 'm_w_up', 'm_w_down', 'm_g_final', 'v_g_mix', 'v_w_in', 'v_conv_w', 'v_conv_b', 'v_dt_bias', 'v_a_log', 'v_d_skip', 'v_ssd_norm', 'v_w_a', 'v_w_b', 'v_w_c', 'v_sink', 'v_w_out', 'v_g_mlp', 'v_w_up', 'v_w_down', 'v_g_final']
TWIN_OUTPUTS = ['loss', 'grad_x', 'grad_g_mix', 'grad_w_in', 'grad_conv_w', 'grad_conv_b', 'grad_dt_bias', 'grad_a_log', 'grad_d_skip', 'grad_ssd_norm', 'grad_w_a', 'grad_w_b', 'grad_w_c', 'grad_sink', 'grad_w_out', 'grad_g_mlp', 'grad_w_up', 'grad_w_down', 'grad_g_final', 'delta_g_mix', 'delta_w_in', 'delta_conv_w', 'delta_conv_b', 'delta_dt_bias', 'delta_a_log', 'delta_d_skip', 'delta_ssd_norm', 'delta_w_a', 'delta_w_b', 'delta_w_c', 'delta_sink', 'delta_w_out', 'delta_g_mlp', 'delta_w_up', 'delta_w_down', 'delta_g_final', 'new_m_g_mix', 'new_m_w_in', 'new_m_conv_w', 'new_m_conv_b', 'new_m_dt_bias', 'new_m_a_log', 'new_m_d_skip', 'new_m_ssd_norm', 'new_m_w_a', 'new_m_w_b', 'new_m_w_c', 'new_m_sink', 'new_m_w_out', 'new_m_g_mlp', 'new_m_w_up', 'new_m_w_down', 'new_m_g_final', 'new_v_g_mix', 'new_v_w_in', 'new_v_conv_w', 'new_v_conv_b', 'new_v_dt_bias', 'new_v_a_log', 'new_v_d_skip', 'new_v_ssd_norm', 'new_v_w_a', 'new_v_w_b', 'new_v_w_c', 'new_v_sink', 'new_v_w_out', 'new_v_g_mlp', 'new_v_w_up', 'new_v_w_down', 'new_v_g_final']
TWIN_LEAF_KINDS = {'loss': 'loss', 'grad_x': 'grad_x', 'grad_g_mix': 'grad_w', 'grad_w_in': 'grad_w', 'grad_conv_w': 'grad_w', 'grad_conv_b': 'grad_w', 'grad_dt_bias': 'grad_w', 'grad_a_log': 'grad_w', 'grad_d_skip': 'grad_w', 'grad_ssd_norm': 'grad_w', 'grad_w_a': 'grad_w', 'grad_w_b': 'grad_w', 'grad_w_c': 'grad_w', 'grad_sink': 'grad_w', 'grad_w_out': 'grad_w', 'grad_g_mlp': 'grad_w', 'grad_w_up': 'grad_w', 'grad_w_down': 'grad_w', 'grad_g_final': 'grad_w', 'delta_g_mix': 'delta_w', 'delta_w_in': 'delta_w', 'delta_conv_w': 'delta_w', 'delta_conv_b': 'delta_w', 'delta_dt_bias': 'delta_w', 'delta_a_log': 'delta_w', 'delta_d_skip': 'delta_w', 'delta_ssd_norm': 'delta_w', 'delta_w_a': 'delta_w', 'delta_w_b': 'delta_w', 'delta_w_c': 'delta_w', 'delta_sink': 'delta_w', 'delta_w_out': 'delta_w', 'delta_g_mlp': 'delta_w', 'delta_w_up': 'delta_w', 'delta_w_down': 'delta_w', 'delta_g_final': 'delta_w', 'new_m_g_mix': 'new_m', 'new_m_w_in': 'new_m', 'new_m_conv_w': 'new_m', 'new_m_conv_b': 'new_m', 'new_m_dt_bias': 'new_m', 'new_m_a_log': 'new_m', 'new_m_d_skip': 'new_m', 'new_m_ssd_norm': 'new_m', 'new_m_w_a': 'new_m', 'new_m_w_b': 'new_m', 'new_m_w_c': 'new_m', 'new_m_sink': 'new_m', 'new_m_w_out': 'new_m', 'new_m_g_mlp': 'new_m', 'new_m_w_up': 'new_m', 'new_m_w_down': 'new_m', 'new_m_g_final': 'new_m', 'new_v_g_mix': 'new_v', 'new_v_w_in': 'new_v', 'new_v_conv_w': 'new_v', 'new_v_conv_b': 'new_v', 'new_v_dt_bias': 'new_v', 'new_v_a_log': 'new_v', 'new_v_d_skip': 'new_v', 'new_v_ssd_norm': 'new_v', 'new_v_w_a': 'new_v', 'new_v_w_b': 'new_v', 'new_v_w_c': 'new_v', 'new_v_sink': 'new_v', 'new_v_w_out': 'new_v', 'new_v_g_mlp': 'new_v', 'new_v_w_up': 'new_v', 'new_v_w_down': 'new_v', 'new_v_g_final': 'new_v'}


def _forward(args):
    return _fwd_reference(*[args[k] for k in FWD_PARAMS])


def _output_shape():
    out = _jax.eval_shape(lambda: _forward(_fwd_setup_inputs(0)))
    return out.shape, out.dtype

N_MICROBATCH = 1
ADAM_LR = 0.001
ADAM_B1 = 0.9
ADAM_B2 = 0.999
ADAM_EPS = 1e-08
ADAM_WD = 0.01
ADAM_STEP = 10
PER_EXAMPLE_BATCH_AXIS = {'x': 0, 'loss_target': 0}
SHARED_INPUTS = []
_WEIGHT_DTYPES = {'g_mix': _jnp.float32, 'w_in': _jnp.float32, 'conv_w': _jnp.float32, 'conv_b': _jnp.float32, 'dt_bias': _jnp.float32, 'a_log': _jnp.float32, 'd_skip': _jnp.float32, 'ssd_norm': _jnp.float32, 'w_a': _jnp.float32, 'w_b': _jnp.float32, 'w_c': _jnp.float32, 'sink': _jnp.float32, 'w_out': _jnp.float32, 'g_mlp': _jnp.float32, 'w_up': _jnp.float32, 'w_down': _jnp.float32, 'g_final': _jnp.float32}
MOMENT_SCALE = {'g_mix': 6.728436e-02, 'w_in': 1.925414e-02, 'conv_w': 3.148844e-02, 'conv_b': 4.897445e-02, 'dt_bias': 8.110552e-02, 'a_log': 7.605237e-02, 'd_skip': 1.707069e-01, 'ssd_norm': 4.169152e-02, 'w_a': 4.206422e-02, 'w_b': 6.395194e-03, 'w_c': 1.034998e-02, 'sink': 3.326211e-04, 'w_out': 4.370840e-02, 'g_mlp': 7.117611e-02, 'w_up': 3.627787e-02, 'w_down': 7.343174e-02, 'g_final': 1.632818e+01}


def _to_microbatches(a, axis):
    t = _jnp.moveaxis(a, axis, 0)
    t = t.reshape((N_MICROBATCH, t.shape[0] // N_MICROBATCH) + t.shape[1:])
    return _jnp.moveaxis(t, 1, axis + 1)


def setup_inputs(seed: int = 0) -> dict:
    inp = _fwd_setup_inputs(seed)
    key = _jax.random.fold_in(_jax.random.key(seed), 7919)
    shape, _ = _output_shape()
    out = dict(inp)
    out["loss_target"] = _jax.random.normal(_jax.random.fold_in(key, 0), shape, _jnp.float32)
    for i, name in enumerate(TWIN_WEIGHTS):
        w = inp[name].astype(_jnp.float32)
        if MOMENT_SCALE is None:
            s = _jnp.sqrt(_jnp.mean(_jnp.square(w)) + 1e-30)
        else:
            s = MOMENT_SCALE[name]
        km, kv = _jax.random.split(_jax.random.fold_in(key, i + 1))
        out[name] = w
        out["m_" + name] = s * _jax.random.normal(km, w.shape, _jnp.float32)
        out["v_" + name] = (s * s) * _jax.random.uniform(kv, w.shape, _jnp.float32, 0.5, 1.5)
    if N_MICROBATCH > 1:
        for name, axis in PER_EXAMPLE_BATCH_AXIS.items():
            out[name] = _to_microbatches(out[name], axis)
    return {'x': out['x'], 'g_mix': out['g_mix'], 'w_in': out['w_in'], 'conv_w': out['conv_w'], 'conv_b': out['conv_b'], 'dt_bias': out['dt_bias'], 'a_log': out['a_log'], 'd_skip': out['d_skip'], 'ssd_norm': out['ssd_norm'], 'w_a': out['w_a'], 'w_b': out['w_b'], 'w_c': out['w_c'], 'sink': out['sink'], 'w_out': out['w_out'], 'g_mlp': out['g_mlp'], 'w_up': out['w_up'], 'w_down': out['w_down'], 'g_final': out['g_final'], 'loss_target': out['loss_target'], 'm_g_mix': out['m_g_mix'], 'm_w_in': out['m_w_in'], 'm_conv_w': out['m_conv_w'], 'm_conv_b': out['m_conv_b'], 'm_dt_bias': out['m_dt_bias'], 'm_a_log': out['m_a_log'], 'm_d_skip': out['m_d_skip'], 'm_ssd_norm': out['m_ssd_norm'], 'm_w_a': out['m_w_a'], 'm_w_b': out['m_w_b'], 'm_w_c': out['m_w_c'], 'm_sink': out['m_sink'], 'm_w_out': out['m_w_out'], 'm_g_mlp': out['m_g_mlp'], 'm_w_up': out['m_w_up'], 'm_w_down': out['m_w_down'], 'm_g_final': out['m_g_final'], 'v_g_mix': out['v_g_mix'], 'v_w_in': out['v_w_in'], 'v_conv_w': out['v_conv_w'], 'v_conv_b': out['v_conv_b'], 'v_dt_bias': out['v_dt_bias'], 'v_a_log': out['v_a_log'], 'v_d_skip': out['v_d_skip'], 'v_ssd_norm': out['v_ssd_norm'], 'v_w_a': out['v_w_a'], 'v_w_b': out['v_w_b'], 'v_w_c': out['v_w_c'], 'v_sink': out['v_sink'], 'v_w_out': out['v_w_out'], 'v_g_mlp': out['v_g_mlp'], 'v_w_up': out['v_w_up'], 'v_w_down': out['v_w_down'], 'v_g_final': out['v_g_final']}


def _loss(weights, diff, rest, loss_target):
    with _jax.named_scope("forward"):
        args = {**rest, TWIN_DIFF_INPUT: diff, **{k: w.astype(_WEIGHT_DTYPES[k]) for k, w in weights.items()}}
        y = _forward(args)
    with _jax.named_scope("loss_head"):
        err = _jnp.square(y.astype(_jnp.float32) - loss_target)
        return 0.5 * _jnp.sum(_jnp.mean(err, axis=-1)) if err.ndim else 0.5 * err


def _adamw(w, g, m, v):
    m = ADAM_B1 * m + (1.0 - ADAM_B1) * g
    v = ADAM_B2 * v + (1.0 - ADAM_B2) * _jnp.square(g)
    m_hat = m / (1.0 - ADAM_B1 ** ADAM_STEP)
    v_hat = v / (1.0 - ADAM_B2 ** ADAM_STEP)
    delta = -ADAM_LR * (m_hat / (_jnp.sqrt(v_hat) + ADAM_EPS) + ADAM_WD * w)
    return delta, m, v


def reference(x, g_mix, w_in, conv_w, conv_b, dt_bias, a_log, d_skip, ssd_norm, w_a, w_b, w_c, sink, w_out, g_mlp, w_up, w_down, g_final, loss_target, m_g_mix, m_w_in, m_conv_w, m_conv_b, m_dt_bias, m_a_log, m_d_skip, m_ssd_norm, m_w_a, m_w_b, m_w_c, m_sink, m_w_out, m_g_mlp, m_w_up, m_w_down, m_g_final, v_g_mix, v_w_in, v_conv_w, v_conv_b, v_dt_bias, v_a_log, v_d_skip, v_ssd_norm, v_w_a, v_w_b, v_w_c, v_sink, v_w_out, v_g_mlp, v_w_up, v_w_down, v_g_final):
    given = dict(x=x, g_mix=g_mix, w_in=w_in, conv_w=conv_w, conv_b=conv_b, dt_bias=dt_bias, a_log=a_log, d_skip=d_skip, ssd_norm=ssd_norm, w_a=w_a, w_b=w_b, w_c=w_c, sink=sink, w_out=w_out, g_mlp=g_mlp, w_up=w_up, w_down=w_down, g_final=g_final, loss_target=loss_target, m_g_mix=m_g_mix, m_w_in=m_w_in, m_conv_w=m_conv_w, m_conv_b=m_conv_b, m_dt_bias=m_dt_bias, m_a_log=m_a_log, m_d_skip=m_d_skip, m_ssd_norm=m_ssd_norm, m_w_a=m_w_a, m_w_b=m_w_b, m_w_c=m_w_c, m_sink=m_sink, m_w_out=m_w_out, m_g_mlp=m_g_mlp, m_w_up=m_w_up, m_w_down=m_w_down, m_g_final=m_g_final, v_g_mix=v_g_mix, v_w_in=v_w_in, v_conv_w=v_conv_w, v_conv_b=v_conv_b, v_dt_bias=v_dt_bias, v_a_log=v_a_log, v_d_skip=v_d_skip, v_ssd_norm=v_ssd_norm, v_w_a=v_w_a, v_w_b=v_w_b, v_w_c=v_w_c, v_sink=v_sink, v_w_out=v_w_out, v_g_mlp=v_g_mlp, v_w_up=v_w_up, v_w_down=v_w_down, v_g_final=v_g_final)
    weights = {n: given[n] for n in TWIN_WEIGHTS}
    shared = {n: given[n] for n in SHARED_INPUTS}
    per_example = {n: given[n] for n in ['x']}
    grad_fn = _jax.value_and_grad(_loss, argnums=(0, 1))

    def one_microbatch(ex, loss_target):
        ex = dict(ex)
        diff = ex.pop(TWIN_DIFF_INPUT)
        return grad_fn(weights, diff, {**shared, **ex}, loss_target)

    if N_MICROBATCH == 1:
        loss, (grad_w, grad_x) = one_microbatch(per_example, given["loss_target"])
    else:
        def body(carry, xs):
            loss_sum, grad_sum = carry
            l_k, (gw_k, gx_k) = one_microbatch(xs[0], xs[1])
            with _jax.named_scope("update"):
                return (loss_sum + l_k, _jax.tree.map(_jnp.add, grad_sum, gw_k)), gx_k

        init = (_jnp.zeros((), _jnp.float32), _jax.tree.map(_jnp.zeros_like, weights))
        (loss, grad_w), grad_x = _jax.lax.scan(body, init, (per_example, given["loss_target"]))
    with _jax.named_scope("update"):
        delta_w, new_m, new_v = {}, {}, {}
        for n in TWIN_WEIGHTS:
            delta_w[n], new_m[n], new_v[n] = _adamw(weights[n], grad_w[n], given["m_" + n], given["v_" + n])
    return (loss, grad_x, *[grad_w[n] for n in TWIN_WEIGHTS], *[delta_w[n] for n in TWIN_WEIGHTS],
            *[new_m[n] for n in TWIN_WEIGHTS], *[new_v[n] for n in TWIN_WEIGHTS])
```

```python
import functools
import math

import jax
import jax.numpy as jnp
from jax import lax
from jax.experimental import pallas as pl
from jax.experimental.pallas import tpu as pltpu

F32 = jnp.float32
BF16 = jnp.bfloat16
HI = lax.Precision.HIGHEST
MESH = pl.DeviceIdType.MESH
N_DEV = 8

SSD_HEADS = 32
SSD_HEAD_DIM = 64
SSD_GROUPS = 8
SSD_STATE = 128
SSD_CHUNK = 128
CONV_WIDTH = 5
HEAD_DIM = 128
ROPE_DIM = 32
ROPE_THETA = 500000.0
DIL_PATTERNS = ((128, 1), (512, 4), (2048, 16))
DIL_HEADS = 8
WIN_Q_HEADS = 16
WIN_KV_HEADS = 4
WIN_HALF = 128
N_BRANCH = 3
EPS = 1e-6
NEG_INF = -1e30
ADAM_LR = 0.001
ADAM_B1 = 0.9
ADAM_B2 = 0.999
ADAM_EPS = 1e-08
ADAM_WD = 0.01
ADAM_STEP = 10

LANE = 128
QBLK = 128
VMEM_LIMIT = 56 * 1024 * 1024
PAD_TO = 512


def _cparams(sem=None):
    return pltpu.CompilerParams(dimension_semantics=sem, vmem_limit_bytes=VMEM_LIMIT)


def _mm(a, b, *, ta=False, tb=False, out_dtype=F32, add=None, tm=512, tn=512, tk=512, name):
    m, k = (a.shape[1], a.shape[0]) if ta else a.shape
    k2, n = (b.shape[1], b.shape[0]) if tb else b.shape
    assert k == k2, (a.shape, b.shape, ta, tb)
    tm, tn, tk = min(tm, m), min(tn, n), min(tk, k)
    assert m % tm == 0 and n % tn == 0 and k % tk == 0, (m, n, k, tm, tn, tk)
    nk = k // tk
    a_spec = pl.BlockSpec((tk, tm), lambda i, j, kk: (kk, i)) if ta else pl.BlockSpec((tm, tk), lambda i, j, kk: (i, kk))
    b_spec = pl.BlockSpec((tn, tk), lambda i, j, kk: (j, kk)) if tb else pl.BlockSpec((tk, tn), lambda i, j, kk: (kk, j))
    o_spec = pl.BlockSpec((tm, tn), lambda i, j, kk: (i, j))
    dims = (((0 if ta else 1,), (1 if tb else 0,)), ((), ()))
    has_add = add is not None

    def body(*refs):
        if has_add:
            a_ref, b_ref, add_ref, o_ref, acc_ref = refs
        else:
            a_ref, b_ref, o_ref, acc_ref = refs
        kk = pl.program_id(2)

        @pl.when(kk == 0)
        def _():
            acc_ref[...] = jnp.zeros_like(acc_ref)

        acc_ref[...] += lax.dot_general(a_ref[...].astype(BF16), b_ref[...].astype(BF16), dims,
                                        preferred_element_type=F32)

        @pl.when(kk == nk - 1)
        def _():
            r = acc_ref[...]
            if has_add:
                r = r + add_ref[...]
            o_ref[...] = r.astype(o_ref.dtype)

    ins = [a, b] + ([add] if has_add else [])
    specs = [a_spec, b_spec] + ([o_spec] if has_add else [])
    return pl.pallas_call(
        body, name=name, grid=(m // tm, n // tn, nk), in_specs=specs, out_specs=o_spec,
        out_shape=jax.ShapeDtypeStruct((m, n), out_dtype),
        scratch_shapes=[pltpu.VMEM((tm, tn), F32)],
        compiler_params=_cparams(("parallel", "parallel", "arbitrary")),
    )(*ins)


def _rowcall(fn, rows, pars, row_outs, par_outs, *, tile, name):
    t = rows[0][0].shape[0]
    tile = min(tile, t)
    assert t % tile == 0
    nr, npar, nro, npo = len(rows), len(pars), len(row_outs), len(par_outs)
    in_specs = [pl.BlockSpec((tile, c), functools.partial(lambda i, cb: (i, cb), cb=cb)) for (_, c, cb) in rows]
    in_specs += [pl.BlockSpec(p.shape, lambda i: (0, 0)) for p in pars]
    out_specs = [pl.BlockSpec((tile, c), lambda i: (i, 0)) for (c, _) in row_outs]
    out_specs += [pl.BlockSpec(s, lambda i: (0, 0)) for s in par_outs]
    out_shape = [jax.ShapeDtypeStruct((t, c), d) for (c, d) in row_outs]
    out_shape += [jax.ShapeDtypeStruct(s, F32) for s in par_outs]

    def body(*refs):
        rv = [r[...] for r in refs[:nr]]
        pv = [r[...] for r in refs[nr:nr + npar]]
        ro_refs = refs[nr + npar:nr + npar + nro]
        po_refs = refs[nr + npar + nro:]
        ro, po = fn(rv, pv)
        for ref, v in zip(ro_refs, ro):
            ref[...] = v.astype(ref.dtype)
        if npo:
            @pl.when(pl.program_id(0) == 0)
            def _():
                for ref in po_refs:
                    ref[...] = jnp.zeros_like(ref)
            for ref, v in zip(po_refs, po):
                ref[...] += v

    res = pl.pallas_call(
        body, name=name, grid=(t // tile,), in_specs=in_specs, out_specs=out_specs, out_shape=out_shape,
        compiler_params=_cparams(("arbitrary",)),
    )(*[r[0] for r in rows], *pars)
    return list(res[:nro]), list(res[nro:])


def _map2d(fn, ins, out_dtype, *, name, tile=256, cw=2048):
    t, w = ins[0].shape
    tile, cw = min(tile, t), min(cw, w)
    assert t % tile == 0 and w % cw == 0

    def body(*refs):
        refs[-1][...] = fn(*[r[...] for r in refs[:-1]]).astype(out_dtype)

    spec = pl.BlockSpec((tile, cw), lambda i, j: (i, j))
    return pl.pallas_call(
        body, name=name, grid=(t // tile, w // cw), in_specs=[spec] * len(ins), out_specs=spec,
        out_shape=jax.ShapeDtypeStruct((t, w), out_dtype), compiler_params=_cparams(("parallel", "parallel")),
    )(*ins)


def _exchange(srcs, *, scatter, name):
    n = len(srcs)
    out_shape = [jax.ShapeDtypeStruct(s.shape if scatter else (N_DEV,) + s.shape, s.dtype) for s in srcs]

    def body(*refs):
        src_refs, out_refs = refs[:n], refs[n:2 * n]
        send_sems, recv_sems, loc_sems = refs[2 * n:]
        x, y, c = lax.axis_index("x"), lax.axis_index("y"), lax.axis_index("c")
        me = 4 * x + 2 * y + c
        copies = []
        for a in range(n):
            for j in range(1, N_DEV):
                px = (1 - x) if (j >> 2) & 1 else x
                py = (1 - y) if (j >> 1) & 1 else y
                pc = (1 - c) if j & 1 else c
                src = src_refs[a].at[4 * px + 2 * py + pc] if scatter else src_refs[a]
                cp = pltpu.make_async_remote_copy(
                    src_ref=src, dst_ref=out_refs[a].at[me], send_sem=send_sems.at[a * 7 + j - 1],
                    recv_sem=recv_sems.at[a * 7 + j - 1], device_id=(px, py, pc), device_id_type=MESH)
                cp.start()
                copies.append(cp)
            src = src_refs[a].at[me] if scatter else src_refs[a]
            cp = pltpu.make_async_copy(src, out_refs[a].at[me], loc_sems.at[a])
            cp.start()
            copies.append(cp)
        for cp in copies:
            cp.wait()

    anyspec = pl.BlockSpec(memory_space=pl.ANY)
    return pl.pallas_call(
        body, name=name, in_specs=[anyspec] * n, out_specs=[anyspec] * n, out_shape=out_shape,
        scratch_shapes=[pltpu.SemaphoreType.DMA((7 * n,)), pltpu.SemaphoreType.DMA((7 * n,)),
                        pltpu.SemaphoreType.DMA((n,))],
        compiler_params=pltpu.CompilerParams(has_side_effects=True),
    )(*srcs)


def _row_tile(r, c, budget_elems=256 * 1024):
    tr = r
    while tr * c > budget_elems and tr % 16 == 0:
        tr //= 2
    return tr


def _adamw(w, recv, m, v, *, name):
    r, c = w.shape
    ns = recv.shape[0]
    tr = _row_tile(r, c)
    bc1 = 1.0 / (1.0 - ADAM_B1 ** ADAM_STEP)
    bc2 = 1.0 / (1.0 - ADAM_B2 ** ADAM_STEP)

    def body(w_ref, r_ref, m_ref, v_ref, g_ref, d_ref, mo_ref, vo_ref):
        g = r_ref[0].astype(F32)
        for s in range(1, ns):
            g = g + r_ref[s].astype(F32)
        mn = ADAM_B1 * m_ref[...] + (1.0 - ADAM_B1) * g
        vn = ADAM_B2 * v_ref[...] + (1.0 - ADAM_B2) * (g * g)
        g_ref[...] = g
        mo_ref[...] = mn
        vo_ref[...] = vn
        d_ref[...] = -ADAM_LR * ((mn * bc1) / (jnp.sqrt(vn * bc2) + ADAM_EPS) + ADAM_WD * w_ref[...])

    spec = pl.BlockSpec((tr, c), lambda i: (i, 0))
    return pl.pallas_call(
        body, name=name, grid=(r // tr,),
        in_specs=[spec, pl.BlockSpec((ns, tr, c), lambda i: (0, i, 0)), spec, spec],
        out_specs=[spec] * 4, out_shape=[jax.ShapeDtypeStruct((r, c), F32)] * 4,
        compiler_params=_cparams(("parallel",)),
    )(w, recv, m, v)


def _sum_slots(recv, *, name):
    ns, r, c = recv.shape

    def body(r_ref, o_ref):
        g = r_ref[0]
        for s in range(1, ns):
            g = g + r_ref[s]
        o_ref[...] = g

    return pl.pallas_call(body, name=name, out_shape=jax.ShapeDtypeStruct((r, c), F32))(recv)


def _rms(x, g):
    return x * lax.rsqrt(jnp.mean(x * x, axis=-1, keepdims=True) + EPS) * g


def _silu(x):
    return x * jax.nn.sigmoid(x)


def _merge_f(a, b, c, g0, g1, g2):
    return jax.nn.sigmoid(g0) * a + jax.nn.sigmoid(g1) * b + jax.nn.sigmoid(g2) * c


def _ssd_post_f(yf, yb, xs, z, dskip, gnorm):
    y = (yf + yb + dskip * xs) * _silu(z)
    return _rms(y, gnorm)


def _combine_f(o0, o1, o2, l0, l1, l2):
    m = jnp.maximum(jnp.maximum(l0, l1), l2)
    e0, e1, e2 = jnp.exp(l0 - m), jnp.exp(l1 - m), jnp.exp(l2 - m)
    return (e0 * o0 + e1 * o1 + e2 * o2) / (e0 + e1 + e2)


def _attn_block(q, k3, v3, sk, i, *, seq, hw, has_sink):
    s = lax.dot_general(q.astype(BF16), k3.astype(BF16), (((1,), (1,)), ((), ())),
                        preferred_element_type=F32) * (HEAD_DIM ** -0.5)
    qpos = i * QBLK + lax.broadcasted_iota(jnp.int32, s.shape, 0)
    kpos = (i - 1) * QBLK + lax.broadcasted_iota(jnp.int32, s.shape, 1)
    valid = (jnp.abs(qpos - kpos) <= hw) & (kpos >= 0) & (kpos < seq)
    s = jnp.where(valid, s, NEG_INF)
    m = jnp.max(s, axis=-1, keepdims=True)
    if has_sink:
        m = jnp.maximum(m, sk)
    m = lax.stop_gradient(m)
    e = jnp.exp(s - m)
    l = jnp.sum(e, axis=-1, keepdims=True)
    if has_sink:
        l = l + jnp.exp(sk - m)
    o = jnp.dot(e.astype(BF16), v3.astype(BF16), preferred_element_type=F32) / l
    return o, m + jnp.log(l)


def _ssd_chunk(state, xs, bm, cm, dtr, dtr_t, bias, bias_t, alog, alog_t, *, reverse):
    t = xs.shape[0]
    hg = dtr.shape[1]
    hp = xs.shape[1]
    p = hp // hg
    dt = jax.nn.softplus(dtr + bias)
    dt_t = jax.nn.softplus(dtr_t + bias_t)
    dta = dt * (-jnp.exp(alog))
    dta_t = dt_t * (-jnp.exp(alog_t))
    li = lax.broadcasted_iota(jnp.int32, (t, t), 0)
    si = lax.broadcasted_iota(jnp.int32, (t, t), 1)
    tri = (li <= si) if reverse else (li >= si)
    trif = tri.astype(F32)
    cs = jnp.dot(trif, dta, precision=HI, preferred_element_type=F32)
    cs_t = lax.dot_general(dta_t, trif, (((1,), (1,)), ((), ())), precision=HI,
                           preferred_element_type=F32)
    total = jnp.sum(dta, axis=0, keepdims=True)
    cb = lax.dot_general(cm.astype(BF16), bm.astype(BF16), (((1,), (1,)), ((), ())),
                         preferred_element_type=F32)
    lane_h = lax.broadcasted_iota(jnp.int32, (1, hp), 1) // p
    col_h = lax.broadcasted_iota(jnp.int32, (1, hg), 1)
    row_h = lax.broadcasted_iota(jnp.int32, (hg, 1), 0)
    dt_x = jnp.zeros((t, hp), F32)
    ecs_x = jnp.zeros((t, hp), F32)
    ds_x = jnp.zeros((t, hp), F32)
    etot_x = jnp.zeros((1, hp), F32)
    decays, masks = [], []
    for h in range(hg):
        oh = (col_h == h).astype(F32)
        oh_t = (row_h == h).astype(F32)
        mk = (lane_h == h).astype(F32)
        dt_h = jnp.sum(dt * oh, axis=1, keepdims=True)
        cs_h = jnp.sum(cs * oh, axis=1, keepdims=True)
        cst_h = jnp.sum(cs_t * oh_t, axis=0, keepdims=True)
        tot_h = jnp.sum(total * oh, axis=1, keepdims=True)
        dt_x = dt_x + dt_h * mk
        ecs_x = ecs_x + jnp.exp(cs_h) * mk
        ds_x = ds_x + jnp.exp(tot_h - cs_h) * mk
        etot_x = etot_x + jnp.exp(tot_h) * mk
        decays.append(jnp.exp(jnp.where(tri, cs_h - cst_h, -jnp.inf)))
        masks.append(mk)
    xdt = xs * dt_x
    y = jnp.dot(cm.astype(BF16), state.astype(BF16), preferred_element_type=F32) * ecs_x
    for h in range(hg):
        y = y + jnp.dot((cb * decays[h]).astype(BF16), (xdt * masks[h]).astype(BF16),
                        preferred_element_type=F32)
    st_new = lax.dot_general(bm.astype(BF16), (xdt * ds_x).astype(BF16), (((0,), (0,)), ((), ())),
                             preferred_element_type=F32)
    return y, state * etot_x + st_new


def _rope_tables(seq):
    half = ROPE_DIM // 2
    inv = ROPE_THETA ** (-jnp.arange(0, ROPE_DIM, 2, dtype=F32) / ROPE_DIM)
    ang = jnp.arange(seq, dtype=F32)[:, None] * inv[None, :]
    cos, sin = jnp.cos(ang), jnp.sin(ang)
    rest = HEAD_DIM - ROPE_DIM
    c = jnp.concatenate([cos, cos, jnp.ones((seq, rest), F32)], axis=1)
    a = jnp.concatenate([-sin, jnp.zeros((seq, HEAD_DIM - half), F32)], axis=1)
    b = jnp.concatenate([jnp.zeros((seq, half), F32), sin, jnp.zeros((seq, rest), F32)], axis=1)
    return c, a, b


def _rope(src, tabs, *, col0, width, seq, group, inverse, out_dtype, name):
    t = src.shape[0]
    half = ROPE_DIM // 2
    cw, tile = 3 * HEAD_DIM, 256
    assert width % cw == 0 and col0 % cw == 0 and seq % tile == 0 and t % tile == 0
    ns = seq // tile

    def body(x_ref, c_ref, a_ref, b_ref, o_ref):
        jb = pl.program_id(1)
        c, a, b = c_ref[...], a_ref[...], b_ref[...]
        for hh in range(3):
            xv = x_ref[:, hh * HEAD_DIM:(hh + 1) * HEAD_DIM].astype(F32)
            if inverse:
                yv = xv * c + pltpu.roll(xv * a, half, 1) + pltpu.roll(xv * b, HEAD_DIM - half, 1)
            else:
                yv = xv * c + pltpu.roll(xv, HEAD_DIM - half, 1) * a + pltpu.roll(xv, half, 1) * b
            if group:
                keep = ((jb * 3 + hh) % group) == (group - 1)
                yv = jnp.where(keep, xv, yv)
            o_ref[:, hh * HEAD_DIM:(hh + 1) * HEAD_DIM] = yv.astype(o_ref.dtype)

    tspec = pl.BlockSpec((tile, HEAD_DIM), lambda i, j: (i % ns, 0))
    return pl.pallas_call(
        body, name=name, grid=(t // tile, width // cw),
        in_specs=[pl.BlockSpec((tile, cw), lambda i, j: (i, col0 // cw + j)), tspec, tspec, tspec],
        out_specs=pl.BlockSpec((tile, cw), lambda i, j: (i, j)),
        out_shape=jax.ShapeDtypeStruct((t, width), out_dtype),
        compiler_params=_cparams(("parallel", "parallel")),
    )(src, *tabs)


def _shift_rows(x, d, tpos):
    if d == 0:
        return x
    s = x.shape[0]
    y = pltpu.roll(x, (-d) % s, 0)
    ok = (tpos + d >= 0) & (tpos + d < s)
    return jnp.where(ok, y, 0.0)


def _conv_fwd(p, w8, bias, *, col0, chans, batch, seq, name):
    cb = 256
    assert chans % cb == 0 and col0 % cb == 0
    pad = (CONV_WIDTH - 1) // 2

    def body(x_ref, w_ref, b_ref, o_ref):
        x = x_ref[...]
        tpos = lax.broadcasted_iota(jnp.int32, x.shape, 0)
        acc = jnp.broadcast_to(b_ref[...], x.shape)
        for k in range(CONV_WIDTH):
            acc = acc + w_ref[k:k + 1, :] * _shift_rows(x, k - pad, tpos)
        o_ref[...] = _silu(acc)

    return pl.pallas_call(
        body, name=name, grid=(chans // cb, batch),
        in_specs=[pl.BlockSpec((seq, cb), lambda j, b: (b, col0 // cb + j)),
                  pl.BlockSpec((8, cb), lambda j, b: (0, j)), pl.BlockSpec((1, cb), lambda j, b: (0, j))],
        out_specs=pl.BlockSpec((seq, cb), lambda j, b: (b, j)),
        out_shape=jax.ShapeDtypeStruct((batch * seq, chans), F32),
        compiler_params=_cparams(("parallel", "arbitrary")),
    )(p, w8, bias)


def _conv_bwd(p, w8, bias, du, *, col0, chans, batch, seq, name):
    cb = 256
    pad = (CONV_WIDTH - 1) // 2

    def body(x_ref, w_ref, b_ref, du_ref, dx_ref, dw_ref, db_ref):
        x = x_ref[...]
        tpos = lax.broadcasted_iota(jnp.int32, x.shape, 0)
        acc = jnp.broadcast_to(b_ref[...], x.shape)
        xs = []
        for k in range(CONV_WIDTH):
            xs.append(_shift_rows(x, k - pad, tpos))
            acc = acc + w_ref[k:k + 1, :] * xs[k]
        sg = jax.nn.sigmoid(acc)
        dacc = du_ref[...] * (sg * (1.0 + acc * (1.0 - sg)))
        dx = jnp.zeros_like(x)
        for k in range(CONV_WIDTH):
            dx = dx + w_ref[k:k + 1, :] * _shift_rows(dacc, pad - k, tpos)
        dx_ref[...] = dx.astype(dx_ref.dtype)

        @pl.when(pl.program_id(1) == 0)
        def _():
            dw_ref[...] = jnp.zeros_like(dw_ref)
            db_ref[...] = jnp.zeros_like(db_ref)

        for k in range(CONV_WIDTH):
            dw_ref[k:k + 1, :] += jnp.sum(dacc * xs[k], axis=0, keepdims=True)
        db_ref[...] += jnp.sum(dacc, axis=0, keepdims=True)

    return pl.pallas_call(
        body, name=name, grid=(chans // cb, batch),
        in_specs=[pl.BlockSpec((seq, cb), lambda j, b: (b, col0 // cb + j)),
                  pl.BlockSpec((8, cb), lambda j, b: (0, j)), pl.BlockSpec((1, cb), lambda j, b: (0, j)),
                  pl.BlockSpec((seq, cb), lambda j, b: (b, j))],
        out_specs=[pl.BlockSpec((seq, cb), lambda j, b: (b, j)), pl.BlockSpec((8, cb), lambda j, b: (0, j)),
                   pl.BlockSpec((1, cb), lambda j, b: (0, j))],
        out_shape=[jax.ShapeDtypeStruct((batch * seq, chans), BF16), jax.ShapeDtypeStruct((8, chans), F32),
                   jax.ShapeDtypeStruct((1, chans), F32)],
        compiler_params=_cparams(("parallel", "arbitrary")),
    )(p, w8, bias, du)


def _scan_specs(batch, nc, groups, hg, inner, reverse_order):
    t, n, hp = SSD_CHUNK, SSD_STATE, hg * SSD_HEAD_DIM
    xcb, ncb = inner // hp, inner // n

    def row(b, c):
        return b * nc + ((nc - 1 - c) if reverse_order else c)

    return dict(
        xs=pl.BlockSpec((t, hp), lambda g, b, c: (row(b, c), g)),
        bm=pl.BlockSpec((t, n), lambda g, b, c: (row(b, c), ncb + g)),
        cm=pl.BlockSpec((t, n), lambda g, b, c: (row(b, c), ncb + groups + g)),
        dtr=pl.BlockSpec((None, t, hg), lambda g, b, c: (g, row(b, c), 0)),
        dtr_t=pl.BlockSpec((None, hg, t), lambda g, b, c: (g, 0, row(b, c))),
        par=pl.BlockSpec((None, 1, hg), lambda g, b, c: (g, 0, 0)),
        par_t=pl.BlockSpec((None, hg, 1), lambda g, b, c: (g, 0, 0)),
        y=pl.BlockSpec((t, hp), lambda g, b, c: (row(b, c), g)),
        nrow=pl.BlockSpec((t, n), lambda g, b, c: (row(b, c), g)),
        st=pl.BlockSpec((None, None, n, hp), lambda g, b, c: (g, row(b, c), 0, 0)),
    )


def _scan_fwd(u, dtr, dtr_t, bias, bias_t, alog, alog_t, *, batch, seq, inner, reverse, name):
    groups, hg = dtr.shape[0], dtr.shape[2]
    nc = seq // SSD_CHUNK
    hp = hg * SSD_HEAD_DIM
    sp = _scan_specs(batch, nc, groups, hg, inner, reverse)

    def body(xs_ref, bm_ref, cm_ref, dtr_ref, dtrt_ref, b_ref, bt_ref, a_ref, at_ref, y_ref, st_ref, state):
        @pl.when(pl.program_id(2) == 0)
        def _():
            state[...] = jnp.zeros_like(state)

        st_in = state[...]
        st_ref[...] = st_in
        y, st_out = _ssd_chunk(st_in, xs_ref[...], bm_ref[...], cm_ref[...], dtr_ref[...], dtrt_ref[...],
                               b_ref[...], bt_ref[...], a_ref[...], at_ref[...], reverse=reverse)
        y_ref[...] = y
        state[...] = st_out

    return pl.pallas_call(
        body, name=name, grid=(groups, batch, nc),
        in_specs=[sp["xs"], sp["bm"], sp["cm"], sp["dtr"], sp["dtr_t"], sp["par"], sp["par_t"], sp["par"], sp["par_t"]],
        out_specs=[sp["y"], sp["st"]],
        out_shape=[jax.ShapeDtypeStruct((batch * seq, inner), F32),
                   jax.ShapeDtypeStruct((groups, batch * nc, SSD_STATE, hp), F32)],
        scratch_shapes=[pltpu.VMEM((SSD_STATE, hp), F32)],
        compiler_params=_cparams(("parallel", "arbitrary", "arbitrary")),
    )(u, u, u, dtr, dtr_t, bias, bias_t, alog, alog_t)


def _scan_bwd(u, dtr, dtr_t, bias, bias_t, alog, alog_t, st, dy, *, batch, seq, inner, reverse, name):
    groups, hg = dtr.shape[0], dtr.shape[2]
    nc = seq // SSD_CHUNK
    hp = hg * SSD_HEAD_DIM
    t = batch * seq
    sp = _scan_specs(batch, nc, groups, hg, inner, not reverse)
    f = functools.partial(_ssd_chunk, reverse=reverse)

    def body(xs_ref, bm_ref, cm_ref, dtr_ref, dtrt_ref, b_ref, bt_ref, a_ref, at_ref, st_ref, dy_ref,
             dxs_ref, dbm_ref, dcm_ref, ddtr_ref, ddtrt_ref, db_ref, dbt_ref, da_ref, dat_ref, dstate):
        first = (pl.program_id(1) == 0) & (pl.program_id(2) == 0)

        @pl.when(pl.program_id(2) == 0)
        def _():
            dstate[...] = jnp.zeros_like(dstate)

        @pl.when(first)
        def _():
            for r in (db_ref, dbt_ref, da_ref, dat_ref):
                r[...] = jnp.zeros_like(r)

        _, vjp = jax.vjp(f, st_ref[...], xs_ref[...], bm_ref[...], cm_ref[...], dtr_ref[...], dtrt_ref[...],
                         b_ref[...], bt_ref[...], a_ref[...], at_ref[...])
        dst, dxs, dbm, dcm, ddtr, ddtrt, db, dbt, da, dat = vjp((dy_ref[...], dstate[...]))
        dstate[...] = dst
        dxs_ref[...] = dxs
        dbm_ref[...] = dbm
        dcm_ref[...] = dcm
        ddtr_ref[...] = ddtr
        ddtrt_ref[...] = ddtrt
        db_ref[...] += db
        dbt_ref[...] += dbt
        da_ref[...] += da
        dat_ref[...] += dat

    gn = groups * SSD_STATE
    return pl.pallas_call(
        body, name=name, grid=(groups, batch, nc),
        in_specs=[sp["xs"], sp["bm"], sp["cm"], sp["dtr"], sp["dtr_t"], sp["par"], sp["par_t"], sp["par"], sp["par_t"],
                  sp["st"], sp["y"]],
        out_specs=[sp["y"], sp["nrow"], sp["nrow"], sp["dtr"], sp["dtr_t"], sp["par"], sp["par_t"], sp["par"], sp["par_t"]],
        out_shape=[jax.ShapeDtypeStruct((t, inner), F32), jax.ShapeDtypeStruct((t, gn), F32),
                   jax.ShapeDtypeStruct((t, gn), F32), jax.ShapeDtypeStruct(dtr.shape, F32),
                   jax.ShapeDtypeStruct(dtr_t.shape, F32), jax.ShapeDtypeStruct(bias.shape, F32),
                   jax.ShapeDtypeStruct(bias_t.shape, F32), jax.ShapeDtypeStruct(alog.shape, F32),
                   jax.ShapeDtypeStruct(alog_t.shape, F32)],
        scratch_shapes=[pltpu.VMEM((SSD_STATE, hp), F32)],
        compiler_params=_cparams(("parallel", "arbitrary", "arbitrary")),
    )(u, u, u, dtr, dtr_t, bias, bias_t, alog, alog_t, st, dy)


def _attn_load(ref, col, blk):
    return ref[pl.ds(pl.multiple_of(blk * QBLK, QBLK), QBLK), col * HEAD_DIM:(col + 1) * HEAD_DIM].astype(F32)


def _lane0(row):
    lane = lax.broadcasted_iota(jnp.int32, row.shape, 1)
    return jnp.sum(jnp.where(lane == 0, row, 0.0), axis=1, keepdims=True)


def _attn_fwd(rq, sinkx, *, batch, seq, dil, nbw, cb0, nh, rep, hw, want_lse, out_dtype, name):
    t, w = rq.shape
    ln = seq // dil
    nb = ln // QBLK
    bw = (rep + 2) * HEAD_DIM
    ow = nh * rep * HEAD_DIM
    has_sink = sinkx is not None
    rq3 = rq.reshape(batch, ln, dil * w)
    f = functools.partial(_attn_block, seq=ln, hw=hw, has_sink=has_sink)

    def body(*refs):
        if has_sink:
            blk_ref, sink_ref = refs[:2]
            outs = refs[2:]
        else:
            blk_ref, sink_ref = refs[0], None
            outs = refs[1:]
        o_ref = outs[0]
        lse_ref = outs[1] if want_lse else None
        g = pl.program_id(2)

        def qblock(i, carry):
            ip, inx = jnp.maximum(i - 1, 0), jnp.minimum(i + 1, nb - 1)
            k3 = jnp.concatenate([_attn_load(blk_ref, rep, ip), _attn_load(blk_ref, rep, i),
                                  _attn_load(blk_ref, rep, inx)], axis=0)
            v3 = jnp.concatenate([_attn_load(blk_ref, rep + 1, ip), _attn_load(blk_ref, rep + 1, i),
                                  _attn_load(blk_ref, rep + 1, inx)], axis=0)
            rows = pl.ds(pl.multiple_of(i * QBLK, QBLK), QBLK)
            for r in range(rep):
                sk = _lane0(sink_ref[pl.ds(g * rep + r, 1), :]) if has_sink else None
                o, lse = f(_attn_load(blk_ref, r, i), k3, v3, sk, i)
                o_ref[rows, r * HEAD_DIM:(r + 1) * HEAD_DIM] = o.astype(o_ref.dtype)
                if want_lse:
                    lse_ref[rows, r * HEAD_DIM:(r + 1) * HEAD_DIM] = jnp.broadcast_to(lse, o.shape)
            return carry

        lax.fori_loop(0, nb, qblock, 0)

    in_specs = [pl.BlockSpec((None, ln, bw), lambda b, r, h: (b, 0, r * nbw + cb0 + h))]
    ins = [rq3]
    if has_sink:
        in_specs.append(pl.BlockSpec(sinkx.shape, lambda b, r, h: (0, 0)))
        ins.append(sinkx)
    ospec = pl.BlockSpec((None, ln, rep * HEAD_DIM), lambda b, r, h: (b, 0, r * nh + h))
    out_shape = [jax.ShapeDtypeStruct((batch, ln, dil * ow), out_dtype)]
    out_specs = [ospec]
    if want_lse:
        out_shape.append(jax.ShapeDtypeStruct((batch, ln, dil * ow), F32))
        out_specs.append(ospec)
    res = pl.pallas_call(
        body, name=name, grid=(batch, dil, nh), in_specs=in_specs, out_specs=out_specs, out_shape=out_shape,
        compiler_params=_cparams(("parallel", "parallel", "parallel")),
    )(*ins)
    return [r.reshape(t, ow) for r in res]


def _attn_bwd(rq, sinkx, do, dlse, *, batch, seq, dil, nbw, cb0, nh, rep, hw, name):
    t, w = rq.shape
    ln = seq // dil
    nb = ln // QBLK
    bw = (rep + 2) * HEAD_DIM
    ow = nh * rep * HEAD_DIM
    has_sink = sinkx is not None
    has_lse = dlse is not None
    f = functools.partial(_attn_block, seq=ln, hw=hw, has_sink=has_sink)

    def body(*refs):
        refs = list(refs)
        blk_ref = refs.pop(0)
        sink_ref = refs.pop(0) if has_sink else None
        do_ref = refs.pop(0)
        dlse_ref = refs.pop(0) if has_lse else None
        d_ref = refs.pop(0)
        dsink_ref = refs.pop(0) if has_sink else None
        g = pl.program_id(2)
        d_ref[:, rep * HEAD_DIM:] = jnp.zeros((ln, 2 * HEAD_DIM), F32)
        if has_sink:
            @pl.when((pl.program_id(0) == 0) & (pl.program_id(1) == 0) & (g == 0))
            def _():
                dsink_ref[...] = jnp.zeros_like(dsink_ref)

        def qblock(i, carry):
            blks = (jnp.maximum(i - 1, 0), i, jnp.minimum(i + 1, nb - 1))
            k3 = jnp.concatenate([_attn_load(blk_ref, rep, bi) for bi in blks], axis=0)
            v3 = jnp.concatenate([_attn_load(blk_ref, rep + 1, bi) for bi in blks], axis=0)
            rows = pl.ds(pl.multiple_of(i * QBLK, QBLK), QBLK)
            dk3 = jnp.zeros_like(k3)
            dv3 = jnp.zeros_like(v3)
            for r in range(rep):
                cols = slice(r * HEAD_DIM, (r + 1) * HEAD_DIM)
                q = _attn_load(blk_ref, r, i)
                dov = do_ref[rows, cols]
                dl = dlse_ref[rows, cols] if has_lse else jnp.zeros_like(dov)
                if has_sink:
                    srow = sink_ref[pl.ds(g * rep + r, 1), :]
                    _, vjp = jax.vjp(lambda q_, k_, v_, s_: f(q_, k_, v_, _lane0(s_), i), q, k3, v3, srow)
                    dq, dk, dv, ds = vjp((dov, jnp.sum(dl, axis=1, keepdims=True)))
                    dsink_ref[pl.ds(g * rep + r, 1), :] += ds
                else:
                    _, vjp = jax.vjp(lambda q_, k_, v_: f(q_, k_, v_, None, i), q, k3, v3)
                    dq, dk, dv = vjp((dov, jnp.sum(dl, axis=1, keepdims=True)))
                d_ref[rows, cols] = dq
                dk3 = dk3 + dk
                dv3 = dv3 + dv
            for m, bi in enumerate(blks):
                brow = pl.ds(pl.multiple_of(bi * QBLK, QBLK), QBLK)
                d_ref[brow, rep * HEAD_DIM:(rep + 1) * HEAD_DIM] += dk3[m * QBLK:(m + 1) * QBLK]
                d_ref[brow, (rep + 1) * HEAD_DIM:] += dv3[m * QBLK:(m + 1) * QBLK]
            return carry

        lax.fori_loop(0, nb, qblock, 0)

    ospec = pl.BlockSpec((None, ln, rep * HEAD_DIM), lambda b, r, h: (b, 0, r * nh + h))
    in_specs = [pl.BlockSpec((None, ln, bw), lambda b, r, h: (b, 0, r * nbw + cb0 + h))]
    ins = [rq.reshape(batch, ln, dil * w)]
    if has_sink:
        in_specs.append(pl.BlockSpec(sinkx.shape, lambda b, r, h: (0, 0)))
        ins.append(sinkx)
    in_specs.append(ospec)
    ins.append(do.reshape(batch, ln, dil * ow))
    if has_lse:
        in_specs.append(ospec)
        ins.append(dlse.reshape(batch, ln, dil * ow))
    dw = nh * bw
    out_specs = [pl.BlockSpec((None, ln, bw), lambda b, r, h: (b, 0, r * nh + h))]
    out_shape = [jax.ShapeDtypeStruct((batch, ln, dil * dw), F32)]
    if has_sink:
        out_specs.append(pl.BlockSpec(sinkx.shape, lambda b, r, h: (0, 0)))
        out_shape.append(jax.ShapeDtypeStruct(sinkx.shape, F32))
    res = pl.pallas_call(
        body, name=name, grid=(batch, dil, nh), in_specs=in_specs, out_specs=out_specs, out_shape=out_shape,
        compiler_params=_cparams(("arbitrary", "arbitrary", "arbitrary")),
    )(*ins)
    return [res[0].reshape(t, dw)] + list(res[1:])


def _final_loss(x, g, target, *, name):
    d = x.shape[1]

    def fn(rv, pv):
        xv, tg = rv
        y, vjp = jax.vjp(_rms, xv, pv[0])
        err = y - tg
        dx, dg = vjp(err * (1.0 / d))
        loss = 0.5 * jnp.sum(err * err) * (1.0 / d)
        return [dx], [dg, jnp.full((1, LANE), loss, F32)]

    (dx,), (dg, loss) = _rowcall(fn, [(x, d, 0), (target, d, 0)], [g], [(d, F32)], [(1, d), (1, LANE)],
                                 tile=256, name=name)
    return dx, dg, loss


class _Layout:
    def __init__(self, d_model):
        self.d = d_model
        self.inner = SSD_HEADS * SSD_HEAD_DIM
        self.gn = SSD_GROUPS * SSD_STATE
        self.xbc = self.inner + 2 * self.gn
        self.ndt = 2 * SSD_HEADS
        self.ngrp = len(DIL_PATTERNS)
        self.dilw = DIL_HEADS * HEAD_DIM
        self.rqd = 3 * self.ngrp * self.dilw
        self.rep = WIN_Q_HEADS // WIN_KV_HEADS
        self.rqw = WIN_KV_HEADS * (self.rep + 2) * HEAD_DIM
        self.qw = WIN_Q_HEADS * HEAD_DIM
        self.kw = WIN_KV_HEADS * HEAD_DIM
        self.gates = N_BRANCH * d_model
        self.n_in = self.inner + self.xbc + self.ndt + self.rqd + self.qw + 2 * self.kw + self.gates
        self.o_gates = 0
        self.o_z = self.gates
        self.o_xbc = self.o_z + self.inner
        self.o_rqd = self.o_xbc + self.xbc
        self.o_rqw = self.o_rqd + self.rqd
        self.o_dt = self.o_rqw + self.rqw
        self.dtw = -(-(self.o_dt + self.ndt) // PAD_TO) * PAD_TO - self.o_dt
        self.width = self.o_dt + self.dtw
        assert self.o_z % self.inner == 0 and self.o_xbc % 256 == 0
        assert self.o_rqd % (3 * HEAD_DIM) == 0 and self.o_rqw % (3 * HEAD_DIM) == 0 and self.dtw % LANE == 0

    def split_points(self):
        sizes = (self.inner, self.xbc, self.ndt, self.rqd, self.qw, self.kw, self.kw, self.gates)
        pts, acc = [], 0
        for s in sizes:
            pts.append((acc, acc + s))
            acc += s
        return pts

    def permute_w(self, w):
        d = w.shape[0]
        z, xbc, dt, qkvd, qw, kw, vw, gates = [w[:, a:b] for a, b in self.split_points()]
        nhd = self.ngrp * DIL_HEADS
        qkvd = qkvd.reshape(d, 3, nhd, HEAD_DIM).transpose(0, 2, 1, 3).reshape(d, self.rqd)
        win = jnp.concatenate([qw.reshape(d, WIN_KV_HEADS, self.rep, HEAD_DIM),
                               kw.reshape(d, WIN_KV_HEADS, 1, HEAD_DIM),
                               vw.reshape(d, WIN_KV_HEADS, 1, HEAD_DIM)], axis=2).reshape(d, self.rqw)
        pad = jnp.zeros((d, self.dtw - self.ndt), w.dtype)
        return jnp.concatenate([gates, z, xbc, qkvd, win, dt, pad], axis=1)

    def unpermute_w(self, wp):
        d = wp.shape[0]
        gates = wp[:, :self.o_z]
        z = wp[:, self.o_z:self.o_xbc]
        xbc = wp[:, self.o_xbc:self.o_rqd]
        qkvd = wp[:, self.o_rqd:self.o_rqw]
        win = wp[:, self.o_rqw:self.o_dt].reshape(d, WIN_KV_HEADS, self.rep + 2, HEAD_DIM)
        dt = wp[:, self.o_dt:self.o_dt + self.ndt]
        nhd = self.ngrp * DIL_HEADS
        qkvd = qkvd.reshape(d, nhd, 3, HEAD_DIM).transpose(0, 2, 1, 3).reshape(d, self.rqd)
        qw = win[:, :, :self.rep].reshape(d, self.qw)
        kw = win[:, :, self.rep].reshape(d, self.kw)
        vw = win[:, :, self.rep + 1].reshape(d, self.kw)
        return jnp.concatenate([z, xbc, dt, qkvd, qw, kw, vw, gates], axis=1)


def _dt_layouts(pdt, dirn, batch_seq):
    hg = SSD_HEADS // SSD_GROUPS
    v = pdt[:, dirn * SSD_HEADS:(dirn + 1) * SSD_HEADS].reshape(batch_seq, SSD_GROUPS, hg)
    return v.transpose(1, 0, 2), v.transpose(1, 2, 0)


def _par_layouts(p):
    hg = SSD_HEADS // SSD_GROUPS
    v = p.reshape(SSD_GROUPS, hg)
    return v[:, None, :], v[:, :, None]


def _layer_fwd(x, lw, lay, tabs, batch, seq):
    d = lay.d
    sv = {"x": x}
    (h,), _ = _rowcall(lambda rv, pv: ([_rms(rv[0], pv[0])], []), [(x, d, 0)], [lw["g_mix"]], [(d, BF16)], [],
                       tile=256, name="norm_mix")
    p = _mm(h, lw["w_in"], name="proj_in")
    sv["h"], sv["p"] = h, p
    u = _conv_fwd(p, lw["conv_w8"], lw["conv_b"], col0=lay.o_xbc, chans=lay.xbc, batch=batch, seq=seq, name="conv_fwd")
    sv["u"] = u
    pdt = p[:, lay.o_dt:lay.o_dt + lay.ndt]
    ys, sv["st"], sv["dtl"] = [], [], []
    for dirn in range(2):
        dtr, dtr_t = _dt_layouts(pdt, dirn, batch * seq)
        bias, bias_t = _par_layouts(lw["dt_bias"][dirn])
        alog, alog_t = _par_layouts(lw["a_log"][dirn])
        y, st = _scan_fwd(u, dtr, dtr_t, bias, bias_t, alog, alog_t, batch=batch, seq=seq, inner=lay.inner,
                          reverse=bool(dirn), name="scan_fwd%d" % dirn)
        ys.append(y)
        sv["st"].append(st)
        sv["dtl"].append((dtr, dtr_t, bias, bias_t, alog, alog_t))
    sv["ys"] = ys
    inner = lay.inner
    (ya,), _ = _rowcall(lambda rv, pv: ([_ssd_post_f(*rv, *pv)], []),
                        [(ys[0], inner, 0), (ys[1], inner, 0), (u, inner, 0), (p, inner, lay.o_z // inner)],
                        [lw["d_skip_x"], lw["ssd_norm"]], [(inner, BF16)], [], tile=128, name="ssd_post")
    sv["ya"] = ya
    rqd = _rope(p, tabs, col0=lay.o_rqd, width=lay.rqd, seq=seq, group=0, inverse=False, out_dtype=BF16, name="rope_dil")
    rqw = _rope(p, tabs, col0=lay.o_rqw, width=lay.rqw, seq=seq, group=lay.rep + 2, inverse=False, out_dtype=BF16,
                name="rope_win")
    sv["rqd"], sv["rqw"] = rqd, rqw
    os_, ls_ = [], []
    for gi, (window, dil) in enumerate(DIL_PATTERNS):
        o, l = _attn_fwd(rqd, None, batch=batch, seq=seq, dil=dil, nbw=lay.ngrp * DIL_HEADS, cb0=gi * DIL_HEADS,
                         nh=DIL_HEADS, rep=1, hw=window // (2 * dil), want_lse=True, out_dtype=F32,
                         name="dil_fwd%d" % gi)
        os_.append(o)
        ls_.append(l)
    sv["os"], sv["ls"] = os_, ls_
    dw = lay.dilw
    (yb,), _ = _rowcall(lambda rv, pv: ([_combine_f(*rv)], []), [(a, dw, 0) for a in os_ + ls_], [], [(dw, BF16)], [],
                        tile=256, name="dil_combine")
    sv["yb"] = yb
    (yc,) = _attn_fwd(rqw, lw["sink_x"], batch=batch, seq=seq, dil=1, nbw=WIN_KV_HEADS, cb0=0, nh=WIN_KV_HEADS,
                      rep=lay.rep, hw=WIN_HALF, want_lse=False, out_dtype=BF16, name="win_fwd")
    sv["yc"] = yc
    ma = _mm(ya, lw["w_a"], name="proj_a")
    mb = _mm(yb, lw["w_b"], name="proj_b")
    mc = _mm(yc, lw["w_c"], name="proj_c")
    sv["mabc"] = (ma, mb, mc)
    (mg,), _ = _rowcall(lambda rv, pv: ([_merge_f(*rv)], []),
                        [(ma, d, 0), (mb, d, 0), (mc, d, 0), (p, d, 0), (p, d, 1), (p, d, 2)], [], [(d, BF16)], [],
                        tile=256, name="merge")
    sv["mg"] = mg
    x1 = _mm(mg, lw["w_out"], add=x, name="proj_out")
    sv["x1"] = x1
    (hm,), _ = _rowcall(lambda rv, pv: ([_rms(rv[0], pv[0])], []), [(x1, d, 0)], [lw["g_mlp"]], [(d, BF16)], [],
                        tile=256, name="norm_mlp")
    up = _mm(hm, lw["w_up"], name="mlp_up")
    act = _map2d(lambda a: jnp.square(jnp.maximum(a, 0.0)), [up], BF16, name="relu2")
    sv["hm"], sv["up"], sv["act"] = hm, up, act
    x2 = _mm(act, lw["w_down"], add=x1, name="mlp_down")
    return x2, sv


def _layer_bwd(dxo, sv, lw, lay, tabs, batch, seq):
    d = lay.d
    inner = lay.inner
    gw, gs = {}, {}
    dact = _mm(dxo, lw["w_down"], tb=True, name="mlp_down_dx")
    gw["w_down"] = _mm(sv["act"], dxo, ta=True, name="mlp_down_dw")
    dup = _map2d(lambda g, a: g * (2.0 * jnp.maximum(a, 0.0)), [dact, sv["up"]], BF16, name="relu2_bwd")
    dhm = _mm(dup, lw["w_up"], tb=True, name="mlp_up_dx")
    gw["w_up"] = _mm(sv["hm"], dup, ta=True, name="mlp_up_dw")

    def norm_bwd(rv, pv):
        xv, dh, dres = rv
        _, vjp = jax.vjp(_rms, xv, pv[0])
        dx, dg = vjp(dh)
        return [dx + dres], [dg]

    (dx1,), (gs["g_mlp"],) = _rowcall(norm_bwd, [(sv["x1"], d, 0), (dhm, d, 0), (dxo, d, 0)], [lw["g_mlp"]],
                                      [(d, F32)], [(1, d)], tile=128, name="norm_mlp_bwd")
    dmg = _mm(dx1, lw["w_out"], tb=True, name="proj_out_dx")
    gw["w_out"] = _mm(sv["mg"], dx1, ta=True, name="proj_out_dw")
    ma, mb, mc = sv["mabc"]
    p = sv["p"]

    def merge_bwd(rv, pv):
        _, vjp = jax.vjp(_merge_f, *rv[:6])
        da, db, dc, d0, d1, d2 = vjp(rv[6])
        return [da, db, dc, jnp.concatenate([d0, d1, d2], axis=1)], []

    (dma, dmb, dmc, dgl), _ = _rowcall(
        merge_bwd, [(ma, d, 0), (mb, d, 0), (mc, d, 0), (p, d, 0), (p, d, 1), (p, d, 2), (dmg, d, 0)], [],
        [(d, BF16), (d, BF16), (d, BF16), (lay.gates, BF16)], [], tile=128, name="merge_bwd")
    dya = _mm(dma, lw["w_a"], tb=True, name="proj_a_dx")
    gw["w_a"] = _mm(sv["ya"], dma, ta=True, name="proj_a_dw")
    dyb = _mm(dmb, lw["w_b"], tb=True, name="proj_b_dx")
    gw["w_b"] = _mm(sv["yb"], dmb, ta=True, name="proj_b_dw")
    dyc = _mm(dmc, lw["w_c"], tb=True, name="proj_c_dx")
    gw["w_c"] = _mm(sv["yc"], dmc, ta=True, name="proj_c_dw")
    drqw, dsink = _attn_bwd(sv["rqw"], lw["sink_x"], dyc, None, batch=batch, seq=seq, dil=1, nbw=WIN_KV_HEADS, cb0=0,
                            nh=WIN_KV_HEADS, rep=lay.rep, hw=WIN_HALF, name="win_bwd")
    gs["sink"] = jnp.sum(dsink, axis=1)
    dw = lay.dilw

    def combine_bwd(rv, pv):
        _, vjp = jax.vjp(_combine_f, *rv[:6])
        return list(vjp(rv[6])), []

    dol, _ = _rowcall(combine_bwd, [(a, dw, 0) for a in sv["os"] + sv["ls"]] + [(dyb, dw, 0)], [],
                      [(dw, F32)] * 6, [], tile=256, name="dil_combine_bwd")
    drq = []
    for gi, (window, dil) in enumerate(DIL_PATTERNS):
        (dg_,) = _attn_bwd(sv["rqd"], None, dol[gi], dol[3 + gi], batch=batch, seq=seq, dil=dil,
                           nbw=lay.ngrp * DIL_HEADS, cb0=gi * DIL_HEADS, nh=DIL_HEADS, rep=1,
                           hw=window // (2 * dil), name="dil_bwd%d" % gi)
        drq.append(_rope(dg_, tabs, col0=0, width=dg_.shape[1], seq=seq, group=0, inverse=True, out_dtype=BF16,
                         name="rope_dil_bwd%d" % gi))
    drq.append(_rope(drqw, tabs, col0=0, width=lay.rqw, seq=seq, group=lay.rep + 2, inverse=True, out_dtype=BF16,
                     name="rope_win_bwd"))
    u, ys = sv["u"], sv["ys"]

    def post_bwd(rv, pv):
        _, vjp = jax.vjp(_ssd_post_f, *rv[:4], *pv)
        dyf, _, dxs, dz, dsk, dgn = vjp(rv[4])
        return [dyf, dxs, dz], [dsk, dgn]

    (dy, dxs_post, dz), (dsk, gs["ssd_norm"]) = _rowcall(
        post_bwd, [(ys[0], inner, 0), (ys[1], inner, 0), (u, inner, 0), (p, inner, lay.o_z // inner), (dya, inner, 0)],
        [lw["d_skip_x"], lw["ssd_norm"]], [(inner, F32), (inner, F32), (inner, BF16)], [(1, inner), (1, inner)],
        tile=128, name="ssd_post_bwd")
    gs["d_skip"] = jnp.sum(dsk.reshape(SSD_HEADS, SSD_HEAD_DIM), axis=1)
    dxs, dbm, dcm = dxs_post, None, None
    ddt, gdb, gda = [], [], []
    for dirn in range(2):
        dtr, dtr_t, bias, bias_t, alog, alog_t = sv["dtl"][dirn]
        r = _scan_bwd(u, dtr, dtr_t, bias, bias_t, alog, alog_t, sv["st"][dirn], dy, batch=batch, seq=seq,
                      inner=inner, reverse=bool(dirn), name="scan_bwd%d" % dirn)
        dxs = dxs + r[0]
        dbm = r[1] if dbm is None else dbm + r[1]
        dcm = r[2] if dcm is None else dcm + r[2]
        ddt.append((r[3] + r[4].transpose(0, 2, 1)).transpose(1, 0, 2).reshape(batch * seq, SSD_HEADS))
        gdb.append((r[5][:, 0, :] + r[6][:, :, 0]).reshape(SSD_HEADS))
        gda.append((r[7][:, 0, :] + r[8][:, :, 0]).reshape(SSD_HEADS))
    gs["dt_bias"] = jnp.stack(gdb)
    gs["a_log"] = jnp.stack(gda)
    du = jnp.concatenate([dxs, dbm, dcm], axis=1)
    dxbc, dcw, dcb = _conv_bwd(p, lw["conv_w8"], lw["conv_b"], du, col0=lay.o_xbc, chans=lay.xbc, batch=batch,
                               seq=seq, name="conv_bwd")
    gs["conv_w"] = dcw[:CONV_WIDTH]
    gs["conv_b"] = dcb[0]
    ddtp = jnp.concatenate(ddt + [jnp.zeros((batch * seq, lay.dtw - lay.ndt), F32)], axis=1).astype(BF16)
    dp = jnp.concatenate([dgl, dz, dxbc] + drq + [ddtp], axis=1)
    dh = _mm(dp, lw["w_in"], tb=True, name="proj_in_dx")
    gw["w_in"] = _mm(sv["h"], dp, ta=True, name="proj_in_dw")
    (dx,), (gs["g_mix"],) = _rowcall(norm_bwd, [(sv["x"], d, 0), (dh, d, 0), (dx1, d, 0)], [lw["g_mix"]],
                                     [(d, F32)], [(1, d)], tile=128, name="norm_mix_bwd")
    return dx, gw, gs


_SHARDED = ("w_in", "w_a", "w_b", "w_c", "w_out", "w_up", "w_down")
_COL_SHARDED = ("w_in", "w_b", "w_up")
_SMALL = ("g_mix", "conv_b", "dt_bias", "a_log", "d_skip", "ssd_norm", "sink", "g_mlp")


def _gathered_to_full(name, g):
    n, nl, r, c = g.shape
    if name in _COL_SHARDED:
        return g.transpose(1, 2, 0, 3).reshape(nl, r, n * c)
    return g.transpose(1, 0, 2, 3).reshape(nl, n * r, c)


def _full_to_slots(name, w):
    nl, r, c = w.shape
    if name in _COL_SHARDED:
        return w.reshape(nl, r, N_DEV, c // N_DEV).transpose(2, 0, 1, 3)
    return w.reshape(nl, N_DEV, r // N_DEV, c).transpose(1, 0, 2, 3)


def _pack(parts):
    flat = jnp.concatenate([p.reshape(-1).astype(F32) for p in parts])
    n = flat.shape[0]
    rows = -(-n // (8 * LANE)) * 8
    return jnp.pad(flat, (0, rows * LANE - n)).reshape(rows, LANE)


def _unpack(buf, shapes):
    flat = buf.reshape(-1)
    out, off = [], 0
    for s in shapes:
        n = math.prod(s)
        out.append(flat[off:off + n].reshape(s))
        off += n
    return out


def kernel(x, g_mix, w_in, conv_w, conv_b, dt_bias, a_log, d_skip, ssd_norm, w_a, w_b, w_c, sink, w_out, g_mlp, w_up, w_down, g_final, loss_target, m_g_mix, m_w_in, m_conv_w, m_conv_b, m_dt_bias, m_a_log, m_d_skip, m_ssd_norm, m_w_a, m_w_b, m_w_c, m_sink, m_w_out, m_g_mlp, m_w_up, m_w_down, m_g_final, v_g_mix, v_w_in, v_conv_w, v_conv_b, v_dt_bias, v_a_log, v_d_skip, v_ssd_norm, v_w_a, v_w_b, v_w_c, v_sink, v_w_out, v_g_mlp, v_w_up, v_w_down, v_g_final):
    batch, seq, d = x.shape
    depth = g_mix.shape[0]
    lay = _Layout(d)
    assert lay.n_in == w_in.shape[2] * N_DEV
    wts = dict(g_mix=g_mix, w_in=w_in, conv_w=conv_w, conv_b=conv_b, dt_bias=dt_bias, a_log=a_log, d_skip=d_skip,
               ssd_norm=ssd_norm, w_a=w_a, w_b=w_b, w_c=w_c, sink=sink, w_out=w_out, g_mlp=g_mlp, w_up=w_up,
               w_down=w_down, g_final=g_final)
    mom = dict(g_mix=m_g_mix, w_in=m_w_in, conv_w=m_conv_w, conv_b=m_conv_b, dt_bias=m_dt_bias, a_log=m_a_log,
               d_skip=m_d_skip, ssd_norm=m_ssd_norm, w_a=m_w_a, w_b=m_w_b, w_c=m_w_c, sink=m_sink, w_out=m_w_out,
               g_mlp=m_g_mlp, w_up=m_w_up, w_down=m_w_down, g_final=m_g_final)
    var = dict(g_mix=v_g_mix, w_in=v_w_in, conv_w=v_conv_w, conv_b=v_conv_b, dt_bias=v_dt_bias, a_log=v_a_log,
               d_skip=v_d_skip, ssd_norm=v_ssd_norm, w_a=v_w_a, w_b=v_w_b, w_c=v_w_c, sink=v_sink, w_out=v_w_out,
               g_mlp=v_g_mlp, w_up=v_w_up, w_down=v_w_down, g_final=v_g_final)
    me = 4 * lax.axis_index("x") + 2 * lax.axis_index("y") + lax.axis_index("c")

    names = list(_SHARDED) + ["conv_w"]
    gathered = _exchange([wts[n].astype(F32 if n == "conv_w" else BF16) for n in names], scatter=False,
                         name="gather_weights")
    full = {n: _gathered_to_full(n, g) if n != "conv_w" else g.transpose(1, 2, 0, 3).reshape(depth, CONV_WIDTH, -1)
            for n, g in zip(names, gathered)}

    tabs = _rope_tables(seq)
    t = batch * seq
    xf = x.reshape(t, d)
    layers = []
    for l in range(depth):
        lw = dict(
            g_mix=g_mix[l][None], g_mlp=g_mlp[l][None], ssd_norm=ssd_norm[l][None], conv_b=conv_b[l][None],
            dt_bias=dt_bias[l], a_log=a_log[l],
            d_skip_x=jnp.repeat(d_skip[l], SSD_HEAD_DIM)[None],
            sink_x=jnp.broadcast_to(sink[l][:, None], (WIN_Q_HEADS, LANE)),
            conv_w8=jnp.pad(full["conv_w"][l], ((0, 8 - CONV_WIDTH), (0, 0))),
            w_in=lay.permute_w(full["w_in"][l]), w_a=full["w_a"][l], w_b=full["w_b"][l], w_c=full["w_c"][l],
            w_out=full["w_out"][l], w_up=full["w_up"][l], w_down=full["w_down"][l])
        layers.append(lw)

    saves = []
    h = xf
    for l in range(depth):
        h, sv = _layer_fwd(h, layers[l], lay, tabs, batch, seq)
        saves.append(sv)
    dx, dgf, loss = _final_loss(h, g_final[None], loss_target.reshape(t, d), name="final_loss")

    gws, gss = [None] * depth, [None] * depth
    for l in reversed(range(depth)):
        dx, gws[l], gss[l] = _layer_bwd(dx, saves[l], layers[l], lay, tabs, batch, seq)
    grad_x = dx.reshape(batch, seq, d)

    slots = []
    for n in _SHARDED:
        gl = [lay.unpermute_w(gws[l][n]) if n == "w_in" else gws[l][n] for l in range(depth)]
        slots.append(_full_to_slots(n, jnp.stack(gl)).astype(BF16))
    recvs = _exchange(slots, scatter=True, name="scatter_grads")

    small_parts = [jnp.stack([gss[l][n] for l in range(depth)]) for n in _SMALL]
    small_parts += [dgf, jnp.stack([gss[l]["conv_w"] for l in range(depth)]), loss[0, :1]]
    small_shapes = [p.shape for p in small_parts]
    (rs,) = _exchange([_pack(small_parts)], scatter=False, name="gather_small")
    red = _unpack(_sum_slots(rs, name="sum_small"), small_shapes)
    gsmall = dict(zip(list(_SMALL) + ["g_final"], red[:len(_SMALL) + 1]))
    gconv_full, loss_sum = red[-2], red[-1]
    cshard = conv_w.shape[2]
    gsmall["conv_w"] = lax.dynamic_slice_in_dim(gconv_full, me * cshard, cshard, axis=2)

    out = {}
    for n, rv in zip(_SHARDED, recvs):
        shp = wts[n].shape
        r2 = (shp[0] * shp[1], shp[2])
        res = _adamw(wts[n].reshape(r2), rv.reshape((N_DEV,) + r2), mom[n].reshape(r2), var[n].reshape(r2),
                     name="adamw_" + n)
        out[n] = [a.reshape(shp) for a in res]
    rep_names = list(_SMALL) + ["g_final"]
    rep_shapes = [wts[n].shape for n in rep_names]
    res = _adamw(_pack([wts[n] for n in rep_names]), _pack([gsmall[n] for n in rep_names])[None],
                 _pack([mom[n] for n in rep_names]), _pack([var[n] for n in rep_names]), name="adamw_small")
    unp = [_unpack(a, rep_shapes) for a in res]
    for i, n in enumerate(rep_names):
        out[n] = [unp[k][i] for k in range(4)]
    cs2 = (depth * CONV_WIDTH, cshard)
    res = _adamw(conv_w.reshape(cs2), gsmall["conv_w"].reshape((1,) + cs2), m_conv_w.reshape(cs2),
                 v_conv_w.reshape(cs2), name="adamw_conv_w")
    out["conv_w"] = [a.reshape(conv_w.shape) for a in res]

    order = ["g_mix", "w_in", "conv_w", "conv_b", "dt_bias", "a_log", "d_skip", "ssd_norm", "w_a", "w_b", "w_c",
             "sink", "w_out", "g_mlp", "w_up", "w_down", "g_final"]
    outs = [loss_sum.reshape(()), grad_x]
    for k in range(4):
        outs += [out[n][k] for n in order]
    return tuple(outs)
```

```python
import functools
import math

import jax
import jax.numpy as jnp
from jax import lax
from jax.experimental import pallas as pl
from jax.experimental.pallas import tpu as pltpu

F32 = jnp.float32
BF16 = jnp.bfloat16
HI = lax.Precision.HIGHEST
MESH = pl.DeviceIdType.MESH
N_DEV = 8

SSD_HEADS = 32
SSD_HEAD_DIM = 64
SSD_GROUPS = 8
SSD_STATE = 128
SSD_CHUNK = 128
CONV_WIDTH = 5
HEAD_DIM = 128
ROPE_DIM = 32
ROPE_THETA = 500000.0
DIL_PATTERNS = ((128, 1), (512, 4), (2048, 16))
DIL_HEADS = 8
WIN_Q_HEADS = 16
WIN_KV_HEADS = 4
WIN_HALF = 128
N_BRANCH = 3
EPS = 1e-6
NEG_INF = -1e30
ADAM_LR = 0.001
ADAM_B1 = 0.9
ADAM_B2 = 0.999
ADAM_EPS = 1e-08
ADAM_WD = 0.01
ADAM_STEP = 10

LANE = 128
QBLK = 128
VMEM_LIMIT = 56 * 1024 * 1024
PAD_TO = 512
MM_VMEM_BUDGET = 40 * 1024 * 1024


def _cparams(sem=None):
    return pltpu.CompilerParams(dimension_semantics=sem, vmem_limit_bytes=VMEM_LIMIT)


def _pick(dim, cands):
    for c in cands:
        if dim % c == 0:
            return c
    return dim


def _mm_tiles(m, n, k, a_bytes, b_bytes, o_bytes, has_add):
    tm = _pick(m, (1024, 512, 256, 128))
    tn = _pick(n, (1024, 1792, 512, 256, 128))
    for tk in (2048, 1792, 1024, 896, 512, 256, 128):
        if k % tk:
            continue
        need = 2 * (tm * tk * a_bytes + tk * tn * b_bytes + tm * tn * (o_bytes + (4 if has_add else 0)))
        need += tm * tn * 4 if k // tk > 1 else 0
        if need <= MM_VMEM_BUDGET:
            return tm, tn, tk
    return tm, tn, _pick(k, (128,))


def _mm(a, b, *, ta=False, tb=False, out_dtype=F32, add=None, name):
    m, k = (a.shape[1], a.shape[0]) if ta else a.shape
    k2, n = (b.shape[1], b.shape[0]) if tb else b.shape
    assert k == k2, (a.shape, b.shape, ta, tb)
    tm, tn, tk = _mm_tiles(m, n, k, a.dtype.itemsize, b.dtype.itemsize, jnp.dtype(out_dtype).itemsize,
                           add is not None)
    assert m % tm == 0 and n % tn == 0 and k % tk == 0, (m, n, k, tm, tn, tk)
    nk = k // tk
    a_spec = pl.BlockSpec((tk, tm), lambda i, j, kk: (kk, i)) if ta else pl.BlockSpec((tm, tk), lambda i, j, kk: (i, kk))
    b_spec = pl.BlockSpec((tn, tk), lambda i, j, kk: (j, kk)) if tb else pl.BlockSpec((tk, tn), lambda i, j, kk: (kk, j))
    o_spec = pl.BlockSpec((tm, tn), lambda i, j, kk: (i, j))
    dims = (((0 if ta else 1,), (1 if tb else 0,)), ((), ()))
    has_add = add is not None

    def body(*refs):
        a_ref, b_ref = refs[:2]
        add_ref = refs[2] if has_add else None
        o_ref = refs[3] if has_add else refs[2]
        part = lax.dot_general(a_ref[...].astype(BF16), b_ref[...].astype(BF16), dims, preferred_element_type=F32)
        if nk == 1:
            if has_add:
                part = part + add_ref[...]
            o_ref[...] = part.astype(o_ref.dtype)
            return
        acc_ref = refs[-1]
        kk = pl.program_id(2)

        @pl.when(kk == 0)
        def _():
            acc_ref[...] = part

        @pl.when(kk > 0)
        def _():
            acc_ref[...] += part

        @pl.when(kk == nk - 1)
        def _():
            r = acc_ref[...]
            if has_add:
                r = r + add_ref[...]
            o_ref[...] = r.astype(o_ref.dtype)

    ins = [a, b] + ([add] if has_add else [])
    specs = [a_spec, b_spec] + ([o_spec] if has_add else [])
    return pl.pallas_call(
        body, name=name, grid=(m // tm, n // tn, nk), in_specs=specs, out_specs=o_spec,
        out_shape=jax.ShapeDtypeStruct((m, n), out_dtype),
        scratch_shapes=[pltpu.VMEM((tm, tn), F32)] if nk > 1 else [],
        compiler_params=_cparams(("parallel", "parallel", "arbitrary")),
    )(*ins)


def _rowcall(fn, rows, pars, row_outs, par_outs, *, tile, name):
    t = rows[0][0].shape[0]
    tile = min(tile, t)
    assert t % tile == 0
    nr, npar, nro, npo = len(rows), len(pars), len(row_outs), len(par_outs)
    in_specs = [pl.BlockSpec((tile, c), functools.partial(lambda i, cb: (i, cb), cb=cb)) for (_, c, cb) in rows]
    in_specs += [pl.BlockSpec(p.shape, lambda i: (0, 0)) for p in pars]
    out_specs = [pl.BlockSpec((tile, c), lambda i: (i, 0)) for (c, _) in row_outs]
    out_specs += [pl.BlockSpec(s, lambda i: (0, 0)) for s in par_outs]
    out_shape = [jax.ShapeDtypeStruct((t, c), d) for (c, d) in row_outs]
    out_shape += [jax.ShapeDtypeStruct(s, F32) for s in par_outs]

    def body(*refs):
        rv = [r[...] for r in refs[:nr]]
        pv = [r[...] for r in refs[nr:nr + npar]]
        ro_refs = refs[nr + npar:nr + npar + nro]
        po_refs = refs[nr + npar + nro:]
        ro, po = fn(rv, pv)
        for ref, v in zip(ro_refs, ro):
            ref[...] = v.astype(ref.dtype)
        if npo:
            @pl.when(pl.program_id(0) == 0)
            def _():
                for ref in po_refs:
                    ref[...] = jnp.zeros_like(ref)
            for ref, v in zip(po_refs, po):
                ref[...] += v

    res = pl.pallas_call(
        body, name=name, grid=(t // tile,), in_specs=in_specs, out_specs=out_specs, out_shape=out_shape,
        compiler_params=_cparams(("arbitrary",)),
    )(*[r[0] for r in rows], *pars)
    return list(res[:nro]), list(res[nro:])


def _map2d(fn, ins, out_dtype, *, name, tile=256, cw=2048):
    t, w = ins[0].shape
    tile, cw = min(tile, t), min(cw, w)
    assert t % tile == 0 and w % cw == 0

    def body(*refs):
        refs[-1][...] = fn(*[r[...] for r in refs[:-1]]).astype(out_dtype)

    spec = pl.BlockSpec((tile, cw), lambda i, j: (i, j))
    return pl.pallas_call(
        body, name=name, grid=(t // tile, w // cw), in_specs=[spec] * len(ins), out_specs=spec,
        out_shape=jax.ShapeDtypeStruct((t, w), out_dtype), compiler_params=_cparams(("parallel", "parallel")),
    )(*ins)


def _exchange(srcs, *, scatter, name):
    n = len(srcs)
    out_shape = [jax.ShapeDtypeStruct(s.shape if scatter else (N_DEV,) + s.shape, s.dtype) for s in srcs]

    def body(*refs):
        src_refs, out_refs = refs[:n], refs[n:2 * n]
        send_sems, recv_sems, loc_sems = refs[2 * n:]
        x, y, c = lax.axis_index("x"), lax.axis_index("y"), lax.axis_index("c")
        me = 4 * x + 2 * y + c
        copies = []
        for a in range(n):
            for j in range(1, N_DEV):
                px = (1 - x) if (j >> 2) & 1 else x
                py = (1 - y) if (j >> 1) & 1 else y
                pc = (1 - c) if j & 1 else c
                src = src_refs[a].at[4 * px + 2 * py + pc] if scatter else src_refs[a]
                cp = pltpu.make_async_remote_copy(
                    src_ref=src, dst_ref=out_refs[a].at[me], send_sem=send_sems.at[a * 7 + j - 1],
                    recv_sem=recv_sems.at[a * 7 + j - 1], device_id=(px, py, pc), device_id_type=MESH)
                cp.start()
                copies.append(cp)
            src = src_refs[a].at[me] if scatter else src_refs[a]
            cp = pltpu.make_async_copy(src, out_refs[a].at[me], loc_sems.at[a])
            cp.start()
            copies.append(cp)
        for cp in copies:
            cp.wait()

    anyspec = pl.BlockSpec(memory_space=pl.ANY)
    return pl.pallas_call(
        body, name=name, in_specs=[anyspec] * n, out_specs=[anyspec] * n, out_shape=out_shape,
        scratch_shapes=[pltpu.SemaphoreType.DMA((7 * n,)), pltpu.SemaphoreType.DMA((7 * n,)),
                        pltpu.SemaphoreType.DMA((n,))],
        compiler_params=pltpu.CompilerParams(has_side_effects=True),
    )(*srcs)


def _row_tile(r, c, budget_elems=256 * 1024):
    tr = r
    while tr * c > budget_elems and tr % 16 == 0:
        tr //= 2
    return tr


def _adamw(w, recv, m, v, *, name):
    r, c = w.shape
    ns = recv.shape[0]
    tr = _row_tile(r, c)
    bc1 = 1.0 / (1.0 - ADAM_B1 ** ADAM_STEP)
    bc2 = 1.0 / (1.0 - ADAM_B2 ** ADAM_STEP)

    def body(w_ref, r_ref, m_ref, v_ref, g_ref, d_ref, mo_ref, vo_ref):
        g = r_ref[0].astype(F32)
        for s in range(1, ns):
            g = g + r_ref[s].astype(F32)
        mn = ADAM_B1 * m_ref[...] + (1.0 - ADAM_B1) * g
        vn = ADAM_B2 * v_ref[...] + (1.0 - ADAM_B2) * (g * g)
        g_ref[...] = g
        mo_ref[...] = mn
        vo_ref[...] = vn
        d_ref[...] = -ADAM_LR * ((mn * bc1) / (jnp.sqrt(vn * bc2) + ADAM_EPS) + ADAM_WD * w_ref[...])

    spec = pl.BlockSpec((tr, c), lambda i: (i, 0))
    return pl.pallas_call(
        body, name=name, grid=(r // tr,),
        in_specs=[spec, pl.BlockSpec((ns, tr, c), lambda i: (0, i, 0)), spec, spec],
        out_specs=[spec] * 4, out_shape=[jax.ShapeDtypeStruct((r, c), F32)] * 4,
        compiler_params=_cparams(("parallel",)),
    )(w, recv, m, v)


def _sum_slots(recv, *, name):
    ns, r, c = recv.shape

    def body(r_ref, o_ref):
        g = r_ref[0]
        for s in range(1, ns):
            g = g + r_ref[s]
        o_ref[...] = g

    return pl.pallas_call(body, name=name, out_shape=jax.ShapeDtypeStruct((r, c), F32))(recv)


def _rms(x, g):
    return x * lax.rsqrt(jnp.mean(x * x, axis=-1, keepdims=True) + EPS) * g


def _silu(x):
    return x * jax.nn.sigmoid(x)


def _merge_f(a, b, c, g0, g1, g2):
    return jax.nn.sigmoid(g0) * a + jax.nn.sigmoid(g1) * b + jax.nn.sigmoid(g2) * c


def _ssd_post_f(yf, yb, xs, z, dskip, gnorm):
    y = (yf + yb + dskip * xs) * _silu(z)
    return _rms(y, gnorm)


def _combine_f(o0, o1, o2, l0, l1, l2):
    m = jnp.maximum(jnp.maximum(l0, l1), l2)
    e0, e1, e2 = jnp.exp(l0 - m), jnp.exp(l1 - m), jnp.exp(l2 - m)
    return (e0 * o0 + e1 * o1 + e2 * o2) / (e0 + e1 + e2)


def _attn_block(q, k3, v3, sk, i, *, seq, hw, has_sink):
    s = lax.dot_general(q.astype(BF16), k3.astype(BF16), (((1,), (1,)), ((), ())),
                        preferred_element_type=F32) * (HEAD_DIM ** -0.5)
    qpos = i * QBLK + lax.broadcasted_iota(jnp.int32, s.shape, 0)
    kpos = (i - 1) * QBLK + lax.broadcasted_iota(jnp.int32, s.shape, 1)
    valid = (jnp.abs(qpos - kpos) <= hw) & (kpos >= 0) & (kpos < seq)
    s = jnp.where(valid, s, NEG_INF)
    m = jnp.max(s, axis=-1, keepdims=True)
    if has_sink:
        m = jnp.maximum(m, sk)
    m = lax.stop_gradient(m)
    e = jnp.exp(s - m)
    l = jnp.sum(e, axis=-1, keepdims=True)
    if has_sink:
        l = l + jnp.exp(sk - m)
    o = jnp.dot(e.astype(BF16), v3.astype(BF16), preferred_element_type=F32) / l
    return o, m + jnp.log(l)


def _ssd_chunk(state, xs, bm, cm, dtr, dtr_t, bias, bias_t, alog, alog_t, *, reverse):
    t = xs.shape[0]
    hg = dtr.shape[1]
    hp = xs.shape[1]
    p = hp // hg
    dt = jax.nn.softplus(dtr + bias)
    dt_t = jax.nn.softplus(dtr_t + bias_t)
    dta = dt * (-jnp.exp(alog))
    dta_t = dt_t * (-jnp.exp(alog_t))
    li = lax.broadcasted_iota(jnp.int32, (t, t), 0)
    si = lax.broadcasted_iota(jnp.int32, (t, t), 1)
    tri = (li <= si) if reverse else (li >= si)
    trif = tri.astype(F32)
    cs = jnp.dot(trif, dta, precision=HI, preferred_element_type=F32)
    cs_t = lax.dot_general(dta_t, trif, (((1,), (1,)), ((), ())), precision=HI,
                           preferred_element_type=F32)
    total = jnp.sum(dta, axis=0, keepdims=True)
    cb = lax.dot_general(cm.astype(BF16), bm.astype(BF16), (((1,), (1,)), ((), ())),
                         preferred_element_type=F32)
    lane_h = lax.broadcasted_iota(jnp.int32, (1, hp), 1) // p
    col_h = lax.broadcasted_iota(jnp.int32, (1, hg), 1)
    row_h = lax.broadcasted_iota(jnp.int32, (hg, 1), 0)
    dt_x = jnp.zeros((t, hp), F32)
    ecs_x = jnp.zeros((t, hp), F32)
    ds_x = jnp.zeros((t, hp), F32)
    etot_x = jnp.zeros((1, hp), F32)
    decays, masks = [], []
    for h in range(hg):
        oh = (col_h == h).astype(F32)
        oh_t = (row_h == h).astype(F32)
        mk = (lane_h == h).astype(F32)
        dt_h = jnp.sum(dt * oh, axis=1, keepdims=True)
        cs_h = jnp.sum(cs * oh, axis=1, keepdims=True)
        cst_h = jnp.sum(cs_t * oh_t, axis=0, keepdims=True)
        tot_h = jnp.sum(total * oh, axis=1, keepdims=True)
        dt_x = dt_x + dt_h * mk
        ecs_x = ecs_x + jnp.exp(cs_h) * mk
        ds_x = ds_x + jnp.exp(tot_h - cs_h) * mk
        etot_x = etot_x + jnp.exp(tot_h) * mk
        decays.append(jnp.exp(jnp.where(tri, cs_h - cst_h, -jnp.inf)))
        masks.append(mk)
    xdt = xs * dt_x
    y = jnp.dot(cm.astype(BF16), state.astype(BF16), preferred_element_type=F32) * ecs_x
    for h in range(hg):
        y = y + jnp.dot((cb * decays[h]).astype(BF16), (xdt * masks[h]).astype(BF16),
                        preferred_element_type=F32)
    st_new = lax.dot_general(bm.astype(BF16), (xdt * ds_x).astype(BF16), (((0,), (0,)), ((), ())),
                             preferred_element_type=F32)
    return y, state * etot_x + st_new


def _rope_tables(seq):
    half = ROPE_DIM // 2
    inv = ROPE_THETA ** (-jnp.arange(0, ROPE_DIM, 2, dtype=F32) / ROPE_DIM)
    ang = jnp.arange(seq, dtype=F32)[:, None] * inv[None, :]
    cos, sin = jnp.cos(ang), jnp.sin(ang)
    rest = HEAD_DIM - ROPE_DIM
    c = jnp.concatenate([cos, cos, jnp.ones((seq, rest), F32)], axis=1)
    a = jnp.concatenate([-sin, jnp.zeros((seq, HEAD_DIM - half), F32)], axis=1)
    b = jnp.concatenate([jnp.zeros((seq, half), F32), sin, jnp.zeros((seq, rest), F32)], axis=1)
    return c, a, b


def _rope(src, tabs, *, col0, width, seq, group, inverse, out_dtype, name):
    t = src.shape[0]
    half = ROPE_DIM // 2
    nhb = 6 if (width % (6 * HEAD_DIM) == 0 and col0 % (6 * HEAD_DIM) == 0) else 3
    cw, tile = nhb * HEAD_DIM, 512
    assert width % cw == 0 and col0 % cw == 0 and seq % tile == 0 and t % tile == 0
    ns = seq // tile

    def body(x_ref, c_ref, a_ref, b_ref, o_ref):
        jb = pl.program_id(1)
        c, a, b = c_ref[...], a_ref[...], b_ref[...]
        for hh in range(nhb):
            xv = x_ref[:, hh * HEAD_DIM:(hh + 1) * HEAD_DIM].astype(F32)
            if inverse:
                yv = xv * c + pltpu.roll(xv * a, half, 1) + pltpu.roll(xv * b, HEAD_DIM - half, 1)
            else:
                yv = xv * c + pltpu.roll(xv, HEAD_DIM - half, 1) * a + pltpu.roll(xv, half, 1) * b
            if group:
                keep = ((jb * nhb + hh) % group) == (group - 1)
                yv = jnp.where(keep, xv, yv)
            o_ref[:, hh * HEAD_DIM:(hh + 1) * HEAD_DIM] = yv.astype(o_ref.dtype)

    tspec = pl.BlockSpec((tile, HEAD_DIM), lambda i, j: (i % ns, 0))
    return pl.pallas_call(
        body, name=name, grid=(t // tile, width // cw),
        in_specs=[pl.BlockSpec((tile, cw), lambda i, j: (i, col0 // cw + j)), tspec, tspec, tspec],
        out_specs=pl.BlockSpec((tile, cw), lambda i, j: (i, j)),
        out_shape=jax.ShapeDtypeStruct((t, width), out_dtype),
        compiler_params=_cparams(("parallel", "parallel")),
    )(src, *tabs)


def _shift_rows(x, d, tpos):
    if d == 0:
        return x
    s = x.shape[0]
    y = pltpu.roll(x, (-d) % s, 0)
    ok = (tpos + d >= 0) & (tpos + d < s)
    return jnp.where(ok, y, 0.0)


def _conv_fwd(p, w8, bias, *, col0, chans, batch, seq, name):
    cb = 256
    assert chans % cb == 0 and col0 % cb == 0
    pad = (CONV_WIDTH - 1) // 2

    def body(x_ref, w_ref, b_ref, o_ref):
        x = x_ref[...]
        tpos = lax.broadcasted_iota(jnp.int32, x.shape, 0)
        acc = jnp.broadcast_to(b_ref[...], x.shape)
        for k in range(CONV_WIDTH):
            acc = acc + w_ref[k:k + 1, :] * _shift_rows(x, k - pad, tpos)
        o_ref[...] = _silu(acc)

    return pl.pallas_call(
        body, name=name, grid=(chans // cb, batch),
        in_specs=[pl.BlockSpec((seq, cb), lambda j, b: (b, col0 // cb + j)),
                  pl.BlockSpec((8, cb), lambda j, b: (0, j)), pl.BlockSpec((1, cb), lambda j, b: (0, j))],
        out_specs=pl.BlockSpec((seq, cb), lambda j, b: (b, j)),
        out_shape=jax.ShapeDtypeStruct((batch * seq, chans), F32),
        compiler_params=_cparams(("parallel", "arbitrary")),
    )(p, w8, bias)


def _conv_bwd(p, w8, bias, du, *, col0, chans, batch, seq, name):
    cb = 256
    pad = (CONV_WIDTH - 1) // 2

    def body(x_ref, w_ref, b_ref, du_ref, dx_ref, dw_ref, db_ref):
        x = x_ref[...]
        tpos = lax.broadcasted_iota(jnp.int32, x.shape, 0)
        acc = jnp.broadcast_to(b_ref[...], x.shape)
        xs = []
        for k in range(CONV_WIDTH):
            xs.append(_shift_rows(x, k - pad, tpos))
            acc = acc + w_ref[k:k + 1, :] * xs[k]
        sg = jax.nn.sigmoid(acc)
        dacc = du_ref[...] * (sg * (1.0 + acc * (1.0 - sg)))
        dx = jnp.zeros_like(x)
        for k in range(CONV_WIDTH):
            dx = dx + w_ref[k:k + 1, :] * _shift_rows(dacc, pad - k, tpos)
        dx_ref[...] = dx.astype(dx_ref.dtype)

        @pl.when(pl.program_id(1) == 0)
        def _():
            dw_ref[...] = jnp.zeros_like(dw_ref)
            db_ref[...] = jnp.zeros_like(db_ref)

        for k in range(CONV_WIDTH):
            dw_ref[k:k + 1, :] += jnp.sum(dacc * xs[k], axis=0, keepdims=True)
        db_ref[...] += jnp.sum(dacc, axis=0, keepdims=True)

    return pl.pallas_call(
        body, name=name, grid=(chans // cb, batch),
        in_specs=[pl.BlockSpec((seq, cb), lambda j, b: (b, col0 // cb + j)),
                  pl.BlockSpec((8, cb), lambda j, b: (0, j)), pl.BlockSpec((1, cb), lambda j, b: (0, j)),
                  pl.BlockSpec((seq, cb), lambda j, b: (b, j))],
        out_specs=[pl.BlockSpec((seq, cb), lambda j, b: (b, j)), pl.BlockSpec((8, cb), lambda j, b: (0, j)),
                   pl.BlockSpec((1, cb), lambda j, b: (0, j))],
        out_shape=[jax.ShapeDtypeStruct((batch * seq, chans), BF16), jax.ShapeDtypeStruct((8, chans), F32),
                   jax.ShapeDtypeStruct((1, chans), F32)],
        compiler_params=_cparams(("parallel", "arbitrary")),
    )(p, w8, bias, du)


def _scan_specs(batch, nc, groups, hg, inner, reverse_order):
    t, n, hp = SSD_CHUNK, SSD_STATE, hg * SSD_HEAD_DIM
    xcb, ncb = inner // hp, inner // n

    def row(b, c):
        return b * nc + ((nc - 1 - c) if reverse_order else c)

    return dict(
        xs=pl.BlockSpec((t, hp), lambda g, b, c: (row(b, c), g)),
        bm=pl.BlockSpec((t, n), lambda g, b, c: (row(b, c), ncb + g)),
        cm=pl.BlockSpec((t, n), lambda g, b, c: (row(b, c), ncb + groups + g)),
        dtr=pl.BlockSpec((None, t, hg), lambda g, b, c: (g, row(b, c), 0)),
        dtr_t=pl.BlockSpec((None, hg, t), lambda g, b, c: (g, 0, row(b, c))),
        par=pl.BlockSpec((None, 1, hg), lambda g, b, c: (g, 0, 0)),
        par_t=pl.BlockSpec((None, hg, 1), lambda g, b, c: (g, 0, 0)),
        y=pl.BlockSpec((t, hp), lambda g, b, c: (row(b, c), g)),
        nrow=pl.BlockSpec((t, n), lambda g, b, c: (row(b, c), g)),
        st=pl.BlockSpec((None, None, n, hp), lambda g, b, c: (g, row(b, c), 0, 0)),
    )


def _scan_fwd(u, dtr, dtr_t, bias, bias_t, alog, alog_t, *, batch, seq, inner, reverse, name):
    groups, hg = dtr.shape[0], dtr.shape[2]
    nc = seq // SSD_CHUNK
    hp = hg * SSD_HEAD_DIM
    sp = _scan_specs(batch, nc, groups, hg, inner, reverse)

    def body(xs_ref, bm_ref, cm_ref, dtr_ref, dtrt_ref, b_ref, bt_ref, a_ref, at_ref, y_ref, st_ref, state):
        @pl.when(pl.program_id(2) == 0)
        def _():
            state[...] = jnp.zeros_like(state)

        st_in = state[...]
        st_ref[...] = st_in
        y, st_out = _ssd_chunk(st_in, xs_ref[...], bm_ref[...], cm_ref[...], dtr_ref[...], dtrt_ref[...],
                               b_ref[...], bt_ref[...], a_ref[...], at_ref[...], reverse=reverse)
        y_ref[...] = y
        state[...] = st_out

    return pl.pallas_call(
        body, name=name, grid=(groups, batch, nc),
        in_specs=[sp["xs"], sp["bm"], sp["cm"], sp["dtr"], sp["dtr_t"], sp["par"], sp["par_t"], sp["par"], sp["par_t"]],
        out_specs=[sp["y"], sp["st"]],
        out_shape=[jax.ShapeDtypeStruct((batch * seq, inner), F32),
                   jax.ShapeDtypeStruct((groups, batch * nc, SSD_STATE, hp), F32)],
        scratch_shapes=[pltpu.VMEM((SSD_STATE, hp), F32)],
        compiler_params=_cparams(("parallel", "arbitrary", "arbitrary")),
    )(u, u, u, dtr, dtr_t, bias, bias_t, alog, alog_t)


def _scan_bwd(u, dtr, dtr_t, bias, bias_t, alog, alog_t, st, dy, *, batch, seq, inner, reverse, name):
    groups, hg = dtr.shape[0], dtr.shape[2]
    nc = seq // SSD_CHUNK
    hp = hg * SSD_HEAD_DIM
    t = batch * seq
    sp = _scan_specs(batch, nc, groups, hg, inner, not reverse)
    f = functools.partial(_ssd_chunk, reverse=reverse)

    def body(xs_ref, bm_ref, cm_ref, dtr_ref, dtrt_ref, b_ref, bt_ref, a_ref, at_ref, st_ref, dy_ref,
             dxs_ref, dbm_ref, dcm_ref, ddtr_ref, ddtrt_ref, db_ref, dbt_ref, da_ref, dat_ref, dstate):
        first = (pl.program_id(1) == 0) & (pl.program_id(2) == 0)

        @pl.when(pl.program_id(2) == 0)
        def _():
            dstate[...] = jnp.zeros_like(dstate)

        @pl.when(first)
        def _():
            for r in (db_ref, dbt_ref, da_ref, dat_ref):
                r[...] = jnp.zeros_like(r)

        _, vjp = jax.vjp(f, st_ref[...], xs_ref[...], bm_ref[...], cm_ref[...], dtr_ref[...], dtrt_ref[...],
                         b_ref[...], bt_ref[...], a_ref[...], at_ref[...])
        dst, dxs, dbm, dcm, ddtr, ddtrt, db, dbt, da, dat = vjp((dy_ref[...], dstate[...]))
        dstate[...] = dst
        dxs_ref[...] = dxs
        dbm_ref[...] = dbm
        dcm_ref[...] = dcm
        ddtr_ref[...] = ddtr
        ddtrt_ref[...] = ddtrt
        db_ref[...] += db
        dbt_ref[...] += dbt
        da_ref[...] += da
        dat_ref[...] += dat

    gn = groups * SSD_STATE
    return pl.pallas_call(
        body, name=name, grid=(groups, batch, nc),
        in_specs=[sp["xs"], sp["bm"], sp["cm"], sp["dtr"], sp["dtr_t"], sp["par"], sp["par_t"], sp["par"], sp["par_t"],
                  sp["st"], sp["y"]],
        out_specs=[sp["y"], sp["nrow"], sp["nrow"], sp["dtr"], sp["dtr_t"], sp["par"], sp["par_t"], sp["par"], sp["par_t"]],
        out_shape=[jax.ShapeDtypeStruct((t, inner), F32), jax.ShapeDtypeStruct((t, gn), F32),
                   jax.ShapeDtypeStruct((t, gn), F32), jax.ShapeDtypeStruct(dtr.shape, F32),
                   jax.ShapeDtypeStruct(dtr_t.shape, F32), jax.ShapeDtypeStruct(bias.shape, F32),
                   jax.ShapeDtypeStruct(bias_t.shape, F32), jax.ShapeDtypeStruct(alog.shape, F32),
                   jax.ShapeDtypeStruct(alog_t.shape, F32)],
        scratch_shapes=[pltpu.VMEM((SSD_STATE, hp), F32)],
        compiler_params=_cparams(("parallel", "arbitrary", "arbitrary")),
    )(u, u, u, dtr, dtr_t, bias, bias_t, alog, alog_t, st, dy)


def _attn_load(ref, col, blk):
    return ref[pl.ds(pl.multiple_of(blk * QBLK, QBLK), QBLK), col * HEAD_DIM:(col + 1) * HEAD_DIM].astype(F32)


def _lane0(row):
    lane = lax.broadcasted_iota(jnp.int32, row.shape, 1)
    return jnp.sum(jnp.where(lane == 0, row, 0.0), axis=1, keepdims=True)


def _attn_fwd(rq, sinkx, *, batch, seq, dil, nbw, cb0, nh, rep, hw, want_lse, out_dtype, name):
    t, w = rq.shape
    ln = seq // dil
    nb = ln // QBLK
    bw = (rep + 2) * HEAD_DIM
    ow = nh * rep * HEAD_DIM
    has_sink = sinkx is not None
    rq3 = rq.reshape(batch, ln, dil * w)
    f = functools.partial(_attn_block, seq=ln, hw=hw, has_sink=has_sink)

    def body(*refs):
        if has_sink:
            blk_ref, sink_ref = refs[:2]
            outs = refs[2:]
        else:
            blk_ref, sink_ref = refs[0], None
            outs = refs[1:]
        o_ref = outs[0]
        lse_ref = outs[1] if want_lse else None
        g = pl.program_id(2)

        def qblock(i, carry):
            ip, inx = jnp.maximum(i - 1, 0), jnp.minimum(i + 1, nb - 1)
            k3 = jnp.concatenate([_attn_load(blk_ref, rep, ip), _attn_load(blk_ref, rep, i),
                                  _attn_load(blk_ref, rep, inx)], axis=0)
            v3 = jnp.concatenate([_attn_load(blk_ref, rep + 1, ip), _attn_load(blk_ref, rep + 1, i),
                                  _attn_load(blk_ref, rep + 1, inx)], axis=0)
            rows = pl.ds(pl.multiple_of(i * QBLK, QBLK), QBLK)
            for r in range(rep):
                sk = _lane0(sink_ref[pl.ds(g * rep + r, 1), :]) if has_sink else None
                o, lse = f(_attn_load(blk_ref, r, i), k3, v3, sk, i)
                o_ref[rows, r * HEAD_DIM:(r + 1) * HEAD_DIM] = o.astype(o_ref.dtype)
                if want_lse:
                    lse_ref[rows, r * HEAD_DIM:(r + 1) * HEAD_DIM] = jnp.broadcast_to(lse, o.shape)
            return carry

        lax.fori_loop(0, nb, qblock, 0)

    in_specs = [pl.BlockSpec((None, ln, bw), lambda b, r, h: (b, 0, r * nbw + cb0 + h))]
    ins = [rq3]
    if has_sink:
        in_specs.append(pl.BlockSpec(sinkx.shape, lambda b, r, h: (0, 0)))
        ins.append(sinkx)
    ospec = pl.BlockSpec((None, ln, rep * HEAD_DIM), lambda b, r, h: (b, 0, r * nh + h))
    out_shape = [jax.ShapeDtypeStruct((batch, ln, dil * ow), out_dtype)]
    out_specs = [ospec]
    if want_lse:
        out_shape.append(jax.ShapeDtypeStruct((batch, ln, dil * ow), F32))
        out_specs.append(ospec)
    res = pl.pallas_call(
        body, name=name, grid=(batch, dil, nh), in_specs=in_specs, out_specs=out_specs, out_shape=out_shape,
        compiler_params=_cparams(("parallel", "parallel", "parallel")),
    )(*ins)
    return [r.reshape(t, ow) for r in res]


def _attn_bwd(rq, sinkx, do, dlse, *, batch, seq, dil, nbw, cb0, nh, rep, hw, name):
    t, w = rq.shape
    ln = seq // dil
    nb = ln // QBLK
    bw = (rep + 2) * HEAD_DIM
    ow = nh * rep * HEAD_DIM
    has_sink = sinkx is not None
    has_lse = dlse is not None
    f = functools.partial(_attn_block, seq=ln, hw=hw, has_sink=has_sink)

    def body(*refs):
        refs = list(refs)
        blk_ref = refs.pop(0)
        sink_ref = refs.pop(0) if has_sink else None
        do_ref = refs.pop(0)
        dlse_ref = refs.pop(0) if has_lse else None
        d_ref = refs.pop(0)
        dsink_ref = refs.pop(0) if has_sink else None
        g = pl.program_id(2)
        d_ref[:, rep * HEAD_DIM:] = jnp.zeros((ln, 2 * HEAD_DIM), F32)
        if has_sink:
            @pl.when((pl.program_id(0) == 0) & (pl.program_id(1) == 0) & (g == 0))
            def _():
                dsink_ref[...] = jnp.zeros_like(dsink_ref)

        def qblock(i, carry):
            blks = (jnp.maximum(i - 1, 0), i, jnp.minimum(i + 1, nb - 1))
            k3 = jnp.concatenate([_attn_load(blk_ref, rep, bi) for bi in blks], axis=0)
            v3 = jnp.concatenate([_attn_load(blk_ref, rep + 1, bi) for bi in blks], axis=0)
            rows = pl.ds(pl.multiple_of(i * QBLK, QBLK), QBLK)
            dk3 = jnp.zeros_like(k3)
            dv3 = jnp.zeros_like(v3)
            for r in range(rep):
                cols = slice(r * HEAD_DIM, (r + 1) * HEAD_DIM)
                q = _attn_load(blk_ref, r, i)
                dov = do_ref[rows, cols]
                dl = dlse_ref[rows, cols] if has_lse else jnp.zeros_like(dov)
                if has_sink:
                    srow = sink_ref[pl.ds(g * rep + r, 1), :]
                    _, vjp = jax.vjp(lambda q_, k_, v_, s_: f(q_, k_, v_, _lane0(s_), i), q, k3, v3, srow)
                    dq, dk, dv, ds = vjp((dov, jnp.sum(dl, axis=1, keepdims=True)))
                    dsink_ref[pl.ds(g * rep + r, 1), :] += ds
                else:
                    _, vjp = jax.vjp(lambda q_, k_, v_: f(q_, k_, v_, None, i), q, k3, v3)
                    dq, dk, dv = vjp((dov, jnp.sum(dl, axis=1, keepdims=True)))
                d_ref[rows, cols] = dq
                dk3 = dk3 + dk
                dv3 = dv3 + dv
            for m, bi in enumerate(blks):
                brow = pl.ds(pl.multiple_of(bi * QBLK, QBLK), QBLK)
                d_ref[brow, rep * HEAD_DIM:(rep + 1) * HEAD_DIM] += dk3[m * QBLK:(m + 1) * QBLK]
                d_ref[brow, (rep + 1) * HEAD_DIM:] += dv3[m * QBLK:(m + 1) * QBLK]
            return carry

        lax.fori_loop(0, nb, qblock, 0)

    ospec = pl.BlockSpec((None, ln, rep * HEAD_DIM), lambda b, r, h: (b, 0, r * nh + h))
    in_specs = [pl.BlockSpec((None, ln, bw), lambda b, r, h: (b, 0, r * nbw + cb0 + h))]
    ins = [rq.reshape(batch, ln, dil * w)]
    if has_sink:
        in_specs.append(pl.BlockSpec(sinkx.shape, lambda b, r, h: (0, 0)))
        ins.append(sinkx)
    in_specs.append(ospec)
    ins.append(do.reshape(batch, ln, dil * ow))
    if has_lse:
        in_specs.append(ospec)
        ins.append(dlse.reshape(batch, ln, dil * ow))
    dw = nh * bw
    out_specs = [pl.BlockSpec((None, ln, bw), lambda b, r, h: (b, 0, r * nh + h))]
    out_shape = [jax.ShapeDtypeStruct((batch, ln, dil * dw), F32)]
    if has_sink:
        out_specs.append(pl.BlockSpec(sinkx.shape, lambda b, r, h: (0, 0)))
        out_shape.append(jax.ShapeDtypeStruct(sinkx.shape, F32))
    res = pl.pallas_call(
        body, name=name, grid=(batch, dil, nh), in_specs=in_specs, out_specs=out_specs, out_shape=out_shape,
        compiler_params=_cparams(("arbitrary", "arbitrary", "arbitrary")),
    )(*ins)
    return [res[0].reshape(t, dw)] + list(res[1:])


def _final_loss(x, g, target, *, name):
    d = x.shape[1]

    def fn(rv, pv):
        xv, tg = rv
        y, vjp = jax.vjp(_rms, xv, pv[0])
        err = y - tg
        dx, dg = vjp(err * (1.0 / d))
        loss = 0.5 * jnp.sum(err * err) * (1.0 / d)
        return [dx], [dg, jnp.full((1, LANE), loss, F32)]

    (dx,), (dg, loss) = _rowcall(fn, [(x, d, 0), (target, d, 0)], [g], [(d, F32)], [(1, d), (1, LANE)],
                                 tile=256, name=name)
    return dx, dg, loss


class _Layout:
    def __init__(self, d_model):
        self.d = d_model
        self.inner = SSD_HEADS * SSD_HEAD_DIM
        self.gn = SSD_GROUPS * SSD_STATE
        self.xbc = self.inner + 2 * self.gn
        self.ndt = 2 * SSD_HEADS
        self.ngrp = len(DIL_PATTERNS)
        self.dilw = DIL_HEADS * HEAD_DIM
        self.rqd = 3 * self.ngrp * self.dilw
        self.rep = WIN_Q_HEADS // WIN_KV_HEADS
        self.rqw = WIN_KV_HEADS * (self.rep + 2) * HEAD_DIM
        self.qw = WIN_Q_HEADS * HEAD_DIM
        self.kw = WIN_KV_HEADS * HEAD_DIM
        self.gates = N_BRANCH * d_model
        self.n_in = self.inner + self.xbc + self.ndt + self.rqd + self.qw + 2 * self.kw + self.gates
        self.o_gates = 0
        self.o_z = self.gates
        self.o_xbc = self.o_z + self.inner
        self.o_rqd = self.o_xbc + self.xbc
        self.o_rqw = self.o_rqd + self.rqd
        self.o_dt = self.o_rqw + self.rqw
        self.dtw = -(-(self.o_dt + self.ndt) // PAD_TO) * PAD_TO - self.o_dt
        self.width = self.o_dt + self.dtw
        assert self.o_z % self.inner == 0 and self.o_xbc % 256 == 0
        assert self.o_rqd % (3 * HEAD_DIM) == 0 and self.o_rqw % (3 * HEAD_DIM) == 0 and self.dtw % LANE == 0

    def split_points(self):
        sizes = (self.inner, self.xbc, self.ndt, self.rqd, self.qw, self.kw, self.kw, self.gates)
        pts, acc = [], 0
        for s in sizes:
            pts.append((acc, acc + s))
            acc += s
        return pts

    def permute_w(self, w):
        d = w.shape[0]
        z, xbc, dt, qkvd, qw, kw, vw, gates = [w[:, a:b] for a, b in self.split_points()]
        nhd = self.ngrp * DIL_HEADS
        qkvd = qkvd.reshape(d, 3, nhd, HEAD_DIM).transpose(0, 2, 1, 3).reshape(d, self.rqd)
        win = jnp.concatenate([qw.reshape(d, WIN_KV_HEADS, self.rep, HEAD_DIM),
                               kw.reshape(d, WIN_KV_HEADS, 1, HEAD_DIM),
                               vw.reshape(d, WIN_KV_HEADS, 1, HEAD_DIM)], axis=2).reshape(d, self.rqw)
        pad = jnp.zeros((d, self.dtw - self.ndt), w.dtype)
        return jnp.concatenate([gates, z, xbc, qkvd, win, dt, pad], axis=1)

    def unpermute_w(self, wp):
        d = wp.shape[0]
        gates = wp[:, :self.o_z]
        z = wp[:, self.o_z:self.o_xbc]
        xbc = wp[:, self.o_xbc:self.o_rqd]
        qkvd = wp[:, self.o_rqd:self.o_rqw]
        win = wp[:, self.o_rqw:self.o_dt].reshape(d, WIN_KV_HEADS, self.rep + 2, HEAD_DIM)
        dt = wp[:, self.o_dt:self.o_dt + self.ndt]
        nhd = self.ngrp * DIL_HEADS
        qkvd = qkvd.reshape(d, nhd, 3, HEAD_DIM).transpose(0, 2, 1, 3).reshape(d, self.rqd)
        qw = win[:, :, :self.rep].reshape(d, self.qw)
        kw = win[:, :, self.rep].reshape(d, self.kw)
        vw = win[:, :, self.rep + 1].reshape(d, self.kw)
        return jnp.concatenate([z, xbc, dt, qkvd, qw, kw, vw, gates], axis=1)


def _dt_layouts(pdt, dirn, batch_seq):
    hg = SSD_HEADS // SSD_GROUPS
    v = pdt[:, dirn * SSD_HEADS:(dirn + 1) * SSD_HEADS].reshape(batch_seq, SSD_GROUPS, hg)
    return v.transpose(1, 0, 2), v.transpose(1, 2, 0)


def _par_layouts(p):
    hg = SSD_HEADS // SSD_GROUPS
    v = p.reshape(SSD_GROUPS, hg)
    return v[:, None, :], v[:, :, None]


def _layer_fwd(x, lw, lay, tabs, batch, seq):
    d = lay.d
    sv = {"x": x}
    (h,), _ = _rowcall(lambda rv, pv: ([_rms(rv[0], pv[0])], []), [(x, d, 0)], [lw["g_mix"]], [(d, BF16)], [],
                       tile=256, name="norm_mix")
    p = _mm(h, lw["w_in"], name="proj_in")
    sv["h"], sv["p"] = h, p
    u = _conv_fwd(p, lw["conv_w8"], lw["conv_b"], col0=lay.o_xbc, chans=lay.xbc, batch=batch, seq=seq, name="conv_fwd")
    sv["u"] = u
    pdt = p[:, lay.o_dt:lay.o_dt + lay.ndt]
    ys, sv["st"], sv["dtl"] = [], [], []
    for dirn in range(2):
        dtr, dtr_t = _dt_layouts(pdt, dirn, batch * seq)
        bias, bias_t = _par_layouts(lw["dt_bias"][dirn])
        alog, alog_t = _par_layouts(lw["a_log"][dirn])
        y, st = _scan_fwd(u, dtr, dtr_t, bias, bias_t, alog, alog_t, batch=batch, seq=seq, inner=lay.inner,
                          reverse=bool(dirn), name="scan_fwd%d" % dirn)
        ys.append(y)
        sv["st"].append(st)
        sv["dtl"].append((dtr, dtr_t, bias, bias_t, alog, alog_t))
    sv["ys"] = ys
    inner = lay.inner
    (ya,), _ = _rowcall(lambda rv, pv: ([_ssd_post_f(*rv, *pv)], []),
                        [(ys[0], inner, 0), (ys[1], inner, 0), (u, inner, 0), (p, inner, lay.o_z // inner)],
                        [lw["d_skip_x"], lw["ssd_norm"]], [(inner, BF16)], [], tile=128, name="ssd_post")
    sv["ya"] = ya
    rqd = _rope(p, tabs, col0=lay.o_rqd, width=lay.rqd, seq=seq, group=0, inverse=False, out_dtype=BF16, name="rope_dil")
    rqw = _rope(p, tabs, col0=lay.o_rqw, width=lay.rqw, seq=seq, group=lay.rep + 2, inverse=False, out_dtype=BF16,
                name="rope_win")
    sv["rqd"], sv["rqw"] = rqd, rqw
    os_, ls_ = [], []
    for gi, (window, dil) in enumerate(DIL_PATTERNS):
        o, l = _attn_fwd(rqd, None, batch=batch, seq=seq, dil=dil, nbw=lay.ngrp * DIL_HEADS, cb0=gi * DIL_HEADS,
                         nh=DIL_HEADS, rep=1, hw=window // (2 * dil), want_lse=True, out_dtype=F32,
                         name="dil_fwd%d" % gi)
        os_.append(o)
        ls_.append(l)
    sv["os"], sv["ls"] = os_, ls_
    dw = lay.dilw
    (yb,), _ = _rowcall(lambda rv, pv: ([_combine_f(*rv)], []), [(a, dw, 0) for a in os_ + ls_], [], [(dw, BF16)], [],
                        tile=256, name="dil_combine")
    sv["yb"] = yb
    (yc,) = _attn_fwd(rqw, lw["sink_x"], batch=batch, seq=seq, dil=1, nbw=WIN_KV_HEADS, cb0=0, nh=WIN_KV_HEADS,
                      rep=lay.rep, hw=WIN_HALF, want_lse=False, out_dtype=BF16, name="win_fwd")
    sv["yc"] = yc
    ma = _mm(ya, lw["w_a"], name="proj_a")
    mb = _mm(yb, lw["w_b"], name="proj_b")
    mc = _mm(yc, lw["w_c"], name="proj_c")
    sv["mabc"] = (ma, mb, mc)
    (mg,), _ = _rowcall(lambda rv, pv: ([_merge_f(*rv)], []),
                        [(ma, d, 0), (mb, d, 0), (mc, d, 0), (p, d, 0), (p, d, 1), (p, d, 2)], [], [(d, BF16)], [],
                        tile=256, name="merge")
    sv["mg"] = mg
    x1 = _mm(mg, lw["w_out"], add=x, name="proj_out")
    sv["x1"] = x1
    (hm,), _ = _rowcall(lambda rv, pv: ([_rms(rv[0], pv[0])], []), [(x1, d, 0)], [lw["g_mlp"]], [(d, BF16)], [],
                        tile=256, name="norm_mlp")
    up = _mm(hm, lw["w_up"], name="mlp_up")
    act = _map2d(lambda a: jnp.square(jnp.maximum(a, 0.0)), [up], BF16, name="relu2")
    sv["hm"], sv["up"], sv["act"] = hm, up, act
    x2 = _mm(act, lw["w_down"], add=x1, name="mlp_down")
    return x2, sv


def _layer_bwd(dxo, sv, lw, lay, tabs, batch, seq):
    d = lay.d
    inner = lay.inner
    gw, gs = {}, {}
    dact = _mm(dxo, lw["w_down"], tb=True, name="mlp_down_dx")
    gw["w_down"] = _mm(sv["act"], dxo, ta=True, name="mlp_down_dw")
    dup = _map2d(lambda g, a: g * (2.0 * jnp.maximum(a, 0.0)), [dact, sv["up"]], BF16, name="relu2_bwd")
    dhm = _mm(dup, lw["w_up"], tb=True, name="mlp_up_dx")
    gw["w_up"] = _mm(sv["hm"], dup, ta=True, name="mlp_up_dw")

    def norm_bwd(rv, pv):
        xv, dh, dres = rv
        _, vjp = jax.vjp(_rms, xv, pv[0])
        dx, dg = vjp(dh)
        return [dx + dres], [dg]

    (dx1,), (gs["g_mlp"],) = _rowcall(norm_bwd, [(sv["x1"], d, 0), (dhm, d, 0), (dxo, d, 0)], [lw["g_mlp"]],
                                      [(d, F32)], [(1, d)], tile=128, name="norm_mlp_bwd")
    dmg = _mm(dx1, lw["w_out"], tb=True, name="proj_out_dx")
    gw["w_out"] = _mm(sv["mg"], dx1, ta=True, name="proj_out_dw")
    ma, mb, mc = sv["mabc"]
    p = sv["p"]

    def merge_bwd(rv, pv):
        _, vjp = jax.vjp(_merge_f, *rv[:6])
        da, db, dc, d0, d1, d2 = vjp(rv[6])
        return [da, db, dc, jnp.concatenate([d0, d1, d2], axis=1)], []

    (dma, dmb, dmc, dgl), _ = _rowcall(
        merge_bwd, [(ma, d, 0), (mb, d, 0), (mc, d, 0), (p, d, 0), (p, d, 1), (p, d, 2), (dmg, d, 0)], [],
        [(d, BF16), (d, BF16), (d, BF16), (lay.gates, BF16)], [], tile=128, name="merge_bwd")
    dya = _mm(dma, lw["w_a"], tb=True, name="proj_a_dx")
    gw["w_a"] = _mm(sv["ya"], dma, ta=True, name="proj_a_dw")
    dyb = _mm(dmb, lw["w_b"], tb=True, name="proj_b_dx")
    gw["w_b"] = _mm(sv["yb"], dmb, ta=True, name="proj_b_dw")
    dyc = _mm(dmc, lw["w_c"], tb=True, name="proj_c_dx")
    gw["w_c"] = _mm(sv["yc"], dmc, ta=True, name="proj_c_dw")
    drqw, dsink = _attn_bwd(sv["rqw"], lw["sink_x"], dyc, None, batch=batch, seq=seq, dil=1, nbw=WIN_KV_HEADS, cb0=0,
                            nh=WIN_KV_HEADS, rep=lay.rep, hw=WIN_HALF, name="win_bwd")
    gs["sink"] = jnp.sum(dsink, axis=1)
    dw = lay.dilw

    def combine_bwd(rv, pv):
        _, vjp = jax.vjp(_combine_f, *rv[:6])
        return list(vjp(rv[6])), []

    dol, _ = _rowcall(combine_bwd, [(a, dw, 0) for a in sv["os"] + sv["ls"]] + [(dyb, dw, 0)], [],
                      [(dw, F32)] * 6, [], tile=256, name="dil_combine_bwd")
    drq = []
    for gi, (window, dil) in enumerate(DIL_PATTERNS):
        (dg_,) = _attn_bwd(sv["rqd"], None, dol[gi], dol[3 + gi], batch=batch, seq=seq, dil=dil,
                           nbw=lay.ngrp * DIL_HEADS, cb0=gi * DIL_HEADS, nh=DIL_HEADS, rep=1,
                           hw=window // (2 * dil), name="dil_bwd%d" % gi)
        drq.append(_rope(dg_, tabs, col0=0, width=dg_.shape[1], seq=seq, group=0, inverse=True, out_dtype=BF16,
                         name="rope_dil_bwd%d" % gi))
    drq.append(_rope(drqw, tabs, col0=0, width=lay.rqw, seq=seq, group=lay.rep + 2, inverse=True, out_dtype=BF16,
                     name="rope_win_bwd"))
    u, ys = sv["u"], sv["ys"]

    def post_bwd(rv, pv):
        _, vjp = jax.vjp(_ssd_post_f, *rv[:4], *pv)
        dyf, _, dxs, dz, dsk, dgn = vjp(rv[4])
        return [dyf, dxs, dz], [dsk, dgn]

    (dy, dxs_post, dz), (dsk, gs["ssd_norm"]) = _rowcall(
        post_bwd, [(ys[0], inner, 0), (ys[1], inner, 0), (u, inner, 0), (p, inner, lay.o_z // inner), (dya, inner, 0)],
        [lw["d_skip_x"], lw["ssd_norm"]], [(inner, F32), (inner, F32), (inner, BF16)], [(1, inner), (1, inner)],
        tile=128, name="ssd_post_bwd")
    gs["d_skip"] = jnp.sum(dsk.reshape(SSD_HEADS, SSD_HEAD_DIM), axis=1)
    dxs, dbm, dcm = dxs_post, None, None
    ddt, gdb, gda = [], [], []
    for dirn in range(2):
        dtr, dtr_t, bias, bias_t, alog, alog_t = sv["dtl"][dirn]
        r = _scan_bwd(u, dtr, dtr_t, bias, bias_t, alog, alog_t, sv["st"][dirn], dy, batch=batch, seq=seq,
                      inner=inner, reverse=bool(dirn), name="scan_bwd%d" % dirn)
        dxs = dxs + r[0]
        dbm = r[1] if dbm is None else dbm + r[1]
        dcm = r[2] if dcm is None else dcm + r[2]
        ddt.append((r[3] + r[4].transpose(0, 2, 1)).transpose(1, 0, 2).reshape(batch * seq, SSD_HEADS))
        gdb.append((r[5][:, 0, :] + r[6][:, :, 0]).reshape(SSD_HEADS))
        gda.append((r[7][:, 0, :] + r[8][:, :, 0]).reshape(SSD_HEADS))
    gs["dt_bias"] = jnp.stack(gdb)
    gs["a_log"] = jnp.stack(gda)
    du = jnp.concatenate([dxs, dbm, dcm], axis=1)
    dxbc, dcw, dcb = _conv_bwd(p, lw["conv_w8"], lw["conv_b"], du, col0=lay.o_xbc, chans=lay.xbc, batch=batch,
                               seq=seq, name="conv_bwd")
    gs["conv_w"] = dcw[:CONV_WIDTH]
    gs["conv_b"] = dcb[0]
    ddtp = jnp.concatenate(ddt + [jnp.zeros((batch * seq, lay.dtw - lay.ndt), F32)], axis=1).astype(BF16)
    dp = jnp.concatenate([dgl, dz, dxbc] + drq + [ddtp], axis=1)
    dh = _mm(dp, lw["w_in"], tb=True, name="proj_in_dx")
    gw["w_in"] = _mm(sv["h"], dp, ta=True, name="proj_in_dw")
    (dx,), (gs["g_mix"],) = _rowcall(norm_bwd, [(sv["x"], d, 0), (dh, d, 0), (dx1, d, 0)], [lw["g_mix"]],
                                     [(d, F32)], [(1, d)], tile=128, name="norm_mix_bwd")
    return dx, gw, gs


_SHARDED = ("w_in", "w_a", "w_b", "w_c", "w_out", "w_up", "w_down")
_COL_SHARDED = ("w_in", "w_b", "w_up")
_SMALL = ("g_mix", "conv_b", "dt_bias", "a_log", "d_skip", "ssd_norm", "sink", "g_mlp")


def _gathered_to_full(name, g):
    n, nl, r, c = g.shape
    if name in _COL_SHARDED:
        return g.transpose(1, 2, 0, 3).reshape(nl, r, n * c)
    return g.transpose(1, 0, 2, 3).reshape(nl, n * r, c)


def _full_to_slots(name, w):
    nl, r, c = w.shape
    if name in _COL_SHARDED:
        return w.reshape(nl, r, N_DEV, c // N_DEV).transpose(2, 0, 1, 3)
    return w.reshape(nl, N_DEV, r // N_DEV, c).transpose(1, 0, 2, 3)


def _pack(parts):
    flat = jnp.concatenate([p.reshape(-1).astype(F32) for p in parts])
    n = flat.shape[0]
    rows = -(-n // (8 * LANE)) * 8
    return jnp.pad(flat, (0, rows * LANE - n)).reshape(rows, LANE)


def _unpack(buf, shapes):
    flat = buf.reshape(-1)
    out, off = [], 0
    for s in shapes:
        n = math.prod(s)
        out.append(flat[off:off + n].reshape(s))
        off += n
    return out


def kernel(x, g_mix, w_in, conv_w, conv_b, dt_bias, a_log, d_skip, ssd_norm, w_a, w_b, w_c, sink, w_out, g_mlp, w_up, w_down, g_final, loss_target, m_g_mix, m_w_in, m_conv_w, m_conv_b, m_dt_bias, m_a_log, m_d_skip, m_ssd_norm, m_w_a, m_w_b, m_w_c, m_sink, m_w_out, m_g_mlp, m_w_up, m_w_down, m_g_final, v_g_mix, v_w_in, v_conv_w, v_conv_b, v_dt_bias, v_a_log, v_d_skip, v_ssd_norm, v_w_a, v_w_b, v_w_c, v_sink, v_w_out, v_g_mlp, v_w_up, v_w_down, v_g_final):
    batch, seq, d = x.shape
    depth = g_mix.shape[0]
    lay = _Layout(d)
    assert lay.n_in == w_in.shape[2] * N_DEV
    wts = dict(g_mix=g_mix, w_in=w_in, conv_w=conv_w, conv_b=conv_b, dt_bias=dt_bias, a_log=a_log, d_skip=d_skip,
               ssd_norm=ssd_norm, w_a=w_a, w_b=w_b, w_c=w_c, sink=sink, w_out=w_out, g_mlp=g_mlp, w_up=w_up,
               w_down=w_down, g_final=g_final)
    mom = dict(g_mix=m_g_mix, w_in=m_w_in, conv_w=m_conv_w, conv_b=m_conv_b, dt_bias=m_dt_bias, a_log=m_a_log,
               d_skip=m_d_skip, ssd_norm=m_ssd_norm, w_a=m_w_a, w_b=m_w_b, w_c=m_w_c, sink=m_sink, w_out=m_w_out,
               g_mlp=m_g_mlp, w_up=m_w_up, w_down=m_w_down, g_final=m_g_final)
    var = dict(g_mix=v_g_mix, w_in=v_w_in, conv_w=v_conv_w, conv_b=v_conv_b, dt_bias=v_dt_bias, a_log=v_a_log,
               d_skip=v_d_skip, ssd_norm=v_ssd_norm, w_a=v_w_a, w_b=v_w_b, w_c=v_w_c, sink=v_sink, w_out=v_w_out,
               g_mlp=v_g_mlp, w_up=v_w_up, w_down=v_w_down, g_final=v_g_final)
    me = 4 * lax.axis_index("x") + 2 * lax.axis_index("y") + lax.axis_index("c")

    names = list(_SHARDED) + ["conv_w"]
    gathered = _exchange([wts[n].astype(F32 if n == "conv_w" else BF16) for n in names], scatter=False,
                         name="gather_weights")
    full = {n: _gathered_to_full(n, g) if n != "conv_w" else g.transpose(1, 2, 0, 3).reshape(depth, CONV_WIDTH, -1)
            for n, g in zip(names, gathered)}

    tabs = _rope_tables(seq)
    t = batch * seq
    xf = x.reshape(t, d)
    layers = []
    for l in range(depth):
        lw = dict(
            g_mix=g_mix[l][None], g_mlp=g_mlp[l][None], ssd_norm=ssd_norm[l][None], conv_b=conv_b[l][None],
            dt_bias=dt_bias[l], a_log=a_log[l],
            d_skip_x=jnp.repeat(d_skip[l], SSD_HEAD_DIM)[None],
            sink_x=jnp.broadcast_to(sink[l][:, None], (WIN_Q_HEADS, LANE)),
            conv_w8=jnp.pad(full["conv_w"][l], ((0, 8 - CONV_WIDTH), (0, 0))),
            w_in=lay.permute_w(full["w_in"][l]), w_a=full["w_a"][l], w_b=full["w_b"][l], w_c=full["w_c"][l],
            w_out=full["w_out"][l], w_up=full["w_up"][l], w_down=full["w_down"][l])
        layers.append(lw)

    saves = []
    h = xf
    for l in range(depth):
        h, sv = _layer_fwd(h, layers[l], lay, tabs, batch, seq)
        saves.append(sv)
    dx, dgf, loss = _final_loss(h, g_final[None], loss_target.reshape(t, d), name="final_loss")

    gws, gss = [None] * depth, [None] * depth
    for l in reversed(range(depth)):
        dx, gws[l], gss[l] = _layer_bwd(dx, saves[l], layers[l], lay, tabs, batch, seq)
    grad_x = dx.reshape(batch, seq, d)

    slots = []
    for n in _SHARDED:
        gl = [lay.unpermute_w(gws[l][n]) if n == "w_in" else gws[l][n] for l in range(depth)]
        slots.append(_full_to_slots(n, jnp.stack(gl)).astype(BF16))
    recvs = _exchange(slots, scatter=True, name="scatter_grads")

    small_parts = [jnp.stack([gss[l][n] for l in range(depth)]) for n in _SMALL]
    small_parts += [dgf, jnp.stack([gss[l]["conv_w"] for l in range(depth)]), loss[0, :1]]
    small_shapes = [p.shape for p in small_parts]
    (rs,) = _exchange([_pack(small_parts)], scatter=False, name="gather_small")
    red = _unpack(_sum_slots(rs, name="sum_small"), small_shapes)
    gsmall = dict(zip(list(_SMALL) + ["g_final"], red[:len(_SMALL) + 1]))
    gconv_full, loss_sum = red[-2], red[-1]
    cshard = conv_w.shape[2]
    gsmall["conv_w"] = lax.dynamic_slice_in_dim(gconv_full, me * cshard, cshard, axis=2)

    out = {}
    for n, rv in zip(_SHARDED, recvs):
        shp = wts[n].shape
        r2 = (shp[0] * shp[1], shp[2])
        res = _adamw(wts[n].reshape(r2), rv.reshape((N_DEV,) + r2), mom[n].reshape(r2), var[n].reshape(r2),
                     name="adamw_" + n)
        out[n] = [a.reshape(shp) for a in res]
    rep_names = list(_SMALL) + ["g_final"]
    rep_shapes = [wts[n].shape for n in rep_names]
    res = _adamw(_pack([wts[n] for n in rep_names]), _pack([gsmall[n] for n in rep_names])[None],
                 _pack([mom[n] for n in rep_names]), _pack([var[n] for n in rep_names]), name="adamw_small")
    unp = [_unpack(a, rep_shapes) for a in res]
    for i, n in enumerate(rep_names):
        out[n] = [unp[k][i] for k in range(4)]
    cs2 = (depth * CONV_WIDTH, cshard)
    res = _adamw(conv_w.reshape(cs2), gsmall["conv_w"].reshape((1,) + cs2), m_conv_w.reshape(cs2),
                 v_conv_w.reshape(cs2), name="adamw_conv_w")
    out["conv_w"] = [a.reshape(conv_w.shape) for a in res]

    order = ["g_mix", "w_in", "conv_w", "conv_b", "dt_bias", "a_log", "d_skip", "ssd_norm", "w_a", "w_b", "w_c",
             "sink", "w_out", "g_mlp", "w_up", "w_down", "g_final"]
    outs = [loss_sum.reshape(()), grad_x]
    for k in range(4):
        outs += [out[n][k] for n in order]
    return tuple(outs)
```

```python
import functools
import math

import jax
import jax.numpy as jnp
from jax import lax
from jax.experimental import pallas as pl
from jax.experimental.pallas import tpu as pltpu

F32 = jnp.float32
BF16 = jnp.bfloat16
HI = lax.Precision.HIGHEST
MESH = pl.DeviceIdType.MESH
N_DEV = 8

SSD_HEADS = 32
SSD_HEAD_DIM = 64
SSD_GROUPS = 8
SSD_STATE = 128
SSD_CHUNK = 128
CONV_WIDTH = 5
HEAD_DIM = 128
ROPE_DIM = 32
ROPE_THETA = 500000.0
DIL_PATTERNS = ((128, 1), (512, 4), (2048, 16))
DIL_HEADS = 8
WIN_Q_HEADS = 16
WIN_KV_HEADS = 4
WIN_HALF = 128
N_BRANCH = 3
EPS = 1e-6
NEG_INF = -1e30
ADAM_LR = 0.001
ADAM_B1 = 0.9
ADAM_B2 = 0.999
ADAM_EPS = 1e-08
ADAM_WD = 0.01
ADAM_STEP = 10

LANE = 128
QBLK = 128
VMEM_LIMIT = 56 * 1024 * 1024
PAD_TO = 512
MM_VMEM_BUDGET = 40 * 1024 * 1024


def _cparams(sem=None):
    return pltpu.CompilerParams(dimension_semantics=sem, vmem_limit_bytes=VMEM_LIMIT)


PIECE_BYTES = 800 * 1024
US_PER_PIECE_BYTE = 8.8e-5
MAX_PIECES = 8
CARRIER_US = {
    "proj_in": 450, "proj_in_dx": 520, "proj_in_dw": 285, "scan_fwd0": 248, "scan_fwd1": 248, "scan_bwd0": 555,
    "scan_bwd1": 555, "win_fwd": 163, "win_bwd": 414, "dil_fwd0": 110, "dil_fwd1": 110, "dil_fwd2": 181,
    "dil_bwd0": 243, "dil_bwd1": 243, "dil_bwd2": 250, "mlp_up": 155, "mlp_down": 171, "mlp_up_dx": 170,
    "mlp_up_dw": 170, "mlp_down_dx": 162, "mlp_down_dw": 189, "conv_bwd": 156, "rope_dil": 139,
    "adamw_w_up": 63, "adamw_w_down": 62,
}


class _Piece:
    def __init__(self, key, row0, rows, scatter, est):
        self.key, self.row0, self.rows, self.scatter, self.est = key, row0, rows, scatter, est


class _Sched:
    def __init__(self):
        self.queue, self.src, self.dst = [], {}, {}

    def post(self, key, src, scatter):
        r, c = src.shape[-2:]
        self.src[key] = src
        self.dst[key] = lax.empty((N_DEV, r, c), src.dtype)
        row_bytes = c * src.dtype.itemsize
        pr = r
        while pr * row_bytes > PIECE_BYTES and pr % 32 == 0:
            pr //= 2
        for row0 in range(0, r, pr):
            self.queue.append(_Piece(key, row0, pr, scatter, pr * row_bytes * US_PER_PIECE_BYTE))

    def take(self, name):
        budget = CARRIER_US.get(name)
        out, used = [], 0.0
        while budget and self.queue and len(out) < MAX_PIECES and used + self.queue[0].est <= 1.1 * budget:
            used += self.queue[0].est
            out.append(self.queue.pop(0))
        return out

    def get(self, key):
        last = max([i for i, p in enumerate(self.queue) if p.key == key], default=-1)
        while last >= 0:
            n = min(last + 1, 4 * MAX_PIECES)
            pieces, self.queue = self.queue[:n], self.queue[n:]
            _exchange_pieces(self, pieces, name="exchange_flush")
            last -= n
        return self.dst[key]


_SCHED = None


def _piece_copies(pieces, keys, src_refs, dst_refs, send_sems, recv_sems, loc_sems):
    x, y, c = lax.axis_index("x"), lax.axis_index("y"), lax.axis_index("c")
    me = 4 * x + 2 * y + c
    cps = []
    for t, p in enumerate(pieces):
        ki = keys.index(p.key)
        rows = pl.ds(p.row0, p.rows)
        for j in range(1, N_DEV):
            px = (1 - x) if (j >> 2) & 1 else x
            py = (1 - y) if (j >> 1) & 1 else y
            pc = (1 - c) if j & 1 else c
            src = src_refs[ki].at[4 * px + 2 * py + pc, rows] if p.scatter else src_refs[ki].at[rows]
            cps.append(pltpu.make_async_remote_copy(
                src_ref=src, dst_ref=dst_refs[ki].at[me, rows], send_sem=send_sems.at[t * 7 + j - 1],
                recv_sem=recv_sems.at[t * 7 + j - 1], device_id=(px, py, pc), device_id_type=MESH))
        src = src_refs[ki].at[me, rows] if p.scatter else src_refs[ki].at[rows]
        cps.append(pltpu.make_async_copy(src, dst_refs[ki].at[me, rows], loc_sems.at[t]))
    return cps


def _carry_call(sched, pieces, body, *, name, grid, in_specs, out_specs, out_shape, scratch_shapes, ins):
    keys = []
    for p in pieces:
        if p.key not in keys:
            keys.append(p.key)
    n_in, n_out, nk, npc = len(ins), len(out_shape), len(keys), len(pieces)
    n_scr = len(scratch_shapes)

    def wrapped(*refs):
        in_refs = refs[:n_in]
        src_refs = refs[n_in:n_in + nk]
        out_refs = refs[n_in + 2 * nk:n_in + 2 * nk + n_out]
        dst_refs = refs[n_in + 2 * nk + n_out:n_in + 3 * nk + n_out]
        scr = refs[n_in + 3 * nk + n_out:]
        inner_scr, sems = scr[:n_scr], scr[n_scr:]
        if grid:
            pids = [pl.program_id(a) for a in range(len(grid))]
            first = functools.reduce(lambda u, v: u & v, [q == 0 for q in pids])
            last = functools.reduce(lambda u, v: u & v, [q == g - 1 for q, g in zip(pids, grid)])

            @pl.when(first)
            def _():
                for cp in _piece_copies(pieces, keys, src_refs, dst_refs, *sems):
                    cp.start()

            body(*in_refs, *out_refs, *inner_scr)

            @pl.when(last)
            def _():
                for cp in _piece_copies(pieces, keys, src_refs, dst_refs, *sems):
                    cp.wait()
        else:
            cps = _piece_copies(pieces, keys, src_refs, dst_refs, *sems)
            for cp in cps:
                cp.start()
            for cp in cps:
                cp.wait()

    anyspec = pl.BlockSpec(memory_space=pl.ANY)
    dsts = [sched.dst[k] for k in keys]
    kwargs = dict(grid=grid) if grid else {}
    res = pl.pallas_call(
        wrapped, name=name, in_specs=list(in_specs) + [anyspec] * (2 * nk), out_specs=list(out_specs) + [anyspec] * nk,
        out_shape=list(out_shape) + [jax.ShapeDtypeStruct(d.shape, d.dtype) for d in dsts],
        input_output_aliases={n_in + nk + i: n_out + i for i in range(nk)},
        scratch_shapes=list(scratch_shapes) + [pltpu.SemaphoreType.DMA((7 * npc,)), pltpu.SemaphoreType.DMA((7 * npc,)),
                                               pltpu.SemaphoreType.DMA((npc,))],
        compiler_params=pltpu.CompilerParams(dimension_semantics=("arbitrary",) * len(grid) if grid else None,
                                             vmem_limit_bytes=VMEM_LIMIT, has_side_effects=True),
        **kwargs,
    )(*ins, *[sched.src[k] for k in keys], *dsts)
    for i, k in enumerate(keys):
        sched.dst[k] = res[n_out + i]
    return list(res[:n_out])


def _exchange_pieces(sched, pieces, *, name):
    _carry_call(sched, pieces, None, name=name, grid=(), in_specs=[], out_specs=[], out_shape=[], scratch_shapes=[],
                ins=[])


def _pcall(body, *, name, grid, in_specs, out_specs, out_shape, scratch_shapes=(), sem=None):
    single = not isinstance(out_shape, (list, tuple))
    out_shape_l = [out_shape] if single else list(out_shape)
    out_specs_l = [out_specs] if single else list(out_specs)

    def run(*ins):
        pieces = _SCHED.take(name) if _SCHED is not None else []
        if pieces:
            res = _carry_call(_SCHED, pieces, body, name=name, grid=grid, in_specs=in_specs, out_specs=out_specs_l,
                              out_shape=out_shape_l, scratch_shapes=list(scratch_shapes), ins=ins)
        else:
            res = pl.pallas_call(body, name=name, grid=grid, in_specs=list(in_specs), out_specs=out_specs_l,
                                 out_shape=out_shape_l, scratch_shapes=list(scratch_shapes),
                                 compiler_params=_cparams(sem))(*ins)
        return res[0] if single else list(res)

    return run


def _pick(dim, cands):
    for c in cands:
        if dim % c == 0:
            return c
    return dim


def _mm_tiles(m, n, k, a_bytes, b_bytes, o_bytes, has_add):
    tm = _pick(m, (1024, 512, 256, 128))
    tn = _pick(n, (1024, 1792, 512, 256, 128))
    for tk in (2048, 1792, 1024, 896, 512, 256, 128):
        if k % tk:
            continue
        need = 2 * (tm * tk * a_bytes + tk * tn * b_bytes + tm * tn * (o_bytes + (4 if has_add else 0)))
        need += tm * tn * 4 if k // tk > 1 else 0
        if need <= MM_VMEM_BUDGET:
            return tm, tn, tk
    return tm, tn, _pick(k, (128,))


def _mm(a, b, *, ta=False, tb=False, out_dtype=F32, add=None, name):
    m, k = (a.shape[1], a.shape[0]) if ta else a.shape
    k2, n = (b.shape[1], b.shape[0]) if tb else b.shape
    assert k == k2, (a.shape, b.shape, ta, tb)
    tm, tn, tk = _mm_tiles(m, n, k, a.dtype.itemsize, b.dtype.itemsize, jnp.dtype(out_dtype).itemsize,
                           add is not None)
    assert m % tm == 0 and n % tn == 0 and k % tk == 0, (m, n, k, tm, tn, tk)
    nk = k // tk
    a_spec = pl.BlockSpec((tk, tm), lambda i, j, kk: (kk, i)) if ta else pl.BlockSpec((tm, tk), lambda i, j, kk: (i, kk))
    b_spec = pl.BlockSpec((tn, tk), lambda i, j, kk: (j, kk)) if tb else pl.BlockSpec((tk, tn), lambda i, j, kk: (kk, j))
    o_spec = pl.BlockSpec((tm, tn), lambda i, j, kk: (i, j))
    dims = (((0 if ta else 1,), (1 if tb else 0,)), ((), ()))
    has_add = add is not None

    def body(*refs):
        a_ref, b_ref = refs[:2]
        add_ref = refs[2] if has_add else None
        o_ref = refs[3] if has_add else refs[2]
        part = lax.dot_general(a_ref[...].astype(BF16), b_ref[...].astype(BF16), dims, preferred_element_type=F32)
        if nk == 1:
            if has_add:
                part = part + add_ref[...]
            o_ref[...] = part.astype(o_ref.dtype)
            return
        acc_ref = refs[-1]
        kk = pl.program_id(2)

        @pl.when(kk == 0)
        def _():
            acc_ref[...] = part

        @pl.when(kk > 0)
        def _():
            acc_ref[...] += part

        @pl.when(kk == nk - 1)
        def _():
            r = acc_ref[...]
            if has_add:
                r = r + add_ref[...]
            o_ref[...] = r.astype(o_ref.dtype)

    ins = [a, b] + ([add] if has_add else [])
    specs = [a_spec, b_spec] + ([o_spec] if has_add else [])
    return _pcall(
        body, name=name, grid=(m // tm, n // tn, nk), in_specs=specs, out_specs=o_spec,
        out_shape=jax.ShapeDtypeStruct((m, n), out_dtype),
        scratch_shapes=[pltpu.VMEM((tm, tn), F32)] if nk > 1 else [],
        sem=("parallel", "parallel", "arbitrary"),
    )(*ins)


def _rowcall(fn, rows, pars, row_outs, par_outs, *, tile, name):
    t = rows[0][0].shape[0]
    tile = min(tile, t)
    assert t % tile == 0
    nr, npar, nro, npo = len(rows), len(pars), len(row_outs), len(par_outs)
    in_specs = [pl.BlockSpec((tile, c), functools.partial(lambda i, cb: (i, cb), cb=cb)) for (_, c, cb) in rows]
    in_specs += [pl.BlockSpec(p.shape, lambda i: (0, 0)) for p in pars]
    out_specs = [pl.BlockSpec((tile, c), lambda i: (i, 0)) for (c, _) in row_outs]
    out_specs += [pl.BlockSpec(s, lambda i: (0, 0)) for s in par_outs]
    out_shape = [jax.ShapeDtypeStruct((t, c), d) for (c, d) in row_outs]
    out_shape += [jax.ShapeDtypeStruct(s, F32) for s in par_outs]

    def body(*refs):
        rv = [r[...] for r in refs[:nr]]
        pv = [r[...] for r in refs[nr:nr + npar]]
        ro_refs = refs[nr + npar:nr + npar + nro]
        po_refs = refs[nr + npar + nro:]
        ro, po = fn(rv, pv)
        for ref, v in zip(ro_refs, ro):
            ref[...] = v.astype(ref.dtype)
        if npo:
            @pl.when(pl.program_id(0) == 0)
            def _():
                for ref in po_refs:
                    ref[...] = jnp.zeros_like(ref)
            for ref, v in zip(po_refs, po):
                ref[...] += v

    res = pl.pallas_call(
        body, name=name, grid=(t // tile,), in_specs=in_specs, out_specs=out_specs, out_shape=out_shape,
        compiler_params=_cparams(("arbitrary",)),
    )(*[r[0] for r in rows], *pars)
    return list(res[:nro]), list(res[nro:])


def _map2d(fn, ins, out_dtype, *, name, tile=256, cw=2048):
    t, w = ins[0].shape
    tile, cw = min(tile, t), min(cw, w)
    assert t % tile == 0 and w % cw == 0

    def body(*refs):
        refs[-1][...] = fn(*[r[...] for r in refs[:-1]]).astype(out_dtype)

    spec = pl.BlockSpec((tile, cw), lambda i, j: (i, j))
    return pl.pallas_call(
        body, name=name, grid=(t // tile, w // cw), in_specs=[spec] * len(ins), out_specs=spec,
        out_shape=jax.ShapeDtypeStruct((t, w), out_dtype), compiler_params=_cparams(("parallel", "parallel")),
    )(*ins)


def _exchange(srcs, *, scatter, name):
    n = len(srcs)
    out_shape = [jax.ShapeDtypeStruct(s.shape if scatter else (N_DEV,) + s.shape, s.dtype) for s in srcs]

    def body(*refs):
        src_refs, out_refs = refs[:n], refs[n:2 * n]
        send_sems, recv_sems, loc_sems = refs[2 * n:]
        x, y, c = lax.axis_index("x"), lax.axis_index("y"), lax.axis_index("c")
        me = 4 * x + 2 * y + c
        copies = []
        for a in range(n):
            for j in range(1, N_DEV):
                px = (1 - x) if (j >> 2) & 1 else x
                py = (1 - y) if (j >> 1) & 1 else y
                pc = (1 - c) if j & 1 else c
                src = src_refs[a].at[4 * px + 2 * py + pc] if scatter else src_refs[a]
                cp = pltpu.make_async_remote_copy(
                    src_ref=src, dst_ref=out_refs[a].at[me], send_sem=send_sems.at[a * 7 + j - 1],
                    recv_sem=recv_sems.at[a * 7 + j - 1], device_id=(px, py, pc), device_id_type=MESH)
                cp.start()
                copies.append(cp)
            src = src_refs[a].at[me] if scatter else src_refs[a]
            cp = pltpu.make_async_copy(src, out_refs[a].at[me], loc_sems.at[a])
            cp.start()
            copies.append(cp)
        for cp in copies:
            cp.wait()

    anyspec = pl.BlockSpec(memory_space=pl.ANY)
    return pl.pallas_call(
        body, name=name, in_specs=[anyspec] * n, out_specs=[anyspec] * n, out_shape=out_shape,
        scratch_shapes=[pltpu.SemaphoreType.DMA((7 * n,)), pltpu.SemaphoreType.DMA((7 * n,)),
                        pltpu.SemaphoreType.DMA((n,))],
        compiler_params=pltpu.CompilerParams(has_side_effects=True),
    )(*srcs)


def _row_tile(r, c, budget_elems=256 * 1024):
    tr = r
    while tr * c > budget_elems and tr % 16 == 0:
        tr //= 2
    return tr


def _adamw(w, recvs, m, v, *, name):
    r, c = w.shape
    nl = len(recvs)
    ns, rl = recvs[0].shape[:2]
    assert rl * nl == r
    tr = _row_tile(rl, c)
    nt = rl // tr
    bc1 = 1.0 / (1.0 - ADAM_B1 ** ADAM_STEP)
    bc2 = 1.0 / (1.0 - ADAM_B2 ** ADAM_STEP)

    def body(*refs):
        w_ref, m_ref, v_ref = refs[:3]
        r_refs = refs[3:3 + nl]
        g_ref, d_ref, mo_ref, vo_ref = refs[3 + nl:]
        i = pl.program_id(0)
        for k in range(nl):
            @pl.when(i // nt == k)
            def _(k=k):
                g = r_refs[k][0].astype(F32)
                for s in range(1, ns):
                    g = g + r_refs[k][s].astype(F32)
                g_ref[...] = g
        g = g_ref[...]
        mn = ADAM_B1 * m_ref[...] + (1.0 - ADAM_B1) * g
        vn = ADAM_B2 * v_ref[...] + (1.0 - ADAM_B2) * (g * g)
        mo_ref[...] = mn
        vo_ref[...] = vn
        d_ref[...] = -ADAM_LR * ((mn * bc1) / (jnp.sqrt(vn * bc2) + ADAM_EPS) + ADAM_WD * w_ref[...])

    spec = pl.BlockSpec((tr, c), lambda i: (i, 0))
    rspecs = [pl.BlockSpec((ns, tr, c), functools.partial(lambda i, k: (0, jnp.clip(i - k * nt, 0, nt - 1), 0), k=k))
              for k in range(nl)]
    return _pcall(
        body, name=name, grid=(r // tr,), in_specs=[spec, spec, spec] + rspecs,
        out_specs=[spec] * 4, out_shape=[jax.ShapeDtypeStruct((r, c), F32)] * 4, sem=("arbitrary",),
    )(w, m, v, *recvs)


def _sum_slots(recv, *, name):
    ns, r, c = recv.shape

    def body(r_ref, o_ref):
        g = r_ref[0]
        for s in range(1, ns):
            g = g + r_ref[s]
        o_ref[...] = g

    return pl.pallas_call(body, name=name, out_shape=jax.ShapeDtypeStruct((r, c), F32))(recv)


def _rms(x, g):
    return x * lax.rsqrt(jnp.mean(x * x, axis=-1, keepdims=True) + EPS) * g


def _silu(x):
    return x * jax.nn.sigmoid(x)


def _merge_f(a, b, c, g0, g1, g2):
    return jax.nn.sigmoid(g0) * a + jax.nn.sigmoid(g1) * b + jax.nn.sigmoid(g2) * c


def _ssd_post_f(yf, yb, xs, z, dskip, gnorm):
    y = (yf + yb + dskip * xs) * _silu(z)
    return _rms(y, gnorm)


def _combine_f(o0, o1, o2, l0, l1, l2):
    m = jnp.maximum(jnp.maximum(l0, l1), l2)
    e0, e1, e2 = jnp.exp(l0 - m), jnp.exp(l1 - m), jnp.exp(l2 - m)
    return (e0 * o0 + e1 * o1 + e2 * o2) / (e0 + e1 + e2)


def _attn_block(q, k3, v3, sk, i, *, seq, hw, has_sink):
    s = lax.dot_general(q.astype(BF16), k3.astype(BF16), (((1,), (1,)), ((), ())),
                        preferred_element_type=F32) * (HEAD_DIM ** -0.5)
    qpos = i * QBLK + lax.broadcasted_iota(jnp.int32, s.shape, 0)
    kpos = (i - 1) * QBLK + lax.broadcasted_iota(jnp.int32, s.shape, 1)
    valid = (jnp.abs(qpos - kpos) <= hw) & (kpos >= 0) & (kpos < seq)
    s = jnp.where(valid, s, NEG_INF)
    m = jnp.max(s, axis=-1, keepdims=True)
    if has_sink:
        m = jnp.maximum(m, sk)
    m = lax.stop_gradient(m)
    e = jnp.exp(s - m)
    l = jnp.sum(e, axis=-1, keepdims=True)
    if has_sink:
        l = l + jnp.exp(sk - m)
    o = jnp.dot(e.astype(BF16), v3.astype(BF16), preferred_element_type=F32) / l
    return o, m + jnp.log(l)


def _ssd_chunk(state, xs, bm, cm, dtr, dtr_t, bias, bias_t, alog, alog_t, *, reverse):
    t = xs.shape[0]
    hg = dtr.shape[1]
    hp = xs.shape[1]
    p = hp // hg
    dt = jax.nn.softplus(dtr + bias)
    dt_t = jax.nn.softplus(dtr_t + bias_t)
    dta = dt * (-jnp.exp(alog))
    dta_t = dt_t * (-jnp.exp(alog_t))
    li = lax.broadcasted_iota(jnp.int32, (t, t), 0)
    si = lax.broadcasted_iota(jnp.int32, (t, t), 1)
    tri = (li <= si) if reverse else (li >= si)
    trif = tri.astype(F32)
    cs = jnp.dot(trif, dta, precision=HI, preferred_element_type=F32)
    cs_t = lax.dot_general(dta_t, trif, (((1,), (1,)), ((), ())), precision=HI,
                           preferred_element_type=F32)
    total = jnp.sum(dta, axis=0, keepdims=True)
    cb = lax.dot_general(cm.astype(BF16), bm.astype(BF16), (((1,), (1,)), ((), ())),
                         preferred_element_type=F32)
    lane_h = lax.broadcasted_iota(jnp.int32, (1, hp), 1) // p
    col_h = lax.broadcasted_iota(jnp.int32, (1, hg), 1)
    row_h = lax.broadcasted_iota(jnp.int32, (hg, 1), 0)
    dt_x = jnp.zeros((t, hp), F32)
    ecs_x = jnp.zeros((t, hp), F32)
    ds_x = jnp.zeros((t, hp), F32)
    etot_x = jnp.zeros((1, hp), F32)
    decays, masks = [], []
    for h in range(hg):
        oh = (col_h == h).astype(F32)
        oh_t = (row_h == h).astype(F32)
        mk = (lane_h == h).astype(F32)
        dt_h = jnp.sum(dt * oh, axis=1, keepdims=True)
        cs_h = jnp.sum(cs * oh, axis=1, keepdims=True)
        cst_h = jnp.sum(cs_t * oh_t, axis=0, keepdims=True)
        tot_h = jnp.sum(total * oh, axis=1, keepdims=True)
        dt_x = dt_x + dt_h * mk
        ecs_x = ecs_x + jnp.exp(cs_h) * mk
        ds_x = ds_x + jnp.exp(tot_h - cs_h) * mk
        etot_x = etot_x + jnp.exp(tot_h) * mk
        decays.append(jnp.exp(jnp.where(tri, cs_h - cst_h, -jnp.inf)))
        masks.append(mk)
    xdt = xs * dt_x
    y = jnp.dot(cm.astype(BF16), state.astype(BF16), preferred_element_type=F32) * ecs_x
    for h in range(hg):
        y = y + jnp.dot((cb * decays[h]).astype(BF16), (xdt * masks[h]).astype(BF16),
                        preferred_element_type=F32)
    st_new = lax.dot_general(bm.astype(BF16), (xdt * ds_x).astype(BF16), (((0,), (0,)), ((), ())),
                             preferred_element_type=F32)
    return y, state * etot_x + st_new


def _rope_tables(seq):
    half = ROPE_DIM // 2
    inv = ROPE_THETA ** (-jnp.arange(0, ROPE_DIM, 2, dtype=F32) / ROPE_DIM)
    ang = jnp.arange(seq, dtype=F32)[:, None] * inv[None, :]
    cos, sin = jnp.cos(ang), jnp.sin(ang)
    rest = HEAD_DIM - ROPE_DIM
    c = jnp.concatenate([cos, cos, jnp.ones((seq, rest), F32)], axis=1)
    a = jnp.concatenate([-sin, jnp.zeros((seq, HEAD_DIM - half), F32)], axis=1)
    b = jnp.concatenate([jnp.zeros((seq, half), F32), sin, jnp.zeros((seq, rest), F32)], axis=1)
    return c, a, b


def _rope(src, tabs, *, col0, width, seq, group, inverse, out_dtype, name):
    t = src.shape[0]
    half = ROPE_DIM // 2
    nhb = 6 if (width % (6 * HEAD_DIM) == 0 and col0 % (6 * HEAD_DIM) == 0) else 3
    cw, tile = nhb * HEAD_DIM, 512
    assert width % cw == 0 and col0 % cw == 0 and seq % tile == 0 and t % tile == 0
    ns = seq // tile

    def body(x_ref, c_ref, a_ref, b_ref, o_ref):
        jb = pl.program_id(1)
        c, a, b = c_ref[...], a_ref[...], b_ref[...]
        for hh in range(nhb):
            xv = x_ref[:, hh * HEAD_DIM:(hh + 1) * HEAD_DIM].astype(F32)
            if inverse:
                yv = xv * c + pltpu.roll(xv * a, half, 1) + pltpu.roll(xv * b, HEAD_DIM - half, 1)
            else:
                yv = xv * c + pltpu.roll(xv, HEAD_DIM - half, 1) * a + pltpu.roll(xv, half, 1) * b
            if group:
                keep = ((jb * nhb + hh) % group) == (group - 1)
                yv = jnp.where(keep, xv, yv)
            o_ref[:, hh * HEAD_DIM:(hh + 1) * HEAD_DIM] = yv.astype(o_ref.dtype)

    tspec = pl.BlockSpec((tile, HEAD_DIM), lambda i, j: (i % ns, 0))
    return _pcall(
        body, name=name, grid=(t // tile, width // cw),
        in_specs=[pl.BlockSpec((tile, cw), lambda i, j: (i, col0 // cw + j)), tspec, tspec, tspec],
        out_specs=pl.BlockSpec((tile, cw), lambda i, j: (i, j)),
        out_shape=jax.ShapeDtypeStruct((t, width), out_dtype),
        sem=("parallel", "parallel"),
    )(src, *tabs)


def _shift_rows(x, d, tpos):
    if d == 0:
        return x
    s = x.shape[0]
    y = pltpu.roll(x, (-d) % s, 0)
    ok = (tpos + d >= 0) & (tpos + d < s)
    return jnp.where(ok, y, 0.0)


def _conv_fwd(p, w8, bias, *, col0, chans, batch, seq, name):
    cb = 256
    assert chans % cb == 0 and col0 % cb == 0
    pad = (CONV_WIDTH - 1) // 2

    def body(x_ref, w_ref, b_ref, o_ref):
        x = x_ref[...]
        tpos = lax.broadcasted_iota(jnp.int32, x.shape, 0)
        acc = jnp.broadcast_to(b_ref[...], x.shape)
        for k in range(CONV_WIDTH):
            acc = acc + w_ref[k:k + 1, :] * _shift_rows(x, k - pad, tpos)
        o_ref[...] = _silu(acc)

    return pl.pallas_call(
        body, name=name, grid=(chans // cb, batch),
        in_specs=[pl.BlockSpec((seq, cb), lambda j, b: (b, col0 // cb + j)),
                  pl.BlockSpec((8, cb), lambda j, b: (0, j)), pl.BlockSpec((1, cb), lambda j, b: (0, j))],
        out_specs=pl.BlockSpec((seq, cb), lambda j, b: (b, j)),
        out_shape=jax.ShapeDtypeStruct((batch * seq, chans), F32),
        compiler_params=_cparams(("parallel", "arbitrary")),
    )(p, w8, bias)


def _conv_bwd(p, w8, bias, du, *, col0, chans, batch, seq, name):
    cb = 256
    pad = (CONV_WIDTH - 1) // 2

    def body(x_ref, w_ref, b_ref, du_ref, dx_ref, dw_ref, db_ref):
        x = x_ref[...]
        tpos = lax.broadcasted_iota(jnp.int32, x.shape, 0)
        acc = jnp.broadcast_to(b_ref[...], x.shape)
        xs = []
        for k in range(CONV_WIDTH):
            xs.append(_shift_rows(x, k - pad, tpos))
            acc = acc + w_ref[k:k + 1, :] * xs[k]
        sg = jax.nn.sigmoid(acc)
        dacc = du_ref[...] * (sg * (1.0 + acc * (1.0 - sg)))
        dx = jnp.zeros_like(x)
        for k in range(CONV_WIDTH):
            dx = dx + w_ref[k:k + 1, :] * _shift_rows(dacc, pad - k, tpos)
        dx_ref[...] = dx.astype(dx_ref.dtype)

        @pl.when(pl.program_id(1) == 0)
        def _():
            dw_ref[...] = jnp.zeros_like(dw_ref)
            db_ref[...] = jnp.zeros_like(db_ref)

        for k in range(CONV_WIDTH):
            dw_ref[k:k + 1, :] += jnp.sum(dacc * xs[k], axis=0, keepdims=True)
        db_ref[...] += jnp.sum(dacc, axis=0, keepdims=True)

    return _pcall(
        body, name=name, grid=(chans // cb, batch),
        in_specs=[pl.BlockSpec((seq, cb), lambda j, b: (b, col0 // cb + j)),
                  pl.BlockSpec((8, cb), lambda j, b: (0, j)), pl.BlockSpec((1, cb), lambda j, b: (0, j)),
                  pl.BlockSpec((seq, cb), lambda j, b: (b, j))],
        out_specs=[pl.BlockSpec((seq, cb), lambda j, b: (b, j)), pl.BlockSpec((8, cb), lambda j, b: (0, j)),
                   pl.BlockSpec((1, cb), lambda j, b: (0, j))],
        out_shape=[jax.ShapeDtypeStruct((batch * seq, chans), BF16), jax.ShapeDtypeStruct((8, chans), F32),
                   jax.ShapeDtypeStruct((1, chans), F32)],
        sem=("parallel", "arbitrary"),
    )(p, w8, bias, du)


def _scan_specs(batch, nc, groups, hg, inner, reverse_order):
    t, n, hp = SSD_CHUNK, SSD_STATE, hg * SSD_HEAD_DIM
    xcb, ncb = inner // hp, inner // n

    def row(b, c):
        return b * nc + ((nc - 1 - c) if reverse_order else c)

    return dict(
        xs=pl.BlockSpec((t, hp), lambda g, b, c: (row(b, c), g)),
        bm=pl.BlockSpec((t, n), lambda g, b, c: (row(b, c), ncb + g)),
        cm=pl.BlockSpec((t, n), lambda g, b, c: (row(b, c), ncb + groups + g)),
        dtr=pl.BlockSpec((None, t, hg), lambda g, b, c: (g, row(b, c), 0)),
        dtr_t=pl.BlockSpec((None, hg, t), lambda g, b, c: (g, 0, row(b, c))),
        par=pl.BlockSpec((None, 1, hg), lambda g, b, c: (g, 0, 0)),
        par_t=pl.BlockSpec((None, hg, 1), lambda g, b, c: (g, 0, 0)),
        y=pl.BlockSpec((t, hp), lambda g, b, c: (row(b, c), g)),
        nrow=pl.BlockSpec((t, n), lambda g, b, c: (row(b, c), g)),
        st=pl.BlockSpec((None, None, n, hp), lambda g, b, c: (g, row(b, c), 0, 0)),
    )


def _scan_fwd(u, dtr, dtr_t, bias, bias_t, alog, alog_t, *, batch, seq, inner, reverse, name):
    groups, hg = dtr.shape[0], dtr.shape[2]
    nc = seq // SSD_CHUNK
    hp = hg * SSD_HEAD_DIM
    sp = _scan_specs(batch, nc, groups, hg, inner, reverse)

    def body(xs_ref, bm_ref, cm_ref, dtr_ref, dtrt_ref, b_ref, bt_ref, a_ref, at_ref, y_ref, st_ref, state):
        @pl.when(pl.program_id(2) == 0)
        def _():
            state[...] = jnp.zeros_like(state)

        st_in = state[...]
        st_ref[...] = st_in
        y, st_out = _ssd_chunk(st_in, xs_ref[...], bm_ref[...], cm_ref[...], dtr_ref[...], dtrt_ref[...],
                               b_ref[...], bt_ref[...], a_ref[...], at_ref[...], reverse=reverse)
        y_ref[...] = y
        state[...] = st_out

    return _pcall(
        body, name=name, grid=(groups, batch, nc),
        in_specs=[sp["xs"], sp["bm"], sp["cm"], sp["dtr"], sp["dtr_t"], sp["par"], sp["par_t"], sp["par"], sp["par_t"]],
        out_specs=[sp["y"], sp["st"]],
        out_shape=[jax.ShapeDtypeStruct((batch * seq, inner), F32),
                   jax.ShapeDtypeStruct((groups, batch * nc, SSD_STATE, hp), F32)],
        scratch_shapes=[pltpu.VMEM((SSD_STATE, hp), F32)],
        sem=("parallel", "arbitrary", "arbitrary"),
    )(u, u, u, dtr, dtr_t, bias, bias_t, alog, alog_t)


def _scan_bwd(u, dtr, dtr_t, bias, bias_t, alog, alog_t, st, dy, *, batch, seq, inner, reverse, name):
    groups, hg = dtr.shape[0], dtr.shape[2]
    nc = seq // SSD_CHUNK
    hp = hg * SSD_HEAD_DIM
    t = batch * seq
    sp = _scan_specs(batch, nc, groups, hg, inner, not reverse)
    f = functools.partial(_ssd_chunk, reverse=reverse)

    def body(xs_ref, bm_ref, cm_ref, dtr_ref, dtrt_ref, b_ref, bt_ref, a_ref, at_ref, st_ref, dy_ref,
             dxs_ref, dbm_ref, dcm_ref, ddtr_ref, ddtrt_ref, db_ref, dbt_ref, da_ref, dat_ref, dstate):
        first = (pl.program_id(1) == 0) & (pl.program_id(2) == 0)

        @pl.when(pl.program_id(2) == 0)
        def _():
            dstate[...] = jnp.zeros_like(dstate)

        @pl.when(first)
        def _():
            for r in (db_ref, dbt_ref, da_ref, dat_ref):
                r[...] = jnp.zeros_like(r)

        _, vjp = jax.vjp(f, st_ref[...], xs_ref[...], bm_ref[...], cm_ref[...], dtr_ref[...], dtrt_ref[...],
                         b_ref[...], bt_ref[...], a_ref[...], at_ref[...])
        dst, dxs, dbm, dcm, ddtr, ddtrt, db, dbt, da, dat = vjp((dy_ref[...], dstate[...]))
        dstate[...] = dst
        dxs_ref[...] = dxs
        dbm_ref[...] = dbm
        dcm_ref[...] = dcm
        ddtr_ref[...] = ddtr
        ddtrt_ref[...] = ddtrt
        db_ref[...] += db
        dbt_ref[...] += dbt
        da_ref[...] += da
        dat_ref[...] += dat

    gn = groups * SSD_STATE
    return _pcall(
        body, name=name, grid=(groups, batch, nc),
        in_specs=[sp["xs"], sp["bm"], sp["cm"], sp["dtr"], sp["dtr_t"], sp["par"], sp["par_t"], sp["par"], sp["par_t"],
                  sp["st"], sp["y"]],
        out_specs=[sp["y"], sp["nrow"], sp["nrow"], sp["dtr"], sp["dtr_t"], sp["par"], sp["par_t"], sp["par"], sp["par_t"]],
        out_shape=[jax.ShapeDtypeStruct((t, inner), F32), jax.ShapeDtypeStruct((t, gn), F32),
                   jax.ShapeDtypeStruct((t, gn), F32), jax.ShapeDtypeStruct(dtr.shape, F32),
                   jax.ShapeDtypeStruct(dtr_t.shape, F32), jax.ShapeDtypeStruct(bias.shape, F32),
                   jax.ShapeDtypeStruct(bias_t.shape, F32), jax.ShapeDtypeStruct(alog.shape, F32),
                   jax.ShapeDtypeStruct(alog_t.shape, F32)],
        scratch_shapes=[pltpu.VMEM((SSD_STATE, hp), F32)],
        sem=("parallel", "arbitrary", "arbitrary"),
    )(u, u, u, dtr, dtr_t, bias, bias_t, alog, alog_t, st, dy)


def _attn_load(ref, col, blk):
    return ref[pl.ds(pl.multiple_of(blk * QBLK, QBLK), QBLK), col * HEAD_DIM:(col + 1) * HEAD_DIM].astype(F32)


def _lane0(row):
    lane = lax.broadcasted_iota(jnp.int32, row.shape, 1)
    return jnp.sum(jnp.where(lane == 0, row, 0.0), axis=1, keepdims=True)


def _attn_fwd(rq, sinkx, *, batch, seq, dil, nbw, cb0, nh, rep, hw, want_lse, out_dtype, name):
    t, w = rq.shape
    ln = seq // dil
    nb = ln // QBLK
    bw = (rep + 2) * HEAD_DIM
    ow = nh * rep * HEAD_DIM
    has_sink = sinkx is not None
    rq3 = rq.reshape(batch, ln, dil * w)
    f = functools.partial(_attn_block, seq=ln, hw=hw, has_sink=has_sink)

    def body(*refs):
        if has_sink:
            blk_ref, sink_ref = refs[:2]
            outs = refs[2:]
        else:
            blk_ref, sink_ref = refs[0], None
            outs = refs[1:]
        o_ref = outs[0]
        lse_ref = outs[1] if want_lse else None
        g = pl.program_id(2)

        def qblock(i, carry):
            ip, inx = jnp.maximum(i - 1, 0), jnp.minimum(i + 1, nb - 1)
            k3 = jnp.concatenate([_attn_load(blk_ref, rep, ip), _attn_load(blk_ref, rep, i),
                                  _attn_load(blk_ref, rep, inx)], axis=0)
            v3 = jnp.concatenate([_attn_load(blk_ref, rep + 1, ip), _attn_load(blk_ref, rep + 1, i),
                                  _attn_load(blk_ref, rep + 1, inx)], axis=0)
            rows = pl.ds(pl.multiple_of(i * QBLK, QBLK), QBLK)
            for r in range(rep):
                sk = _lane0(sink_ref[pl.ds(g * rep + r, 1), :]) if has_sink else None
                o, lse = f(_attn_load(blk_ref, r, i), k3, v3, sk, i)
                o_ref[rows, r * HEAD_DIM:(r + 1) * HEAD_DIM] = o.astype(o_ref.dtype)
                if want_lse:
                    lse_ref[rows, r * HEAD_DIM:(r + 1) * HEAD_DIM] = jnp.broadcast_to(lse, o.shape)
            return carry

        lax.fori_loop(0, nb, qblock, 0)

    in_specs = [pl.BlockSpec((None, ln, bw), lambda b, r, h: (b, 0, r * nbw + cb0 + h))]
    ins = [rq3]
    if has_sink:
        in_specs.append(pl.BlockSpec(sinkx.shape, lambda b, r, h: (0, 0)))
        ins.append(sinkx)
    ospec = pl.BlockSpec((None, ln, rep * HEAD_DIM), lambda b, r, h: (b, 0, r * nh + h))
    out_shape = [jax.ShapeDtypeStruct((batch, ln, dil * ow), out_dtype)]
    out_specs = [ospec]
    if want_lse:
        out_shape.append(jax.ShapeDtypeStruct((batch, ln, dil * ow), F32))
        out_specs.append(ospec)
    res = _pcall(
        body, name=name, grid=(batch, dil, nh), in_specs=in_specs, out_specs=out_specs, out_shape=out_shape,
        sem=("parallel", "parallel", "parallel"),
    )(*ins)
    return [r.reshape(t, ow) for r in res]


def _attn_bwd(rq, sinkx, do, dlse, *, batch, seq, dil, nbw, cb0, nh, rep, hw, name):
    t, w = rq.shape
    ln = seq // dil
    nb = ln // QBLK
    bw = (rep + 2) * HEAD_DIM
    ow = nh * rep * HEAD_DIM
    has_sink = sinkx is not None
    has_lse = dlse is not None
    f = functools.partial(_attn_block, seq=ln, hw=hw, has_sink=has_sink)

    def body(*refs):
        refs = list(refs)
        blk_ref = refs.pop(0)
        sink_ref = refs.pop(0) if has_sink else None
        do_ref = refs.pop(0)
        dlse_ref = refs.pop(0) if has_lse else None
        d_ref = refs.pop(0)
        dsink_ref = refs.pop(0) if has_sink else None
        g = pl.program_id(2)
        d_ref[:, rep * HEAD_DIM:] = jnp.zeros((ln, 2 * HEAD_DIM), F32)
        if has_sink:
            @pl.when((pl.program_id(0) == 0) & (pl.program_id(1) == 0) & (g == 0))
            def _():
                dsink_ref[...] = jnp.zeros_like(dsink_ref)

        def qblock(i, carry):
            blks = (jnp.maximum(i - 1, 0), i, jnp.minimum(i + 1, nb - 1))
            k3 = jnp.concatenate([_attn_load(blk_ref, rep, bi) for bi in blks], axis=0)
            v3 = jnp.concatenate([_attn_load(blk_ref, rep + 1, bi) for bi in blks], axis=0)
            rows = pl.ds(pl.multiple_of(i * QBLK, QBLK), QBLK)
            dk3 = jnp.zeros_like(k3)
            dv3 = jnp.zeros_like(v3)
            for r in range(rep):
                cols = slice(r * HEAD_DIM, (r + 1) * HEAD_DIM)
                q = _attn_load(blk_ref, r, i)
                dov = do_ref[rows, cols]
                dl = dlse_ref[rows, cols] if has_lse else jnp.zeros_like(dov)
                if has_sink:
                    srow = sink_ref[pl.ds(g * rep + r, 1), :]
                    _, vjp = jax.vjp(lambda q_, k_, v_, s_: f(q_, k_, v_, _lane0(s_), i), q, k3, v3, srow)
                    dq, dk, dv, ds = vjp((dov, jnp.sum(dl, axis=1, keepdims=True)))
                    dsink_ref[pl.ds(g * rep + r, 1), :] += ds
                else:
                    _, vjp = jax.vjp(lambda q_, k_, v_: f(q_, k_, v_, None, i), q, k3, v3)
                    dq, dk, dv = vjp((dov, jnp.sum(dl, axis=1, keepdims=True)))
                d_ref[rows, cols] = dq
                dk3 = dk3 + dk
                dv3 = dv3 + dv
            for m, bi in enumerate(blks):
                brow = pl.ds(pl.multiple_of(bi * QBLK, QBLK), QBLK)
                d_ref[brow, rep * HEAD_DIM:(rep + 1) * HEAD_DIM] += dk3[m * QBLK:(m + 1) * QBLK]
                d_ref[brow, (rep + 1) * HEAD_DIM:] += dv3[m * QBLK:(m + 1) * QBLK]
            return carry

        lax.fori_loop(0, nb, qblock, 0)

    ospec = pl.BlockSpec((None, ln, rep * HEAD_DIM), lambda b, r, h: (b, 0, r * nh + h))
    in_specs = [pl.BlockSpec((None, ln, bw), lambda b, r, h: (b, 0, r * nbw + cb0 + h))]
    ins = [rq.reshape(batch, ln, dil * w)]
    if has_sink:
        in_specs.append(pl.BlockSpec(sinkx.shape, lambda b, r, h: (0, 0)))
        ins.append(sinkx)
    in_specs.append(ospec)
    ins.append(do.reshape(batch, ln, dil * ow))
    if has_lse:
        in_specs.append(ospec)
        ins.append(dlse.reshape(batch, ln, dil * ow))
    dw = nh * bw
    out_specs = [pl.BlockSpec((None, ln, bw), lambda b, r, h: (b, 0, r * nh + h))]
    out_shape = [jax.ShapeDtypeStruct((batch, ln, dil * dw), F32)]
    if has_sink:
        out_specs.append(pl.BlockSpec(sinkx.shape, lambda b, r, h: (0, 0)))
        out_shape.append(jax.ShapeDtypeStruct(sinkx.shape, F32))
    res = _pcall(
        body, name=name, grid=(batch, dil, nh), in_specs=in_specs, out_specs=out_specs, out_shape=out_shape,
        sem=("arbitrary", "arbitrary", "arbitrary"),
    )(*ins)
    return [res[0].reshape(t, dw)] + list(res[1:])


def _final_loss(x, g, target, *, name):
    d = x.shape[1]

    def fn(rv, pv):
        xv, tg = rv
        y, vjp = jax.vjp(_rms, xv, pv[0])
        err = y - tg
        dx, dg = vjp(err * (1.0 / d))
        loss = 0.5 * jnp.sum(err * err) * (1.0 / d)
        return [dx], [dg, jnp.full((1, LANE), loss, F32)]

    (dx,), (dg, loss) = _rowcall(fn, [(x, d, 0), (target, d, 0)], [g], [(d, F32)], [(1, d), (1, LANE)],
                                 tile=256, name=name)
    return dx, dg, loss


class _Layout:
    def __init__(self, d_model):
        self.d = d_model
        self.inner = SSD_HEADS * SSD_HEAD_DIM
        self.gn = SSD_GROUPS * SSD_STATE
        self.xbc = self.inner + 2 * self.gn
        self.ndt = 2 * SSD_HEADS
        self.ngrp = len(DIL_PATTERNS)
        self.dilw = DIL_HEADS * HEAD_DIM
        self.rqd = 3 * self.ngrp * self.dilw
        self.rep = WIN_Q_HEADS // WIN_KV_HEADS
        self.rqw = WIN_KV_HEADS * (self.rep + 2) * HEAD_DIM
        self.qw = WIN_Q_HEADS * HEAD_DIM
        self.kw = WIN_KV_HEADS * HEAD_DIM
        self.gates = N_BRANCH * d_model
        self.n_in = self.inner + self.xbc + self.ndt + self.rqd + self.qw + 2 * self.kw + self.gates
        self.o_gates = 0
        self.o_z = self.gates
        self.o_xbc = self.o_z + self.inner
        self.o_rqd = self.o_xbc + self.xbc
        self.o_rqw = self.o_rqd + self.rqd
        self.o_dt = self.o_rqw + self.rqw
        self.dtw = -(-(self.o_dt + self.ndt) // PAD_TO) * PAD_TO - self.o_dt
        self.width = self.o_dt + self.dtw
        assert self.o_z % self.inner == 0 and self.o_xbc % 256 == 0
        assert self.o_rqd % (3 * HEAD_DIM) == 0 and self.o_rqw % (3 * HEAD_DIM) == 0 and self.dtw % LANE == 0

    def split_points(self):
        sizes = (self.inner, self.xbc, self.ndt, self.rqd, self.qw, self.kw, self.kw, self.gates)
        pts, acc = [], 0
        for s in sizes:
            pts.append((acc, acc + s))
            acc += s
        return pts

    def permute_w(self, w):
        d = w.shape[0]
        z, xbc, dt, qkvd, qw, kw, vw, gates = [w[:, a:b] for a, b in self.split_points()]
        nhd = self.ngrp * DIL_HEADS
        qkvd = qkvd.reshape(d, 3, nhd, HEAD_DIM).transpose(0, 2, 1, 3).reshape(d, self.rqd)
        win = jnp.concatenate([qw.reshape(d, WIN_KV_HEADS, self.rep, HEAD_DIM),
                               kw.reshape(d, WIN_KV_HEADS, 1, HEAD_DIM),
                               vw.reshape(d, WIN_KV_HEADS, 1, HEAD_DIM)], axis=2).reshape(d, self.rqw)
        pad = jnp.zeros((d, self.dtw - self.ndt), w.dtype)
        return jnp.concatenate([gates, z, xbc, qkvd, win, dt, pad], axis=1)

    def unpermute_w(self, wp):
        d = wp.shape[0]
        gates = wp[:, :self.o_z]
        z = wp[:, self.o_z:self.o_xbc]
        xbc = wp[:, self.o_xbc:self.o_rqd]
        qkvd = wp[:, self.o_rqd:self.o_rqw]
        win = wp[:, self.o_rqw:self.o_dt].reshape(d, WIN_KV_HEADS, self.rep + 2, HEAD_DIM)
        dt = wp[:, self.o_dt:self.o_dt + self.ndt]
        nhd = self.ngrp * DIL_HEADS
        qkvd = qkvd.reshape(d, nhd, 3, HEAD_DIM).transpose(0, 2, 1, 3).reshape(d, self.rqd)
        qw = win[:, :, :self.rep].reshape(d, self.qw)
        kw = win[:, :, self.rep].reshape(d, self.kw)
        vw = win[:, :, self.rep + 1].reshape(d, self.kw)
        return jnp.concatenate([z, xbc, dt, qkvd, qw, kw, vw, gates], axis=1)


def _dt_layouts(pdt, dirn, batch_seq):
    hg = SSD_HEADS // SSD_GROUPS
    v = pdt[:, dirn * SSD_HEADS:(dirn + 1) * SSD_HEADS].reshape(batch_seq, SSD_GROUPS, hg)
    return v.transpose(1, 0, 2), v.transpose(1, 2, 0)


def _par_layouts(p):
    hg = SSD_HEADS // SSD_GROUPS
    v = p.reshape(SSD_GROUPS, hg)
    return v[:, None, :], v[:, :, None]


def _layer_fwd(x, lw, lay, tabs, batch, seq):
    d = lay.d
    sv = {"x": x}
    (h,), _ = _rowcall(lambda rv, pv: ([_rms(rv[0], pv[0])], []), [(x, d, 0)], [lw["g_mix"]], [(d, BF16)], [],
                       tile=256, name="norm_mix")
    p = _mm(h, lw["w_in"], name="proj_in")
    sv["h"], sv["p"] = h, p
    u = _conv_fwd(p, lw["conv_w8"], lw["conv_b"], col0=lay.o_xbc, chans=lay.xbc, batch=batch, seq=seq, name="conv_fwd")
    sv["u"] = u
    pdt = p[:, lay.o_dt:lay.o_dt + lay.ndt]
    ys, sv["st"], sv["dtl"] = [], [], []
    for dirn in range(2):
        dtr, dtr_t = _dt_layouts(pdt, dirn, batch * seq)
        bias, bias_t = _par_layouts(lw["dt_bias"][dirn])
        alog, alog_t = _par_layouts(lw["a_log"][dirn])
        y, st = _scan_fwd(u, dtr, dtr_t, bias, bias_t, alog, alog_t, batch=batch, seq=seq, inner=lay.inner,
                          reverse=bool(dirn), name="scan_fwd%d" % dirn)
        ys.append(y)
        sv["st"].append(st)
        sv["dtl"].append((dtr, dtr_t, bias, bias_t, alog, alog_t))
    sv["ys"] = ys
    inner = lay.inner
    (ya,), _ = _rowcall(lambda rv, pv: ([_ssd_post_f(*rv, *pv)], []),
                        [(ys[0], inner, 0), (ys[1], inner, 0), (u, inner, 0), (p, inner, lay.o_z // inner)],
                        [lw["d_skip_x"], lw["ssd_norm"]], [(inner, BF16)], [], tile=128, name="ssd_post")
    sv["ya"] = ya
    rqd = _rope(p, tabs, col0=lay.o_rqd, width=lay.rqd, seq=seq, group=0, inverse=False, out_dtype=BF16, name="rope_dil")
    rqw = _rope(p, tabs, col0=lay.o_rqw, width=lay.rqw, seq=seq, group=lay.rep + 2, inverse=False, out_dtype=BF16,
                name="rope_win")
    sv["rqd"], sv["rqw"] = rqd, rqw
    os_, ls_ = [], []
    for gi, (window, dil) in enumerate(DIL_PATTERNS):
        o, l = _attn_fwd(rqd, None, batch=batch, seq=seq, dil=dil, nbw=lay.ngrp * DIL_HEADS, cb0=gi * DIL_HEADS,
                         nh=DIL_HEADS, rep=1, hw=window // (2 * dil), want_lse=True, out_dtype=F32,
                         name="dil_fwd%d" % gi)
        os_.append(o)
        ls_.append(l)
    sv["os"], sv["ls"] = os_, ls_
    dw = lay.dilw
    (yb,), _ = _rowcall(lambda rv, pv: ([_combine_f(*rv)], []), [(a, dw, 0) for a in os_ + ls_], [], [(dw, BF16)], [],
                        tile=256, name="dil_combine")
    sv["yb"] = yb
    (yc,) = _attn_fwd(rqw, lw["sink_x"], batch=batch, seq=seq, dil=1, nbw=WIN_KV_HEADS, cb0=0, nh=WIN_KV_HEADS,
                      rep=lay.rep, hw=WIN_HALF, want_lse=False, out_dtype=BF16, name="win_fwd")
    sv["yc"] = yc
    ma = _mm(ya, lw["w_a"], name="proj_a")
    mb = _mm(yb, lw["w_b"], name="proj_b")
    mc = _mm(yc, lw["w_c"], name="proj_c")
    sv["mabc"] = (ma, mb, mc)
    (mg,), _ = _rowcall(lambda rv, pv: ([_merge_f(*rv)], []),
                        [(ma, d, 0), (mb, d, 0), (mc, d, 0), (p, d, 0), (p, d, 1), (p, d, 2)], [], [(d, BF16)], [],
                        tile=256, name="merge")
    sv["mg"] = mg
    x1 = _mm(mg, lw["w_out"], add=x, name="proj_out")
    sv["x1"] = x1
    (hm,), _ = _rowcall(lambda rv, pv: ([_rms(rv[0], pv[0])], []), [(x1, d, 0)], [lw["g_mlp"]], [(d, BF16)], [],
                        tile=256, name="norm_mlp")
    up = _mm(hm, lw["w_up"], name="mlp_up")
    act = _map2d(lambda a: jnp.square(jnp.maximum(a, 0.0)), [up], BF16, name="relu2")
    sv["hm"], sv["up"], sv["act"] = hm, up, act
    x2 = _mm(act, lw["w_down"], add=x1, name="mlp_down")
    return x2, sv


def _layer_bwd(dxo, sv, lw, lay, tabs, batch, seq, post):
    d = lay.d
    inner = lay.inner
    gs = {}
    dact = _mm(dxo, lw["w_down"], tb=True, name="mlp_down_dx")
    post("w_down", _mm(sv["act"], dxo, ta=True, name="mlp_down_dw"))
    dup = _map2d(lambda g, a: g * (2.0 * jnp.maximum(a, 0.0)), [dact, sv["up"]], BF16, name="relu2_bwd")
    dhm = _mm(dup, lw["w_up"], tb=True, name="mlp_up_dx")
    post("w_up", _mm(sv["hm"], dup, ta=True, name="mlp_up_dw"))

    def norm_bwd(rv, pv):
        xv, dh, dres = rv
        _, vjp = jax.vjp(_rms, xv, pv[0])
        dx, dg = vjp(dh)
        return [dx + dres], [dg]

    (dx1,), (gs["g_mlp"],) = _rowcall(norm_bwd, [(sv["x1"], d, 0), (dhm, d, 0), (dxo, d, 0)], [lw["g_mlp"]],
                                      [(d, F32)], [(1, d)], tile=128, name="norm_mlp_bwd")
    dmg = _mm(dx1, lw["w_out"], tb=True, name="proj_out_dx")
    post("w_out", _mm(sv["mg"], dx1, ta=True, name="proj_out_dw"))
    ma, mb, mc = sv["mabc"]
    p = sv["p"]

    def merge_bwd(rv, pv):
        _, vjp = jax.vjp(_merge_f, *rv[:6])
        da, db, dc, d0, d1, d2 = vjp(rv[6])
        return [da, db, dc, jnp.concatenate([d0, d1, d2], axis=1)], []

    (dma, dmb, dmc, dgl), _ = _rowcall(
        merge_bwd, [(ma, d, 0), (mb, d, 0), (mc, d, 0), (p, d, 0), (p, d, 1), (p, d, 2), (dmg, d, 0)], [],
        [(d, BF16), (d, BF16), (d, BF16), (lay.gates, BF16)], [], tile=128, name="merge_bwd")
    dya = _mm(dma, lw["w_a"], tb=True, name="proj_a_dx")
    post("w_a", _mm(sv["ya"], dma, ta=True, name="proj_a_dw"))
    dyb = _mm(dmb, lw["w_b"], tb=True, name="proj_b_dx")
    post("w_b", _mm(sv["yb"], dmb, ta=True, name="proj_b_dw"))
    dyc = _mm(dmc, lw["w_c"], tb=True, name="proj_c_dx")
    post("w_c", _mm(sv["yc"], dmc, ta=True, name="proj_c_dw"))
    drqw, dsink = _attn_bwd(sv["rqw"], lw["sink_x"], dyc, None, batch=batch, seq=seq, dil=1, nbw=WIN_KV_HEADS, cb0=0,
                            nh=WIN_KV_HEADS, rep=lay.rep, hw=WIN_HALF, name="win_bwd")
    gs["sink"] = jnp.sum(dsink, axis=1)
    dw = lay.dilw

    def combine_bwd(rv, pv):
        _, vjp = jax.vjp(_combine_f, *rv[:6])
        return list(vjp(rv[6])), []

    dol, _ = _rowcall(combine_bwd, [(a, dw, 0) for a in sv["os"] + sv["ls"]] + [(dyb, dw, 0)], [],
                      [(dw, F32)] * 6, [], tile=256, name="dil_combine_bwd")
    drq = []
    for gi, (window, dil) in enumerate(DIL_PATTERNS):
        (dg_,) = _attn_bwd(sv["rqd"], None, dol[gi], dol[3 + gi], batch=batch, seq=seq, dil=dil,
                           nbw=lay.ngrp * DIL_HEADS, cb0=gi * DIL_HEADS, nh=DIL_HEADS, rep=1,
                           hw=window // (2 * dil), name="dil_bwd%d" % gi)
        drq.append(_rope(dg_, tabs, col0=0, width=dg_.shape[1], seq=seq, group=0, inverse=True, out_dtype=BF16,
                         name="rope_dil_bwd%d" % gi))
    drq.append(_rope(drqw, tabs, col0=0, width=lay.rqw, seq=seq, group=lay.rep + 2, inverse=True, out_dtype=BF16,
                     name="rope_win_bwd"))
    u, ys = sv["u"], sv["ys"]

    def post_bwd(rv, pv):
        _, vjp = jax.vjp(_ssd_post_f, *rv[:4], *pv)
        dyf, _, dxs, dz, dsk, dgn = vjp(rv[4])
        return [dyf, dxs, dz], [dsk, dgn]

    (dy, dxs_post, dz), (dsk, gs["ssd_norm"]) = _rowcall(
        post_bwd, [(ys[0], inner, 0), (ys[1], inner, 0), (u, inner, 0), (p, inner, lay.o_z // inner), (dya, inner, 0)],
        [lw["d_skip_x"], lw["ssd_norm"]], [(inner, F32), (inner, F32), (inner, BF16)], [(1, inner), (1, inner)],
        tile=128, name="ssd_post_bwd")
    gs["d_skip"] = jnp.sum(dsk.reshape(SSD_HEADS, SSD_HEAD_DIM), axis=1)
    dxs, dbm, dcm = dxs_post, None, None
    ddt, gdb, gda = [], [], []
    for dirn in range(2):
        dtr, dtr_t, bias, bias_t, alog, alog_t = sv["dtl"][dirn]
        r = _scan_bwd(u, dtr, dtr_t, bias, bias_t, alog, alog_t, sv["st"][dirn], dy, batch=batch, seq=seq,
                      inner=inner, reverse=bool(dirn), name="scan_bwd%d" % dirn)
        dxs = dxs + r[0]
        dbm = r[1] if dbm is None else dbm + r[1]
        dcm = r[2] if dcm is None else dcm + r[2]
        ddt.append((r[3] + r[4].transpose(0, 2, 1)).transpose(1, 0, 2).reshape(batch * seq, SSD_HEADS))
        gdb.append((r[5][:, 0, :] + r[6][:, :, 0]).reshape(SSD_HEADS))
        gda.append((r[7][:, 0, :] + r[8][:, :, 0]).reshape(SSD_HEADS))
    gs["dt_bias"] = jnp.stack(gdb)
    gs["a_log"] = jnp.stack(gda)
    du = jnp.concatenate([dxs, dbm, dcm], axis=1)
    dxbc, dcw, dcb = _conv_bwd(p, lw["conv_w8"], lw["conv_b"], du, col0=lay.o_xbc, chans=lay.xbc, batch=batch,
                               seq=seq, name="conv_bwd")
    gs["conv_w"] = dcw[:CONV_WIDTH]
    gs["conv_b"] = dcb[0]
    ddtp = jnp.concatenate(ddt + [jnp.zeros((batch * seq, lay.dtw - lay.ndt), F32)], axis=1).astype(BF16)
    dp = jnp.concatenate([dgl, dz, dxbc] + drq + [ddtp], axis=1)
    hd = d // 2
    for part in range(2):
        post(("w_in", part), _mm(sv["h"][:, part * hd:(part + 1) * hd], dp, ta=True, name="proj_in_dw"))
    dh = _mm(dp, lw["w_in"], tb=True, name="proj_in_dx")
    (dx,), (gs["g_mix"],) = _rowcall(norm_bwd, [(sv["x"], d, 0), (dh, d, 0), (dx1, d, 0)], [lw["g_mix"]],
                                     [(d, F32)], [(1, d)], tile=128, name="norm_mix_bwd")
    return dx, gs


_SHARDED = ("w_in", "w_a", "w_b", "w_c", "w_out", "w_up", "w_down")
_COL_SHARDED = ("w_in", "w_b", "w_up")
_SMALL = ("g_mix", "conv_b", "dt_bias", "a_log", "d_skip", "ssd_norm", "sink", "g_mlp")


def _gathered_to_full(name, g):
    n, r, c = g.shape
    if name in _COL_SHARDED:
        return g.transpose(1, 0, 2).reshape(r, n * c)
    return g.reshape(n * r, c)


def _full_to_slots(name, w):
    r, c = w.shape
    if name in _COL_SHARDED:
        return w.reshape(r, N_DEV, c // N_DEV).transpose(1, 0, 2)
    return w.reshape(N_DEV, r // N_DEV, c)


class _LayerWeights:
    def __init__(self, sched, layer, lay, small):
        self.sched, self.layer, self.lay, self.vals = sched, layer, lay, dict(small)

    def __getitem__(self, name):
        if name not in self.vals:
            full = _gathered_to_full(name, self.sched.get(("w", name, self.layer)))
            self.vals[name] = self.lay.permute_w(full) if name == "w_in" else full
        return self.vals[name]


def _pack(parts):
    flat = jnp.concatenate([p.reshape(-1).astype(F32) for p in parts])
    n = flat.shape[0]
    rows = -(-n // (8 * LANE)) * 8
    return jnp.pad(flat, (0, rows * LANE - n)).reshape(rows, LANE)


def _unpack(buf, shapes):
    flat = buf.reshape(-1)
    out, off = [], 0
    for s in shapes:
        n = math.prod(s)
        out.append(flat[off:off + n].reshape(s))
        off += n
    return out


def kernel(x, g_mix, w_in, conv_w, conv_b, dt_bias, a_log, d_skip, ssd_norm, w_a, w_b, w_c, sink, w_out, g_mlp, w_up, w_down, g_final, loss_target, m_g_mix, m_w_in, m_conv_w, m_conv_b, m_dt_bias, m_a_log, m_d_skip, m_ssd_norm, m_w_a, m_w_b, m_w_c, m_sink, m_w_out, m_g_mlp, m_w_up, m_w_down, m_g_final, v_g_mix, v_w_in, v_conv_w, v_conv_b, v_dt_bias, v_a_log, v_d_skip, v_ssd_norm, v_w_a, v_w_b, v_w_c, v_sink, v_w_out, v_g_mlp, v_w_up, v_w_down, v_g_final):
    batch, seq, d = x.shape
    depth = g_mix.shape[0]
    lay = _Layout(d)
    assert lay.n_in == w_in.shape[2] * N_DEV
    wts = dict(g_mix=g_mix, w_in=w_in, conv_w=conv_w, conv_b=conv_b, dt_bias=dt_bias, a_log=a_log, d_skip=d_skip,
               ssd_norm=ssd_norm, w_a=w_a, w_b=w_b, w_c=w_c, sink=sink, w_out=w_out, g_mlp=g_mlp, w_up=w_up,
               w_down=w_down, g_final=g_final)
    mom = dict(g_mix=m_g_mix, w_in=m_w_in, conv_w=m_conv_w, conv_b=m_conv_b, dt_bias=m_dt_bias, a_log=m_a_log,
               d_skip=m_d_skip, ssd_norm=m_ssd_norm, w_a=m_w_a, w_b=m_w_b, w_c=m_w_c, sink=m_sink, w_out=m_w_out,
               g_mlp=m_g_mlp, w_up=m_w_up, w_down=m_w_down, g_final=m_g_final)
    var = dict(g_mix=v_g_mix, w_in=v_w_in, conv_w=v_conv_w, conv_b=v_conv_b, dt_bias=v_dt_bias, a_log=v_a_log,
               d_skip=v_d_skip, ssd_norm=v_ssd_norm, w_a=v_w_a, w_b=v_w_b, w_c=v_w_c, sink=v_sink, w_out=v_w_out,
               g_mlp=v_g_mlp, w_up=v_w_up, w_down=v_w_down, g_final=v_g_final)
    me = 4 * lax.axis_index("x") + 2 * lax.axis_index("y") + lax.axis_index("c")

    global _SCHED
    sched = _SCHED = _Sched()
    (gconv,) = _exchange([conv_w], scatter=False, name="gather_conv_w")
    conv_full = gconv.transpose(1, 2, 0, 3).reshape(depth, CONV_WIDTH, -1)
    for l in range(depth):
        for n in _SHARDED:
            sched.post(("w", n, l), wts[n][l].astype(BF16), scatter=False)

    tabs = _rope_tables(seq)
    t = batch * seq
    xf = x.reshape(t, d)
    layers = []
    for l in range(depth):
        small = dict(
            g_mix=g_mix[l][None], g_mlp=g_mlp[l][None], ssd_norm=ssd_norm[l][None], conv_b=conv_b[l][None],
            dt_bias=dt_bias[l], a_log=a_log[l],
            d_skip_x=jnp.repeat(d_skip[l], SSD_HEAD_DIM)[None],
            sink_x=jnp.broadcast_to(sink[l][:, None], (WIN_Q_HEADS, LANE)),
            conv_w8=jnp.pad(conv_full[l], ((0, 8 - CONV_WIDTH), (0, 0))))
        layers.append(_LayerWeights(sched, l, lay, small))

    saves = []
    h = xf
    for l in range(depth):
        h, sv = _layer_fwd(h, layers[l], lay, tabs, batch, seq)
        saves.append(sv)
    dx, dgf, loss = _final_loss(h, g_final[None], loss_target.reshape(t, d), name="final_loss")

    gss = [None] * depth
    for l in reversed(range(depth)):
        def post(n, g, l=l):
            if isinstance(n, tuple):
                key, n, g = ("g", "w_in", l, n[1]), "w_in", lay.unpermute_w(g)
            else:
                key = ("g", n, l)
            sched.post(key, _full_to_slots(n, g).astype(BF16), scatter=True)

        dx, gss[l] = _layer_bwd(dx, saves[l], layers[l], lay, tabs, batch, seq, post)
    grad_x = dx.reshape(batch, seq, d)

    small_parts = [jnp.stack([gss[l][n] for l in range(depth)]) for n in _SMALL]
    small_parts += [dgf, jnp.stack([gss[l]["conv_w"] for l in range(depth)]), loss[0, :1]]
    small_shapes = [p.shape for p in small_parts]
    (rs,) = _exchange([_pack(small_parts)], scatter=False, name="gather_small")
    red = _unpack(_sum_slots(rs, name="sum_small"), small_shapes)
    gsmall = dict(zip(list(_SMALL) + ["g_final"], red[:len(_SMALL) + 1]))
    gconv_full, loss_sum = red[-2], red[-1]
    cshard = conv_w.shape[2]
    gsmall["conv_w"] = lax.dynamic_slice_in_dim(gconv_full, me * cshard, cshard, axis=2)

    out = {}
    rep_names = list(_SMALL) + ["g_final"]
    rep_shapes = [wts[n].shape for n in rep_names]
    res = _adamw(_pack([wts[n] for n in rep_names]), [_pack([gsmall[n] for n in rep_names])[None]],
                 _pack([mom[n] for n in rep_names]), _pack([var[n] for n in rep_names]), name="adamw_small")
    unp = [_unpack(a, rep_shapes) for a in res]
    for i, n in enumerate(rep_names):
        out[n] = [unp[k][i] for k in range(4)]
    cs2 = (depth * CONV_WIDTH, cshard)
    res = _adamw(conv_w.reshape(cs2), [gsmall["conv_w"].reshape((1,) + cs2)], m_conv_w.reshape(cs2),
                 v_conv_w.reshape(cs2), name="adamw_conv_w")
    out["conv_w"] = [a.reshape(conv_w.shape) for a in res]
    for n in ("w_down", "w_up", "w_out", "w_a", "w_b", "w_c", "w_in"):
        shp = wts[n].shape
        r2 = (shp[0] * shp[1], shp[2])
        if n == "w_in":
            recvs = [sched.get(("g", n, l, part)) for l in range(depth) for part in range(2)]
        else:
            recvs = [sched.get(("g", n, l)) for l in range(depth)]
        res = _adamw(wts[n].reshape(r2), recvs, mom[n].reshape(r2), var[n].reshape(r2), name="adamw_" + n)
        out[n] = [a.reshape(shp) for a in res]

    order = ["g_mix", "w_in", "conv_w", "conv_b", "dt_bias", "a_log", "d_skip", "ssd_norm", "w_a", "w_b", "w_c",
             "sink", "w_out", "g_mlp", "w_up", "w_down", "g_final"]
    outs = [loss_sum.reshape(()), grad_x]
    for k in range(4):
        outs += [out[n][k] for n in order]
    return tuple(outs)
```

```python
import functools
import math

import jax
import jax.numpy as jnp
from jax import lax
from jax.experimental import pallas as pl
from jax.experimental.pallas import tpu as pltpu

F32 = jnp.float32
BF16 = jnp.bfloat16
HI = lax.Precision.HIGHEST
MESH = pl.DeviceIdType.MESH
N_DEV = 8

SSD_HEADS = 32
SSD_HEAD_DIM = 64
SSD_GROUPS = 8
SSD_STATE = 128
SSD_CHUNK = 128
CONV_WIDTH = 5
HEAD_DIM = 128
ROPE_DIM = 32
ROPE_THETA = 500000.0
DIL_PATTERNS = ((128, 1), (512, 4), (2048, 16))
DIL_HEADS = 8
WIN_Q_HEADS = 16
WIN_KV_HEADS = 4
WIN_HALF = 128
N_BRANCH = 3
EPS = 1e-6
NEG_INF = -1e30
ADAM_LR = 0.001
ADAM_B1 = 0.9
ADAM_B2 = 0.999
ADAM_EPS = 1e-08
ADAM_WD = 0.01
ADAM_STEP = 10

LANE = 128
QBLK = 128
VMEM_LIMIT = 56 * 1024 * 1024
PAD_TO = 512
MM_VMEM_BUDGET = 40 * 1024 * 1024


def _cparams(sem=None):
    return pltpu.CompilerParams(dimension_semantics=sem, vmem_limit_bytes=VMEM_LIMIT)


PIECE_BYTES = 800 * 1024
US_PER_PIECE_BYTE = 8.8e-5
MAX_PIECES = 8
CARRIER_US = {
    "proj_in": 450, "proj_in_dx": 520, "proj_in_dw": 285, "scan_fwd0": 248, "scan_fwd1": 248, "scan_bwd0": 555,
    "scan_bwd1": 555, "win_fwd": 163, "win_bwd": 414, "dil_fwd0": 110, "dil_fwd1": 110, "dil_fwd2": 181,
    "dil_bwd0": 243, "dil_bwd1": 243, "dil_bwd2": 250, "mlp_up": 155, "mlp_down": 171, "mlp_up_dx": 170,
    "mlp_up_dw": 170, "mlp_down_dx": 162, "mlp_down_dw": 189, "conv_bwd_x": 75, "rope_dil": 139,
    "adamw_w_up": 63, "adamw_w_down": 62,
}


class _Piece:
    def __init__(self, key, row0, rows, scatter, est):
        self.key, self.row0, self.rows, self.scatter, self.est = key, row0, rows, scatter, est


class _Sched:
    def __init__(self):
        self.queue, self.src, self.dst = [], {}, {}

    def post(self, key, src, scatter):
        r, c = src.shape[-2:]
        self.src[key] = src
        self.dst[key] = lax.empty((N_DEV, r, c), src.dtype)
        row_bytes = c * src.dtype.itemsize
        pr = r
        while pr * row_bytes > PIECE_BYTES and pr % 32 == 0:
            pr //= 2
        for row0 in range(0, r, pr):
            self.queue.append(_Piece(key, row0, pr, scatter, pr * row_bytes * US_PER_PIECE_BYTE))

    def take(self, name):
        budget = CARRIER_US.get(name)
        out, used = [], 0.0
        while budget and self.queue and len(out) < MAX_PIECES and used + self.queue[0].est <= 1.1 * budget:
            used += self.queue[0].est
            out.append(self.queue.pop(0))
        return out

    def get(self, key):
        last = max([i for i, p in enumerate(self.queue) if p.key == key], default=-1)
        while last >= 0:
            n = min(last + 1, 4 * MAX_PIECES)
            pieces, self.queue = self.queue[:n], self.queue[n:]
            _exchange_pieces(self, pieces, name="exchange_flush")
            last -= n
        return self.dst[key]


_SCHED = None


def _piece_copies(pieces, keys, src_refs, dst_refs, send_sems, recv_sems, loc_sems):
    x, y, c = lax.axis_index("x"), lax.axis_index("y"), lax.axis_index("c")
    me = 4 * x + 2 * y + c
    cps = []
    for t, p in enumerate(pieces):
        ki = keys.index(p.key)
        rows = pl.ds(p.row0, p.rows)
        for j in range(1, N_DEV):
            px = (1 - x) if (j >> 2) & 1 else x
            py = (1 - y) if (j >> 1) & 1 else y
            pc = (1 - c) if j & 1 else c
            src = src_refs[ki].at[4 * px + 2 * py + pc, rows] if p.scatter else src_refs[ki].at[rows]
            cps.append(pltpu.make_async_remote_copy(
                src_ref=src, dst_ref=dst_refs[ki].at[me, rows], send_sem=send_sems.at[t * 7 + j - 1],
                recv_sem=recv_sems.at[t * 7 + j - 1], device_id=(px, py, pc), device_id_type=MESH))
        src = src_refs[ki].at[me, rows] if p.scatter else src_refs[ki].at[rows]
        cps.append(pltpu.make_async_copy(src, dst_refs[ki].at[me, rows], loc_sems.at[t]))
    return cps


def _carry_call(sched, pieces, body, *, name, grid, in_specs, out_specs, out_shape, scratch_shapes, ins, aliases=None):
    keys = []
    for p in pieces:
        if p.key not in keys:
            keys.append(p.key)
    n_in, n_out, nk, npc = len(ins), len(out_shape), len(keys), len(pieces)
    n_scr = len(scratch_shapes)

    def wrapped(*refs):
        in_refs = refs[:n_in]
        src_refs = refs[n_in:n_in + nk]
        out_refs = refs[n_in + 2 * nk:n_in + 2 * nk + n_out]
        dst_refs = refs[n_in + 2 * nk + n_out:n_in + 3 * nk + n_out]
        scr = refs[n_in + 3 * nk + n_out:]
        inner_scr, sems = scr[:n_scr], scr[n_scr:]
        if grid:
            pids = [pl.program_id(a) for a in range(len(grid))]
            first = functools.reduce(lambda u, v: u & v, [q == 0 for q in pids])
            last = functools.reduce(lambda u, v: u & v, [q == g - 1 for q, g in zip(pids, grid)])

            @pl.when(first)
            def _():
                for cp in _piece_copies(pieces, keys, src_refs, dst_refs, *sems):
                    cp.start()

            body(*in_refs, *out_refs, *inner_scr)

            @pl.when(last)
            def _():
                for cp in _piece_copies(pieces, keys, src_refs, dst_refs, *sems):
                    cp.wait()
        else:
            cps = _piece_copies(pieces, keys, src_refs, dst_refs, *sems)
            for cp in cps:
                cp.start()
            for cp in cps:
                cp.wait()

    anyspec = pl.BlockSpec(memory_space=pl.ANY)
    dsts = [sched.dst[k] for k in keys]
    kwargs = dict(grid=grid) if grid else {}
    res = pl.pallas_call(
        wrapped, name=name, in_specs=list(in_specs) + [anyspec] * (2 * nk), out_specs=list(out_specs) + [anyspec] * nk,
        out_shape=list(out_shape) + [jax.ShapeDtypeStruct(d.shape, d.dtype) for d in dsts],
        input_output_aliases={**(aliases or {}), **{n_in + nk + i: n_out + i for i in range(nk)}},
        scratch_shapes=list(scratch_shapes) + [pltpu.SemaphoreType.DMA((7 * npc,)), pltpu.SemaphoreType.DMA((7 * npc,)),
                                               pltpu.SemaphoreType.DMA((npc,))],
        compiler_params=pltpu.CompilerParams(dimension_semantics=("arbitrary",) * len(grid) if grid else None,
                                             vmem_limit_bytes=VMEM_LIMIT, has_side_effects=True),
        **kwargs,
    )(*ins, *[sched.src[k] for k in keys], *dsts)
    for i, k in enumerate(keys):
        sched.dst[k] = res[n_out + i]
    return list(res[:n_out])


def _exchange_pieces(sched, pieces, *, name):
    _carry_call(sched, pieces, None, name=name, grid=(), in_specs=[], out_specs=[], out_shape=[], scratch_shapes=[],
                ins=[])


def _pcall(body, *, name, grid, in_specs, out_specs, out_shape, scratch_shapes=(), sem=None, into=None):
    single = not isinstance(out_shape, (list, tuple))
    out_shape_l = [out_shape] if single else list(out_shape)
    out_specs_l = [out_specs] if single else list(out_specs)
    into = into or {}

    def run(*ins):
        n0, nb = len(ins), len(into)
        specs = list(in_specs) + [pl.BlockSpec(memory_space=pl.ANY)] * nb
        aliases = {n0 + k: oi for k, oi in enumerate(into)}
        for oi, buf in into.items():
            out_shape_l[oi] = jax.ShapeDtypeStruct(buf.shape, buf.dtype)
        kbody = (lambda *refs: body(*refs[:n0], *refs[n0 + nb:])) if nb else body
        args = list(ins) + list(into.values())
        pieces = _SCHED.take(name) if _SCHED is not None else []
        if pieces:
            res = _carry_call(_SCHED, pieces, kbody, name=name, grid=grid, in_specs=specs, out_specs=out_specs_l,
                              out_shape=out_shape_l, scratch_shapes=list(scratch_shapes), ins=args, aliases=aliases)
        else:
            res = pl.pallas_call(kbody, name=name, grid=grid, in_specs=specs, out_specs=out_specs_l,
                                 out_shape=out_shape_l, scratch_shapes=list(scratch_shapes),
                                 input_output_aliases=aliases, compiler_params=_cparams(sem))(*args)
        return res[0] if single else list(res)

    return run


def _pick(dim, cands):
    for c in cands:
        if dim % c == 0:
            return c
    return dim


def _mm_tiles(m, n, k, a_bytes, b_bytes, o_bytes, has_add):
    tm = _pick(m, (1024, 512, 256, 128))
    tn = _pick(n, (1024, 1792, 512, 256, 128))
    for tk in (2048, 1792, 1024, 896, 512, 256, 128):
        if k % tk:
            continue
        need = 2 * (tm * tk * a_bytes + tk * tn * b_bytes + tm * tn * (o_bytes + (4 if has_add else 0)))
        need += tm * tn * 4 if k // tk > 1 else 0
        if need <= MM_VMEM_BUDGET:
            return tm, tn, tk
    return tm, tn, _pick(k, (128,))


def _mm(a, b, *, ta=False, tb=False, out_dtype=F32, add=None, post=None, name):
    m, k = (a.shape[1], a.shape[0]) if ta else a.shape
    k2, n = (b.shape[1], b.shape[0]) if tb else b.shape
    assert k == k2, (a.shape, b.shape, ta, tb)
    pfn, pins, pdts = post if post is not None else (None, [], [out_dtype])
    o_bytes = sum(jnp.dtype(dt).itemsize for dt in pdts) + sum(e.dtype.itemsize for e in pins)
    tm, tn, tk = _mm_tiles(m, n, k, a.dtype.itemsize, b.dtype.itemsize, o_bytes, add is not None)
    assert m % tm == 0 and n % tn == 0 and k % tk == 0, (m, n, k, tm, tn, tk)
    nk = k // tk
    a_spec = pl.BlockSpec((tk, tm), lambda i, j, kk: (kk, i)) if ta else pl.BlockSpec((tm, tk), lambda i, j, kk: (i, kk))
    b_spec = pl.BlockSpec((tn, tk), lambda i, j, kk: (j, kk)) if tb else pl.BlockSpec((tk, tn), lambda i, j, kk: (kk, j))
    o_spec = pl.BlockSpec((tm, tn), lambda i, j, kk: (i, j))
    dims = (((0 if ta else 1,), (1 if tb else 0,)), ((), ()))
    has_add = add is not None

    nx, no = (1 if has_add else 0) + len(pins), len(pdts)

    def body(*refs):
        a_ref, b_ref = refs[:2]
        x_refs = refs[2:2 + nx]
        o_refs = refs[2 + nx:2 + nx + no]
        part = lax.dot_general(a_ref[...].astype(BF16), b_ref[...].astype(BF16), dims, preferred_element_type=F32)

        def finish(r):
            if has_add:
                r = r + x_refs[0][...]
            outs = pfn(r, *[x[...] for x in x_refs[1 if has_add else 0:]]) if pfn else (r,)
            for o_ref, v in zip(o_refs, outs):
                o_ref[...] = v.astype(o_ref.dtype)

        if nk == 1:
            finish(part)
            return
        acc_ref = refs[-1]
        kk = pl.program_id(2)

        @pl.when(kk == 0)
        def _():
            acc_ref[...] = part

        @pl.when(kk > 0)
        def _():
            acc_ref[...] += part

        @pl.when(kk == nk - 1)
        def _():
            finish(acc_ref[...])

    ins = [a, b] + ([add] if has_add else []) + list(pins)
    specs = [a_spec, b_spec] + [o_spec] * nx
    res = _pcall(
        body, name=name, grid=(m // tm, n // tn, nk), in_specs=specs, out_specs=[o_spec] * no,
        out_shape=[jax.ShapeDtypeStruct((m, n), dt) for dt in pdts],
        scratch_shapes=[pltpu.VMEM((tm, tn), F32)] if nk > 1 else [],
        sem=("parallel", "parallel", "arbitrary"),
    )(*ins)
    return res if post is not None else res[0]


def _rowcall(fn, rows, pars, row_outs, par_outs, *, tile, name):
    t = rows[0][0].shape[0]
    tile = min(tile, t)
    assert t % tile == 0
    nr, npar, nro, npo = len(rows), len(pars), len(row_outs), len(par_outs)
    in_specs = [pl.BlockSpec((tile, c), functools.partial(lambda i, cb: (i, cb), cb=cb)) for (_, c, cb) in rows]
    in_specs += [pl.BlockSpec(p.shape, lambda i: (0, 0)) for p in pars]
    into = {k: ro[2] for k, ro in enumerate(row_outs) if len(ro) == 4}
    out_specs = [pl.BlockSpec((tile, ro[0]), functools.partial(lambda i, cb: (i, cb), cb=ro[3] if len(ro) == 4 else 0))
                 for ro in row_outs]
    out_specs += [pl.BlockSpec(s, lambda i: (0, 0)) for s in par_outs]
    out_shape = [jax.ShapeDtypeStruct((t, ro[0]), ro[1]) for ro in row_outs]
    out_shape += [jax.ShapeDtypeStruct(s, F32) for s in par_outs]

    def body(*refs):
        rv = [r[...] for r in refs[:nr]]
        pv = [r[...] for r in refs[nr:nr + npar]]
        ro_refs = refs[nr + npar:nr + npar + nro]
        po_refs = refs[nr + npar + nro:]
        ro, po = fn(rv, pv)
        for ref, v in zip(ro_refs, ro):
            ref[...] = v.astype(ref.dtype)
        if npo:
            @pl.when(pl.program_id(0) == 0)
            def _():
                for ref in po_refs:
                    ref[...] = jnp.zeros_like(ref)
            for ref, v in zip(po_refs, po):
                ref[...] += v

    res = _pcall(
        body, name=name, grid=(t // tile,), in_specs=in_specs, out_specs=out_specs, out_shape=out_shape,
        sem=("arbitrary",), into=into,
    )(*[r[0] for r in rows], *pars)
    return list(res[:nro]), list(res[nro:])


def _map2d(fn, ins, out_dtype, *, name, tile=256, cw=2048):
    t, w = ins[0].shape
    tile, cw = min(tile, t), min(cw, w)
    assert t % tile == 0 and w % cw == 0

    def body(*refs):
        refs[-1][...] = fn(*[r[...] for r in refs[:-1]]).astype(out_dtype)

    spec = pl.BlockSpec((tile, cw), lambda i, j: (i, j))
    return pl.pallas_call(
        body, name=name, grid=(t // tile, w // cw), in_specs=[spec] * len(ins), out_specs=spec,
        out_shape=jax.ShapeDtypeStruct((t, w), out_dtype), compiler_params=_cparams(("parallel", "parallel")),
    )(*ins)


def _exchange(srcs, *, scatter, name):
    n = len(srcs)
    out_shape = [jax.ShapeDtypeStruct(s.shape if scatter else (N_DEV,) + s.shape, s.dtype) for s in srcs]

    def body(*refs):
        src_refs, out_refs = refs[:n], refs[n:2 * n]
        send_sems, recv_sems, loc_sems = refs[2 * n:]
        x, y, c = lax.axis_index("x"), lax.axis_index("y"), lax.axis_index("c")
        me = 4 * x + 2 * y + c
        copies = []
        for a in range(n):
            for j in range(1, N_DEV):
                px = (1 - x) if (j >> 2) & 1 else x
                py = (1 - y) if (j >> 1) & 1 else y
                pc = (1 - c) if j & 1 else c
                src = src_refs[a].at[4 * px + 2 * py + pc] if scatter else src_refs[a]
                cp = pltpu.make_async_remote_copy(
                    src_ref=src, dst_ref=out_refs[a].at[me], send_sem=send_sems.at[a * 7 + j - 1],
                    recv_sem=recv_sems.at[a * 7 + j - 1], device_id=(px, py, pc), device_id_type=MESH)
                cp.start()
                copies.append(cp)
            src = src_refs[a].at[me] if scatter else src_refs[a]
            cp = pltpu.make_async_copy(src, out_refs[a].at[me], loc_sems.at[a])
            cp.start()
            copies.append(cp)
        for cp in copies:
            cp.wait()

    anyspec = pl.BlockSpec(memory_space=pl.ANY)
    return pl.pallas_call(
        body, name=name, in_specs=[anyspec] * n, out_specs=[anyspec] * n, out_shape=out_shape,
        scratch_shapes=[pltpu.SemaphoreType.DMA((7 * n,)), pltpu.SemaphoreType.DMA((7 * n,)),
                        pltpu.SemaphoreType.DMA((n,))],
        compiler_params=pltpu.CompilerParams(has_side_effects=True),
    )(*srcs)


def _row_tile(r, c, budget_elems=256 * 1024):
    tr = r
    while tr * c > budget_elems and tr % 16 == 0:
        tr //= 2
    return tr


def _adamw(w, recvs, m, v, *, name):
    r, c = w.shape
    nl = len(recvs)
    ns, rl = recvs[0].shape[:2]
    assert rl * nl == r
    tr = _row_tile(rl, c)
    nt = rl // tr
    bc1 = 1.0 / (1.0 - ADAM_B1 ** ADAM_STEP)
    bc2 = 1.0 / (1.0 - ADAM_B2 ** ADAM_STEP)

    def body(*refs):
        w_ref, m_ref, v_ref = refs[:3]
        r_refs = refs[3:3 + nl]
        g_ref, d_ref, mo_ref, vo_ref = refs[3 + nl:]
        i = pl.program_id(0)
        for k in range(nl):
            @pl.when(i // nt == k)
            def _(k=k):
                g = r_refs[k][0].astype(F32)
                for s in range(1, ns):
                    g = g + r_refs[k][s].astype(F32)
                g_ref[...] = g
        g = g_ref[...]
        mn = ADAM_B1 * m_ref[...] + (1.0 - ADAM_B1) * g
        vn = ADAM_B2 * v_ref[...] + (1.0 - ADAM_B2) * (g * g)
        mo_ref[...] = mn
        vo_ref[...] = vn
        d_ref[...] = -ADAM_LR * ((mn * bc1) / (jnp.sqrt(vn * bc2) + ADAM_EPS) + ADAM_WD * w_ref[...])

    spec = pl.BlockSpec((tr, c), lambda i: (i, 0))
    rspecs = [pl.BlockSpec((ns, tr, c), functools.partial(lambda i, k: (0, jnp.clip(i - k * nt, 0, nt - 1), 0), k=k))
              for k in range(nl)]
    return _pcall(
        body, name=name, grid=(r // tr,), in_specs=[spec, spec, spec] + rspecs,
        out_specs=[spec] * 4, out_shape=[jax.ShapeDtypeStruct((r, c), F32)] * 4, sem=("arbitrary",),
    )(w, m, v, *recvs)


def _sum_slots(recv, *, name):
    ns, r, c = recv.shape

    def body(r_ref, o_ref):
        g = r_ref[0]
        for s in range(1, ns):
            g = g + r_ref[s]
        o_ref[...] = g

    return pl.pallas_call(body, name=name, out_shape=jax.ShapeDtypeStruct((r, c), F32))(recv)


def _rms(x, g):
    return x * lax.rsqrt(jnp.mean(x * x, axis=-1, keepdims=True) + EPS) * g


def _silu(x):
    return x * jax.nn.sigmoid(x)


def _merge_f(a, b, c, g0, g1, g2):
    return jax.nn.sigmoid(g0) * a + jax.nn.sigmoid(g1) * b + jax.nn.sigmoid(g2) * c


def _ssd_post_f(yf, yb, xs, z, dskip, gnorm):
    y = (yf + yb + dskip * xs) * _silu(z)
    return _rms(y, gnorm)


def _combine_f(o0, o1, o2, l0, l1, l2):
    m = jnp.maximum(jnp.maximum(l0, l1), l2)
    e0, e1, e2 = jnp.exp(l0 - m), jnp.exp(l1 - m), jnp.exp(l2 - m)
    return (e0 * o0 + e1 * o1 + e2 * o2) / (e0 + e1 + e2)


def _key_window(ln, hw):
    return min(ln, QBLK + 2 * hw)


def _key_start(i, ln, hw):
    return pl.multiple_of(jnp.clip(i * QBLK - hw, 0, ln - _key_window(ln, hw)), 64)


def _attn_block(q, k3, v3, sk, qs, ks, *, hw, has_sink):
    s = lax.dot_general(q.astype(BF16), k3.astype(BF16), (((1,), (1,)), ((), ())),
                        preferred_element_type=F32) * (HEAD_DIM ** -0.5)
    qpos = qs + lax.broadcasted_iota(jnp.int32, s.shape, 0)
    kpos = ks + lax.broadcasted_iota(jnp.int32, s.shape, 1)
    valid = jnp.abs(qpos - kpos) <= hw
    s = jnp.where(valid, s, NEG_INF)
    m = jnp.max(s, axis=-1, keepdims=True)
    if has_sink:
        m = jnp.maximum(m, sk)
    m = lax.stop_gradient(m)
    e = jnp.exp(s - m)
    l = jnp.sum(e, axis=-1, keepdims=True)
    if has_sink:
        l = l + jnp.exp(sk - m)
    o = jnp.dot(e.astype(BF16), v3.astype(BF16), preferred_element_type=F32) / l
    return o, m + jnp.log(l)


def _ssd_chunk(state, xs, bm, cm, dtr, dtr_t, bias, bias_t, alog, alog_t, *, reverse):
    t = xs.shape[0]
    hg = dtr.shape[1]
    hp = xs.shape[1]
    p = hp // hg
    dt = jax.nn.softplus(dtr + bias)
    dt_t = jax.nn.softplus(dtr_t + bias_t)
    dta = dt * (-jnp.exp(alog))
    dta_t = dt_t * (-jnp.exp(alog_t))
    li = lax.broadcasted_iota(jnp.int32, (t, t), 0)
    si = lax.broadcasted_iota(jnp.int32, (t, t), 1)
    tri = (li <= si) if reverse else (li >= si)
    trif = tri.astype(F32)
    cs = jnp.dot(trif, dta, precision=HI, preferred_element_type=F32)
    cs_t = lax.dot_general(dta_t, trif, (((1,), (1,)), ((), ())), precision=HI,
                           preferred_element_type=F32)
    total = jnp.sum(dta, axis=0, keepdims=True)
    cb = lax.dot_general(cm.astype(BF16), bm.astype(BF16), (((1,), (1,)), ((), ())),
                         preferred_element_type=F32)
    lane_h = lax.broadcasted_iota(jnp.int32, (1, hp), 1) // p
    col_h = lax.broadcasted_iota(jnp.int32, (1, hg), 1)
    row_h = lax.broadcasted_iota(jnp.int32, (hg, 1), 0)
    dt_x = jnp.zeros((t, hp), F32)
    ecs_x = jnp.zeros((t, hp), F32)
    ds_x = jnp.zeros((t, hp), F32)
    etot_x = jnp.zeros((1, hp), F32)
    decays, masks = [], []
    for h in range(hg):
        oh = (col_h == h).astype(F32)
        oh_t = (row_h == h).astype(F32)
        mk = (lane_h == h).astype(F32)
        dt_h = jnp.sum(dt * oh, axis=1, keepdims=True)
        cs_h = jnp.sum(cs * oh, axis=1, keepdims=True)
        cst_h = jnp.sum(cs_t * oh_t, axis=0, keepdims=True)
        tot_h = jnp.sum(total * oh, axis=1, keepdims=True)
        dt_x = dt_x + dt_h * mk
        ecs_x = ecs_x + jnp.exp(cs_h) * mk
        ds_x = ds_x + jnp.exp(tot_h - cs_h) * mk
        etot_x = etot_x + jnp.exp(tot_h) * mk
        decays.append(jnp.exp(jnp.where(tri, cs_h - cst_h, -jnp.inf)))
        masks.append(mk)
    xdt = xs * dt_x
    y = jnp.dot(cm.astype(BF16), state.astype(BF16), preferred_element_type=F32) * ecs_x
    for h in range(hg):
        y = y + jnp.dot((cb * decays[h]).astype(BF16), (xdt * masks[h]).astype(BF16),
                        preferred_element_type=F32)
    st_new = lax.dot_general(bm.astype(BF16), (xdt * ds_x).astype(BF16), (((0,), (0,)), ((), ())),
                             preferred_element_type=F32)
    return y, state * etot_x + st_new


def _rope_tables(seq):
    half = ROPE_DIM // 2
    inv = ROPE_THETA ** (-jnp.arange(0, ROPE_DIM, 2, dtype=F32) / ROPE_DIM)
    ang = jnp.arange(seq, dtype=F32)[:, None] * inv[None, :]
    cos, sin = jnp.cos(ang), jnp.sin(ang)
    rest = HEAD_DIM - ROPE_DIM
    c = jnp.concatenate([cos, cos, jnp.ones((seq, rest), F32)], axis=1)
    a = jnp.concatenate([-sin, jnp.zeros((seq, HEAD_DIM - half), F32)], axis=1)
    b = jnp.concatenate([jnp.zeros((seq, half), F32), sin, jnp.zeros((seq, rest), F32)], axis=1)
    return c, a, b


def _rope(src, tabs, *, col0, width, seq, group, inverse, out_dtype, name, into=None, out_col0=0):
    t = src.shape[0]
    half = ROPE_DIM // 2
    nhb = 6 if all(v % (6 * HEAD_DIM) == 0 for v in (width, col0, out_col0)) else 3
    cw, tile = nhb * HEAD_DIM, 512
    assert width % cw == 0 and col0 % cw == 0 and out_col0 % cw == 0 and seq % tile == 0 and t % tile == 0
    ns = seq // tile

    def body(x_ref, c_ref, a_ref, b_ref, o_ref):
        jb = pl.program_id(1)
        c, a, b = c_ref[...], a_ref[...], b_ref[...]
        for hh in range(nhb):
            xv = x_ref[:, hh * HEAD_DIM:(hh + 1) * HEAD_DIM].astype(F32)
            if inverse:
                yv = xv * c + pltpu.roll(xv * a, half, 1) + pltpu.roll(xv * b, HEAD_DIM - half, 1)
            else:
                yv = xv * c + pltpu.roll(xv, HEAD_DIM - half, 1) * a + pltpu.roll(xv, half, 1) * b
            if group:
                keep = ((jb * nhb + hh) % group) == (group - 1)
                yv = jnp.where(keep, xv, yv)
            o_ref[:, hh * HEAD_DIM:(hh + 1) * HEAD_DIM] = yv.astype(o_ref.dtype)

    tspec = pl.BlockSpec((tile, HEAD_DIM), lambda i, j: (i % ns, 0))
    return _pcall(
        body, name=name, grid=(t // tile, width // cw),
        in_specs=[pl.BlockSpec((tile, cw), lambda i, j: (i, col0 // cw + j)), tspec, tspec, tspec],
        out_specs=pl.BlockSpec((tile, cw), lambda i, j: (i, out_col0 // cw + j)),
        out_shape=jax.ShapeDtypeStruct((t, width), out_dtype),
        sem=("parallel", "parallel"), into=None if into is None else {0: into},
    )(src, *tabs)


def _shift_rows(x, d, tpos):
    if d == 0:
        return x
    s = x.shape[0]
    y = pltpu.roll(x, (-d) % s, 0)
    ok = (tpos + d >= 0) & (tpos + d < s)
    return jnp.where(ok, y, 0.0)


def _conv_fwd(p, w8, bias, *, col0, chans, batch, seq, name):
    cb = 256
    assert chans % cb == 0 and col0 % cb == 0
    pad = (CONV_WIDTH - 1) // 2

    def body(x_ref, w_ref, b_ref, o_ref):
        x = x_ref[...]
        tpos = lax.broadcasted_iota(jnp.int32, x.shape, 0)
        acc = jnp.broadcast_to(b_ref[...], x.shape)
        for k in range(CONV_WIDTH):
            acc = acc + w_ref[k:k + 1, :] * _shift_rows(x, k - pad, tpos)
        o_ref[...] = _silu(acc)

    return pl.pallas_call(
        body, name=name, grid=(chans // cb, batch),
        in_specs=[pl.BlockSpec((seq, cb), lambda j, b: (b, col0 // cb + j)),
                  pl.BlockSpec((8, cb), lambda j, b: (0, j)), pl.BlockSpec((1, cb), lambda j, b: (0, j))],
        out_specs=pl.BlockSpec((seq, cb), lambda j, b: (b, j)),
        out_shape=jax.ShapeDtypeStruct((batch * seq, chans), F32),
        compiler_params=_cparams(("parallel", "arbitrary")),
    )(p, w8, bias)


def _conv_bwd(p, w8, bias, dus, *, col0, ch0, chans, batch, seq, into, out_col0, name):
    cb = 256 if chans % 256 == 0 and ch0 % 256 == 0 else 128
    assert chans % cb == 0 and ch0 % cb == 0 and col0 % cb == 0 and out_col0 % cb == 0
    pad = (CONV_WIDTH - 1) // 2
    ndu = len(dus)

    def body(*refs):
        x_ref, w_ref, b_ref = refs[:3]
        du_refs = refs[3:3 + ndu]
        dx_ref, dw_ref, db_ref = refs[3 + ndu:]
        du = du_refs[0][...]
        for r in du_refs[1:]:
            du = du + r[...]
        _conv_bwd_block(x_ref, w_ref, b_ref, du, dx_ref, dw_ref, db_ref)

    c0 = (col0 + ch0) // cb
    return _pcall(
        body, name=name, grid=(chans // cb, batch),
        in_specs=[pl.BlockSpec((seq, cb), lambda j, b: (b, c0 + j)),
                  pl.BlockSpec((8, cb), lambda j, b: (0, ch0 // cb + j)),
                  pl.BlockSpec((1, cb), lambda j, b: (0, ch0 // cb + j))]
        + [pl.BlockSpec((seq, cb), lambda j, b: (b, j))] * ndu,
        out_specs=[pl.BlockSpec((seq, cb), lambda j, b: (b, (out_col0 + ch0) // cb + j)),
                   pl.BlockSpec((8, cb), lambda j, b: (0, j)), pl.BlockSpec((1, cb), lambda j, b: (0, j))],
        out_shape=[jax.ShapeDtypeStruct(into.shape, into.dtype), jax.ShapeDtypeStruct((8, chans), F32),
                   jax.ShapeDtypeStruct((1, chans), F32)],
        sem=("parallel", "arbitrary"), into={0: into},
    )(p, w8, bias, *dus)


def _conv_bwd_block(x_ref, w_ref, b_ref, du, dx_ref, dw_ref, db_ref):
    pad = (CONV_WIDTH - 1) // 2
    x = x_ref[...]
    tpos = lax.broadcasted_iota(jnp.int32, x.shape, 0)
    acc = jnp.broadcast_to(b_ref[...], x.shape)
    xs = []
    for k in range(CONV_WIDTH):
        xs.append(_shift_rows(x, k - pad, tpos))
        acc = acc + w_ref[k:k + 1, :] * xs[k]
    sg = jax.nn.sigmoid(acc)
    dacc = du * (sg * (1.0 + acc * (1.0 - sg)))
    dx = jnp.zeros_like(x)
    for k in range(CONV_WIDTH):
        dx = dx + w_ref[k:k + 1, :] * _shift_rows(dacc, pad - k, tpos)
    dx_ref[...] = dx.astype(dx_ref.dtype)

    @pl.when(pl.program_id(1) == 0)
    def _():
        dw_ref[...] = jnp.zeros_like(dw_ref)
        db_ref[...] = jnp.zeros_like(db_ref)

    for k in range(CONV_WIDTH):
        dw_ref[k:k + 1, :] += jnp.sum(dacc * xs[k], axis=0, keepdims=True)
    db_ref[...] += jnp.sum(dacc, axis=0, keepdims=True)


def _scan_gpb(groups):
    return 2 if groups % 2 == 0 else 1


def _scan_specs(batch, nc, groups, hg, inner, reverse_order):
    gpb = _scan_gpb(groups)
    t, n, hp = SSD_CHUNK, SSD_STATE, hg * SSD_HEAD_DIM
    ncb = inner // (gpb * n)
    ngb = groups // gpb

    def row(b, c):
        return b * nc + ((nc - 1 - c) if reverse_order else c)

    return dict(
        xs=pl.BlockSpec((t, gpb * hp), lambda g, b, c: (row(b, c), g)),
        bm=pl.BlockSpec((t, gpb * n), lambda g, b, c: (row(b, c), ncb + g)),
        cm=pl.BlockSpec((t, gpb * n), lambda g, b, c: (row(b, c), ncb + ngb + g)),
        dtr=pl.BlockSpec((gpb, t, hg), lambda g, b, c: (g, row(b, c), 0)),
        dtr_t=pl.BlockSpec((gpb, hg, t), lambda g, b, c: (g, 0, row(b, c))),
        par=pl.BlockSpec((gpb, 1, hg), lambda g, b, c: (g, 0, 0)),
        par_t=pl.BlockSpec((gpb, hg, 1), lambda g, b, c: (g, 0, 0)),
        y=pl.BlockSpec((t, gpb * hp), lambda g, b, c: (row(b, c), g)),
        nrow=pl.BlockSpec((t, gpb * n), lambda g, b, c: (row(b, c), g)),
        st=pl.BlockSpec((gpb, None, n, hp), lambda g, b, c: (g, row(b, c), 0, 0)),
    )


def _scan_fwd(u, dtr, dtr_t, bias, bias_t, alog, alog_t, *, batch, seq, inner, reverse, name):
    groups, hg = dtr.shape[0], dtr.shape[2]
    nc = seq // SSD_CHUNK
    hp = hg * SSD_HEAD_DIM
    sp = _scan_specs(batch, nc, groups, hg, inner, reverse)
    gpb, n = _scan_gpb(groups), SSD_STATE

    def body(xs_ref, bm_ref, cm_ref, dtr_ref, dtrt_ref, b_ref, bt_ref, a_ref, at_ref, y_ref, st_ref, state):
        @pl.when(pl.program_id(2) == 0)
        def _():
            state[...] = jnp.zeros_like(state)

        for k in range(gpb):
            xc, nc_ = slice(k * hp, (k + 1) * hp), slice(k * n, (k + 1) * n)
            st_in = state[k]
            st_ref[k] = st_in
            y, st_out = _ssd_chunk(st_in, xs_ref[:, xc], bm_ref[:, nc_], cm_ref[:, nc_], dtr_ref[k], dtrt_ref[k],
                                   b_ref[k], bt_ref[k], a_ref[k], at_ref[k], reverse=reverse)
            y_ref[:, xc] = y
            state[k] = st_out

    return _pcall(
        body, name=name, grid=(groups // gpb, batch, nc),
        in_specs=[sp["xs"], sp["bm"], sp["cm"], sp["dtr"], sp["dtr_t"], sp["par"], sp["par_t"], sp["par"], sp["par_t"]],
        out_specs=[sp["y"], sp["st"]],
        out_shape=[jax.ShapeDtypeStruct((batch * seq, inner), F32),
                   jax.ShapeDtypeStruct((groups, batch * nc, SSD_STATE, hp), F32)],
        scratch_shapes=[pltpu.VMEM((gpb, SSD_STATE, hp), F32)],
        sem=("parallel", "arbitrary", "arbitrary"),
    )(u, u, u, dtr, dtr_t, bias, bias_t, alog, alog_t)


def _scan_bwd(u, dtr, dtr_t, bias, bias_t, alog, alog_t, st, dy, *, batch, seq, inner, reverse, name):
    groups, hg = dtr.shape[0], dtr.shape[2]
    nc = seq // SSD_CHUNK
    hp = hg * SSD_HEAD_DIM
    t = batch * seq
    sp = _scan_specs(batch, nc, groups, hg, inner, not reverse)
    f = functools.partial(_ssd_chunk, reverse=reverse)
    gpb, n = _scan_gpb(groups), SSD_STATE

    def body(xs_ref, bm_ref, cm_ref, dtr_ref, dtrt_ref, b_ref, bt_ref, a_ref, at_ref, st_ref, dy_ref,
             dxs_ref, dbm_ref, dcm_ref, ddtr_ref, ddtrt_ref, db_ref, dbt_ref, da_ref, dat_ref, dstate):
        first = (pl.program_id(1) == 0) & (pl.program_id(2) == 0)

        @pl.when(pl.program_id(2) == 0)
        def _():
            dstate[...] = jnp.zeros_like(dstate)

        @pl.when(first)
        def _():
            for r in (db_ref, dbt_ref, da_ref, dat_ref):
                r[...] = jnp.zeros_like(r)

        for k in range(gpb):
            xc, nc_ = slice(k * hp, (k + 1) * hp), slice(k * n, (k + 1) * n)
            _, vjp = jax.vjp(f, st_ref[k], xs_ref[:, xc], bm_ref[:, nc_], cm_ref[:, nc_], dtr_ref[k], dtrt_ref[k],
                             b_ref[k], bt_ref[k], a_ref[k], at_ref[k])
            dst, dxs, dbm, dcm, ddtr, ddtrt, db, dbt, da, dat = vjp((dy_ref[:, xc], dstate[k]))
            dstate[k] = dst
            dxs_ref[:, xc] = dxs
            dbm_ref[:, nc_] = dbm
            dcm_ref[:, nc_] = dcm
            ddtr_ref[k] = ddtr
            ddtrt_ref[k] = ddtrt
            db_ref[k] += db
            dbt_ref[k] += dbt
            da_ref[k] += da
            dat_ref[k] += dat

    gn = groups * SSD_STATE
    return _pcall(
        body, name=name, grid=(groups // gpb, batch, nc),
        in_specs=[sp["xs"], sp["bm"], sp["cm"], sp["dtr"], sp["dtr_t"], sp["par"], sp["par_t"], sp["par"], sp["par_t"],
                  sp["st"], sp["y"]],
        out_specs=[sp["y"], sp["nrow"], sp["nrow"], sp["dtr"], sp["dtr_t"], sp["par"], sp["par_t"], sp["par"], sp["par_t"]],
        out_shape=[jax.ShapeDtypeStruct((t, inner), F32), jax.ShapeDtypeStruct((t, gn), F32),
                   jax.ShapeDtypeStruct((t, gn), F32), jax.ShapeDtypeStruct(dtr.shape, F32),
                   jax.ShapeDtypeStruct(dtr_t.shape, F32), jax.ShapeDtypeStruct(bias.shape, F32),
                   jax.ShapeDtypeStruct(bias_t.shape, F32), jax.ShapeDtypeStruct(alog.shape, F32),
                   jax.ShapeDtypeStruct(alog_t.shape, F32)],
        scratch_shapes=[pltpu.VMEM((gpb, SSD_STATE, hp), F32)],
        sem=("parallel", "arbitrary", "arbitrary"),
    )(u, u, u, dtr, dtr_t, bias, bias_t, alog, alog_t, st, dy)


def _attn_load(ref, col, blk):
    return ref[pl.ds(pl.multiple_of(blk * QBLK, QBLK), QBLK), col * HEAD_DIM:(col + 1) * HEAD_DIM].astype(F32)


def _lane0(row):
    lane = lax.broadcasted_iota(jnp.int32, row.shape, 1)
    return jnp.sum(jnp.where(lane == 0, row, 0.0), axis=1, keepdims=True)


def _attn_fwd(rq, sinkx, *, batch, seq, dil, nbw, cb0, nh, rep, hw, want_lse, out_dtype, name):
    t, w = rq.shape
    ln = seq // dil
    nb = ln // QBLK
    bw = (rep + 2) * HEAD_DIM
    ow = nh * rep * HEAD_DIM
    has_sink = sinkx is not None
    rq3 = rq.reshape(batch, ln, dil * w)
    kwin = _key_window(ln, hw)
    f = functools.partial(_attn_block, hw=hw, has_sink=has_sink)

    def body(*refs):
        if has_sink:
            blk_ref, sink_ref = refs[:2]
            outs = refs[2:]
        else:
            blk_ref, sink_ref = refs[0], None
            outs = refs[1:]
        o_ref = outs[0]
        lse_ref = outs[1] if want_lse else None
        g = pl.program_id(2)

        def qblock(i, carry):
            ks = _key_start(i, ln, hw)
            k3 = blk_ref[pl.ds(ks, kwin), rep * HEAD_DIM:(rep + 1) * HEAD_DIM].astype(F32)
            v3 = blk_ref[pl.ds(ks, kwin), (rep + 1) * HEAD_DIM:].astype(F32)
            rows = pl.ds(pl.multiple_of(i * QBLK, QBLK), QBLK)
            for r in range(rep):
                sk = _lane0(sink_ref[pl.ds(g * rep + r, 1), :]) if has_sink else None
                o, lse = f(_attn_load(blk_ref, r, i), k3, v3, sk, i * QBLK, ks)
                o_ref[rows, r * HEAD_DIM:(r + 1) * HEAD_DIM] = o.astype(o_ref.dtype)
                if want_lse:
                    lse_ref[rows, r * HEAD_DIM:(r + 1) * HEAD_DIM] = jnp.broadcast_to(lse, o.shape)
            return carry

        lax.fori_loop(0, nb, qblock, 0)

    in_specs = [pl.BlockSpec((None, ln, bw), lambda b, r, h: (b, 0, r * nbw + cb0 + h))]
    ins = [rq3]
    if has_sink:
        in_specs.append(pl.BlockSpec(sinkx.shape, lambda b, r, h: (0, 0)))
        ins.append(sinkx)
    ospec = pl.BlockSpec((None, ln, rep * HEAD_DIM), lambda b, r, h: (b, 0, r * nh + h))
    out_shape = [jax.ShapeDtypeStruct((batch, ln, dil * ow), out_dtype)]
    out_specs = [ospec]
    if want_lse:
        out_shape.append(jax.ShapeDtypeStruct((batch, ln, dil * ow), F32))
        out_specs.append(ospec)
    res = _pcall(
        body, name=name, grid=(batch, dil, nh), in_specs=in_specs, out_specs=out_specs, out_shape=out_shape,
        sem=("parallel", "parallel", "parallel"),
    )(*ins)
    return [r.reshape(t, ow) for r in res]


def _attn_bwd(rq, sinkx, do, dlse, *, batch, seq, dil, nbw, cb0, nh, rep, hw, name):
    t, w = rq.shape
    ln = seq // dil
    nb = ln // QBLK
    bw = (rep + 2) * HEAD_DIM
    ow = nh * rep * HEAD_DIM
    has_sink = sinkx is not None
    has_lse = dlse is not None
    kwin = _key_window(ln, hw)
    f = functools.partial(_attn_block, hw=hw, has_sink=has_sink)

    def body(*refs):
        refs = list(refs)
        blk_ref = refs.pop(0)
        sink_ref = refs.pop(0) if has_sink else None
        do_ref = refs.pop(0)
        dlse_ref = refs.pop(0) if has_lse else None
        d_ref = refs.pop(0)
        dsink_ref = refs.pop(0) if has_sink else None
        g = pl.program_id(2)
        d_ref[:, rep * HEAD_DIM:] = jnp.zeros((ln, 2 * HEAD_DIM), F32)
        if has_sink:
            @pl.when((pl.program_id(0) == 0) & (pl.program_id(1) == 0) & (g == 0))
            def _():
                dsink_ref[...] = jnp.zeros_like(dsink_ref)

        def qblock(i, carry):
            ks = _key_start(i, ln, hw)
            krows = pl.ds(ks, kwin)
            k3 = blk_ref[krows, rep * HEAD_DIM:(rep + 1) * HEAD_DIM].astype(F32)
            v3 = blk_ref[krows, (rep + 1) * HEAD_DIM:].astype(F32)
            rows = pl.ds(pl.multiple_of(i * QBLK, QBLK), QBLK)
            dk3 = jnp.zeros_like(k3)
            dv3 = jnp.zeros_like(v3)
            for r in range(rep):
                cols = slice(r * HEAD_DIM, (r + 1) * HEAD_DIM)
                q = _attn_load(blk_ref, r, i)
                dov = do_ref[rows, cols]
                dl = dlse_ref[rows, cols] if has_lse else jnp.zeros_like(dov)
                if has_sink:
                    srow = sink_ref[pl.ds(g * rep + r, 1), :]
                    _, vjp = jax.vjp(lambda q_, k_, v_, s_: f(q_, k_, v_, _lane0(s_), i * QBLK, ks), q, k3, v3, srow)
                    dq, dk, dv, ds = vjp((dov, jnp.sum(dl, axis=1, keepdims=True)))
                    dsink_ref[pl.ds(g * rep + r, 1), :] += ds
                else:
                    _, vjp = jax.vjp(lambda q_, k_, v_: f(q_, k_, v_, None, i * QBLK, ks), q, k3, v3)
                    dq, dk, dv = vjp((dov, jnp.sum(dl, axis=1, keepdims=True)))
                d_ref[rows, cols] = dq
                dk3 = dk3 + dk
                dv3 = dv3 + dv
            d_ref[krows, rep * HEAD_DIM:(rep + 1) * HEAD_DIM] += dk3
            d_ref[krows, (rep + 1) * HEAD_DIM:] += dv3
            return carry

        lax.fori_loop(0, nb, qblock, 0)

    ospec = pl.BlockSpec((None, ln, rep * HEAD_DIM), lambda b, r, h: (b, 0, r * nh + h))
    in_specs = [pl.BlockSpec((None, ln, bw), lambda b, r, h: (b, 0, r * nbw + cb0 + h))]
    ins = [rq.reshape(batch, ln, dil * w)]
    if has_sink:
        in_specs.append(pl.BlockSpec(sinkx.shape, lambda b, r, h: (0, 0)))
        ins.append(sinkx)
    in_specs.append(ospec)
    ins.append(do.reshape(batch, ln, dil * ow))
    if has_lse:
        in_specs.append(ospec)
        ins.append(dlse.reshape(batch, ln, dil * ow))
    dw = nh * bw
    out_specs = [pl.BlockSpec((None, ln, bw), lambda b, r, h: (b, 0, r * nh + h))]
    out_shape = [jax.ShapeDtypeStruct((batch, ln, dil * dw), F32)]
    if has_sink:
        out_specs.append(pl.BlockSpec(sinkx.shape, lambda b, r, h: (0, 0)))
        out_shape.append(jax.ShapeDtypeStruct(sinkx.shape, F32))
    res = _pcall(
        body, name=name, grid=(batch, dil, nh), in_specs=in_specs, out_specs=out_specs, out_shape=out_shape,
        sem=("arbitrary", "arbitrary", "arbitrary"),
    )(*ins)
    return [res[0].reshape(t, dw)] + list(res[1:])


def _final_loss(x, g, target, *, name):
    d = x.shape[1]

    def fn(rv, pv):
        xv, tg = rv
        y, vjp = jax.vjp(_rms, xv, pv[0])
        err = y - tg
        dx, dg = vjp(err * (1.0 / d))
        loss = 0.5 * jnp.sum(err * err) * (1.0 / d)
        return [dx], [dg, jnp.full((1, LANE), loss, F32)]

    (dx,), (dg, loss) = _rowcall(fn, [(x, d, 0), (target, d, 0)], [g], [(d, F32)], [(1, d), (1, LANE)],
                                 tile=256, name=name)
    return dx, dg, loss


class _Layout:
    def __init__(self, d_model):
        self.d = d_model
        self.inner = SSD_HEADS * SSD_HEAD_DIM
        self.gn = SSD_GROUPS * SSD_STATE
        self.xbc = self.inner + 2 * self.gn
        self.ndt = 2 * SSD_HEADS
        self.ngrp = len(DIL_PATTERNS)
        self.dilw = DIL_HEADS * HEAD_DIM
        self.rqd = 3 * self.ngrp * self.dilw
        self.rep = WIN_Q_HEADS // WIN_KV_HEADS
        self.rqw = WIN_KV_HEADS * (self.rep + 2) * HEAD_DIM
        self.qw = WIN_Q_HEADS * HEAD_DIM
        self.kw = WIN_KV_HEADS * HEAD_DIM
        self.gates = N_BRANCH * d_model
        self.n_in = self.inner + self.xbc + self.ndt + self.rqd + self.qw + 2 * self.kw + self.gates
        self.o_gates = 0
        self.o_z = self.gates
        self.o_xbc = self.o_z + self.inner
        self.o_rqd = self.o_xbc + self.xbc
        self.o_rqw = self.o_rqd + self.rqd
        self.o_dt = self.o_rqw + self.rqw
        self.dtw = -(-(self.o_dt + self.ndt) // PAD_TO) * PAD_TO - self.o_dt
        self.width = self.o_dt + self.dtw
        assert self.o_z % self.inner == 0 and self.o_xbc % 256 == 0
        assert self.o_rqd % (3 * HEAD_DIM) == 0 and self.o_rqw % (3 * HEAD_DIM) == 0 and self.dtw % LANE == 0
        assert self.o_dt % self.dtw == 0

    def split_points(self):
        sizes = (self.inner, self.xbc, self.ndt, self.rqd, self.qw, self.kw, self.kw, self.gates)
        pts, acc = [], 0
        for s in sizes:
            pts.append((acc, acc + s))
            acc += s
        return pts

    def permute_w(self, w):
        d = w.shape[0]
        z, xbc, dt, qkvd, qw, kw, vw, gates = [w[:, a:b] for a, b in self.split_points()]
        nhd = self.ngrp * DIL_HEADS
        qkvd = qkvd.reshape(d, 3, nhd, HEAD_DIM).transpose(0, 2, 1, 3).reshape(d, self.rqd)
        win = jnp.concatenate([qw.reshape(d, WIN_KV_HEADS, self.rep, HEAD_DIM),
                               kw.reshape(d, WIN_KV_HEADS, 1, HEAD_DIM),
                               vw.reshape(d, WIN_KV_HEADS, 1, HEAD_DIM)], axis=2).reshape(d, self.rqw)
        pad = jnp.zeros((d, self.dtw - self.ndt), w.dtype)
        return jnp.concatenate([gates, z, xbc, qkvd, win, dt, pad], axis=1)

    def unpermute_w(self, wp):
        d = wp.shape[0]
        gates = wp[:, :self.o_z]
        z = wp[:, self.o_z:self.o_xbc]
        xbc = wp[:, self.o_xbc:self.o_rqd]
        qkvd = wp[:, self.o_rqd:self.o_rqw]
        win = wp[:, self.o_rqw:self.o_dt].reshape(d, WIN_KV_HEADS, self.rep + 2, HEAD_DIM)
        dt = wp[:, self.o_dt:self.o_dt + self.ndt]
        nhd = self.ngrp * DIL_HEADS
        qkvd = qkvd.reshape(d, nhd, 3, HEAD_DIM).transpose(0, 2, 1, 3).reshape(d, self.rqd)
        qw = win[:, :, :self.rep].reshape(d, self.qw)
        kw = win[:, :, self.rep].reshape(d, self.kw)
        vw = win[:, :, self.rep + 1].reshape(d, self.kw)
        return jnp.concatenate([z, xbc, dt, qkvd, qw, kw, vw, gates], axis=1)


def _dt_layouts(pdt, dirn, batch_seq):
    hg = SSD_HEADS // SSD_GROUPS
    v = pdt[:, dirn * SSD_HEADS:(dirn + 1) * SSD_HEADS].reshape(batch_seq, SSD_GROUPS, hg)
    return v.transpose(1, 0, 2), v.transpose(1, 2, 0)


def _par_layouts(p):
    hg = SSD_HEADS // SSD_GROUPS
    v = p.reshape(SSD_GROUPS, hg)
    return v[:, None, :], v[:, :, None]


def _layer_fwd(x, lw, lay, tabs, batch, seq):
    d = lay.d
    sv = {"x": x}
    (h,), _ = _rowcall(lambda rv, pv: ([_rms(rv[0], pv[0])], []), [(x, d, 0)], [lw["g_mix"]], [(d, BF16)], [],
                       tile=256, name="norm_mix")
    p = _mm(h, lw["w_in"], name="proj_in")
    sv["h"], sv["p"] = h, p
    u = _conv_fwd(p, lw["conv_w8"], lw["conv_b"], col0=lay.o_xbc, chans=lay.xbc, batch=batch, seq=seq, name="conv_fwd")
    sv["u"] = u
    pdt = p[:, lay.o_dt:lay.o_dt + lay.ndt]
    ys, sv["st"], sv["dtl"] = [], [], []
    for dirn in range(2):
        dtr, dtr_t = _dt_layouts(pdt, dirn, batch * seq)
        bias, bias_t = _par_layouts(lw["dt_bias"][dirn])
        alog, alog_t = _par_layouts(lw["a_log"][dirn])
        y, st = _scan_fwd(u, dtr, dtr_t, bias, bias_t, alog, alog_t, batch=batch, seq=seq, inner=lay.inner,
                          reverse=bool(dirn), name="scan_fwd%d" % dirn)
        ys.append(y)
        sv["st"].append(st)
        sv["dtl"].append((dtr, dtr_t, bias, bias_t, alog, alog_t))
    sv["ys"] = ys
    inner = lay.inner
    (ya,), _ = _rowcall(lambda rv, pv: ([_ssd_post_f(*rv, *pv)], []),
                        [(ys[0], inner, 0), (ys[1], inner, 0), (u, inner, 0), (p, inner, lay.o_z // inner)],
                        [lw["d_skip_x"], lw["ssd_norm"]], [(inner, BF16)], [], tile=128, name="ssd_post")
    sv["ya"] = ya
    rqd = _rope(p, tabs, col0=lay.o_rqd, width=lay.rqd, seq=seq, group=0, inverse=False, out_dtype=BF16, name="rope_dil")
    rqw = _rope(p, tabs, col0=lay.o_rqw, width=lay.rqw, seq=seq, group=lay.rep + 2, inverse=False, out_dtype=BF16,
                name="rope_win")
    sv["rqd"], sv["rqw"] = rqd, rqw
    os_, ls_ = [], []
    for gi, (window, dil) in enumerate(DIL_PATTERNS):
        o, l = _attn_fwd(rqd, None, batch=batch, seq=seq, dil=dil, nbw=lay.ngrp * DIL_HEADS, cb0=gi * DIL_HEADS,
                         nh=DIL_HEADS, rep=1, hw=window // (2 * dil), want_lse=True, out_dtype=F32,
                         name="dil_fwd%d" % gi)
        os_.append(o)
        ls_.append(l)
    sv["os"], sv["ls"] = os_, ls_
    dw = lay.dilw
    (yb,), _ = _rowcall(lambda rv, pv: ([_combine_f(*rv)], []), [(a, dw, 0) for a in os_ + ls_], [], [(dw, BF16)], [],
                        tile=256, name="dil_combine")
    sv["yb"] = yb
    (yc,) = _attn_fwd(rqw, lw["sink_x"], batch=batch, seq=seq, dil=1, nbw=WIN_KV_HEADS, cb0=0, nh=WIN_KV_HEADS,
                      rep=lay.rep, hw=WIN_HALF, want_lse=False, out_dtype=BF16, name="win_fwd")
    sv["yc"] = yc
    ma = _mm(ya, lw["w_a"], name="proj_a")
    mb = _mm(yb, lw["w_b"], name="proj_b")
    mc = _mm(yc, lw["w_c"], name="proj_c")
    sv["mabc"] = (ma, mb, mc)
    (mg,), _ = _rowcall(lambda rv, pv: ([_merge_f(*rv)], []),
                        [(ma, d, 0), (mb, d, 0), (mc, d, 0), (p, d, 0), (p, d, 1), (p, d, 2)], [], [(d, BF16)], [],
                        tile=256, name="merge")
    sv["mg"] = mg
    x1 = _mm(mg, lw["w_out"], add=x, name="proj_out")
    sv["x1"] = x1
    (hm,), _ = _rowcall(lambda rv, pv: ([_rms(rv[0], pv[0])], []), [(x1, d, 0)], [lw["g_mlp"]], [(d, BF16)], [],
                        tile=256, name="norm_mlp")
    up, act = _mm(hm, lw["w_up"], post=(lambda r: (r, jnp.square(jnp.maximum(r, 0.0))), [], [F32, BF16]),
                  name="mlp_up")
    sv["hm"], sv["up"], sv["act"] = hm, up, act
    x2 = _mm(act, lw["w_down"], add=x1, name="mlp_down")
    return x2, sv


def _layer_bwd(dxo, sv, lw, lay, tabs, batch, seq, post):
    d = lay.d
    inner = lay.inner
    gs = {}
    (dup,) = _mm(dxo, lw["w_down"], tb=True, name="mlp_down_dx",
                 post=(lambda r, a: (r * (2.0 * jnp.maximum(a, 0.0)),), [sv["up"]], [BF16]))
    post("w_down", _mm(sv["act"], dxo, ta=True, name="mlp_down_dw"))
    dhm = _mm(dup, lw["w_up"], tb=True, name="mlp_up_dx")
    post("w_up", _mm(sv["hm"], dup, ta=True, name="mlp_up_dw"))

    def norm_bwd(rv, pv):
        xv, dh, dres = rv
        _, vjp = jax.vjp(_rms, xv, pv[0])
        dx, dg = vjp(dh)
        return [dx + dres], [dg]

    (dx1,), (gs["g_mlp"],) = _rowcall(norm_bwd, [(sv["x1"], d, 0), (dhm, d, 0), (dxo, d, 0)], [lw["g_mlp"]],
                                      [(d, F32)], [(1, d)], tile=128, name="norm_mlp_bwd")
    dmg = _mm(dx1, lw["w_out"], tb=True, name="proj_out_dx")
    post("w_out", _mm(sv["mg"], dx1, ta=True, name="proj_out_dw"))
    ma, mb, mc = sv["mabc"]
    p = sv["p"]

    def merge_bwd(rv, pv):
        _, vjp = jax.vjp(_merge_f, *rv[:6])
        da, db, dc, d0, d1, d2 = vjp(rv[6])
        return [da, db, dc, jnp.concatenate([d0, d1, d2], axis=1)], []

    dp = lax.empty((batch * seq, lay.width), BF16)
    (dma, dmb, dmc, dp), _ = _rowcall(
        merge_bwd, [(ma, d, 0), (mb, d, 0), (mc, d, 0), (p, d, 0), (p, d, 1), (p, d, 2), (dmg, d, 0)], [],
        [(d, BF16), (d, BF16), (d, BF16), (lay.gates, BF16, dp, 0)], [], tile=128, name="merge_bwd")
    dya = _mm(dma, lw["w_a"], tb=True, name="proj_a_dx")
    post("w_a", _mm(sv["ya"], dma, ta=True, name="proj_a_dw"))
    dyb = _mm(dmb, lw["w_b"], tb=True, name="proj_b_dx")
    post("w_b", _mm(sv["yb"], dmb, ta=True, name="proj_b_dw"))
    dyc = _mm(dmc, lw["w_c"], tb=True, name="proj_c_dx")
    post("w_c", _mm(sv["yc"], dmc, ta=True, name="proj_c_dw"))
    drqw, dsink = _attn_bwd(sv["rqw"], lw["sink_x"], dyc, None, batch=batch, seq=seq, dil=1, nbw=WIN_KV_HEADS, cb0=0,
                            nh=WIN_KV_HEADS, rep=lay.rep, hw=WIN_HALF, name="win_bwd")
    gs["sink"] = jnp.sum(dsink, axis=1)
    dw = lay.dilw

    def combine_bwd(rv, pv):
        _, vjp = jax.vjp(_combine_f, *rv[:6])
        return list(vjp(rv[6])), []

    dol, _ = _rowcall(combine_bwd, [(a, dw, 0) for a in sv["os"] + sv["ls"]] + [(dyb, dw, 0)], [],
                      [(dw, F32)] * 6, [], tile=256, name="dil_combine_bwd")
    for gi, (window, dil) in enumerate(DIL_PATTERNS):
        (dg_,) = _attn_bwd(sv["rqd"], None, dol[gi], dol[3 + gi], batch=batch, seq=seq, dil=dil,
                           nbw=lay.ngrp * DIL_HEADS, cb0=gi * DIL_HEADS, nh=DIL_HEADS, rep=1,
                           hw=window // (2 * dil), name="dil_bwd%d" % gi)
        dp = _rope(dg_, tabs, col0=0, width=dg_.shape[1], seq=seq, group=0, inverse=True, out_dtype=BF16,
                   name="rope_dil_bwd%d" % gi, into=dp, out_col0=lay.o_rqd + gi * dg_.shape[1])
    dp = _rope(drqw, tabs, col0=0, width=lay.rqw, seq=seq, group=lay.rep + 2, inverse=True, out_dtype=BF16,
               name="rope_win_bwd", into=dp, out_col0=lay.o_rqw)
    u, ys = sv["u"], sv["ys"]

    def post_bwd(rv, pv):
        _, vjp = jax.vjp(_ssd_post_f, *rv[:4], *pv)
        dyf, _, dxs, dz, dsk, dgn = vjp(rv[4])
        return [dyf, dxs, dz], [dsk, dgn]

    (dy, dxs_post, dp), (dsk, gs["ssd_norm"]) = _rowcall(
        post_bwd, [(ys[0], inner, 0), (ys[1], inner, 0), (u, inner, 0), (p, inner, lay.o_z // inner), (dya, inner, 0)],
        [lw["d_skip_x"], lw["ssd_norm"]], [(inner, F32), (inner, F32), (inner, BF16, dp, lay.o_z // inner)],
        [(1, inner), (1, inner)], tile=128, name="ssd_post_bwd")
    gs["d_skip"] = jnp.sum(dsk.reshape(SSD_HEADS, SSD_HEAD_DIM), axis=1)
    rs = []
    ddt, gdb, gda = [], [], []
    for dirn in range(2):
        dtr, dtr_t, bias, bias_t, alog, alog_t = sv["dtl"][dirn]
        r = _scan_bwd(u, dtr, dtr_t, bias, bias_t, alog, alog_t, sv["st"][dirn], dy, batch=batch, seq=seq,
                      inner=inner, reverse=bool(dirn), name="scan_bwd%d" % dirn)
        rs.append(r)
        ddt.append((r[3] + r[4].transpose(0, 2, 1)).transpose(1, 0, 2).reshape(batch * seq, SSD_HEADS))
        gdb.append((r[5][:, 0, :] + r[6][:, :, 0]).reshape(SSD_HEADS))
        gda.append((r[7][:, 0, :] + r[8][:, :, 0]).reshape(SSD_HEADS))
    gs["dt_bias"] = jnp.stack(gdb)
    gs["a_log"] = jnp.stack(gda)
    dcws, dcbs = [], []
    for tag, ch0, chans, dus in (("x", 0, inner, [dxs_post, rs[0][0], rs[1][0]]),
                                 ("b", inner, lay.gn, [rs[0][1], rs[1][1]]),
                                 ("c", inner + lay.gn, lay.gn, [rs[0][2], rs[1][2]])):
        dp, dcw, dcb = _conv_bwd(p, lw["conv_w8"], lw["conv_b"], dus, col0=lay.o_xbc, ch0=ch0, chans=chans,
                                 batch=batch, seq=seq, into=dp, out_col0=lay.o_xbc, name="conv_bwd_" + tag)
        dcws.append(dcw)
        dcbs.append(dcb)
    gs["conv_w"] = jnp.concatenate(dcws, axis=1)[:CONV_WIDTH]
    gs["conv_b"] = jnp.concatenate(dcbs, axis=1)[0]
    ddtp = jnp.concatenate(ddt + [jnp.zeros((batch * seq, lay.dtw - lay.ndt), F32)], axis=1)
    (dp,), _ = _rowcall(lambda rv, pv: ([rv[0]], []), [(ddtp, lay.dtw, 0)], [],
                        [(lay.dtw, BF16, dp, lay.o_dt // lay.dtw)], [], tile=512, name="ddt_store")
    hd = d // 2
    for part in range(2):
        post(("w_in", part), _mm(sv["h"][:, part * hd:(part + 1) * hd], dp, ta=True, name="proj_in_dw"))
    dh = _mm(dp, lw["w_in"], tb=True, name="proj_in_dx")
    (dx,), (gs["g_mix"],) = _rowcall(norm_bwd, [(sv["x"], d, 0), (dh, d, 0), (dx1, d, 0)], [lw["g_mix"]],
                                     [(d, F32)], [(1, d)], tile=128, name="norm_mix_bwd")
    return dx, gs


_SHARDED = ("w_in", "w_a", "w_b", "w_c", "w_out", "w_up", "w_down")
_COL_SHARDED = ("w_in", "w_b", "w_up")
_SMALL = ("g_mix", "conv_b", "dt_bias", "a_log", "d_skip", "ssd_norm", "sink", "g_mlp")


def _gathered_to_full(name, g):
    n, r, c = g.shape
    if name in _COL_SHARDED:
        return g.transpose(1, 0, 2).reshape(r, n * c)
    return g.reshape(n * r, c)


def _full_to_slots(name, w):
    r, c = w.shape
    if name in _COL_SHARDED:
        return w.reshape(r, N_DEV, c // N_DEV).transpose(1, 0, 2)
    return w.reshape(N_DEV, r // N_DEV, c)


class _LayerWeights:
    def __init__(self, sched, layer, lay, small):
        self.sched, self.layer, self.lay, self.vals = sched, layer, lay, dict(small)

    def __getitem__(self, name):
        if name not in self.vals:
            full = _gathered_to_full(name, self.sched.get(("w", name, self.layer)))
            self.vals[name] = self.lay.permute_w(full) if name == "w_in" else full
        return self.vals[name]


def _pack(parts):
    flat = jnp.concatenate([p.reshape(-1).astype(F32) for p in parts])
    n = flat.shape[0]
    rows = -(-n // (8 * LANE)) * 8
    return jnp.pad(flat, (0, rows * LANE - n)).reshape(rows, LANE)


def _unpack(buf, shapes):
    flat = buf.reshape(-1)
    out, off = [], 0
    for s in shapes:
        n = math.prod(s)
        out.append(flat[off:off + n].reshape(s))
        off += n
    return out


def kernel(x, g_mix, w_in, conv_w, conv_b, dt_bias, a_log, d_skip, ssd_norm, w_a, w_b, w_c, sink, w_out, g_mlp, w_up, w_down, g_final, loss_target, m_g_mix, m_w_in, m_conv_w, m_conv_b, m_dt_bias, m_a_log, m_d_skip, m_ssd_norm, m_w_a, m_w_b, m_w_c, m_sink, m_w_out, m_g_mlp, m_w_up, m_w_down, m_g_final, v_g_mix, v_w_in, v_conv_w, v_conv_b, v_dt_bias, v_a_log, v_d_skip, v_ssd_norm, v_w_a, v_w_b, v_w_c, v_sink, v_w_out, v_g_mlp, v_w_up, v_w_down, v_g_final):
    batch, seq, d = x.shape
    depth = g_mix.shape[0]
    lay = _Layout(d)
    assert lay.n_in == w_in.shape[2] * N_DEV
    wts = dict(g_mix=g_mix, w_in=w_in, conv_w=conv_w, conv_b=conv_b, dt_bias=dt_bias, a_log=a_log, d_skip=d_skip,
               ssd_norm=ssd_norm, w_a=w_a, w_b=w_b, w_c=w_c, sink=sink, w_out=w_out, g_mlp=g_mlp, w_up=w_up,
               w_down=w_down, g_final=g_final)
    mom = dict(g_mix=m_g_mix, w_in=m_w_in, conv_w=m_conv_w, conv_b=m_conv_b, dt_bias=m_dt_bias, a_log=m_a_log,
               d_skip=m_d_skip, ssd_norm=m_ssd_norm, w_a=m_w_a, w_b=m_w_b, w_c=m_w_c, sink=m_sink, w_out=m_w_out,
               g_mlp=m_g_mlp, w_up=m_w_up, w_down=m_w_down, g_final=m_g_final)
    var = dict(g_mix=v_g_mix, w_in=v_w_in, conv_w=v_conv_w, conv_b=v_conv_b, dt_bias=v_dt_bias, a_log=v_a_log,
               d_skip=v_d_skip, ssd_norm=v_ssd_norm, w_a=v_w_a, w_b=v_w_b, w_c=v_w_c, sink=v_sink, w_out=v_w_out,
               g_mlp=v_g_mlp, w_up=v_w_up, w_down=v_w_down, g_final=v_g_final)
    me = 4 * lax.axis_index("x") + 2 * lax.axis_index("y") + lax.axis_index("c")

    global _SCHED
    sched = _SCHED = _Sched()
    (gconv,) = _exchange([conv_w], scatter=False, name="gather_conv_w")
    conv_full = gconv.transpose(1, 2, 0, 3).reshape(depth, CONV_WIDTH, -1)
    for l in range(depth):
        for n in _SHARDED:
            sched.post(("w", n, l), wts[n][l].astype(BF16), scatter=False)

    tabs = _rope_tables(seq)
    t = batch * seq
    xf = x.reshape(t, d)
    layers = []
    for l in range(depth):
        small = dict(
            g_mix=g_mix[l][None], g_mlp=g_mlp[l][None], ssd_norm=ssd_norm[l][None], conv_b=conv_b[l][None],
            dt_bias=dt_bias[l], a_log=a_log[l],
            d_skip_x=jnp.repeat(d_skip[l], SSD_HEAD_DIM)[None],
            sink_x=jnp.broadcast_to(sink[l][:, None], (WIN_Q_HEADS, LANE)),
            conv_w8=jnp.pad(conv_full[l], ((0, 8 - CONV_WIDTH), (0, 0))))
        layers.append(_LayerWeights(sched, l, lay, small))

    saves = []
    h = xf
    for l in range(depth):
        h, sv = _layer_fwd(h, layers[l], lay, tabs, batch, seq)
        saves.append(sv)
    dx, dgf, loss = _final_loss(h, g_final[None], loss_target.reshape(t, d), name="final_loss")

    gss = [None] * depth
    for l in reversed(range(depth)):
        def post(n, g, l=l):
            if isinstance(n, tuple):
                key, n, g = ("g", "w_in", l, n[1]), "w_in", lay.unpermute_w(g)
            else:
                key = ("g", n, l)
            sched.post(key, _full_to_slots(n, g).astype(BF16), scatter=True)

        dx, gss[l] = _layer_bwd(dx, saves[l], layers[l], lay, tabs, batch, seq, post)
    grad_x = dx.reshape(batch, seq, d)

    small_parts = [jnp.stack([gss[l][n] for l in range(depth)]) for n in _SMALL]
    small_parts += [dgf, jnp.stack([gss[l]["conv_w"] for l in range(depth)]), loss[0, :1]]
    small_shapes = [p.shape for p in small_parts]
    (rs,) = _exchange([_pack(small_parts)], scatter=False, name="gather_small")
    red = _unpack(_sum_slots(rs, name="sum_small"), small_shapes)
    gsmall = dict(zip(list(_SMALL) + ["g_final"], red[:len(_SMALL) + 1]))
    gconv_full, loss_sum = red[-2], red[-1]
    cshard = conv_w.shape[2]
    gsmall["conv_w"] = lax.dynamic_slice_in_dim(gconv_full, me * cshard, cshard, axis=2)

    out = {}
    rep_names = list(_SMALL) + ["g_final"]
    rep_shapes = [wts[n].shape for n in rep_names]
    res = _adamw(_pack([wts[n] for n in rep_names]), [_pack([gsmall[n] for n in rep_names])[None]],
                 _pack([mom[n] for n in rep_names]), _pack([var[n] for n in rep_names]), name="adamw_small")
    unp = [_unpack(a, rep_shapes) for a in res]
    for i, n in enumerate(rep_names):
        out[n] = [unp[k][i] for k in range(4)]
    cs2 = (depth * CONV_WIDTH, cshard)
    res = _adamw(conv_w.reshape(cs2), [gsmall["conv_w"].reshape((1,) + cs2)], m_conv_w.reshape(cs2),
                 v_conv_w.reshape(cs2), name="adamw_conv_w")
    out["conv_w"] = [a.reshape(conv_w.shape) for a in res]
    for n in ("w_down", "w_up", "w_out", "w_a", "w_b", "w_c", "w_in"):
        shp = wts[n].shape
        r2 = (shp[0] * shp[1], shp[2])
        if n == "w_in":
            recvs = [sched.get(("g", n, l, part)) for l in range(depth) for part in range(2)]
        else:
            recvs = [sched.get(("g", n, l)) for l in range(depth)]
        res = _adamw(wts[n].reshape(r2), recvs, mom[n].reshape(r2), var[n].reshape(r2), name="adamw_" + n)
        out[n] = [a.reshape(shp) for a in res]

    order = ["g_mix", "w_in", "conv_w", "conv_b", "dt_bias", "a_log", "d_skip", "ssd_norm", "w_a", "w_b", "w_c",
             "sink", "w_out", "g_mlp", "w_up", "w_down", "g_final"]
    outs = [loss_sum.reshape(()), grad_x]
    for k in range(4):
        outs += [out[n][k] for n in order]
    return tuple(outs)
```

```python
import functools
import math

import jax
import jax.numpy as jnp
from jax import lax
from jax.experimental import pallas as pl
from jax.experimental.pallas import tpu as pltpu

F32 = jnp.float32
BF16 = jnp.bfloat16
HI = lax.Precision.HIGHEST
MESH = pl.DeviceIdType.MESH
N_DEV = 8

SSD_HEADS = 32
SSD_HEAD_DIM = 64
SSD_GROUPS = 8
SSD_STATE = 128
SSD_CHUNK = 128
CONV_WIDTH = 5
HEAD_DIM = 128
ROPE_DIM = 32
ROPE_THETA = 500000.0
DIL_PATTERNS = ((128, 1), (512, 4), (2048, 16))
DIL_HEADS = 8
WIN_Q_HEADS = 16
WIN_KV_HEADS = 4
WIN_HALF = 128
N_BRANCH = 3
EPS = 1e-6
NEG_INF = -1e30
ADAM_LR = 0.001
ADAM_B1 = 0.9
ADAM_B2 = 0.999
ADAM_EPS = 1e-08
ADAM_WD = 0.01
ADAM_STEP = 10

LANE = 128
QBLK = 128
VMEM_LIMIT = 56 * 1024 * 1024
PAD_TO = 512
MM_VMEM_BUDGET = 40 * 1024 * 1024
W_IN_PARTS = 4


def _cparams(sem=None):
    return pltpu.CompilerParams(dimension_semantics=sem, vmem_limit_bytes=VMEM_LIMIT)


PIECE_BYTES = 400 * 1024
US_PER_PIECE_BYTE = 8.8e-5
MAX_PIECES = 8
CARRIER_US = {
    "proj_in": 450, "proj_in_dx": 560, "proj_in_dw": 140, "scan_fwd0": 230, "scan_fwd1": 230, "scan_bwd0": 500,
    "scan_bwd1": 500, "win_fwd": 160, "win_bwd": 400, "dil_fwd0": 105, "dil_fwd1": 105, "dil_fwd2": 175,
    "dil_bwd0": 200, "dil_bwd1": 190, "dil_bwd2": 200, "mlp_up": 155, "mlp_down": 170, "mlp_up_dx": 190,
    "mlp_up_dw": 190, "mlp_down_dx": 165, "mlp_down_dw": 188, "conv_bwd_x": 75, "rope_dil": 150,
    "adamw_w_up": 63, "adamw_w_down": 62, "proj_a": 42, "proj_c": 42, "proj_out": 42, "proj_a_dx": 42,
    "proj_c_dx": 42, "proj_out_dx": 42, "proj_a_dw": 42, "proj_c_dw": 42, "proj_out_dw": 42, "conv_fwd": 65,
    "merge": 65, "merge_bwd": 100, "ssd_post": 50, "ssd_post_bwd": 90, "rope_win": 90, "norm_mix_bwd": 50,
    "norm_mlp_bwd": 50, "dil_combine_bwd": 70,
}


class _Piece:
    def __init__(self, key, row0, rows, scatter, est):
        self.key, self.row0, self.rows, self.scatter, self.est = key, row0, rows, scatter, est


def _coalesce(pieces):
    out = []
    for p in pieces:
        q = out[-1] if out else None
        if q is not None and q.key == p.key and q.scatter == p.scatter and q.row0 + q.rows == p.row0:
            out[-1] = _Piece(q.key, q.row0, q.rows + p.rows, q.scatter, q.est + p.est)
        else:
            out.append(p)
    return out


class _Sched:
    def __init__(self):
        self.queue, self.src, self.dst = [], {}, {}

    def post(self, key, src, scatter):
        r, c = src.shape[-2:]
        self.src[key] = src
        self.dst[key] = lax.empty((N_DEV, r, c), src.dtype)
        row_bytes = c * src.dtype.itemsize
        pr = r
        while pr * row_bytes > PIECE_BYTES and pr % 32 == 0:
            pr //= 2
        for row0 in range(0, r, pr):
            self.queue.append(_Piece(key, row0, pr, scatter, pr * row_bytes * US_PER_PIECE_BYTE))

    def take(self, name):
        budget = CARRIER_US.get(name)
        out, used = [], 0.0
        while budget and self.queue and used + self.queue[0].est <= 1.1 * budget:
            used += self.queue[0].est
            out.append(self.queue.pop(0))
        return _coalesce(out)

    def get(self, key):
        last = max([i for i, p in enumerate(self.queue) if p.key == key], default=-1)
        if last >= 0:
            pieces, self.queue = _coalesce(self.queue[:last + 1]), self.queue[last + 1:]
            for i in range(0, len(pieces), MAX_PIECES):
                _exchange_pieces(self, pieces[i:i + MAX_PIECES], name="exchange_flush")
        return self.dst[key]


_SCHED = None


def _piece_copies(pieces, keys, src_refs, dst_refs, send_sems, recv_sems, loc_sems):
    x, y, c = lax.axis_index("x"), lax.axis_index("y"), lax.axis_index("c")
    me = 4 * x + 2 * y + c
    cps = []
    for t, p in enumerate(pieces):
        ki = keys.index(p.key)
        rows = pl.ds(p.row0, p.rows)
        for j in range(1, N_DEV):
            px = (1 - x) if (j >> 2) & 1 else x
            py = (1 - y) if (j >> 1) & 1 else y
            pc = (1 - c) if j & 1 else c
            src = src_refs[ki].at[4 * px + 2 * py + pc, rows] if p.scatter else src_refs[ki].at[rows]
            cps.append(pltpu.make_async_remote_copy(
                src_ref=src, dst_ref=dst_refs[ki].at[me, rows], send_sem=send_sems.at[t * 7 + j - 1],
                recv_sem=recv_sems.at[t * 7 + j - 1], device_id=(px, py, pc), device_id_type=MESH))
        src = src_refs[ki].at[me, rows] if p.scatter else src_refs[ki].at[rows]
        cps.append(pltpu.make_async_copy(src, dst_refs[ki].at[me, rows], loc_sems.at[t]))
    return cps


def _two_level_gather(pieces, keys, src_refs, dst_refs, send_sems, recv_sems, loc_sems):
    x, y, c = lax.axis_index("x"), lax.axis_index("y"), lax.axis_index("c")
    me = 4 * x + 2 * y + c
    sib = (x, y, 1 - c)
    chips = [(1 - x, y), (x, 1 - y), (1 - x, 1 - y)]

    def copy(t, k, src, dst, to):
        return pltpu.make_async_remote_copy(src_ref=src, dst_ref=dst, send_sem=send_sems.at[7 * t + k],
                                            recv_sem=recv_sems.at[7 * t + k], device_id=to, device_id_type=MESH)

    def landing(p, px, py, pc):
        return dst_refs[keys.index(p.key)].at[4 * px + 2 * py + pc, pl.ds(p.row0, p.rows)]

    sends, local = [], []
    for t, p in enumerate(pieces):
        src = src_refs[keys.index(p.key)].at[pl.ds(p.row0, p.rows)]
        mine = landing(p, x, y, c)
        sends.append(copy(t, 0, src, mine, sib))
        sends += [copy(t, 1 + j, src, mine, (px, py, c)) for j, (px, py) in enumerate(chips)]
        local.append(pltpu.make_async_copy(src, mine, loc_sems.at[t]))
    for cp in sends + local:
        cp.start()
    passed = []
    for t, p in enumerate(pieces):
        for j, (px, py) in enumerate(chips):
            blk = landing(p, px, py, c)
            copy(t, 1 + j, blk, blk, (x, y, c)).wait_recv()
            fwd = copy(t, 4 + j, blk, blk, sib)
            fwd.start()
            passed.append(fwd)
    for t, p in enumerate(pieces):
        blk = landing(p, x, y, 1 - c)
        copy(t, 0, blk, blk, (x, y, c)).wait_recv()
        for j, (px, py) in enumerate(chips):
            blk = landing(p, px, py, 1 - c)
            copy(t, 4 + j, blk, blk, (x, y, c)).wait_recv()
    for cp in sends + passed:
        cp.wait_send()
    for cp in local:
        cp.wait()


def _carry_call(sched, pieces, body, *, name, grid, in_specs, out_specs, out_shape, scratch_shapes, ins, aliases=None):
    keys = []
    for p in pieces:
        if p.key not in keys:
            keys.append(p.key)
    n_in, n_out, nk, npc = len(ins), len(out_shape), len(keys), len(pieces)
    n_scr = len(scratch_shapes)

    def wrapped(*refs):
        in_refs = refs[:n_in]
        src_refs = refs[n_in:n_in + nk]
        out_refs = refs[n_in + 2 * nk:n_in + 2 * nk + n_out]
        dst_refs = refs[n_in + 2 * nk + n_out:n_in + 3 * nk + n_out]
        scr = refs[n_in + 3 * nk + n_out:]
        inner_scr, sems = scr[:n_scr], scr[n_scr:]
        if grid:
            pids = [pl.program_id(a) for a in range(len(grid))]
            first = functools.reduce(lambda u, v: u & v, [q == 0 for q in pids])
            last = functools.reduce(lambda u, v: u & v, [q == g - 1 for q, g in zip(pids, grid)])

            @pl.when(first)
            def _():
                for cp in _piece_copies(pieces, keys, src_refs, dst_refs, *sems):
                    cp.start()

            body(*in_refs, *out_refs, *inner_scr)

            @pl.when(last)
            def _():
                for cp in _piece_copies(pieces, keys, src_refs, dst_refs, *sems):
                    cp.wait()
        elif all(not p.scatter for p in pieces):
            _two_level_gather(pieces, keys, src_refs, dst_refs, *sems)
        else:
            cps = _piece_copies(pieces, keys, src_refs, dst_refs, *sems)
            for cp in cps:
                cp.start()
            for cp in cps:
                cp.wait()

    anyspec = pl.BlockSpec(memory_space=pl.ANY)
    dsts = [sched.dst[k] for k in keys]
    kwargs = dict(grid=grid) if grid else {}
    res = pl.pallas_call(
        wrapped, name=name, in_specs=list(in_specs) + [anyspec] * (2 * nk), out_specs=list(out_specs) + [anyspec] * nk,
        out_shape=list(out_shape) + [jax.ShapeDtypeStruct(d.shape, d.dtype) for d in dsts],
        input_output_aliases={**(aliases or {}), **{n_in + nk + i: n_out + i for i in range(nk)}},
        scratch_shapes=list(scratch_shapes) + [pltpu.SemaphoreType.DMA((7 * npc,)), pltpu.SemaphoreType.DMA((7 * npc,)),
                                               pltpu.SemaphoreType.DMA((npc,))],
        compiler_params=pltpu.CompilerParams(dimension_semantics=("arbitrary",) * len(grid) if grid else None,
                                             vmem_limit_bytes=VMEM_LIMIT, has_side_effects=True),
        **kwargs,
    )(*ins, *[sched.src[k] for k in keys], *dsts)
    for i, k in enumerate(keys):
        sched.dst[k] = res[n_out + i]
    return list(res[:n_out])


def _exchange_pieces(sched, pieces, *, name):
    _carry_call(sched, pieces, None, name=name, grid=(), in_specs=[], out_specs=[], out_shape=[], scratch_shapes=[],
                ins=[])


def _pcall(body, *, name, grid, in_specs, out_specs, out_shape, scratch_shapes=(), sem=None, into=None):
    single = not isinstance(out_shape, (list, tuple))
    out_shape_l = [out_shape] if single else list(out_shape)
    out_specs_l = [out_specs] if single else list(out_specs)
    into = into or {}

    def run(*ins):
        n0, nb = len(ins), len(into)
        specs = list(in_specs) + [pl.BlockSpec(memory_space=pl.ANY)] * nb
        aliases = {n0 + k: oi for k, oi in enumerate(into)}
        for oi, buf in into.items():
            out_shape_l[oi] = jax.ShapeDtypeStruct(buf.shape, buf.dtype)
        kbody = (lambda *refs: body(*refs[:n0], *refs[n0 + nb:])) if nb else body
        args = list(ins) + list(into.values())
        pieces = _SCHED.take(name) if _SCHED is not None else []
        if pieces:
            res = _carry_call(_SCHED, pieces, kbody, name=name, grid=grid, in_specs=specs, out_specs=out_specs_l,
                              out_shape=out_shape_l, scratch_shapes=list(scratch_shapes), ins=args, aliases=aliases)
        else:
            res = pl.pallas_call(kbody, name=name, grid=grid, in_specs=specs, out_specs=out_specs_l,
                                 out_shape=out_shape_l, scratch_shapes=list(scratch_shapes),
                                 input_output_aliases=aliases, compiler_params=_cparams(sem))(*args)
        return res[0] if single else list(res)

    return run


def _pick(dim, cands):
    for c in cands:
        if dim % c == 0:
            return c
    return dim


def _mm_tiles(m, n, k, a_bytes, b_bytes, o_bytes, has_add):
    tm = _pick(m, (1024, 512, 256, 128))
    tn = _pick(n, (1024, 1792, 512, 256, 128))
    for tk in (2048, 1792, 1024, 896, 512, 256, 128):
        if k % tk:
            continue
        need = 2 * (tm * tk * a_bytes + tk * tn * b_bytes + tm * tn * (o_bytes + (4 if has_add else 0)))
        need += tm * tn * 4 if k // tk > 1 else 0
        if need <= MM_VMEM_BUDGET:
            return tm, tn, tk
    return tm, tn, _pick(k, (128,))


def _mm(a, b, *, ta=False, tb=False, out_dtype=F32, add=None, post=None, name):
    m, k = (a.shape[1], a.shape[0]) if ta else a.shape
    k2, n = (b.shape[1], b.shape[0]) if tb else b.shape
    assert k == k2, (a.shape, b.shape, ta, tb)
    pfn, pins, pdts = post if post is not None else (None, [], [out_dtype])
    o_bytes = sum(jnp.dtype(dt).itemsize for dt in pdts) + sum(e.dtype.itemsize for e in pins)
    tm, tn, tk = _mm_tiles(m, n, k, a.dtype.itemsize, b.dtype.itemsize, o_bytes, add is not None)
    assert m % tm == 0 and n % tn == 0 and k % tk == 0, (m, n, k, tm, tn, tk)
    nk = k // tk
    a_spec = pl.BlockSpec((tk, tm), lambda i, j, kk: (kk, i)) if ta else pl.BlockSpec((tm, tk), lambda i, j, kk: (i, kk))
    b_spec = pl.BlockSpec((tn, tk), lambda i, j, kk: (j, kk)) if tb else pl.BlockSpec((tk, tn), lambda i, j, kk: (kk, j))
    o_spec = pl.BlockSpec((tm, tn), lambda i, j, kk: (i, j))
    dims = (((0 if ta else 1,), (1 if tb else 0,)), ((), ()))
    has_add = add is not None

    nx, no = (1 if has_add else 0) + len(pins), len(pdts)

    def body(*refs):
        a_ref, b_ref = refs[:2]
        x_refs = refs[2:2 + nx]
        o_refs = refs[2 + nx:2 + nx + no]
        part = lax.dot_general(a_ref[...].astype(BF16), b_ref[...].astype(BF16), dims, preferred_element_type=F32)

        def finish(r):
            if has_add:
                r = r + x_refs[0][...]
            outs = pfn(r, *[x[...] for x in x_refs[1 if has_add else 0:]]) if pfn else (r,)
            for o_ref, v in zip(o_refs, outs):
                o_ref[...] = v.astype(o_ref.dtype)

        if nk == 1:
            finish(part)
            return
        acc_ref = refs[-1]
        kk = pl.program_id(2)

        @pl.when(kk == 0)
        def _():
            acc_ref[...] = part

        @pl.when(kk > 0)
        def _():
            acc_ref[...] += part

        @pl.when(kk == nk - 1)
        def _():
            finish(acc_ref[...])

    ins = [a, b] + ([add] if has_add else []) + list(pins)
    specs = [a_spec, b_spec] + [o_spec] * nx
    res = _pcall(
        body, name=name, grid=(m // tm, n // tn, nk), in_specs=specs, out_specs=[o_spec] * no,
        out_shape=[jax.ShapeDtypeStruct((m, n), dt) for dt in pdts],
        scratch_shapes=[pltpu.VMEM((tm, tn), F32)] if nk > 1 else [],
        sem=("parallel", "parallel", "arbitrary"),
    )(*ins)
    return res if post is not None else res[0]


def _rowcall(fn, rows, pars, row_outs, par_outs, *, tile, name):
    t = rows[0][0].shape[0]
    tile = min(tile, t)
    assert t % tile == 0
    nr, npar, nro, npo = len(rows), len(pars), len(row_outs), len(par_outs)
    in_specs = [pl.BlockSpec((tile, c), functools.partial(lambda i, cb: (i, cb), cb=cb)) for (_, c, cb) in rows]
    in_specs += [pl.BlockSpec(p.shape, lambda i: (0, 0)) for p in pars]
    into = {k: ro[2] for k, ro in enumerate(row_outs) if len(ro) == 4}
    out_specs = [pl.BlockSpec((tile, ro[0]), functools.partial(lambda i, cb: (i, cb), cb=ro[3] if len(ro) == 4 else 0))
                 for ro in row_outs]
    out_specs += [pl.BlockSpec(s, lambda i: (0, 0)) for s in par_outs]
    out_shape = [jax.ShapeDtypeStruct((t, ro[0]), ro[1]) for ro in row_outs]
    out_shape += [jax.ShapeDtypeStruct(s, F32) for s in par_outs]

    def body(*refs):
        rv = [r[...] for r in refs[:nr]]
        pv = [r[...] for r in refs[nr:nr + npar]]
        ro_refs = refs[nr + npar:nr + npar + nro]
        po_refs = refs[nr + npar + nro:]
        ro, po = fn(rv, pv)
        for ref, v in zip(ro_refs, ro):
            ref[...] = v.astype(ref.dtype)
        if npo:
            @pl.when(pl.program_id(0) == 0)
            def _():
                for ref in po_refs:
                    ref[...] = jnp.zeros_like(ref)
            for ref, v in zip(po_refs, po):
                ref[...] += v

    res = _pcall(
        body, name=name, grid=(t // tile,), in_specs=in_specs, out_specs=out_specs, out_shape=out_shape,
        sem=("arbitrary",), into=into,
    )(*[r[0] for r in rows], *pars)
    return list(res[:nro]), list(res[nro:])


def _map2d(fn, ins, out_dtype, *, name, tile=256, cw=2048):
    t, w = ins[0].shape
    tile, cw = min(tile, t), min(cw, w)
    assert t % tile == 0 and w % cw == 0

    def body(*refs):
        refs[-1][...] = fn(*[r[...] for r in refs[:-1]]).astype(out_dtype)

    spec = pl.BlockSpec((tile, cw), lambda i, j: (i, j))
    return pl.pallas_call(
        body, name=name, grid=(t // tile, w // cw), in_specs=[spec] * len(ins), out_specs=spec,
        out_shape=jax.ShapeDtypeStruct((t, w), out_dtype), compiler_params=_cparams(("parallel", "parallel")),
    )(*ins)


def _exchange(srcs, *, scatter, name):
    n = len(srcs)
    out_shape = [jax.ShapeDtypeStruct(s.shape if scatter else (N_DEV,) + s.shape, s.dtype) for s in srcs]

    def body(*refs):
        src_refs, out_refs = refs[:n], refs[n:2 * n]
        send_sems, recv_sems, loc_sems = refs[2 * n:]
        x, y, c = lax.axis_index("x"), lax.axis_index("y"), lax.axis_index("c")
        me = 4 * x + 2 * y + c
        copies = []
        for a in range(n):
            for j in range(1, N_DEV):
                px = (1 - x) if (j >> 2) & 1 else x
                py = (1 - y) if (j >> 1) & 1 else y
                pc = (1 - c) if j & 1 else c
                src = src_refs[a].at[4 * px + 2 * py + pc] if scatter else src_refs[a]
                cp = pltpu.make_async_remote_copy(
                    src_ref=src, dst_ref=out_refs[a].at[me], send_sem=send_sems.at[a * 7 + j - 1],
                    recv_sem=recv_sems.at[a * 7 + j - 1], device_id=(px, py, pc), device_id_type=MESH)
                cp.start()
                copies.append(cp)
            src = src_refs[a].at[me] if scatter else src_refs[a]
            cp = pltpu.make_async_copy(src, out_refs[a].at[me], loc_sems.at[a])
            cp.start()
            copies.append(cp)
        for cp in copies:
            cp.wait()

    anyspec = pl.BlockSpec(memory_space=pl.ANY)
    return pl.pallas_call(
        body, name=name, in_specs=[anyspec] * n, out_specs=[anyspec] * n, out_shape=out_shape,
        scratch_shapes=[pltpu.SemaphoreType.DMA((7 * n,)), pltpu.SemaphoreType.DMA((7 * n,)),
                        pltpu.SemaphoreType.DMA((n,))],
        compiler_params=pltpu.CompilerParams(has_side_effects=True),
    )(*srcs)


def _row_tile(r, c, budget_elems=256 * 1024):
    tr = r
    while tr * c > budget_elems and tr % 16 == 0:
        tr //= 2
    return tr


def _adamw(w, recvs, m, v, *, name):
    r, c = w.shape
    nl = len(recvs)
    ns, rl = recvs[0].shape[:2]
    assert rl * nl == r
    tr = _row_tile(rl, c, budget_elems=(512 * 1024) // max(nl, 2))
    nt = rl // tr
    bc1 = 1.0 / (1.0 - ADAM_B1 ** ADAM_STEP)
    bc2 = 1.0 / (1.0 - ADAM_B2 ** ADAM_STEP)

    def body(*refs):
        w_ref, m_ref, v_ref = refs[:3]
        r_refs = refs[3:3 + nl]
        g_ref, d_ref, mo_ref, vo_ref = refs[3 + nl:]
        i = pl.program_id(0)
        for k in range(nl):
            @pl.when(i // nt == k)
            def _(k=k):
                g = r_refs[k][0].astype(F32)
                for s in range(1, ns):
                    g = g + r_refs[k][s].astype(F32)
                g_ref[...] = g
        g = g_ref[...]
        mn = ADAM_B1 * m_ref[...] + (1.0 - ADAM_B1) * g
        vn = ADAM_B2 * v_ref[...] + (1.0 - ADAM_B2) * (g * g)
        mo_ref[...] = mn
        vo_ref[...] = vn
        d_ref[...] = -ADAM_LR * ((mn * bc1) / (jnp.sqrt(vn * bc2) + ADAM_EPS) + ADAM_WD * w_ref[...])

    spec = pl.BlockSpec((tr, c), lambda i: (i, 0))
    rspecs = [pl.BlockSpec((ns, tr, c), functools.partial(lambda i, k: (0, jnp.clip(i - k * nt, 0, nt - 1), 0), k=k))
              for k in range(nl)]
    return _pcall(
        body, name=name, grid=(r // tr,), in_specs=[spec, spec, spec] + rspecs,
        out_specs=[spec] * 4, out_shape=[jax.ShapeDtypeStruct((r, c), F32)] * 4, sem=("arbitrary",),
    )(w, m, v, *recvs)


def _sum_slots(recv, *, name):
    ns, r, c = recv.shape

    def body(r_ref, o_ref):
        g = r_ref[0]
        for s in range(1, ns):
            g = g + r_ref[s]
        o_ref[...] = g

    return pl.pallas_call(body, name=name, out_shape=jax.ShapeDtypeStruct((r, c), F32))(recv)


def _rms(x, g):
    return x * lax.rsqrt(jnp.mean(x * x, axis=-1, keepdims=True) + EPS) * g


def _silu(x):
    return x * jax.nn.sigmoid(x)


def _merge_f(a, b, c, g0, g1, g2):
    return jax.nn.sigmoid(g0) * a + jax.nn.sigmoid(g1) * b + jax.nn.sigmoid(g2) * c


def _ssd_post_f(yf, yb, xs, z, dskip, gnorm):
    y = (yf + yb + dskip * xs) * _silu(z)
    return _rms(y, gnorm)


def _combine_f(o0, o1, o2, l0, l1, l2):
    m = jnp.maximum(jnp.maximum(l0, l1), l2)
    e0, e1, e2 = jnp.exp(l0 - m), jnp.exp(l1 - m), jnp.exp(l2 - m)
    return (e0 * o0 + e1 * o1 + e2 * o2) / (e0 + e1 + e2)


def _key_window(ln, hw):
    return min(ln, QBLK + 2 * hw)


def _key_start(i, ln, hw):
    return pl.multiple_of(jnp.clip(i * QBLK - hw, 0, ln - _key_window(ln, hw)), 64)


def _attn_block(q, k3, v3, sk, qs, ks, *, hw, has_sink):
    s = lax.dot_general(q.astype(BF16), k3.astype(BF16), (((1,), (1,)), ((), ())),
                        preferred_element_type=F32) * (HEAD_DIM ** -0.5)
    qpos = qs + lax.broadcasted_iota(jnp.int32, s.shape, 0)
    kpos = ks + lax.broadcasted_iota(jnp.int32, s.shape, 1)
    valid = jnp.abs(qpos - kpos) <= hw
    s = jnp.where(valid, s, NEG_INF)
    m = jnp.max(s, axis=-1, keepdims=True)
    if has_sink:
        m = jnp.maximum(m, sk)
    m = lax.stop_gradient(m)
    e = jnp.exp(s - m)
    l = jnp.sum(e, axis=-1, keepdims=True)
    if has_sink:
        l = l + jnp.exp(sk - m)
    o = jnp.dot(e.astype(BF16), v3.astype(BF16), preferred_element_type=F32) / l
    return o, m + jnp.log(l)


def _ssd_chunk(state, xs, bm, cm, dtr, dtr_t, bias, bias_t, alog, alog_t, *, reverse):
    t = xs.shape[0]
    hg = dtr.shape[1]
    hp = xs.shape[1]
    p = hp // hg
    dt = jax.nn.softplus(dtr + bias)
    dt_t = jax.nn.softplus(dtr_t + bias_t)
    dta = dt * (-jnp.exp(alog))
    dta_t = dt_t * (-jnp.exp(alog_t))
    li = lax.broadcasted_iota(jnp.int32, (t, t), 0)
    si = lax.broadcasted_iota(jnp.int32, (t, t), 1)
    tri = (li <= si) if reverse else (li >= si)
    trif = tri.astype(F32)
    cs = jnp.dot(trif, dta, precision=HI, preferred_element_type=F32)
    cs_t = lax.dot_general(dta_t, trif, (((1,), (1,)), ((), ())), precision=HI,
                           preferred_element_type=F32)
    total = jnp.sum(dta, axis=0, keepdims=True)
    cb = lax.dot_general(cm.astype(BF16), bm.astype(BF16), (((1,), (1,)), ((), ())),
                         preferred_element_type=F32)
    lane_h = lax.broadcasted_iota(jnp.int32, (1, hp), 1) // p
    col_h = lax.broadcasted_iota(jnp.int32, (1, hg), 1)
    row_h = lax.broadcasted_iota(jnp.int32, (hg, 1), 0)
    dt_x = jnp.zeros((t, hp), F32)
    ecs_x = jnp.zeros((t, hp), F32)
    ds_x = jnp.zeros((t, hp), F32)
    etot_x = jnp.zeros((1, hp), F32)
    decays, masks = [], []
    for h in range(hg):
        oh = (col_h == h).astype(F32)
        oh_t = (row_h == h).astype(F32)
        mk = (lane_h == h).astype(F32)
        dt_h = jnp.sum(dt * oh, axis=1, keepdims=True)
        cs_h = jnp.sum(cs * oh, axis=1, keepdims=True)
        cst_h = jnp.sum(cs_t * oh_t, axis=0, keepdims=True)
        tot_h = jnp.sum(total * oh, axis=1, keepdims=True)
        dt_x = dt_x + dt_h * mk
        ecs_x = ecs_x + jnp.exp(cs_h) * mk
        ds_x = ds_x + jnp.exp(tot_h - cs_h) * mk
        etot_x = etot_x + jnp.exp(tot_h) * mk
        decays.append(jnp.exp(jnp.where(tri, cs_h - cst_h, -jnp.inf)))
        masks.append(mk)
    xdt = xs * dt_x
    y = jnp.dot(cm.astype(BF16), state.astype(BF16), preferred_element_type=F32) * ecs_x
    for h in range(hg):
        y = y + jnp.dot((cb * decays[h]).astype(BF16), (xdt * masks[h]).astype(BF16),
                        preferred_element_type=F32)
    st_new = lax.dot_general(bm.astype(BF16), (xdt * ds_x).astype(BF16), (((0,), (0,)), ((), ())),
                             preferred_element_type=F32)
    return y, state * etot_x + st_new


def _rope_tables(seq):
    half = ROPE_DIM // 2
    inv = ROPE_THETA ** (-jnp.arange(0, ROPE_DIM, 2, dtype=F32) / ROPE_DIM)
    ang = jnp.arange(seq, dtype=F32)[:, None] * inv[None, :]
    cos, sin = jnp.cos(ang), jnp.sin(ang)
    rest = HEAD_DIM - ROPE_DIM
    c = jnp.concatenate([cos, cos, jnp.ones((seq, rest), F32)], axis=1)
    a = jnp.concatenate([-sin, jnp.zeros((seq, HEAD_DIM - half), F32)], axis=1)
    b = jnp.concatenate([jnp.zeros((seq, half), F32), sin, jnp.zeros((seq, rest), F32)], axis=1)
    return c, a, b


def _rope(src, tabs, *, col0, width, seq, group, inverse, out_dtype, name, into=None, out_col0=0):
    t = src.shape[0]
    half = ROPE_DIM // 2
    nhb = 6 if all(v % (6 * HEAD_DIM) == 0 for v in (width, col0, out_col0)) else 3
    cw, tile = nhb * HEAD_DIM, 512
    assert width % cw == 0 and col0 % cw == 0 and out_col0 % cw == 0 and seq % tile == 0 and t % tile == 0
    ns = seq // tile

    def body(x_ref, c_ref, a_ref, b_ref, o_ref):
        jb = pl.program_id(1)
        c, a, b = c_ref[...], a_ref[...], b_ref[...]
        for hh in range(nhb):
            xv = x_ref[:, hh * HEAD_DIM:(hh + 1) * HEAD_DIM].astype(F32)
            if inverse:
                yv = xv * c + pltpu.roll(xv * a, half, 1) + pltpu.roll(xv * b, HEAD_DIM - half, 1)
            else:
                yv = xv * c + pltpu.roll(xv, HEAD_DIM - half, 1) * a + pltpu.roll(xv, half, 1) * b
            if group:
                keep = ((jb * nhb + hh) % group) == (group - 1)
                yv = jnp.where(keep, xv, yv)
            o_ref[:, hh * HEAD_DIM:(hh + 1) * HEAD_DIM] = yv.astype(o_ref.dtype)

    tspec = pl.BlockSpec((tile, HEAD_DIM), lambda i, j: (i % ns, 0))
    return _pcall(
        body, name=name, grid=(t // tile, width // cw),
        in_specs=[pl.BlockSpec((tile, cw), lambda i, j: (i, col0 // cw + j)), tspec, tspec, tspec],
        out_specs=pl.BlockSpec((tile, cw), lambda i, j: (i, out_col0 // cw + j)),
        out_shape=jax.ShapeDtypeStruct((t, width), out_dtype),
        sem=("parallel", "parallel"), into=None if into is None else {0: into},
    )(src, *tabs)


def _shift_rows(x, d, tpos):
    if d == 0:
        return x
    s = x.shape[0]
    y = pltpu.roll(x, (-d) % s, 0)
    ok = (tpos + d >= 0) & (tpos + d < s)
    return jnp.where(ok, y, 0.0)


def _conv_fwd(p, w8, bias, *, col0, chans, batch, seq, name):
    cb = 256
    assert chans % cb == 0 and col0 % cb == 0
    pad = (CONV_WIDTH - 1) // 2

    def body(x_ref, w_ref, b_ref, o_ref):
        x = x_ref[...]
        tpos = lax.broadcasted_iota(jnp.int32, x.shape, 0)
        acc = jnp.broadcast_to(b_ref[...], x.shape)
        for k in range(CONV_WIDTH):
            acc = acc + w_ref[k:k + 1, :] * _shift_rows(x, k - pad, tpos)
        o_ref[...] = _silu(acc)

    return _pcall(
        body, name=name, grid=(chans // cb, batch),
        in_specs=[pl.BlockSpec((seq, cb), lambda j, b: (b, col0 // cb + j)),
                  pl.BlockSpec((8, cb), lambda j, b: (0, j)), pl.BlockSpec((1, cb), lambda j, b: (0, j))],
        out_specs=pl.BlockSpec((seq, cb), lambda j, b: (b, j)),
        out_shape=jax.ShapeDtypeStruct((batch * seq, chans), F32),
        sem=("parallel", "arbitrary"),
    )(p, w8, bias)


def _conv_bwd(p, w8, bias, dus, *, col0, ch0, chans, batch, seq, into, out_col0, name):
    cb = 256 if chans % 256 == 0 and ch0 % 256 == 0 else 128
    assert chans % cb == 0 and ch0 % cb == 0 and col0 % cb == 0 and out_col0 % cb == 0
    pad = (CONV_WIDTH - 1) // 2
    ndu = len(dus)

    def body(*refs):
        x_ref, w_ref, b_ref = refs[:3]
        du_refs = refs[3:3 + ndu]
        dx_ref, dw_ref, db_ref = refs[3 + ndu:]
        du = du_refs[0][...]
        for r in du_refs[1:]:
            du = du + r[...]
        _conv_bwd_block(x_ref, w_ref, b_ref, du, dx_ref, dw_ref, db_ref)

    c0 = (col0 + ch0) // cb
    return _pcall(
        body, name=name, grid=(chans // cb, batch),
        in_specs=[pl.BlockSpec((seq, cb), lambda j, b: (b, c0 + j)),
                  pl.BlockSpec((8, cb), lambda j, b: (0, ch0 // cb + j)),
                  pl.BlockSpec((1, cb), lambda j, b: (0, ch0 // cb + j))]
        + [pl.BlockSpec((seq, cb), lambda j, b: (b, j))] * ndu,
        out_specs=[pl.BlockSpec((seq, cb), lambda j, b: (b, (out_col0 + ch0) // cb + j)),
                   pl.BlockSpec((8, cb), lambda j, b: (0, j)), pl.BlockSpec((1, cb), lambda j, b: (0, j))],
        out_shape=[jax.ShapeDtypeStruct(into.shape, into.dtype), jax.ShapeDtypeStruct((8, chans), F32),
                   jax.ShapeDtypeStruct((1, chans), F32)],
        sem=("parallel", "arbitrary"), into={0: into},
    )(p, w8, bias, *dus)


def _conv_bwd_block(x_ref, w_ref, b_ref, du, dx_ref, dw_ref, db_ref):
    pad = (CONV_WIDTH - 1) // 2
    x = x_ref[...]
    tpos = lax.broadcasted_iota(jnp.int32, x.shape, 0)
    acc = jnp.broadcast_to(b_ref[...], x.shape)
    xs = []
    for k in range(CONV_WIDTH):
        xs.append(_shift_rows(x, k - pad, tpos))
        acc = acc + w_ref[k:k + 1, :] * xs[k]
    sg = jax.nn.sigmoid(acc)
    dacc = du * (sg * (1.0 + acc * (1.0 - sg)))
    dx = jnp.zeros_like(x)
    for k in range(CONV_WIDTH):
        dx = dx + w_ref[k:k + 1, :] * _shift_rows(dacc, pad - k, tpos)
    dx_ref[...] = dx.astype(dx_ref.dtype)

    @pl.when(pl.program_id(1) == 0)
    def _():
        dw_ref[...] = jnp.zeros_like(dw_ref)
        db_ref[...] = jnp.zeros_like(db_ref)

    for k in range(CONV_WIDTH):
        dw_ref[k:k + 1, :] += jnp.sum(dacc * xs[k], axis=0, keepdims=True)
    db_ref[...] += jnp.sum(dacc, axis=0, keepdims=True)


def _scan_gpb(groups):
    return 2 if groups % 2 == 0 else 1


def _scan_specs(batch, nc, groups, hg, inner, reverse_order):
    gpb = _scan_gpb(groups)
    t, n, hp = SSD_CHUNK, SSD_STATE, hg * SSD_HEAD_DIM
    ncb = inner // (gpb * n)
    ngb = groups // gpb

    def row(b, c):
        return b * nc + ((nc - 1 - c) if reverse_order else c)

    return dict(
        xs=pl.BlockSpec((t, gpb * hp), lambda g, b, c: (row(b, c), g)),
        bm=pl.BlockSpec((t, gpb * n), lambda g, b, c: (row(b, c), ncb + g)),
        cm=pl.BlockSpec((t, gpb * n), lambda g, b, c: (row(b, c), ncb + ngb + g)),
        dtr=pl.BlockSpec((gpb, t, hg), lambda g, b, c: (g, row(b, c), 0)),
        dtr_t=pl.BlockSpec((gpb, hg, t), lambda g, b, c: (g, 0, row(b, c))),
        par=pl.BlockSpec((gpb, 1, hg), lambda g, b, c: (g, 0, 0)),
        par_t=pl.BlockSpec((gpb, hg, 1), lambda g, b, c: (g, 0, 0)),
        y=pl.BlockSpec((t, gpb * hp), lambda g, b, c: (row(b, c), g)),
        nrow=pl.BlockSpec((t, gpb * n), lambda g, b, c: (row(b, c), g)),
        st=pl.BlockSpec((gpb, None, n, hp), lambda g, b, c: (g, row(b, c), 0, 0)),
    )


def _scan_fwd(u, dtr, dtr_t, bias, bias_t, alog, alog_t, *, batch, seq, inner, reverse, name):
    groups, hg = dtr.shape[0], dtr.shape[2]
    nc = seq // SSD_CHUNK
    hp = hg * SSD_HEAD_DIM
    sp = _scan_specs(batch, nc, groups, hg, inner, reverse)
    gpb, n = _scan_gpb(groups), SSD_STATE

    def body(xs_ref, bm_ref, cm_ref, dtr_ref, dtrt_ref, b_ref, bt_ref, a_ref, at_ref, y_ref, st_ref, state):
        @pl.when(pl.program_id(2) == 0)
        def _():
            state[...] = jnp.zeros_like(state)

        for k in range(gpb):
            xc, nc_ = slice(k * hp, (k + 1) * hp), slice(k * n, (k + 1) * n)
            st_in = state[k]
            st_ref[k] = st_in
            y, st_out = _ssd_chunk(st_in, xs_ref[:, xc], bm_ref[:, nc_], cm_ref[:, nc_], dtr_ref[k], dtrt_ref[k],
                                   b_ref[k], bt_ref[k], a_ref[k], at_ref[k], reverse=reverse)
            y_ref[:, xc] = y
            state[k] = st_out

    return _pcall(
        body, name=name, grid=(groups // gpb, batch, nc),
        in_specs=[sp["xs"], sp["bm"], sp["cm"], sp["dtr"], sp["dtr_t"], sp["par"], sp["par_t"], sp["par"], sp["par_t"]],
        out_specs=[sp["y"], sp["st"]],
        out_shape=[jax.ShapeDtypeStruct((batch * seq, inner), F32),
                   jax.ShapeDtypeStruct((groups, batch * nc, SSD_STATE, hp), F32)],
        scratch_shapes=[pltpu.VMEM((gpb, SSD_STATE, hp), F32)],
        sem=("parallel", "arbitrary", "arbitrary"),
    )(u, u, u, dtr, dtr_t, bias, bias_t, alog, alog_t)


def _scan_bwd(u, dtr, dtr_t, bias, bias_t, alog, alog_t, st, dy, *, batch, seq, inner, reverse, name):
    groups, hg = dtr.shape[0], dtr.shape[2]
    nc = seq // SSD_CHUNK
    hp = hg * SSD_HEAD_DIM
    t = batch * seq
    sp = _scan_specs(batch, nc, groups, hg, inner, not reverse)
    f = functools.partial(_ssd_chunk, reverse=reverse)
    gpb, n = _scan_gpb(groups), SSD_STATE

    def body(xs_ref, bm_ref, cm_ref, dtr_ref, dtrt_ref, b_ref, bt_ref, a_ref, at_ref, st_ref, dy_ref,
             dxs_ref, dbm_ref, dcm_ref, ddtr_ref, ddtrt_ref, db_ref, dbt_ref, da_ref, dat_ref, dstate):
        first = (pl.program_id(1) == 0) & (pl.program_id(2) == 0)

        @pl.when(pl.program_id(2) == 0)
        def _():
            dstate[...] = jnp.zeros_like(dstate)

        @pl.when(first)
        def _():
            for r in (db_ref, dbt_ref, da_ref, dat_ref):
                r[...] = jnp.zeros_like(r)

        for k in range(gpb):
            xc, nc_ = slice(k * hp, (k + 1) * hp), slice(k * n, (k + 1) * n)
            _, vjp = jax.vjp(f, st_ref[k], xs_ref[:, xc], bm_ref[:, nc_], cm_ref[:, nc_], dtr_ref[k], dtrt_ref[k],
                             b_ref[k], bt_ref[k], a_ref[k], at_ref[k])
            dst, dxs, dbm, dcm, ddtr, ddtrt, db, dbt, da, dat = vjp((dy_ref[:, xc], dstate[k]))
            dstate[k] = dst
            dxs_ref[:, xc] = dxs
            dbm_ref[:, nc_] = dbm
            dcm_ref[:, nc_] = dcm
            ddtr_ref[k] = ddtr
            ddtrt_ref[k] = ddtrt
            db_ref[k] += db
            dbt_ref[k] += dbt
            da_ref[k] += da
            dat_ref[k] += dat

    gn = groups * SSD_STATE
    return _pcall(
        body, name=name, grid=(groups // gpb, batch, nc),
        in_specs=[sp["xs"], sp["bm"], sp["cm"], sp["dtr"], sp["dtr_t"], sp["par"], sp["par_t"], sp["par"], sp["par_t"],
                  sp["st"], sp["y"]],
        out_specs=[sp["y"], sp["nrow"], sp["nrow"], sp["dtr"], sp["dtr_t"], sp["par"], sp["par_t"], sp["par"], sp["par_t"]],
        out_shape=[jax.ShapeDtypeStruct((t, inner), F32), jax.ShapeDtypeStruct((t, gn), F32),
                   jax.ShapeDtypeStruct((t, gn), F32), jax.ShapeDtypeStruct(dtr.shape, F32),
                   jax.ShapeDtypeStruct(dtr_t.shape, F32), jax.ShapeDtypeStruct(bias.shape, F32),
                   jax.ShapeDtypeStruct(bias_t.shape, F32), jax.ShapeDtypeStruct(alog.shape, F32),
                   jax.ShapeDtypeStruct(alog_t.shape, F32)],
        scratch_shapes=[pltpu.VMEM((gpb, SSD_STATE, hp), F32)],
        sem=("parallel", "arbitrary", "arbitrary"),
    )(u, u, u, dtr, dtr_t, bias, bias_t, alog, alog_t, st, dy)


def _attn_load(ref, col, blk):
    return ref[pl.ds(pl.multiple_of(blk * QBLK, QBLK), QBLK), col * HEAD_DIM:(col + 1) * HEAD_DIM].astype(F32)


def _lane0(row):
    lane = lax.broadcasted_iota(jnp.int32, row.shape, 1)
    return jnp.sum(jnp.where(lane == 0, row, 0.0), axis=1, keepdims=True)


def _attn_fwd(rq, sinkx, *, batch, seq, dil, nbw, cb0, nh, rep, hw, want_lse, out_dtype, name):
    t, w = rq.shape
    ln = seq // dil
    nb = ln // QBLK
    bw = (rep + 2) * HEAD_DIM
    ow = nh * rep * HEAD_DIM
    has_sink = sinkx is not None
    rq3 = rq.reshape(batch, ln, dil * w)
    kwin = _key_window(ln, hw)
    f = functools.partial(_attn_block, hw=hw, has_sink=has_sink)

    def body(*refs):
        if has_sink:
            blk_ref, sink_ref = refs[:2]
            outs = refs[2:]
        else:
            blk_ref, sink_ref = refs[0], None
            outs = refs[1:]
        o_ref = outs[0]
        lse_ref = outs[1] if want_lse else None
        g = pl.program_id(2)

        def qblock(i, carry):
            ks = _key_start(i, ln, hw)
            k3 = blk_ref[pl.ds(ks, kwin), rep * HEAD_DIM:(rep + 1) * HEAD_DIM].astype(F32)
            v3 = blk_ref[pl.ds(ks, kwin), (rep + 1) * HEAD_DIM:].astype(F32)
            rows = pl.ds(pl.multiple_of(i * QBLK, QBLK), QBLK)
            for r in range(rep):
                sk = _lane0(sink_ref[pl.ds(g * rep + r, 1), :]) if has_sink else None
                o, lse = f(_attn_load(blk_ref, r, i), k3, v3, sk, i * QBLK, ks)
                o_ref[rows, r * HEAD_DIM:(r + 1) * HEAD_DIM] = o.astype(o_ref.dtype)
                if want_lse:
                    lse_ref[rows, r * HEAD_DIM:(r + 1) * HEAD_DIM] = jnp.broadcast_to(lse, o.shape)
            return carry

        lax.fori_loop(0, nb, qblock, 0, unroll=2 if nb % 2 == 0 else 1)

    in_specs = [pl.BlockSpec((None, ln, bw), lambda b, r, h: (b, 0, r * nbw + cb0 + h))]
    ins = [rq3]
    if has_sink:
        in_specs.append(pl.BlockSpec(sinkx.shape, lambda b, r, h: (0, 0)))
        ins.append(sinkx)
    ospec = pl.BlockSpec((None, ln, rep * HEAD_DIM), lambda b, r, h: (b, 0, r * nh + h))
    out_shape = [jax.ShapeDtypeStruct((batch, ln, dil * ow), out_dtype)]
    out_specs = [ospec]
    if want_lse:
        out_shape.append(jax.ShapeDtypeStruct((batch, ln, dil * ow), F32))
        out_specs.append(ospec)
    res = _pcall(
        body, name=name, grid=(batch, dil, nh), in_specs=in_specs, out_specs=out_specs, out_shape=out_shape,
        sem=("parallel", "parallel", "parallel"),
    )(*ins)
    return [r.reshape(t, ow) for r in res]


def _attn_bwd(rq, sinkx, do, dlse, *, batch, seq, dil, nbw, cb0, nh, rep, hw, name):
    t, w = rq.shape
    ln = seq // dil
    nb = ln // QBLK
    bw = (rep + 2) * HEAD_DIM
    ow = nh * rep * HEAD_DIM
    has_sink = sinkx is not None
    has_lse = dlse is not None
    kwin = _key_window(ln, hw)
    f = functools.partial(_attn_block, hw=hw, has_sink=has_sink)

    def body(*refs):
        refs = list(refs)
        blk_ref = refs.pop(0)
        sink_ref = refs.pop(0) if has_sink else None
        do_ref = refs.pop(0)
        dlse_ref = refs.pop(0) if has_lse else None
        d_ref = refs.pop(0)
        dsink_ref = refs.pop(0) if has_sink else None
        g = pl.program_id(2)
        d_ref[:, rep * HEAD_DIM:] = jnp.zeros((ln, 2 * HEAD_DIM), F32)
        if has_sink:
            @pl.when((pl.program_id(0) == 0) & (pl.program_id(1) == 0) & (g == 0))
            def _():
                dsink_ref[...] = jnp.zeros_like(dsink_ref)

        def qblock(i, carry):
            ks = _key_start(i, ln, hw)
            krows = pl.ds(ks, kwin)
            k3 = blk_ref[krows, rep * HEAD_DIM:(rep + 1) * HEAD_DIM].astype(F32)
            v3 = blk_ref[krows, (rep + 1) * HEAD_DIM:].astype(F32)
            rows = pl.ds(pl.multiple_of(i * QBLK, QBLK), QBLK)
            dk3 = jnp.zeros_like(k3)
            dv3 = jnp.zeros_like(v3)
            for r in range(rep):
                cols = slice(r * HEAD_DIM, (r + 1) * HEAD_DIM)
                q = _attn_load(blk_ref, r, i)
                dov = do_ref[rows, cols]
                dl = dlse_ref[rows, cols] if has_lse else jnp.zeros_like(dov)
                if has_sink:
                    srow = sink_ref[pl.ds(g * rep + r, 1), :]
                    _, vjp = jax.vjp(lambda q_, k_, v_, s_: f(q_, k_, v_, _lane0(s_), i * QBLK, ks), q, k3, v3, srow)
                    dq, dk, dv, ds = vjp((dov, jnp.sum(dl, axis=1, keepdims=True)))
                    dsink_ref[pl.ds(g * rep + r, 1), :] += ds
                else:
                    _, vjp = jax.vjp(lambda q_, k_, v_: f(q_, k_, v_, None, i * QBLK, ks), q, k3, v3)
                    dq, dk, dv = vjp((dov, jnp.sum(dl, axis=1, keepdims=True)))
                d_ref[rows, cols] = dq
                dk3 = dk3 + dk
                dv3 = dv3 + dv
            d_ref[krows, rep * HEAD_DIM:(rep + 1) * HEAD_DIM] += dk3
            d_ref[krows, (rep + 1) * HEAD_DIM:] += dv3
            return carry

        lax.fori_loop(0, nb, qblock, 0, unroll=2 if nb % 2 == 0 else 1)

    ospec = pl.BlockSpec((None, ln, rep * HEAD_DIM), lambda b, r, h: (b, 0, r * nh + h))
    in_specs = [pl.BlockSpec((None, ln, bw), lambda b, r, h: (b, 0, r * nbw + cb0 + h))]
    ins = [rq.reshape(batch, ln, dil * w)]
    if has_sink:
        in_specs.append(pl.BlockSpec(sinkx.shape, lambda b, r, h: (0, 0)))
        ins.append(sinkx)
    in_specs.append(ospec)
    ins.append(do.reshape(batch, ln, dil * ow))
    if has_lse:
        in_specs.append(ospec)
        ins.append(dlse.reshape(batch, ln, dil * ow))
    dw = nh * bw
    out_specs = [pl.BlockSpec((None, ln, bw), lambda b, r, h: (b, 0, r * nh + h))]
    out_shape = [jax.ShapeDtypeStruct((batch, ln, dil * dw), F32)]
    if has_sink:
        out_specs.append(pl.BlockSpec(sinkx.shape, lambda b, r, h: (0, 0)))
        out_shape.append(jax.ShapeDtypeStruct(sinkx.shape, F32))
    res = _pcall(
        body, name=name, grid=(batch, dil, nh), in_specs=in_specs, out_specs=out_specs, out_shape=out_shape,
        sem=("arbitrary", "arbitrary", "arbitrary"),
    )(*ins)
    return [res[0].reshape(t, dw)] + list(res[1:])


def _final_loss(x, g, target, *, name):
    d = x.shape[1]

    def fn(rv, pv):
        xv, tg = rv
        y, vjp = jax.vjp(_rms, xv, pv[0])
        err = y - tg
        dx, dg = vjp(err * (1.0 / d))
        loss = 0.5 * jnp.sum(err * err) * (1.0 / d)
        return [dx], [dg, jnp.full((1, LANE), loss, F32)]

    (dx,), (dg, loss) = _rowcall(fn, [(x, d, 0), (target, d, 0)], [g], [(d, F32)], [(1, d), (1, LANE)],
                                 tile=256, name=name)
    return dx, dg, loss


class _Layout:
    def __init__(self, d_model):
        self.d = d_model
        self.inner = SSD_HEADS * SSD_HEAD_DIM
        self.gn = SSD_GROUPS * SSD_STATE
        self.xbc = self.inner + 2 * self.gn
        self.ndt = 2 * SSD_HEADS
        self.ngrp = len(DIL_PATTERNS)
        self.dilw = DIL_HEADS * HEAD_DIM
        self.rqd = 3 * self.ngrp * self.dilw
        self.rep = WIN_Q_HEADS // WIN_KV_HEADS
        self.rqw = WIN_KV_HEADS * (self.rep + 2) * HEAD_DIM
        self.qw = WIN_Q_HEADS * HEAD_DIM
        self.kw = WIN_KV_HEADS * HEAD_DIM
        self.gates = N_BRANCH * d_model
        self.n_in = self.inner + self.xbc + self.ndt + self.rqd + self.qw + 2 * self.kw + self.gates
        self.o_gates = 0
        self.o_z = self.gates
        self.o_xbc = self.o_z + self.inner
        self.o_rqd = self.o_xbc + self.xbc
        self.o_rqw = self.o_rqd + self.rqd
        self.o_dt = self.o_rqw + self.rqw
        self.dtw = -(-(self.o_dt + self.ndt) // PAD_TO) * PAD_TO - self.o_dt
        self.width = self.o_dt + self.dtw
        assert self.o_z % self.inner == 0 and self.o_xbc % 256 == 0
        assert self.o_rqd % (3 * HEAD_DIM) == 0 and self.o_rqw % (3 * HEAD_DIM) == 0 and self.dtw % LANE == 0
        assert self.o_dt % self.dtw == 0

    def split_points(self):
        sizes = (self.inner, self.xbc, self.ndt, self.rqd, self.qw, self.kw, self.kw, self.gates)
        pts, acc = [], 0
        for s in sizes:
            pts.append((acc, acc + s))
            acc += s
        return pts

    def permute_w(self, w):
        d = w.shape[0]
        z, xbc, dt, qkvd, qw, kw, vw, gates = [w[:, a:b] for a, b in self.split_points()]
        nhd = self.ngrp * DIL_HEADS
        qkvd = qkvd.reshape(d, 3, nhd, HEAD_DIM).transpose(0, 2, 1, 3).reshape(d, self.rqd)
        win = jnp.concatenate([qw.reshape(d, WIN_KV_HEADS, self.rep, HEAD_DIM),
                               kw.reshape(d, WIN_KV_HEADS, 1, HEAD_DIM),
                               vw.reshape(d, WIN_KV_HEADS, 1, HEAD_DIM)], axis=2).reshape(d, self.rqw)
        pad = jnp.zeros((d, self.dtw - self.ndt), w.dtype)
        return jnp.concatenate([gates, z, xbc, qkvd, win, dt, pad], axis=1)

    def unpermute_w(self, wp):
        d = wp.shape[0]
        gates = wp[:, :self.o_z]
        z = wp[:, self.o_z:self.o_xbc]
        xbc = wp[:, self.o_xbc:self.o_rqd]
        qkvd = wp[:, self.o_rqd:self.o_rqw]
        win = wp[:, self.o_rqw:self.o_dt].reshape(d, WIN_KV_HEADS, self.rep + 2, HEAD_DIM)
        dt = wp[:, self.o_dt:self.o_dt + self.ndt]
        nhd = self.ngrp * DIL_HEADS
        qkvd = qkvd.reshape(d, nhd, 3, HEAD_DIM).transpose(0, 2, 1, 3).reshape(d, self.rqd)
        qw = win[:, :, :self.rep].reshape(d, self.qw)
        kw = win[:, :, self.rep].reshape(d, self.kw)
        vw = win[:, :, self.rep + 1].reshape(d, self.kw)
        return jnp.concatenate([z, xbc, dt, qkvd, qw, kw, vw, gates], axis=1)


def _dt_layouts(pdt, dirn, batch_seq):
    hg = SSD_HEADS // SSD_GROUPS
    v = pdt[:, dirn * SSD_HEADS:(dirn + 1) * SSD_HEADS].reshape(batch_seq, SSD_GROUPS, hg)
    return v.transpose(1, 0, 2), v.transpose(1, 2, 0)


def _par_layouts(p):
    hg = SSD_HEADS // SSD_GROUPS
    v = p.reshape(SSD_GROUPS, hg)
    return v[:, None, :], v[:, :, None]


def _layer_fwd(x, lw, lay, tabs, batch, seq):
    d = lay.d
    sv = {"x": x}
    (h,), _ = _rowcall(lambda rv, pv: ([_rms(rv[0], pv[0])], []), [(x, d, 0)], [lw["g_mix"]], [(d, BF16)], [],
                       tile=256, name="norm_mix")
    p = _mm(h, lw["w_in"], name="proj_in")
    sv["h"], sv["p"] = h, p
    u = _conv_fwd(p, lw["conv_w8"], lw["conv_b"], col0=lay.o_xbc, chans=lay.xbc, batch=batch, seq=seq, name="conv_fwd")
    sv["u"] = u
    pdt = p[:, lay.o_dt:lay.o_dt + lay.ndt]
    ys, sv["st"], sv["dtl"] = [], [], []
    for dirn in range(2):
        dtr, dtr_t = _dt_layouts(pdt, dirn, batch * seq)
        bias, bias_t = _par_layouts(lw["dt_bias"][dirn])
        alog, alog_t = _par_layouts(lw["a_log"][dirn])
        y, st = _scan_fwd(u, dtr, dtr_t, bias, bias_t, alog, alog_t, batch=batch, seq=seq, inner=lay.inner,
                          reverse=bool(dirn), name="scan_fwd%d" % dirn)
        ys.append(y)
        sv["st"].append(st)
        sv["dtl"].append((dtr, dtr_t, bias, bias_t, alog, alog_t))
    sv["ys"] = ys
    inner = lay.inner
    (ya,), _ = _rowcall(lambda rv, pv: ([_ssd_post_f(*rv, *pv)], []),
                        [(ys[0], inner, 0), (ys[1], inner, 0), (u, inner, 0), (p, inner, lay.o_z // inner)],
                        [lw["d_skip_x"], lw["ssd_norm"]], [(inner, BF16)], [], tile=128, name="ssd_post")
    sv["ya"] = ya
    rqd = _rope(p, tabs, col0=lay.o_rqd, width=lay.rqd, seq=seq, group=0, inverse=False, out_dtype=BF16, name="rope_dil")
    rqw = _rope(p, tabs, col0=lay.o_rqw, width=lay.rqw, seq=seq, group=lay.rep + 2, inverse=False, out_dtype=BF16,
                name="rope_win")
    sv["rqd"], sv["rqw"] = rqd, rqw
    os_, ls_ = [], []
    for gi, (window, dil) in enumerate(DIL_PATTERNS):
        o, l = _attn_fwd(rqd, None, batch=batch, seq=seq, dil=dil, nbw=lay.ngrp * DIL_HEADS, cb0=gi * DIL_HEADS,
                         nh=DIL_HEADS, rep=1, hw=window // (2 * dil), want_lse=True, out_dtype=F32,
                         name="dil_fwd%d" % gi)
        os_.append(o)
        ls_.append(l)
    sv["os"], sv["ls"] = os_, ls_
    dw = lay.dilw
    (yb,), _ = _rowcall(lambda rv, pv: ([_combine_f(*rv)], []), [(a, dw, 0) for a in os_ + ls_], [], [(dw, BF16)], [],
                        tile=256, name="dil_combine")
    sv["yb"] = yb
    (yc,) = _attn_fwd(rqw, lw["sink_x"], batch=batch, seq=seq, dil=1, nbw=WIN_KV_HEADS, cb0=0, nh=WIN_KV_HEADS,
                      rep=lay.rep, hw=WIN_HALF, want_lse=False, out_dtype=BF16, name="win_fwd")
    sv["yc"] = yc
    ma = _mm(ya, lw["w_a"], name="proj_a")
    mb = _mm(yb, lw["w_b"], name="proj_b")
    mc = _mm(yc, lw["w_c"], name="proj_c")
    sv["mabc"] = (ma, mb, mc)
    (mg,), _ = _rowcall(lambda rv, pv: ([_merge_f(*rv)], []),
                        [(ma, d, 0), (mb, d, 0), (mc, d, 0), (p, d, 0), (p, d, 1), (p, d, 2)], [], [(d, BF16)], [],
                        tile=256, name="merge")
    sv["mg"] = mg
    x1 = _mm(mg, lw["w_out"], add=x, name="proj_out")
    sv["x1"] = x1
    (hm,), _ = _rowcall(lambda rv, pv: ([_rms(rv[0], pv[0])], []), [(x1, d, 0)], [lw["g_mlp"]], [(d, BF16)], [],
                        tile=256, name="norm_mlp")
    up, act = _mm(hm, lw["w_up"], post=(lambda r: (r, jnp.square(jnp.maximum(r, 0.0))), [], [F32, BF16]),
                  name="mlp_up")
    sv["hm"], sv["up"], sv["act"] = hm, up, act
    x2 = _mm(act, lw["w_down"], add=x1, name="mlp_down")
    return x2, sv


def _layer_bwd(dxo, sv, lw, lay, tabs, batch, seq, post):
    d = lay.d
    inner = lay.inner
    gs = {}
    (dup,) = _mm(dxo, lw["w_down"], tb=True, name="mlp_down_dx",
                 post=(lambda r, a: (r * (2.0 * jnp.maximum(a, 0.0)),), [sv["up"]], [BF16]))
    post("w_down", _mm(sv["act"], dxo, ta=True, name="mlp_down_dw"))
    dhm = _mm(dup, lw["w_up"], tb=True, name="mlp_up_dx")
    post("w_up", _mm(sv["hm"], dup, ta=True, name="mlp_up_dw"))

    def norm_bwd(rv, pv):
        xv, dh, dres = rv
        _, vjp = jax.vjp(_rms, xv, pv[0])
        dx, dg = vjp(dh)
        return [dx + dres], [dg]

    (dx1,), (gs["g_mlp"],) = _rowcall(norm_bwd, [(sv["x1"], d, 0), (dhm, d, 0), (dxo, d, 0)], [lw["g_mlp"]],
                                      [(d, F32)], [(1, d)], tile=128, name="norm_mlp_bwd")
    dmg = _mm(dx1, lw["w_out"], tb=True, name="proj_out_dx")
    post("w_out", _mm(sv["mg"], dx1, ta=True, name="proj_out_dw"))
    ma, mb, mc = sv["mabc"]
    p = sv["p"]

    def merge_bwd(rv, pv):
        _, vjp = jax.vjp(_merge_f, *rv[:6])
        da, db, dc, d0, d1, d2 = vjp(rv[6])
        return [da, db, dc, jnp.concatenate([d0, d1, d2], axis=1)], []

    dp = lax.empty((batch * seq, lay.width), BF16)
    (dma, dmb, dmc, dp), _ = _rowcall(
        merge_bwd, [(ma, d, 0), (mb, d, 0), (mc, d, 0), (p, d, 0), (p, d, 1), (p, d, 2), (dmg, d, 0)], [],
        [(d, BF16), (d, BF16), (d, BF16), (lay.gates, BF16, dp, 0)], [], tile=128, name="merge_bwd")
    dya = _mm(dma, lw["w_a"], tb=True, name="proj_a_dx")
    post("w_a", _mm(sv["ya"], dma, ta=True, name="proj_a_dw"))
    dyb = _mm(dmb, lw["w_b"], tb=True, name="proj_b_dx")
    post("w_b", _mm(sv["yb"], dmb, ta=True, name="proj_b_dw"))
    dyc = _mm(dmc, lw["w_c"], tb=True, name="proj_c_dx")
    post("w_c", _mm(sv["yc"], dmc, ta=True, name="proj_c_dw"))
    drqw, dsink = _attn_bwd(sv["rqw"], lw["sink_x"], dyc, None, batch=batch, seq=seq, dil=1, nbw=WIN_KV_HEADS, cb0=0,
                            nh=WIN_KV_HEADS, rep=lay.rep, hw=WIN_HALF, name="win_bwd")
    gs["sink"] = jnp.sum(dsink, axis=1)
    dw = lay.dilw

    def combine_bwd(rv, pv):
        _, vjp = jax.vjp(_combine_f, *rv[:6])
        return list(vjp(rv[6])), []

    dol, _ = _rowcall(combine_bwd, [(a, dw, 0) for a in sv["os"] + sv["ls"]] + [(dyb, dw, 0)], [],
                      [(dw, F32)] * 6, [], tile=256, name="dil_combine_bwd")
    for gi, (window, dil) in enumerate(DIL_PATTERNS):
        (dg_,) = _attn_bwd(sv["rqd"], None, dol[gi], dol[3 + gi], batch=batch, seq=seq, dil=dil,
                           nbw=lay.ngrp * DIL_HEADS, cb0=gi * DIL_HEADS, nh=DIL_HEADS, rep=1,
                           hw=window // (2 * dil), name="dil_bwd%d" % gi)
        dp = _rope(dg_, tabs, col0=0, width=dg_.shape[1], seq=seq, group=0, inverse=True, out_dtype=BF16,
                   name="rope_dil_bwd%d" % gi, into=dp, out_col0=lay.o_rqd + gi * dg_.shape[1])
    dp = _rope(drqw, tabs, col0=0, width=lay.rqw, seq=seq, group=lay.rep + 2, inverse=True, out_dtype=BF16,
               name="rope_win_bwd", into=dp, out_col0=lay.o_rqw)
    u, ys = sv["u"], sv["ys"]

    def post_bwd(rv, pv):
        _, vjp = jax.vjp(_ssd_post_f, *rv[:4], *pv)
        dyf, _, dxs, dz, dsk, dgn = vjp(rv[4])
        return [dyf, dxs, dz], [dsk, dgn]

    (dy, dxs_post, dp), (dsk, gs["ssd_norm"]) = _rowcall(
        post_bwd, [(ys[0], inner, 0), (ys[1], inner, 0), (u, inner, 0), (p, inner, lay.o_z // inner), (dya, inner, 0)],
        [lw["d_skip_x"], lw["ssd_norm"]], [(inner, F32), (inner, F32), (inner, BF16, dp, lay.o_z // inner)],
        [(1, inner), (1, inner)], tile=128, name="ssd_post_bwd")
    gs["d_skip"] = jnp.sum(dsk.reshape(SSD_HEADS, SSD_HEAD_DIM), axis=1)
    rs = []
    ddt, gdb, gda = [], [], []
    for dirn in range(2):
        dtr, dtr_t, bias, bias_t, alog, alog_t = sv["dtl"][dirn]
        r = _scan_bwd(u, dtr, dtr_t, bias, bias_t, alog, alog_t, sv["st"][dirn], dy, batch=batch, seq=seq,
                      inner=inner, reverse=bool(dirn), name="scan_bwd%d" % dirn)
        rs.append(r)
        ddt.append((r[3] + r[4].transpose(0, 2, 1)).transpose(1, 0, 2).reshape(batch * seq, SSD_HEADS))
        gdb.append((r[5][:, 0, :] + r[6][:, :, 0]).reshape(SSD_HEADS))
        gda.append((r[7][:, 0, :] + r[8][:, :, 0]).reshape(SSD_HEADS))
    gs["dt_bias"] = jnp.stack(gdb)
    gs["a_log"] = jnp.stack(gda)
    dcws, dcbs = [], []
    for tag, ch0, chans, dus in (("x", 0, inner, [dxs_post, rs[0][0], rs[1][0]]),
                                 ("b", inner, lay.gn, [rs[0][1], rs[1][1]]),
                                 ("c", inner + lay.gn, lay.gn, [rs[0][2], rs[1][2]])):
        dp, dcw, dcb = _conv_bwd(p, lw["conv_w8"], lw["conv_b"], dus, col0=lay.o_xbc, ch0=ch0, chans=chans,
                                 batch=batch, seq=seq, into=dp, out_col0=lay.o_xbc, name="conv_bwd_" + tag)
        dcws.append(dcw)
        dcbs.append(dcb)
    gs["conv_w"] = jnp.concatenate(dcws, axis=1)[:CONV_WIDTH]
    gs["conv_b"] = jnp.concatenate(dcbs, axis=1)[0]
    ddtp = jnp.concatenate(ddt + [jnp.zeros((batch * seq, lay.dtw - lay.ndt), F32)], axis=1)
    (dp,), _ = _rowcall(lambda rv, pv: ([rv[0]], []), [(ddtp, lay.dtw, 0)], [],
                        [(lay.dtw, BF16, dp, lay.o_dt // lay.dtw)], [], tile=512, name="ddt_store")
    hd = d // W_IN_PARTS
    for part in range(W_IN_PARTS):
        post(("w_in", part), _mm(sv["h"][:, part * hd:(part + 1) * hd], dp, ta=True, name="proj_in_dw"))
    dh = _mm(dp, lw["w_in"], tb=True, name="proj_in_dx")
    (dx,), (gs["g_mix"],) = _rowcall(norm_bwd, [(sv["x"], d, 0), (dh, d, 0), (dx1, d, 0)], [lw["g_mix"]],
                                     [(d, F32)], [(1, d)], tile=128, name="norm_mix_bwd")
    return dx, gs


_SHARDED = ("w_in", "w_a", "w_b", "w_c", "w_out", "w_up", "w_down")
_COL_SHARDED = ("w_in", "w_b", "w_up")
_SMALL = ("g_mix", "conv_b", "dt_bias", "a_log", "d_skip", "ssd_norm", "sink", "g_mlp")


def _gathered_to_full(name, g):
    n, r, c = g.shape
    if name in _COL_SHARDED:
        return g.transpose(1, 0, 2).reshape(r, n * c)
    return g.reshape(n * r, c)


def _full_to_slots(name, w):
    r, c = w.shape
    if name in _COL_SHARDED:
        return w.reshape(r, N_DEV, c // N_DEV).transpose(1, 0, 2)
    return w.reshape(N_DEV, r // N_DEV, c)


class _LayerWeights:
    def __init__(self, sched, layer, lay, small):
        self.sched, self.layer, self.lay, self.vals = sched, layer, lay, dict(small)

    def __getitem__(self, name):
        if name not in self.vals:
            full = _gathered_to_full(name, self.sched.get(("w", name, self.layer)))
            self.vals[name] = self.lay.permute_w(full) if name == "w_in" else full
        return self.vals[name]


def _pack(parts):
    flat = jnp.concatenate([p.reshape(-1).astype(F32) for p in parts])
    n = flat.shape[0]
    rows = -(-n // (8 * LANE)) * 8
    return jnp.pad(flat, (0, rows * LANE - n)).reshape(rows, LANE)


def _unpack(buf, shapes):
    flat = buf.reshape(-1)
    out, off = [], 0
    for s in shapes:
        n = math.prod(s)
        out.append(flat[off:off + n].reshape(s))
        off += n
    return out


def kernel(x, g_mix, w_in, conv_w, conv_b, dt_bias, a_log, d_skip, ssd_norm, w_a, w_b, w_c, sink, w_out, g_mlp, w_up, w_down, g_final, loss_target, m_g_mix, m_w_in, m_conv_w, m_conv_b, m_dt_bias, m_a_log, m_d_skip, m_ssd_norm, m_w_a, m_w_b, m_w_c, m_sink, m_w_out, m_g_mlp, m_w_up, m_w_down, m_g_final, v_g_mix, v_w_in, v_conv_w, v_conv_b, v_dt_bias, v_a_log, v_d_skip, v_ssd_norm, v_w_a, v_w_b, v_w_c, v_sink, v_w_out, v_g_mlp, v_w_up, v_w_down, v_g_final):
    batch, seq, d = x.shape
    depth = g_mix.shape[0]
    lay = _Layout(d)
    assert lay.n_in == w_in.shape[2] * N_DEV
    wts = dict(g_mix=g_mix, w_in=w_in, conv_w=conv_w, conv_b=conv_b, dt_bias=dt_bias, a_log=a_log, d_skip=d_skip,
               ssd_norm=ssd_norm, w_a=w_a, w_b=w_b, w_c=w_c, sink=sink, w_out=w_out, g_mlp=g_mlp, w_up=w_up,
               w_down=w_down, g_final=g_final)
    mom = dict(g_mix=m_g_mix, w_in=m_w_in, conv_w=m_conv_w, conv_b=m_conv_b, dt_bias=m_dt_bias, a_log=m_a_log,
               d_skip=m_d_skip, ssd_norm=m_ssd_norm, w_a=m_w_a, w_b=m_w_b, w_c=m_w_c, sink=m_sink, w_out=m_w_out,
               g_mlp=m_g_mlp, w_up=m_w_up, w_down=m_w_down, g_final=m_g_final)
    var = dict(g_mix=v_g_mix, w_in=v_w_in, conv_w=v_conv_w, conv_b=v_conv_b, dt_bias=v_dt_bias, a_log=v_a_log,
               d_skip=v_d_skip, ssd_norm=v_ssd_norm, w_a=v_w_a, w_b=v_w_b, w_c=v_w_c, sink=v_sink, w_out=v_w_out,
               g_mlp=v_g_mlp, w_up=v_w_up, w_down=v_w_down, g_final=v_g_final)
    me = 4 * lax.axis_index("x") + 2 * lax.axis_index("y") + lax.axis_index("c")

    global _SCHED
    sched = _SCHED = _Sched()
    (gconv,) = _exchange([conv_w], scatter=False, name="gather_conv_w")
    conv_full = gconv.transpose(1, 2, 0, 3).reshape(depth, CONV_WIDTH, -1)
    for l in range(depth):
        for n in _SHARDED:
            sched.post(("w", n, l), wts[n][l].astype(BF16), scatter=False)

    tabs = _rope_tables(seq)
    t = batch * seq
    xf = x.reshape(t, d)
    layers = []
    for l in range(depth):
        small = dict(
            g_mix=g_mix[l][None], g_mlp=g_mlp[l][None], ssd_norm=ssd_norm[l][None], conv_b=conv_b[l][None],
            dt_bias=dt_bias[l], a_log=a_log[l],
            d_skip_x=jnp.repeat(d_skip[l], SSD_HEAD_DIM)[None],
            sink_x=jnp.broadcast_to(sink[l][:, None], (WIN_Q_HEADS, LANE)),
            conv_w8=jnp.pad(conv_full[l], ((0, 8 - CONV_WIDTH), (0, 0))))
        layers.append(_LayerWeights(sched, l, lay, small))

    saves = []
    h = xf
    for l in range(depth):
        h, sv = _layer_fwd(h, layers[l], lay, tabs, batch, seq)
        saves.append(sv)
    dx, dgf, loss = _final_loss(h, g_final[None], loss_target.reshape(t, d), name="final_loss")

    gss = [None] * depth
    for l in reversed(range(depth)):
        def post(n, g, l=l):
            if isinstance(n, tuple):
                key, n, g = ("g", "w_in", l, n[1]), "w_in", lay.unpermute_w(g)
            else:
                key = ("g", n, l)
            sched.post(key, _full_to_slots(n, g).astype(BF16), scatter=True)

        dx, gss[l] = _layer_bwd(dx, saves[l], layers[l], lay, tabs, batch, seq, post)
    grad_x = dx.reshape(batch, seq, d)

    small_parts = [jnp.stack([gss[l][n] for l in range(depth)]) for n in _SMALL]
    small_parts += [dgf, jnp.stack([gss[l]["conv_w"] for l in range(depth)]), loss[0, :1]]
    small_shapes = [p.shape for p in small_parts]
    (rs,) = _exchange([_pack(small_parts)], scatter=False, name="gather_small")
    red = _unpack(_sum_slots(rs, name="sum_small"), small_shapes)
    gsmall = dict(zip(list(_SMALL) + ["g_final"], red[:len(_SMALL) + 1]))
    gconv_full, loss_sum = red[-2], red[-1]
    cshard = conv_w.shape[2]
    gsmall["conv_w"] = lax.dynamic_slice_in_dim(gconv_full, me * cshard, cshard, axis=2)

    out = {}
    rep_names = list(_SMALL) + ["g_final"]
    rep_shapes = [wts[n].shape for n in rep_names]
    res = _adamw(_pack([wts[n] for n in rep_names]), [_pack([gsmall[n] for n in rep_names])[None]],
                 _pack([mom[n] for n in rep_names]), _pack([var[n] for n in rep_names]), name="adamw_small")
    unp = [_unpack(a, rep_shapes) for a in res]
    for i, n in enumerate(rep_names):
        out[n] = [unp[k][i] for k in range(4)]
    cs2 = (depth * CONV_WIDTH, cshard)
    res = _adamw(conv_w.reshape(cs2), [gsmall["conv_w"].reshape((1,) + cs2)], m_conv_w.reshape(cs2),
                 v_conv_w.reshape(cs2), name="adamw_conv_w")
    out["conv_w"] = [a.reshape(conv_w.shape) for a in res]
    for n in ("w_down", "w_up", "w_out", "w_a", "w_b", "w_c", "w_in"):
        shp = wts[n].shape
        r2 = (shp[0] * shp[1], shp[2])
        if n == "w_in":
            recvs = [sched.get(("g", n, l, part)) for l in range(depth) for part in range(W_IN_PARTS)]
        else:
            recvs = [sched.get(("g", n, l)) for l in range(depth)]
        res = _adamw(wts[n].reshape(r2), recvs, mom[n].reshape(r2), var[n].reshape(r2), name="adamw_" + n)
        out[n] = [a.reshape(shp) for a in res]

    order = ["g_mix", "w_in", "conv_w", "conv_b", "dt_bias", "a_log", "d_skip", "ssd_norm", "w_a", "w_b", "w_c",
             "sink", "w_out", "g_mlp", "w_up", "w_down", "g_final"]
    outs = [loss_sum.reshape(()), grad_x]
    for k in range(4):
        outs += [out[n][k] for n in order]
    return tuple(outs)
```

```python
import functools
import math

import jax
import jax.numpy as jnp
from jax import lax
from jax.experimental import pallas as pl
from jax.experimental.pallas import tpu as pltpu

F32 = jnp.float32
BF16 = jnp.bfloat16
HI = lax.Precision.HIGHEST
MESH = pl.DeviceIdType.MESH
N_DEV = 8

SSD_HEADS = 32
SSD_HEAD_DIM = 64
SSD_GROUPS = 8
SSD_STATE = 128
SSD_CHUNK = 128
CONV_WIDTH = 5
HEAD_DIM = 128
ROPE_DIM = 32
ROPE_THETA = 500000.0
DIL_PATTERNS = ((128, 1), (512, 4), (2048, 16))
DIL_HEADS = 8
WIN_Q_HEADS = 16
WIN_KV_HEADS = 4
WIN_HALF = 128
N_BRANCH = 3
EPS = 1e-6
NEG_INF = -1e30
ADAM_LR = 0.001
ADAM_B1 = 0.9
ADAM_B2 = 0.999
ADAM_EPS = 1e-08
ADAM_WD = 0.01
ADAM_STEP = 10

LANE = 128
QBLK = 128
VMEM_LIMIT = 56 * 1024 * 1024
PAD_TO = 512
MM_VMEM_BUDGET = 40 * 1024 * 1024
W_IN_PARTS = 4


def _cparams(sem=None):
    return pltpu.CompilerParams(dimension_semantics=sem, vmem_limit_bytes=VMEM_LIMIT)


PIECE_BYTES = 400 * 1024
US_PER_PIECE_BYTE = 8.8e-5
MAX_PIECES = 8
CARRIER_US = {
    "proj_in": 450, "proj_in_dx": 560, "proj_in_dw": 140, "scan_fwd0": 230, "scan_fwd1": 230, "scan_bwd0": 500,
    "scan_bwd1": 500, "win_fwd": 160, "win_bwd": 400, "dil_fwd0": 105, "dil_fwd1": 105, "dil_fwd2": 175,
    "dil_bwd0": 200, "dil_bwd1": 190, "dil_bwd2": 200, "mlp_up": 155, "mlp_down": 170, "mlp_up_dx": 190,
    "mlp_up_dw": 190, "mlp_down_dx": 165, "mlp_down_dw": 188, "conv_bwd_x": 75, "rope_dil": 48,
    "adamw_w_up": 63, "adamw_w_down": 62, "proj_a": 42, "proj_c": 42, "proj_out": 42, "proj_a_dx": 42,
    "proj_c_dx": 42, "proj_out_dx": 42, "proj_a_dw": 42, "proj_c_dw": 42, "proj_out_dw": 42, "conv_fwd": 65,
    "merge": 65, "merge_bwd": 100, "ssd_post": 50, "ssd_post_bwd": 90, "rope_win": 90, "norm_mix_bwd": 50,
    "norm_mlp_bwd": 50, "dil_combine_bwd": 70,
}


class _Piece:
    def __init__(self, key, row0, rows, scatter, est):
        self.key, self.row0, self.rows, self.scatter, self.est = key, row0, rows, scatter, est


def _coalesce(pieces):
    out = []
    for p in pieces:
        q = out[-1] if out else None
        if q is not None and q.key == p.key and q.scatter == p.scatter and q.row0 + q.rows == p.row0:
            out[-1] = _Piece(q.key, q.row0, q.rows + p.rows, q.scatter, q.est + p.est)
        else:
            out.append(p)
    return out


class _Sched:
    def __init__(self):
        self.queue, self.src, self.dst = [], {}, {}

    def post(self, key, src, scatter):
        r, c = src.shape[-2:]
        self.src[key] = src
        self.dst[key] = lax.empty((N_DEV, r, c), src.dtype)
        row_bytes = c * src.dtype.itemsize
        pr = r
        while pr * row_bytes > PIECE_BYTES and pr % 32 == 0:
            pr //= 2
        for row0 in range(0, r, pr):
            self.queue.append(_Piece(key, row0, pr, scatter, pr * row_bytes * US_PER_PIECE_BYTE))

    def take(self, name):
        budget = CARRIER_US.get(name)
        out, used = [], 0.0
        while budget and self.queue and used + self.queue[0].est <= 1.1 * budget:
            used += self.queue[0].est
            out.append(self.queue.pop(0))
        return _coalesce(out)

    def get(self, key):
        last = max([i for i, p in enumerate(self.queue) if p.key == key], default=-1)
        if last >= 0:
            pieces, self.queue = _coalesce(self.queue[:last + 1]), self.queue[last + 1:]
            for i in range(0, len(pieces), MAX_PIECES):
                _exchange_pieces(self, pieces[i:i + MAX_PIECES], name="exchange_flush")
        return self.dst[key]


_SCHED = None


def _piece_copies(pieces, keys, src_refs, dst_refs, send_sems, recv_sems, loc_sems):
    x, y, c = lax.axis_index("x"), lax.axis_index("y"), lax.axis_index("c")
    me = 4 * x + 2 * y + c
    cps = []
    for t, p in enumerate(pieces):
        ki = keys.index(p.key)
        rows = pl.ds(p.row0, p.rows)
        for j in range(1, N_DEV):
            px = (1 - x) if (j >> 2) & 1 else x
            py = (1 - y) if (j >> 1) & 1 else y
            pc = (1 - c) if j & 1 else c
            src = src_refs[ki].at[4 * px + 2 * py + pc, rows] if p.scatter else src_refs[ki].at[rows]
            cps.append(pltpu.make_async_remote_copy(
                src_ref=src, dst_ref=dst_refs[ki].at[me, rows], send_sem=send_sems.at[t * 7 + j - 1],
                recv_sem=recv_sems.at[t * 7 + j - 1], device_id=(px, py, pc), device_id_type=MESH))
        src = src_refs[ki].at[me, rows] if p.scatter else src_refs[ki].at[rows]
        cps.append(pltpu.make_async_copy(src, dst_refs[ki].at[me, rows], loc_sems.at[t]))
    return cps


def _two_level_gather(pieces, keys, src_refs, dst_refs, send_sems, recv_sems, loc_sems):
    x, y, c = lax.axis_index("x"), lax.axis_index("y"), lax.axis_index("c")
    me = 4 * x + 2 * y + c
    sib = (x, y, 1 - c)
    chips = [(1 - x, y), (x, 1 - y), (1 - x, 1 - y)]

    def copy(t, k, src, dst, to):
        return pltpu.make_async_remote_copy(src_ref=src, dst_ref=dst, send_sem=send_sems.at[7 * t + k],
                                            recv_sem=recv_sems.at[7 * t + k], device_id=to, device_id_type=MESH)

    def landing(p, px, py, pc):
        return dst_refs[keys.index(p.key)].at[4 * px + 2 * py + pc, pl.ds(p.row0, p.rows)]

    sends, local = [], []
    for t, p in enumerate(pieces):
        src = src_refs[keys.index(p.key)].at[pl.ds(p.row0, p.rows)]
        mine = landing(p, x, y, c)
        sends.append(copy(t, 0, src, mine, sib))
        sends += [copy(t, 1 + j, src, mine, (px, py, c)) for j, (px, py) in enumerate(chips)]
        local.append(pltpu.make_async_copy(src, mine, loc_sems.at[t]))
    for cp in sends + local:
        cp.start()
    passed = []
    for t, p in enumerate(pieces):
        for j, (px, py) in enumerate(chips):
            blk = landing(p, px, py, c)
            copy(t, 1 + j, blk, blk, (x, y, c)).wait_recv()
            fwd = copy(t, 4 + j, blk, blk, sib)
            fwd.start()
            passed.append(fwd)
    for t, p in enumerate(pieces):
        blk = landing(p, x, y, 1 - c)
        copy(t, 0, blk, blk, (x, y, c)).wait_recv()
        for j, (px, py) in enumerate(chips):
            blk = landing(p, px, py, 1 - c)
            copy(t, 4 + j, blk, blk, (x, y, c)).wait_recv()
    for cp in sends + passed:
        cp.wait_send()
    for cp in local:
        cp.wait()


def _carry_call(sched, pieces, body, *, name, grid, in_specs, out_specs, out_shape, scratch_shapes, ins, aliases=None):
    keys = []
    for p in pieces:
        if p.key not in keys:
            keys.append(p.key)
    n_in, n_out, nk, npc = len(ins), len(out_shape), len(keys), len(pieces)
    n_scr = len(scratch_shapes)

    def wrapped(*refs):
        in_refs = refs[:n_in]
        src_refs = refs[n_in:n_in + nk]
        out_refs = refs[n_in + 2 * nk:n_in + 2 * nk + n_out]
        dst_refs = refs[n_in + 2 * nk + n_out:n_in + 3 * nk + n_out]
        scr = refs[n_in + 3 * nk + n_out:]
        inner_scr, sems = scr[:n_scr], scr[n_scr:]
        if grid:
            pids = [pl.program_id(a) for a in range(len(grid))]
            first = functools.reduce(lambda u, v: u & v, [q == 0 for q in pids])
            last = functools.reduce(lambda u, v: u & v, [q == g - 1 for q, g in zip(pids, grid)])

            @pl.when(first)
            def _():
                for cp in _piece_copies(pieces, keys, src_refs, dst_refs, *sems):
                    cp.start()

            body(*in_refs, *out_refs, *inner_scr)

            @pl.when(last)
            def _():
                for cp in _piece_copies(pieces, keys, src_refs, dst_refs, *sems):
                    cp.wait()
        elif all(not p.scatter for p in pieces):
            _two_level_gather(pieces, keys, src_refs, dst_refs, *sems)
        else:
            cps = _piece_copies(pieces, keys, src_refs, dst_refs, *sems)
            for cp in cps:
                cp.start()
            for cp in cps:
                cp.wait()

    anyspec = pl.BlockSpec(memory_space=pl.ANY)
    dsts = [sched.dst[k] for k in keys]
    kwargs = dict(grid=grid) if grid else {}
    res = pl.pallas_call(
        wrapped, name=name, in_specs=list(in_specs) + [anyspec] * (2 * nk), out_specs=list(out_specs) + [anyspec] * nk,
        out_shape=list(out_shape) + [jax.ShapeDtypeStruct(d.shape, d.dtype) for d in dsts],
        input_output_aliases={**(aliases or {}), **{n_in + nk + i: n_out + i for i in range(nk)}},
        scratch_shapes=list(scratch_shapes) + [pltpu.SemaphoreType.DMA((7 * npc,)), pltpu.SemaphoreType.DMA((7 * npc,)),
                                               pltpu.SemaphoreType.DMA((npc,))],
        compiler_params=pltpu.CompilerParams(dimension_semantics=("arbitrary",) * len(grid) if grid else None,
                                             vmem_limit_bytes=VMEM_LIMIT, has_side_effects=True),
        **kwargs,
    )(*ins, *[sched.src[k] for k in keys], *dsts)
    for i, k in enumerate(keys):
        sched.dst[k] = res[n_out + i]
    return list(res[:n_out])


def _exchange_pieces(sched, pieces, *, name):
    _carry_call(sched, pieces, None, name=name, grid=(), in_specs=[], out_specs=[], out_shape=[], scratch_shapes=[],
                ins=[])


def _pcall(body, *, name, grid, in_specs, out_specs, out_shape, scratch_shapes=(), sem=None, into=None):
    single = not isinstance(out_shape, (list, tuple))
    out_shape_l = [out_shape] if single else list(out_shape)
    out_specs_l = [out_specs] if single else list(out_specs)
    into = into or {}

    def run(*ins):
        n0, nb = len(ins), len(into)
        specs = list(in_specs) + [pl.BlockSpec(memory_space=pl.ANY)] * nb
        aliases = {n0 + k: oi for k, oi in enumerate(into)}
        for oi, buf in into.items():
            out_shape_l[oi] = jax.ShapeDtypeStruct(buf.shape, buf.dtype)
        kbody = (lambda *refs: body(*refs[:n0], *refs[n0 + nb:])) if nb else body
        args = list(ins) + list(into.values())
        pieces = _SCHED.take(name) if _SCHED is not None else []
        if pieces:
            res = _carry_call(_SCHED, pieces, kbody, name=name, grid=grid, in_specs=specs, out_specs=out_specs_l,
                              out_shape=out_shape_l, scratch_shapes=list(scratch_shapes), ins=args, aliases=aliases)
        else:
            res = pl.pallas_call(kbody, name=name, grid=grid, in_specs=specs, out_specs=out_specs_l,
                                 out_shape=out_shape_l, scratch_shapes=list(scratch_shapes),
                                 input_output_aliases=aliases, compiler_params=_cparams(sem))(*args)
        return res[0] if single else list(res)

    return run


def _pick(dim, cands):
    for c in cands:
        if dim % c == 0:
            return c
    return dim


def _mm_tiles(m, n, k, a_bytes, b_bytes, o_bytes, has_add):
    tm = _pick(m, (1024, 512, 256, 128))
    tn = _pick(n, (1024, 1792, 512, 256, 128))
    for tk in (2048, 1792, 1024, 896, 512, 256, 128):
        if k % tk:
            continue
        need = 2 * (tm * tk * a_bytes + tk * tn * b_bytes + tm * tn * (o_bytes + (4 if has_add else 0)))
        need += tm * tn * 4 if k // tk > 1 else 0
        if need <= MM_VMEM_BUDGET:
            return tm, tn, tk
    return tm, tn, _pick(k, (128,))


def _mm(a, b, *, ta=False, tb=False, out_dtype=F32, add=None, post=None, name):
    m, k = (a.shape[1], a.shape[0]) if ta else a.shape
    k2, n = (b.shape[1], b.shape[0]) if tb else b.shape
    assert k == k2, (a.shape, b.shape, ta, tb)
    pfn, pins, pdts = post if post is not None else (None, [], [out_dtype])
    o_bytes = sum(jnp.dtype(dt).itemsize for dt in pdts) + sum(e.dtype.itemsize for e in pins)
    tm, tn, tk = _mm_tiles(m, n, k, a.dtype.itemsize, b.dtype.itemsize, o_bytes, add is not None)
    assert m % tm == 0 and n % tn == 0 and k % tk == 0, (m, n, k, tm, tn, tk)
    nk = k // tk
    a_spec = pl.BlockSpec((tk, tm), lambda i, j, kk: (kk, i)) if ta else pl.BlockSpec((tm, tk), lambda i, j, kk: (i, kk))
    b_spec = pl.BlockSpec((tn, tk), lambda i, j, kk: (j, kk)) if tb else pl.BlockSpec((tk, tn), lambda i, j, kk: (kk, j))
    o_spec = pl.BlockSpec((tm, tn), lambda i, j, kk: (i, j))
    dims = (((0 if ta else 1,), (1 if tb else 0,)), ((), ()))
    has_add = add is not None

    nx, no = (1 if has_add else 0) + len(pins), len(pdts)

    def body(*refs):
        a_ref, b_ref = refs[:2]
        x_refs = refs[2:2 + nx]
        o_refs = refs[2 + nx:2 + nx + no]
        part = lax.dot_general(a_ref[...].astype(BF16), b_ref[...].astype(BF16), dims, preferred_element_type=F32)

        def finish(r):
            if has_add:
                r = r + x_refs[0][...]
            outs = pfn(r, *[x[...] for x in x_refs[1 if has_add else 0:]]) if pfn else (r,)
            for o_ref, v in zip(o_refs, outs):
                o_ref[...] = v.astype(o_ref.dtype)

        if nk == 1:
            finish(part)
            return
        acc_ref = refs[-1]
        kk = pl.program_id(2)

        @pl.when(kk == 0)
        def _():
            acc_ref[...] = part

        @pl.when(kk > 0)
        def _():
            acc_ref[...] += part

        @pl.when(kk == nk - 1)
        def _():
            finish(acc_ref[...])

    ins = [a, b] + ([add] if has_add else []) + list(pins)
    specs = [a_spec, b_spec] + [o_spec] * nx
    res = _pcall(
        body, name=name, grid=(m // tm, n // tn, nk), in_specs=specs, out_specs=[o_spec] * no,
        out_shape=[jax.ShapeDtypeStruct((m, n), dt) for dt in pdts],
        scratch_shapes=[pltpu.VMEM((tm, tn), F32)] if nk > 1 else [],
        sem=("parallel", "parallel", "arbitrary"),
    )(*ins)
    return res if post is not None else res[0]


def _rowcall(fn, rows, pars, row_outs, par_outs, *, tile, name):
    t = rows[0][0].shape[0]
    tile = min(tile, t)
    assert t % tile == 0
    nr, npar, nro, npo = len(rows), len(pars), len(row_outs), len(par_outs)
    in_specs = [pl.BlockSpec((tile, c), functools.partial(lambda i, cb: (i, cb), cb=cb)) for (_, c, cb) in rows]
    in_specs += [pl.BlockSpec(p.shape, lambda i: (0, 0)) for p in pars]
    into = {k: ro[2] for k, ro in enumerate(row_outs) if len(ro) == 4}
    out_specs = [pl.BlockSpec((tile, ro[0]), functools.partial(lambda i, cb: (i, cb), cb=ro[3] if len(ro) == 4 else 0))
                 for ro in row_outs]
    out_specs += [pl.BlockSpec(s, lambda i: (0, 0)) for s in par_outs]
    out_shape = [jax.ShapeDtypeStruct((t, ro[0]), ro[1]) for ro in row_outs]
    out_shape += [jax.ShapeDtypeStruct(s, F32) for s in par_outs]

    def body(*refs):
        rv = [r[...] for r in refs[:nr]]
        pv = [r[...] for r in refs[nr:nr + npar]]
        ro_refs = refs[nr + npar:nr + npar + nro]
        po_refs = refs[nr + npar + nro:]
        ro, po = fn(rv, pv)
        for ref, v in zip(ro_refs, ro):
            ref[...] = v.astype(ref.dtype)
        if npo:
            @pl.when(pl.program_id(0) == 0)
            def _():
                for ref in po_refs:
                    ref[...] = jnp.zeros_like(ref)
            for ref, v in zip(po_refs, po):
                ref[...] += v

    res = _pcall(
        body, name=name, grid=(t // tile,), in_specs=in_specs, out_specs=out_specs, out_shape=out_shape,
        sem=("arbitrary",), into=into,
    )(*[r[0] for r in rows], *pars)
    return list(res[:nro]), list(res[nro:])


def _map2d(fn, ins, out_dtype, *, name, tile=256, cw=2048):
    t, w = ins[0].shape
    tile, cw = min(tile, t), min(cw, w)
    assert t % tile == 0 and w % cw == 0

    def body(*refs):
        refs[-1][...] = fn(*[r[...] for r in refs[:-1]]).astype(out_dtype)

    spec = pl.BlockSpec((tile, cw), lambda i, j: (i, j))
    return pl.pallas_call(
        body, name=name, grid=(t // tile, w // cw), in_specs=[spec] * len(ins), out_specs=spec,
        out_shape=jax.ShapeDtypeStruct((t, w), out_dtype), compiler_params=_cparams(("parallel", "parallel")),
    )(*ins)


def _exchange(srcs, *, scatter, name):
    n = len(srcs)
    out_shape = [jax.ShapeDtypeStruct(s.shape if scatter else (N_DEV,) + s.shape, s.dtype) for s in srcs]

    def body(*refs):
        src_refs, out_refs = refs[:n], refs[n:2 * n]
        send_sems, recv_sems, loc_sems = refs[2 * n:]
        x, y, c = lax.axis_index("x"), lax.axis_index("y"), lax.axis_index("c")
        me = 4 * x + 2 * y + c
        copies = []
        for a in range(n):
            for j in range(1, N_DEV):
                px = (1 - x) if (j >> 2) & 1 else x
                py = (1 - y) if (j >> 1) & 1 else y
                pc = (1 - c) if j & 1 else c
                src = src_refs[a].at[4 * px + 2 * py + pc] if scatter else src_refs[a]
                cp = pltpu.make_async_remote_copy(
                    src_ref=src, dst_ref=out_refs[a].at[me], send_sem=send_sems.at[a * 7 + j - 1],
                    recv_sem=recv_sems.at[a * 7 + j - 1], device_id=(px, py, pc), device_id_type=MESH)
                cp.start()
                copies.append(cp)
            src = src_refs[a].at[me] if scatter else src_refs[a]
            cp = pltpu.make_async_copy(src, out_refs[a].at[me], loc_sems.at[a])
            cp.start()
            copies.append(cp)
        for cp in copies:
            cp.wait()

    anyspec = pl.BlockSpec(memory_space=pl.ANY)
    return pl.pallas_call(
        body, name=name, in_specs=[anyspec] * n, out_specs=[anyspec] * n, out_shape=out_shape,
        scratch_shapes=[pltpu.SemaphoreType.DMA((7 * n,)), pltpu.SemaphoreType.DMA((7 * n,)),
                        pltpu.SemaphoreType.DMA((n,))],
        compiler_params=pltpu.CompilerParams(has_side_effects=True),
    )(*srcs)


def _row_tile(r, c, budget_elems=256 * 1024):
    tr = r
    while tr * c > budget_elems and tr % 16 == 0:
        tr //= 2
    return tr


def _adamw(w, recvs, m, v, *, name):
    r, c = w.shape
    nl = len(recvs)
    ns, rl = recvs[0].shape[:2]
    assert rl * nl == r
    tr = _row_tile(rl, c, budget_elems=(1024 * 1024) // max(nl, 4))
    nt = rl // tr
    bc1 = 1.0 / (1.0 - ADAM_B1 ** ADAM_STEP)
    bc2 = 1.0 / (1.0 - ADAM_B2 ** ADAM_STEP)

    def body(*refs):
        w_ref, m_ref, v_ref = refs[:3]
        r_refs = refs[3:3 + nl]
        g_ref, d_ref, mo_ref, vo_ref = refs[3 + nl:]
        i = pl.program_id(0)
        for k in range(nl):
            @pl.when(i // nt == k)
            def _(k=k):
                g = r_refs[k][0].astype(F32)
                for s in range(1, ns):
                    g = g + r_refs[k][s].astype(F32)
                g_ref[...] = g
        g = g_ref[...]
        mn = ADAM_B1 * m_ref[...] + (1.0 - ADAM_B1) * g
        vn = ADAM_B2 * v_ref[...] + (1.0 - ADAM_B2) * (g * g)
        mo_ref[...] = mn
        vo_ref[...] = vn
        d_ref[...] = -ADAM_LR * ((mn * bc1) / (jnp.sqrt(vn * bc2) + ADAM_EPS) + ADAM_WD * w_ref[...])

    spec = pl.BlockSpec((tr, c), lambda i: (i, 0))
    rspecs = [pl.BlockSpec((ns, tr, c), functools.partial(lambda i, k: (0, jnp.clip(i - k * nt, 0, nt - 1), 0), k=k))
              for k in range(nl)]
    return _pcall(
        body, name=name, grid=(r // tr,), in_specs=[spec, spec, spec] + rspecs,
        out_specs=[spec] * 4, out_shape=[jax.ShapeDtypeStruct((r, c), F32)] * 4, sem=("arbitrary",),
    )(w, m, v, *recvs)


def _sum_slots(recv, *, name):
    ns, r, c = recv.shape

    def body(r_ref, o_ref):
        g = r_ref[0]
        for s in range(1, ns):
            g = g + r_ref[s]
        o_ref[...] = g

    return pl.pallas_call(body, name=name, out_shape=jax.ShapeDtypeStruct((r, c), F32))(recv)


def _rms(x, g):
    return x * lax.rsqrt(jnp.mean(x * x, axis=-1, keepdims=True) + EPS) * g


def _silu(x):
    return x * jax.nn.sigmoid(x)


def _merge_f(a, b, c, g0, g1, g2):
    return jax.nn.sigmoid(g0) * a + jax.nn.sigmoid(g1) * b + jax.nn.sigmoid(g2) * c


def _ssd_post_f(yf, yb, xs, z, dskip, gnorm):
    y = (yf + yb + dskip * xs) * _silu(z)
    return _rms(y, gnorm)


def _combine_f(o0, o1, o2, l0, l1, l2):
    m = jnp.maximum(jnp.maximum(l0, l1), l2)
    e0, e1, e2 = jnp.exp(l0 - m), jnp.exp(l1 - m), jnp.exp(l2 - m)
    return (e0 * o0 + e1 * o1 + e2 * o2) / (e0 + e1 + e2)


def _key_window(ln, hw):
    return min(ln, QBLK + 2 * hw)


def _key_start(i, ln, hw):
    return pl.multiple_of(jnp.clip(i * QBLK - hw, 0, ln - _key_window(ln, hw)), 64)


def _attn_block(q, k3, v3, sk, qs, ks, *, hw, has_sink):
    s = lax.dot_general(q.astype(BF16), k3.astype(BF16), (((1,), (1,)), ((), ())),
                        preferred_element_type=F32) * (HEAD_DIM ** -0.5)
    qpos = qs + lax.broadcasted_iota(jnp.int32, s.shape, 0)
    kpos = ks + lax.broadcasted_iota(jnp.int32, s.shape, 1)
    valid = jnp.abs(qpos - kpos) <= hw
    s = jnp.where(valid, s, NEG_INF)
    m = jnp.max(s, axis=-1, keepdims=True)
    if has_sink:
        m = jnp.maximum(m, sk)
    m = lax.stop_gradient(m)
    e = jnp.exp(s - m)
    l = jnp.sum(e, axis=-1, keepdims=True)
    if has_sink:
        l = l + jnp.exp(sk - m)
    o = jnp.dot(e.astype(BF16), v3.astype(BF16), preferred_element_type=F32) / l
    return o, m + jnp.log(l)


def _ssd_chunk(state, xs, bm, cm, dtr, dtr_t, bias, bias_t, alog, alog_t, *, reverse):
    t = xs.shape[0]
    hg = dtr.shape[1]
    hp = xs.shape[1]
    p = hp // hg
    dt = jax.nn.softplus(dtr + bias)
    dt_t = jax.nn.softplus(dtr_t + bias_t)
    dta = dt * (-jnp.exp(alog))
    dta_t = dt_t * (-jnp.exp(alog_t))
    li = lax.broadcasted_iota(jnp.int32, (t, t), 0)
    si = lax.broadcasted_iota(jnp.int32, (t, t), 1)
    tri = (li <= si) if reverse else (li >= si)
    trif = tri.astype(F32)
    cs = jnp.dot(trif, dta, precision=HI, preferred_element_type=F32)
    cs_t = lax.dot_general(dta_t, trif, (((1,), (1,)), ((), ())), precision=HI,
                           preferred_element_type=F32)
    total = jnp.sum(dta, axis=0, keepdims=True)
    cb = lax.dot_general(cm.astype(BF16), bm.astype(BF16), (((1,), (1,)), ((), ())),
                         preferred_element_type=F32)
    lane_h = lax.broadcasted_iota(jnp.int32, (1, hp), 1) // p
    col_h = lax.broadcasted_iota(jnp.int32, (1, hg), 1)
    row_h = lax.broadcasted_iota(jnp.int32, (hg, 1), 0)
    dt_x = jnp.zeros((t, hp), F32)
    ecs_x = jnp.zeros((t, hp), F32)
    ds_x = jnp.zeros((t, hp), F32)
    etot_x = jnp.zeros((1, hp), F32)
    decays, masks = [], []
    for h in range(hg):
        oh = (col_h == h).astype(F32)
        oh_t = (row_h == h).astype(F32)
        mk = (lane_h == h).astype(F32)
        dt_h = jnp.sum(dt * oh, axis=1, keepdims=True)
        cs_h = jnp.sum(cs * oh, axis=1, keepdims=True)
        cst_h = jnp.sum(cs_t * oh_t, axis=0, keepdims=True)
        tot_h = jnp.sum(total * oh, axis=1, keepdims=True)
        dt_x = dt_x + dt_h * mk
        ecs_x = ecs_x + jnp.exp(cs_h) * mk
        ds_x = ds_x + jnp.exp(tot_h - cs_h) * mk
        etot_x = etot_x + jnp.exp(tot_h) * mk
        decays.append(jnp.exp(jnp.where(tri, cs_h - cst_h, -jnp.inf)))
        masks.append(mk)
    xdt = xs * dt_x
    y = jnp.dot(cm.astype(BF16), state.astype(BF16), preferred_element_type=F32) * ecs_x
    for h in range(hg):
        y = y + jnp.dot((cb * decays[h]).astype(BF16), (xdt * masks[h]).astype(BF16),
                        preferred_element_type=F32)
    st_new = lax.dot_general(bm.astype(BF16), (xdt * ds_x).astype(BF16), (((0,), (0,)), ((), ())),
                             preferred_element_type=F32)
    return y, state * etot_x + st_new


def _rope_tables(seq):
    half = ROPE_DIM // 2
    inv = ROPE_THETA ** (-jnp.arange(0, ROPE_DIM, 2, dtype=F32) / ROPE_DIM)
    ang = jnp.arange(seq, dtype=F32)[:, None] * inv[None, :]
    cos, sin = jnp.cos(ang), jnp.sin(ang)
    rest = HEAD_DIM - ROPE_DIM
    c = jnp.concatenate([cos, cos, jnp.ones((seq, rest), F32)], axis=1)
    a = jnp.concatenate([-sin, jnp.zeros((seq, HEAD_DIM - half), F32)], axis=1)
    b = jnp.concatenate([jnp.zeros((seq, half), F32), sin, jnp.zeros((seq, rest), F32)], axis=1)
    return c, a, b


def _rope(src, tabs, *, col0, width, seq, group, inverse, out_dtype, name, into=None, out_col0=0):
    t = src.shape[0]
    half = ROPE_DIM // 2
    nhb = 6 if all(v % (6 * HEAD_DIM) == 0 for v in (width, col0, out_col0)) else 3
    cw, tile = nhb * HEAD_DIM, 512
    assert width % cw == 0 and col0 % cw == 0 and out_col0 % cw == 0 and seq % tile == 0 and t % tile == 0
    ns = seq // tile

    def body(x_ref, c_ref, a_ref, b_ref, o_ref):
        jb = pl.program_id(1)
        c, a, b = c_ref[...], a_ref[...], b_ref[...]
        for hh in range(nhb):
            xv = x_ref[:, hh * HEAD_DIM:(hh + 1) * HEAD_DIM].astype(F32)
            if inverse:
                yv = xv * c + pltpu.roll(xv * a, half, 1) + pltpu.roll(xv * b, HEAD_DIM - half, 1)
            else:
                yv = xv * c + pltpu.roll(xv, HEAD_DIM - half, 1) * a + pltpu.roll(xv, half, 1) * b
            if group:
                keep = ((jb * nhb + hh) % group) == (group - 1)
                yv = jnp.where(keep, xv, yv)
            o_ref[:, hh * HEAD_DIM:(hh + 1) * HEAD_DIM] = yv.astype(o_ref.dtype)

    tspec = pl.BlockSpec((tile, HEAD_DIM), lambda i, j: (i % ns, 0))
    return _pcall(
        body, name=name, grid=(t // tile, width // cw),
        in_specs=[pl.BlockSpec((tile, cw), lambda i, j: (i, col0 // cw + j)), tspec, tspec, tspec],
        out_specs=pl.BlockSpec((tile, cw), lambda i, j: (i, out_col0 // cw + j)),
        out_shape=jax.ShapeDtypeStruct((t, width), out_dtype),
        sem=("parallel", "parallel"), into=None if into is None else {0: into},
    )(src, *tabs)


def _shift_rows(x, d, tpos):
    if d == 0:
        return x
    s = x.shape[0]
    y = pltpu.roll(x, (-d) % s, 0)
    ok = (tpos + d >= 0) & (tpos + d < s)
    return jnp.where(ok, y, 0.0)


def _conv_fwd(p, w8, bias, *, col0, chans, batch, seq, name):
    cb = 256
    assert chans % cb == 0 and col0 % cb == 0
    pad = (CONV_WIDTH - 1) // 2

    def body(x_ref, w_ref, b_ref, o_ref):
        x = x_ref[...]
        tpos = lax.broadcasted_iota(jnp.int32, x.shape, 0)
        acc = jnp.broadcast_to(b_ref[...], x.shape)
        for k in range(CONV_WIDTH):
            acc = acc + w_ref[k:k + 1, :] * _shift_rows(x, k - pad, tpos)
        o_ref[...] = _silu(acc)

    return _pcall(
        body, name=name, grid=(chans // cb, batch),
        in_specs=[pl.BlockSpec((seq, cb), lambda j, b: (b, col0 // cb + j)),
                  pl.BlockSpec((8, cb), lambda j, b: (0, j)), pl.BlockSpec((1, cb), lambda j, b: (0, j))],
        out_specs=pl.BlockSpec((seq, cb), lambda j, b: (b, j)),
        out_shape=jax.ShapeDtypeStruct((batch * seq, chans), F32),
        sem=("parallel", "arbitrary"),
    )(p, w8, bias)


def _conv_bwd(p, w8, bias, dus, *, col0, ch0, chans, batch, seq, into, out_col0, name):
    cb = 256 if chans % 256 == 0 and ch0 % 256 == 0 else 128
    assert chans % cb == 0 and ch0 % cb == 0 and col0 % cb == 0 and out_col0 % cb == 0
    pad = (CONV_WIDTH - 1) // 2
    ndu = len(dus)

    def body(*refs):
        x_ref, w_ref, b_ref = refs[:3]
        du_refs = refs[3:3 + ndu]
        dx_ref, dw_ref, db_ref = refs[3 + ndu:]
        du = du_refs[0][...]
        for r in du_refs[1:]:
            du = du + r[...]
        _conv_bwd_block(x_ref, w_ref, b_ref, du, dx_ref, dw_ref, db_ref)

    c0 = (col0 + ch0) // cb
    return _pcall(
        body, name=name, grid=(chans // cb, batch),
        in_specs=[pl.BlockSpec((seq, cb), lambda j, b: (b, c0 + j)),
                  pl.BlockSpec((8, cb), lambda j, b: (0, ch0 // cb + j)),
                  pl.BlockSpec((1, cb), lambda j, b: (0, ch0 // cb + j))]
        + [pl.BlockSpec((seq, cb), lambda j, b: (b, j))] * ndu,
        out_specs=[pl.BlockSpec((seq, cb), lambda j, b: (b, (out_col0 + ch0) // cb + j)),
                   pl.BlockSpec((8, cb), lambda j, b: (0, j)), pl.BlockSpec((1, cb), lambda j, b: (0, j))],
        out_shape=[jax.ShapeDtypeStruct(into.shape, into.dtype), jax.ShapeDtypeStruct((8, chans), F32),
                   jax.ShapeDtypeStruct((1, chans), F32)],
        sem=("parallel", "arbitrary"), into={0: into},
    )(p, w8, bias, *dus)


def _conv_bwd_block(x_ref, w_ref, b_ref, du, dx_ref, dw_ref, db_ref):
    pad = (CONV_WIDTH - 1) // 2
    x = x_ref[...]
    tpos = lax.broadcasted_iota(jnp.int32, x.shape, 0)
    acc = jnp.broadcast_to(b_ref[...], x.shape)
    xs = []
    for k in range(CONV_WIDTH):
        xs.append(_shift_rows(x, k - pad, tpos))
        acc = acc + w_ref[k:k + 1, :] * xs[k]
    sg = jax.nn.sigmoid(acc)
    dacc = du * (sg * (1.0 + acc * (1.0 - sg)))
    dx = jnp.zeros_like(x)
    for k in range(CONV_WIDTH):
        dx = dx + w_ref[k:k + 1, :] * _shift_rows(dacc, pad - k, tpos)
    dx_ref[...] = dx.astype(dx_ref.dtype)

    @pl.when(pl.program_id(1) == 0)
    def _():
        dw_ref[...] = jnp.zeros_like(dw_ref)
        db_ref[...] = jnp.zeros_like(db_ref)

    for k in range(CONV_WIDTH):
        dw_ref[k:k + 1, :] += jnp.sum(dacc * xs[k], axis=0, keepdims=True)
    db_ref[...] += jnp.sum(dacc, axis=0, keepdims=True)


def _scan_gpb(groups):
    return 2 if groups % 2 == 0 else 1


def _scan_specs(batch, nc, groups, hg, inner, reverse_order):
    gpb = _scan_gpb(groups)
    t, n, hp = SSD_CHUNK, SSD_STATE, hg * SSD_HEAD_DIM
    ncb = inner // (gpb * n)
    ngb = groups // gpb

    def row(b, c):
        return b * nc + ((nc - 1 - c) if reverse_order else c)

    return dict(
        xs=pl.BlockSpec((t, gpb * hp), lambda g, b, c: (row(b, c), g)),
        bm=pl.BlockSpec((t, gpb * n), lambda g, b, c: (row(b, c), ncb + g)),
        cm=pl.BlockSpec((t, gpb * n), lambda g, b, c: (row(b, c), ncb + ngb + g)),
        dtr=pl.BlockSpec((gpb, t, hg), lambda g, b, c: (g, row(b, c), 0)),
        dtr_t=pl.BlockSpec((gpb, hg, t), lambda g, b, c: (g, 0, row(b, c))),
        par=pl.BlockSpec((gpb, 1, hg), lambda g, b, c: (g, 0, 0)),
        par_t=pl.BlockSpec((gpb, hg, 1), lambda g, b, c: (g, 0, 0)),
        y=pl.BlockSpec((t, gpb * hp), lambda g, b, c: (row(b, c), g)),
        nrow=pl.BlockSpec((t, gpb * n), lambda g, b, c: (row(b, c), g)),
        st=pl.BlockSpec((gpb, None, n, hp), lambda g, b, c: (g, row(b, c), 0, 0)),
    )


def _scan_fwd(u, dtr, dtr_t, bias, bias_t, alog, alog_t, *, batch, seq, inner, reverse, name):
    groups, hg = dtr.shape[0], dtr.shape[2]
    nc = seq // SSD_CHUNK
    hp = hg * SSD_HEAD_DIM
    sp = _scan_specs(batch, nc, groups, hg, inner, reverse)
    gpb, n = _scan_gpb(groups), SSD_STATE

    def body(xs_ref, bm_ref, cm_ref, dtr_ref, dtrt_ref, b_ref, bt_ref, a_ref, at_ref, y_ref, st_ref, state):
        @pl.when(pl.program_id(2) == 0)
        def _():
            state[...] = jnp.zeros_like(state)

        for k in range(gpb):
            xc, nc_ = slice(k * hp, (k + 1) * hp), slice(k * n, (k + 1) * n)
            st_in = state[k]
            st_ref[k] = st_in
            y, st_out = _ssd_chunk(st_in, xs_ref[:, xc], bm_ref[:, nc_], cm_ref[:, nc_], dtr_ref[k], dtrt_ref[k],
                                   b_ref[k], bt_ref[k], a_ref[k], at_ref[k], reverse=reverse)
            y_ref[:, xc] = y
            state[k] = st_out

    return _pcall(
        body, name=name, grid=(groups // gpb, batch, nc),
        in_specs=[sp["xs"], sp["bm"], sp["cm"], sp["dtr"], sp["dtr_t"], sp["par"], sp["par_t"], sp["par"], sp["par_t"]],
        out_specs=[sp["y"], sp["st"]],
        out_shape=[jax.ShapeDtypeStruct((batch * seq, inner), F32),
                   jax.ShapeDtypeStruct((groups, batch * nc, SSD_STATE, hp), F32)],
        scratch_shapes=[pltpu.VMEM((gpb, SSD_STATE, hp), F32)],
        sem=("parallel", "arbitrary", "arbitrary"),
    )(u, u, u, dtr, dtr_t, bias, bias_t, alog, alog_t)


def _scan_bwd(u, dtr, dtr_t, bias, bias_t, alog, alog_t, st, dy, *, batch, seq, inner, reverse, name):
    groups, hg = dtr.shape[0], dtr.shape[2]
    nc = seq // SSD_CHUNK
    hp = hg * SSD_HEAD_DIM
    t = batch * seq
    sp = _scan_specs(batch, nc, groups, hg, inner, not reverse)
    f = functools.partial(_ssd_chunk, reverse=reverse)
    gpb, n = _scan_gpb(groups), SSD_STATE

    def body(xs_ref, bm_ref, cm_ref, dtr_ref, dtrt_ref, b_ref, bt_ref, a_ref, at_ref, st_ref, dy_ref,
             dxs_ref, dbm_ref, dcm_ref, ddtr_ref, ddtrt_ref, db_ref, dbt_ref, da_ref, dat_ref, dstate):
        first = (pl.program_id(1) == 0) & (pl.program_id(2) == 0)

        @pl.when(pl.program_id(2) == 0)
        def _():
            dstate[...] = jnp.zeros_like(dstate)

        @pl.when(first)
        def _():
            for r in (db_ref, dbt_ref, da_ref, dat_ref):
                r[...] = jnp.zeros_like(r)

        for k in range(gpb):
            xc, nc_ = slice(k * hp, (k + 1) * hp), slice(k * n, (k + 1) * n)
            _, vjp = jax.vjp(f, st_ref[k], xs_ref[:, xc], bm_ref[:, nc_], cm_ref[:, nc_], dtr_ref[k], dtrt_ref[k],
                             b_ref[k], bt_ref[k], a_ref[k], at_ref[k])
            dst, dxs, dbm, dcm, ddtr, ddtrt, db, dbt, da, dat = vjp((dy_ref[:, xc], dstate[k]))
            dstate[k] = dst
            dxs_ref[:, xc] = dxs
            dbm_ref[:, nc_] = dbm
            dcm_ref[:, nc_] = dcm
            ddtr_ref[k] = ddtr
            ddtrt_ref[k] = ddtrt
            db_ref[k] += db
            dbt_ref[k] += dbt
            da_ref[k] += da
            dat_ref[k] += dat

    gn = groups * SSD_STATE
    return _pcall(
        body, name=name, grid=(groups // gpb, batch, nc),
        in_specs=[sp["xs"], sp["bm"], sp["cm"], sp["dtr"], sp["dtr_t"], sp["par"], sp["par_t"], sp["par"], sp["par_t"],
                  sp["st"], sp["y"]],
        out_specs=[sp["y"], sp["nrow"], sp["nrow"], sp["dtr"], sp["dtr_t"], sp["par"], sp["par_t"], sp["par"], sp["par_t"]],
        out_shape=[jax.ShapeDtypeStruct((t, inner), F32), jax.ShapeDtypeStruct((t, gn), F32),
                   jax.ShapeDtypeStruct((t, gn), F32), jax.ShapeDtypeStruct(dtr.shape, F32),
                   jax.ShapeDtypeStruct(dtr_t.shape, F32), jax.ShapeDtypeStruct(bias.shape, F32),
                   jax.ShapeDtypeStruct(bias_t.shape, F32), jax.ShapeDtypeStruct(alog.shape, F32),
                   jax.ShapeDtypeStruct(alog_t.shape, F32)],
        scratch_shapes=[pltpu.VMEM((gpb, SSD_STATE, hp), F32)],
        sem=("parallel", "arbitrary", "arbitrary"),
    )(u, u, u, dtr, dtr_t, bias, bias_t, alog, alog_t, st, dy)


def _heads_per_step(nh, nb, nbw, cb0):
    hps = min(nh, max(1, 8 // nb))
    while nh % hps or nbw % hps or cb0 % hps:
        hps -= 1
    return hps


def _attn_load(ref, col, blk):
    return ref[pl.ds(pl.multiple_of(blk * QBLK, QBLK), QBLK), col * HEAD_DIM:(col + 1) * HEAD_DIM].astype(F32)


def _lane0(row):
    lane = lax.broadcasted_iota(jnp.int32, row.shape, 1)
    return jnp.sum(jnp.where(lane == 0, row, 0.0), axis=1, keepdims=True)


def _attn_fwd(rq, sinkx, *, batch, seq, dil, nbw, cb0, nh, rep, hw, want_lse, out_dtype, name):
    t, w = rq.shape
    ln = seq // dil
    nb = ln // QBLK
    bw = (rep + 2) * HEAD_DIM
    ow = nh * rep * HEAD_DIM
    has_sink = sinkx is not None
    rq3 = rq.reshape(batch, ln, dil * w)
    kwin = _key_window(ln, hw)
    hps = _heads_per_step(nh, nb, nbw, cb0)
    f = functools.partial(_attn_block, hw=hw, has_sink=has_sink)

    def body(*refs):
        if has_sink:
            blk_ref, sink_ref = refs[:2]
            outs = refs[2:]
        else:
            blk_ref, sink_ref = refs[0], None
            outs = refs[1:]
        o_ref = outs[0]
        lse_ref = outs[1] if want_lse else None
        g = pl.program_id(2)

        def qblock(i, carry):
            ks = _key_start(i, ln, hw)
            rows = pl.ds(pl.multiple_of(i * QBLK, QBLK), QBLK)
            for hh in range(hps):
                hb = hh * (rep + 2)
                k3 = blk_ref[pl.ds(ks, kwin), (hb + rep) * HEAD_DIM:(hb + rep + 1) * HEAD_DIM].astype(F32)
                v3 = blk_ref[pl.ds(ks, kwin), (hb + rep + 1) * HEAD_DIM:(hb + rep + 2) * HEAD_DIM].astype(F32)
                for r in range(rep):
                    sk = _lane0(sink_ref[pl.ds((g * hps + hh) * rep + r, 1), :]) if has_sink else None
                    o, lse = f(_attn_load(blk_ref, hb + r, i), k3, v3, sk, i * QBLK, ks)
                    oc = slice((hh * rep + r) * HEAD_DIM, (hh * rep + r + 1) * HEAD_DIM)
                    o_ref[rows, oc] = o.astype(o_ref.dtype)
                    if want_lse:
                        lse_ref[rows, oc] = jnp.broadcast_to(lse, o.shape)
            return carry

        if nb == 1:
            qblock(0, 0)
        else:
            lax.fori_loop(0, nb, qblock, 0, unroll=2 if nb % 2 == 0 else 1)

    in_specs = [pl.BlockSpec((None, ln, hps * bw), lambda b, r, h: (b, 0, (r * nbw + cb0) // hps + h))]
    ins = [rq3]
    if has_sink:
        in_specs.append(pl.BlockSpec(sinkx.shape, lambda b, r, h: (0, 0)))
        ins.append(sinkx)
    ospec = pl.BlockSpec((None, ln, hps * rep * HEAD_DIM), lambda b, r, h: (b, 0, r * (nh // hps) + h))
    out_shape = [jax.ShapeDtypeStruct((batch, ln, dil * ow), out_dtype)]
    out_specs = [ospec]
    if want_lse:
        out_shape.append(jax.ShapeDtypeStruct((batch, ln, dil * ow), F32))
        out_specs.append(ospec)
    res = _pcall(
        body, name=name, grid=(batch, dil, nh // hps), in_specs=in_specs, out_specs=out_specs, out_shape=out_shape,
        sem=("parallel", "parallel", "parallel"),
    )(*ins)
    return [r.reshape(t, ow) for r in res]


def _attn_bwd(rq, sinkx, do, dlse, *, batch, seq, dil, nbw, cb0, nh, rep, hw, name):
    t, w = rq.shape
    ln = seq // dil
    nb = ln // QBLK
    bw = (rep + 2) * HEAD_DIM
    ow = nh * rep * HEAD_DIM
    has_sink = sinkx is not None
    has_lse = dlse is not None
    kwin = _key_window(ln, hw)
    hps = _heads_per_step(nh, nb, nbw, cb0)
    f = functools.partial(_attn_block, hw=hw, has_sink=has_sink)

    def body(*refs):
        refs = list(refs)
        blk_ref = refs.pop(0)
        sink_ref = refs.pop(0) if has_sink else None
        do_ref = refs.pop(0)
        dlse_ref = refs.pop(0) if has_lse else None
        d_ref = refs.pop(0)
        dsink_ref = refs.pop(0) if has_sink else None
        g = pl.program_id(2)
        for hh in range(hps):
            kv0 = (hh * (rep + 2) + rep) * HEAD_DIM
            d_ref[:, kv0:kv0 + 2 * HEAD_DIM] = jnp.zeros((ln, 2 * HEAD_DIM), F32)
        if has_sink:
            @pl.when((pl.program_id(0) == 0) & (pl.program_id(1) == 0) & (g == 0))
            def _():
                dsink_ref[...] = jnp.zeros_like(dsink_ref)

        def qblock(i, carry):
            ks = _key_start(i, ln, hw)
            krows = pl.ds(ks, kwin)
            rows = pl.ds(pl.multiple_of(i * QBLK, QBLK), QBLK)
            for hh in range(hps):
                hb = hh * (rep + 2)
                kc = slice((hb + rep) * HEAD_DIM, (hb + rep + 1) * HEAD_DIM)
                vc = slice((hb + rep + 1) * HEAD_DIM, (hb + rep + 2) * HEAD_DIM)
                k3 = blk_ref[krows, kc].astype(F32)
                v3 = blk_ref[krows, vc].astype(F32)
                dk3 = jnp.zeros_like(k3)
                dv3 = jnp.zeros_like(v3)
                for r in range(rep):
                    oc = slice((hh * rep + r) * HEAD_DIM, (hh * rep + r + 1) * HEAD_DIM)
                    q = _attn_load(blk_ref, hb + r, i)
                    dov = do_ref[rows, oc]
                    dl = dlse_ref[rows, oc] if has_lse else jnp.zeros_like(dov)
                    if has_sink:
                        srow_i = (g * hps + hh) * rep + r
                        srow = sink_ref[pl.ds(srow_i, 1), :]
                        _, vjp = jax.vjp(lambda q_, k_, v_, s_: f(q_, k_, v_, _lane0(s_), i * QBLK, ks), q, k3, v3, srow)
                        dq, dk, dv, ds = vjp((dov, jnp.sum(dl, axis=1, keepdims=True)))
                        dsink_ref[pl.ds(srow_i, 1), :] += ds
                    else:
                        _, vjp = jax.vjp(lambda q_, k_, v_: f(q_, k_, v_, None, i * QBLK, ks), q, k3, v3)
                        dq, dk, dv = vjp((dov, jnp.sum(dl, axis=1, keepdims=True)))
                    d_ref[rows, (hb + r) * HEAD_DIM:(hb + r + 1) * HEAD_DIM] = dq
                    dk3 = dk3 + dk
                    dv3 = dv3 + dv
                d_ref[krows, kc] += dk3
                d_ref[krows, vc] += dv3
            return carry

        if nb == 1:
            qblock(0, 0)
        else:
            lax.fori_loop(0, nb, qblock, 0, unroll=2 if nb % 2 == 0 else 1)

    ospec = pl.BlockSpec((None, ln, hps * rep * HEAD_DIM), lambda b, r, h: (b, 0, r * (nh // hps) + h))
    in_specs = [pl.BlockSpec((None, ln, hps * bw), lambda b, r, h: (b, 0, (r * nbw + cb0) // hps + h))]
    ins = [rq.reshape(batch, ln, dil * w)]
    if has_sink:
        in_specs.append(pl.BlockSpec(sinkx.shape, lambda b, r, h: (0, 0)))
        ins.append(sinkx)
    in_specs.append(ospec)
    ins.append(do.reshape(batch, ln, dil * ow))
    if has_lse:
        in_specs.append(ospec)
        ins.append(dlse.reshape(batch, ln, dil * ow))
    dw = nh * bw
    out_specs = [pl.BlockSpec((None, ln, hps * bw), lambda b, r, h: (b, 0, r * (nh // hps) + h))]
    out_shape = [jax.ShapeDtypeStruct((batch, ln, dil * dw), F32)]
    if has_sink:
        out_specs.append(pl.BlockSpec(sinkx.shape, lambda b, r, h: (0, 0)))
        out_shape.append(jax.ShapeDtypeStruct(sinkx.shape, F32))
    res = _pcall(
        body, name=name, grid=(batch, dil, nh // hps), in_specs=in_specs, out_specs=out_specs, out_shape=out_shape,
        sem=("arbitrary", "arbitrary", "arbitrary"),
    )(*ins)
    return [res[0].reshape(t, dw)] + list(res[1:])


def _final_loss(x, g, target, *, name):
    d = x.shape[1]

    def fn(rv, pv):
        xv, tg = rv
        y, vjp = jax.vjp(_rms, xv, pv[0])
        err = y - tg
        dx, dg = vjp(err * (1.0 / d))
        loss = 0.5 * jnp.sum(err * err) * (1.0 / d)
        return [dx], [dg, jnp.full((1, LANE), loss, F32)]

    (dx,), (dg, loss) = _rowcall(fn, [(x, d, 0), (target, d, 0)], [g], [(d, F32)], [(1, d), (1, LANE)],
                                 tile=256, name=name)
    return dx, dg, loss


class _Layout:
    def __init__(self, d_model):
        self.d = d_model
        self.inner = SSD_HEADS * SSD_HEAD_DIM
        self.gn = SSD_GROUPS * SSD_STATE
        self.xbc = self.inner + 2 * self.gn
        self.ndt = 2 * SSD_HEADS
        self.ngrp = len(DIL_PATTERNS)
        self.dilw = DIL_HEADS * HEAD_DIM
        self.rqd = 3 * self.ngrp * self.dilw
        self.rep = WIN_Q_HEADS // WIN_KV_HEADS
        self.rqw = WIN_KV_HEADS * (self.rep + 2) * HEAD_DIM
        self.qw = WIN_Q_HEADS * HEAD_DIM
        self.kw = WIN_KV_HEADS * HEAD_DIM
        self.gates = N_BRANCH * d_model
        self.n_in = self.inner + self.xbc + self.ndt + self.rqd + self.qw + 2 * self.kw + self.gates
        self.o_gates = 0
        self.o_z = self.gates
        self.o_xbc = self.o_z + self.inner
        self.o_rqd = self.o_xbc + self.xbc
        self.o_rqw = self.o_rqd + self.rqd
        self.o_dt = self.o_rqw + self.rqw
        self.dtw = -(-(self.o_dt + self.ndt) // PAD_TO) * PAD_TO - self.o_dt
        self.width = self.o_dt + self.dtw
        assert self.o_z % self.inner == 0 and self.o_xbc % 256 == 0
        assert self.o_rqd % (3 * HEAD_DIM) == 0 and self.o_rqw % (3 * HEAD_DIM) == 0 and self.dtw % LANE == 0
        assert self.o_dt % self.dtw == 0

    def split_points(self):
        sizes = (self.inner, self.xbc, self.ndt, self.rqd, self.qw, self.kw, self.kw, self.gates)
        pts, acc = [], 0
        for s in sizes:
            pts.append((acc, acc + s))
            acc += s
        return pts

    def permute_w(self, w):
        d = w.shape[0]
        z, xbc, dt, qkvd, qw, kw, vw, gates = [w[:, a:b] for a, b in self.split_points()]
        nhd = self.ngrp * DIL_HEADS
        qkvd = qkvd.reshape(d, 3, nhd, HEAD_DIM).transpose(0, 2, 1, 3).reshape(d, self.rqd)
        win = jnp.concatenate([qw.reshape(d, WIN_KV_HEADS, self.rep, HEAD_DIM),
                               kw.reshape(d, WIN_KV_HEADS, 1, HEAD_DIM),
                               vw.reshape(d, WIN_KV_HEADS, 1, HEAD_DIM)], axis=2).reshape(d, self.rqw)
        pad = jnp.zeros((d, self.dtw - self.ndt), w.dtype)
        return jnp.concatenate([gates, z, xbc, qkvd, win, dt, pad], axis=1)

    def unpermute_w(self, wp):
        d = wp.shape[0]
        gates = wp[:, :self.o_z]
        z = wp[:, self.o_z:self.o_xbc]
        xbc = wp[:, self.o_xbc:self.o_rqd]
        qkvd = wp[:, self.o_rqd:self.o_rqw]
        win = wp[:, self.o_rqw:self.o_dt].reshape(d, WIN_KV_HEADS, self.rep + 2, HEAD_DIM)
        dt = wp[:, self.o_dt:self.o_dt + self.ndt]
        nhd = self.ngrp * DIL_HEADS
        qkvd = qkvd.reshape(d, nhd, 3, HEAD_DIM).transpose(0, 2, 1, 3).reshape(d, self.rqd)
        qw = win[:, :, :self.rep].reshape(d, self.qw)
        kw = win[:, :, self.rep].reshape(d, self.kw)
        vw = win[:, :, self.rep + 1].reshape(d, self.kw)
        return jnp.concatenate([z, xbc, dt, qkvd, qw, kw, vw, gates], axis=1)


def _dt_layouts(pdt, dirn, batch_seq):
    hg = SSD_HEADS // SSD_GROUPS
    v = pdt[:, dirn * SSD_HEADS:(dirn + 1) * SSD_HEADS].reshape(batch_seq, SSD_GROUPS, hg)
    return v.transpose(1, 0, 2), v.transpose(1, 2, 0)


def _par_layouts(p):
    hg = SSD_HEADS // SSD_GROUPS
    v = p.reshape(SSD_GROUPS, hg)
    return v[:, None, :], v[:, :, None]


def _layer_fwd(x, lw, lay, tabs, batch, seq):
    d = lay.d
    sv = {"x": x}
    (h,), _ = _rowcall(lambda rv, pv: ([_rms(rv[0], pv[0])], []), [(x, d, 0)], [lw["g_mix"]], [(d, BF16)], [],
                       tile=256, name="norm_mix")
    p = _mm(h, lw["w_in"], name="proj_in")
    sv["h"], sv["p"] = h, p
    u = _conv_fwd(p, lw["conv_w8"], lw["conv_b"], col0=lay.o_xbc, chans=lay.xbc, batch=batch, seq=seq, name="conv_fwd")
    sv["u"] = u
    pdt = p[:, lay.o_dt:lay.o_dt + lay.ndt]
    ys, sv["st"], sv["dtl"] = [], [], []
    for dirn in range(2):
        dtr, dtr_t = _dt_layouts(pdt, dirn, batch * seq)
        bias, bias_t = _par_layouts(lw["dt_bias"][dirn])
        alog, alog_t = _par_layouts(lw["a_log"][dirn])
        y, st = _scan_fwd(u, dtr, dtr_t, bias, bias_t, alog, alog_t, batch=batch, seq=seq, inner=lay.inner,
                          reverse=bool(dirn), name="scan_fwd%d" % dirn)
        ys.append(y)
        sv["st"].append(st)
        sv["dtl"].append((dtr, dtr_t, bias, bias_t, alog, alog_t))
    sv["ys"] = ys
    inner = lay.inner
    (ya,), _ = _rowcall(lambda rv, pv: ([_ssd_post_f(*rv, *pv)], []),
                        [(ys[0], inner, 0), (ys[1], inner, 0), (u, inner, 0), (p, inner, lay.o_z // inner)],
                        [lw["d_skip_x"], lw["ssd_norm"]], [(inner, BF16)], [], tile=128, name="ssd_post")
    sv["ya"] = ya
    gw = 3 * lay.dilw
    rqd = [_rope(p, tabs, col0=lay.o_rqd + gi * gw, width=gw, seq=seq, group=0, inverse=False, out_dtype=BF16,
                 name="rope_dil") for gi in range(lay.ngrp)]
    rqw = _rope(p, tabs, col0=lay.o_rqw, width=lay.rqw, seq=seq, group=lay.rep + 2, inverse=False, out_dtype=BF16,
                name="rope_win")
    sv["rqd"], sv["rqw"] = rqd, rqw
    os_, ls_ = [], []
    for gi, (window, dil) in enumerate(DIL_PATTERNS):
        o, l = _attn_fwd(rqd[gi], None, batch=batch, seq=seq, dil=dil, nbw=DIL_HEADS, cb0=0,
                         nh=DIL_HEADS, rep=1, hw=window // (2 * dil), want_lse=True, out_dtype=F32,
                         name="dil_fwd%d" % gi)
        os_.append(o)
        ls_.append(l)
    sv["os"], sv["ls"] = os_, ls_
    dw = lay.dilw
    (yb,), _ = _rowcall(lambda rv, pv: ([_combine_f(*rv)], []), [(a, dw, 0) for a in os_ + ls_], [], [(dw, BF16)], [],
                        tile=256, name="dil_combine")
    sv["yb"] = yb
    (yc,) = _attn_fwd(rqw, lw["sink_x"], batch=batch, seq=seq, dil=1, nbw=WIN_KV_HEADS, cb0=0, nh=WIN_KV_HEADS,
                      rep=lay.rep, hw=WIN_HALF, want_lse=False, out_dtype=BF16, name="win_fwd")
    sv["yc"] = yc
    ma = _mm(ya, lw["w_a"], name="proj_a")
    mb = _mm(yb, lw["w_b"], name="proj_b")
    mc = _mm(yc, lw["w_c"], name="proj_c")
    sv["mabc"] = (ma, mb, mc)
    (mg,), _ = _rowcall(lambda rv, pv: ([_merge_f(*rv)], []),
                        [(ma, d, 0), (mb, d, 0), (mc, d, 0), (p, d, 0), (p, d, 1), (p, d, 2)], [], [(d, BF16)], [],
                        tile=256, name="merge")
    sv["mg"] = mg
    x1 = _mm(mg, lw["w_out"], add=x, name="proj_out")
    sv["x1"] = x1
    (hm,), _ = _rowcall(lambda rv, pv: ([_rms(rv[0], pv[0])], []), [(x1, d, 0)], [lw["g_mlp"]], [(d, BF16)], [],
                        tile=256, name="norm_mlp")
    up, act = _mm(hm, lw["w_up"], post=(lambda r: (r, jnp.square(jnp.maximum(r, 0.0))), [], [F32, BF16]),
                  name="mlp_up")
    sv["hm"], sv["up"], sv["act"] = hm, up, act
    x2 = _mm(act, lw["w_down"], add=x1, name="mlp_down")
    return x2, sv


def _layer_bwd(dxo, sv, lw, lay, tabs, batch, seq, post):
    d = lay.d
    inner = lay.inner
    gs = {}
    (dup,) = _mm(dxo, lw["w_down"], tb=True, name="mlp_down_dx",
                 post=(lambda r, a: (r * (2.0 * jnp.maximum(a, 0.0)),), [sv["up"]], [BF16]))
    post("w_down", _mm(sv["act"], dxo, ta=True, name="mlp_down_dw"))
    dhm = _mm(dup, lw["w_up"], tb=True, name="mlp_up_dx")
    post("w_up", _mm(sv["hm"], dup, ta=True, name="mlp_up_dw"))

    def norm_bwd(rv, pv):
        xv, dh, dres = rv
        _, vjp = jax.vjp(_rms, xv, pv[0])
        dx, dg = vjp(dh)
        return [dx + dres], [dg]

    (dx1,), (gs["g_mlp"],) = _rowcall(norm_bwd, [(sv["x1"], d, 0), (dhm, d, 0), (dxo, d, 0)], [lw["g_mlp"]],
                                      [(d, F32)], [(1, d)], tile=128, name="norm_mlp_bwd")
    dmg = _mm(dx1, lw["w_out"], tb=True, name="proj_out_dx")
    post("w_out", _mm(sv["mg"], dx1, ta=True, name="proj_out_dw"))
    ma, mb, mc = sv["mabc"]
    p = sv["p"]

    def merge_bwd(rv, pv):
        _, vjp = jax.vjp(_merge_f, *rv[:6])
        da, db, dc, d0, d1, d2 = vjp(rv[6])
        return [da, db, dc, jnp.concatenate([d0, d1, d2], axis=1)], []

    dp = lax.empty((batch * seq, lay.width), BF16)
    (dma, dmb, dmc, dp), _ = _rowcall(
        merge_bwd, [(ma, d, 0), (mb, d, 0), (mc, d, 0), (p, d, 0), (p, d, 1), (p, d, 2), (dmg, d, 0)], [],
        [(d, BF16), (d, BF16), (d, BF16), (lay.gates, BF16, dp, 0)], [], tile=128, name="merge_bwd")
    dya = _mm(dma, lw["w_a"], tb=True, name="proj_a_dx")
    post("w_a", _mm(sv["ya"], dma, ta=True, name="proj_a_dw"))
    dyb = _mm(dmb, lw["w_b"], tb=True, name="proj_b_dx")
    post("w_b", _mm(sv["yb"], dmb, ta=True, name="proj_b_dw"))
    dyc = _mm(dmc, lw["w_c"], tb=True, name="proj_c_dx")
    post("w_c", _mm(sv["yc"], dmc, ta=True, name="proj_c_dw"))
    drqw, dsink = _attn_bwd(sv["rqw"], lw["sink_x"], dyc, None, batch=batch, seq=seq, dil=1, nbw=WIN_KV_HEADS, cb0=0,
                            nh=WIN_KV_HEADS, rep=lay.rep, hw=WIN_HALF, name="win_bwd")
    gs["sink"] = jnp.sum(dsink, axis=1)
    dw = lay.dilw

    def combine_bwd(rv, pv):
        _, vjp = jax.vjp(_combine_f, *rv[:6])
        return list(vjp(rv[6])), []

    dol, _ = _rowcall(combine_bwd, [(a, dw, 0) for a in sv["os"] + sv["ls"]] + [(dyb, dw, 0)], [],
                      [(dw, F32)] * 6, [], tile=256, name="dil_combine_bwd")
    for gi, (window, dil) in enumerate(DIL_PATTERNS):
        (dg_,) = _attn_bwd(sv["rqd"][gi], None, dol[gi], dol[3 + gi], batch=batch, seq=seq, dil=dil,
                           nbw=DIL_HEADS, cb0=0, nh=DIL_HEADS, rep=1,
                           hw=window // (2 * dil), name="dil_bwd%d" % gi)
        dp = _rope(dg_, tabs, col0=0, width=dg_.shape[1], seq=seq, group=0, inverse=True, out_dtype=BF16,
                   name="rope_dil_bwd%d" % gi, into=dp, out_col0=lay.o_rqd + gi * dg_.shape[1])
    dp = _rope(drqw, tabs, col0=0, width=lay.rqw, seq=seq, group=lay.rep + 2, inverse=True, out_dtype=BF16,
               name="rope_win_bwd", into=dp, out_col0=lay.o_rqw)
    u, ys = sv["u"], sv["ys"]

    def post_bwd(rv, pv):
        _, vjp = jax.vjp(_ssd_post_f, *rv[:4], *pv)
        dyf, _, dxs, dz, dsk, dgn = vjp(rv[4])
        return [dyf, dxs, dz], [dsk, dgn]

    (dy, dxs_post, dp), (dsk, gs["ssd_norm"]) = _rowcall(
        post_bwd, [(ys[0], inner, 0), (ys[1], inner, 0), (u, inner, 0), (p, inner, lay.o_z // inner), (dya, inner, 0)],
        [lw["d_skip_x"], lw["ssd_norm"]], [(inner, F32), (inner, F32), (inner, BF16, dp, lay.o_z // inner)],
        [(1, inner), (1, inner)], tile=128, name="ssd_post_bwd")
    gs["d_skip"] = jnp.sum(dsk.reshape(SSD_HEADS, SSD_HEAD_DIM), axis=1)
    rs = []
    ddt, gdb, gda = [], [], []
    for dirn in range(2):
        dtr, dtr_t, bias, bias_t, alog, alog_t = sv["dtl"][dirn]
        r = _scan_bwd(u, dtr, dtr_t, bias, bias_t, alog, alog_t, sv["st"][dirn], dy, batch=batch, seq=seq,
                      inner=inner, reverse=bool(dirn), name="scan_bwd%d" % dirn)
        rs.append(r)
        ddt.append((r[3] + r[4].transpose(0, 2, 1)).transpose(1, 0, 2).reshape(batch * seq, SSD_HEADS))
        gdb.append((r[5][:, 0, :] + r[6][:, :, 0]).reshape(SSD_HEADS))
        gda.append((r[7][:, 0, :] + r[8][:, :, 0]).reshape(SSD_HEADS))
    gs["dt_bias"] = jnp.stack(gdb)
    gs["a_log"] = jnp.stack(gda)
    dcws, dcbs = [], []
    for tag, ch0, chans, dus in (("x", 0, inner, [dxs_post, rs[0][0], rs[1][0]]),
                                 ("b", inner, lay.gn, [rs[0][1], rs[1][1]]),
                                 ("c", inner + lay.gn, lay.gn, [rs[0][2], rs[1][2]])):
        dp, dcw, dcb = _conv_bwd(p, lw["conv_w8"], lw["conv_b"], dus, col0=lay.o_xbc, ch0=ch0, chans=chans,
                                 batch=batch, seq=seq, into=dp, out_col0=lay.o_xbc, name="conv_bwd_" + tag)
        dcws.append(dcw)
        dcbs.append(dcb)
    gs["conv_w"] = jnp.concatenate(dcws, axis=1)[:CONV_WIDTH]
    gs["conv_b"] = jnp.concatenate(dcbs, axis=1)[0]
    ddtp = jnp.concatenate(ddt + [jnp.zeros((batch * seq, lay.dtw - lay.ndt), F32)], axis=1)
    (dp,), _ = _rowcall(lambda rv, pv: ([rv[0]], []), [(ddtp, lay.dtw, 0)], [],
                        [(lay.dtw, BF16, dp, lay.o_dt // lay.dtw)], [], tile=512, name="ddt_store")
    hd = d // W_IN_PARTS
    for part in range(W_IN_PARTS):
        post(("w_in", part), _mm(sv["h"][:, part * hd:(part + 1) * hd], dp, ta=True, name="proj_in_dw"))
    dh = _mm(dp, lw["w_in"], tb=True, name="proj_in_dx")
    (dx,), (gs["g_mix"],) = _rowcall(norm_bwd, [(sv["x"], d, 0), (dh, d, 0), (dx1, d, 0)], [lw["g_mix"]],
                                     [(d, F32)], [(1, d)], tile=128, name="norm_mix_bwd")
    return dx, gs


_SHARDED = ("w_in", "w_a", "w_b", "w_c", "w_out", "w_up", "w_down")
_COL_SHARDED = ("w_in", "w_b", "w_up")
_SMALL = ("g_mix", "conv_b", "dt_bias", "a_log", "d_skip", "ssd_norm", "sink", "g_mlp")


def _gathered_to_full(name, g):
    n, r, c = g.shape
    if name in _COL_SHARDED:
        return g.transpose(1, 0, 2).reshape(r, n * c)
    return g.reshape(n * r, c)


def _full_to_slots(name, w):
    r, c = w.shape
    if name in _COL_SHARDED:
        return w.reshape(r, N_DEV, c // N_DEV).transpose(1, 0, 2)
    return w.reshape(N_DEV, r // N_DEV, c)


class _LayerWeights:
    def __init__(self, sched, layer, lay, small):
        self.sched, self.layer, self.lay, self.vals = sched, layer, lay, dict(small)

    def __getitem__(self, name):
        if name not in self.vals:
            full = _gathered_to_full(name, self.sched.get(("w", name, self.layer)))
            self.vals[name] = self.lay.permute_w(full) if name == "w_in" else full
        return self.vals[name]


def _pack(parts):
    flat = jnp.concatenate([p.reshape(-1).astype(F32) for p in parts])
    n = flat.shape[0]
    rows = -(-n // (8 * LANE)) * 8
    return jnp.pad(flat, (0, rows * LANE - n)).reshape(rows, LANE)


def _unpack(buf, shapes):
    flat = buf.reshape(-1)
    out, off = [], 0
    for s in shapes:
        n = math.prod(s)
        out.append(flat[off:off + n].reshape(s))
        off += n
    return out


def kernel(x, g_mix, w_in, conv_w, conv_b, dt_bias, a_log, d_skip, ssd_norm, w_a, w_b, w_c, sink, w_out, g_mlp, w_up, w_down, g_final, loss_target, m_g_mix, m_w_in, m_conv_w, m_conv_b, m_dt_bias, m_a_log, m_d_skip, m_ssd_norm, m_w_a, m_w_b, m_w_c, m_sink, m_w_out, m_g_mlp, m_w_up, m_w_down, m_g_final, v_g_mix, v_w_in, v_conv_w, v_conv_b, v_dt_bias, v_a_log, v_d_skip, v_ssd_norm, v_w_a, v_w_b, v_w_c, v_sink, v_w_out, v_g_mlp, v_w_up, v_w_down, v_g_final):
    batch, seq, d = x.shape
    depth = g_mix.shape[0]
    lay = _Layout(d)
    assert lay.n_in == w_in.shape[2] * N_DEV
    wts = dict(g_mix=g_mix, w_in=w_in, conv_w=conv_w, conv_b=conv_b, dt_bias=dt_bias, a_log=a_log, d_skip=d_skip,
               ssd_norm=ssd_norm, w_a=w_a, w_b=w_b, w_c=w_c, sink=sink, w_out=w_out, g_mlp=g_mlp, w_up=w_up,
               w_down=w_down, g_final=g_final)
    mom = dict(g_mix=m_g_mix, w_in=m_w_in, conv_w=m_conv_w, conv_b=m_conv_b, dt_bias=m_dt_bias, a_log=m_a_log,
               d_skip=m_d_skip, ssd_norm=m_ssd_norm, w_a=m_w_a, w_b=m_w_b, w_c=m_w_c, sink=m_sink, w_out=m_w_out,
               g_mlp=m_g_mlp, w_up=m_w_up, w_down=m_w_down, g_final=m_g_final)
    var = dict(g_mix=v_g_mix, w_in=v_w_in, conv_w=v_conv_w, conv_b=v_conv_b, dt_bias=v_dt_bias, a_log=v_a_log,
               d_skip=v_d_skip, ssd_norm=v_ssd_norm, w_a=v_w_a, w_b=v_w_b, w_c=v_w_c, sink=v_sink, w_out=v_w_out,
               g_mlp=v_g_mlp, w_up=v_w_up, w_down=v_w_down, g_final=v_g_final)
    me = 4 * lax.axis_index("x") + 2 * lax.axis_index("y") + lax.axis_index("c")

    global _SCHED
    sched = _SCHED = _Sched()
    (gconv,) = _exchange([conv_w], scatter=False, name="gather_conv_w")
    conv_full = gconv.transpose(1, 2, 0, 3).reshape(depth, CONV_WIDTH, -1)
    for l in range(depth):
        for n in _SHARDED:
            sched.post(("w", n, l), wts[n][l].astype(BF16), scatter=False)

    tabs = _rope_tables(seq)
    t = batch * seq
    xf = x.reshape(t, d)
    layers = []
    for l in range(depth):
        small = dict(
            g_mix=g_mix[l][None], g_mlp=g_mlp[l][None], ssd_norm=ssd_norm[l][None], conv_b=conv_b[l][None],
            dt_bias=dt_bias[l], a_log=a_log[l],
            d_skip_x=jnp.repeat(d_skip[l], SSD_HEAD_DIM)[None],
            sink_x=jnp.broadcast_to(sink[l][:, None], (WIN_Q_HEADS, LANE)),
            conv_w8=jnp.pad(conv_full[l], ((0, 8 - CONV_WIDTH), (0, 0))))
        layers.append(_LayerWeights(sched, l, lay, small))

    saves = []
    h = xf
    for l in range(depth):
        h, sv = _layer_fwd(h, layers[l], lay, tabs, batch, seq)
        saves.append(sv)
    dx, dgf, loss = _final_loss(h, g_final[None], loss_target.reshape(t, d), name="final_loss")

    gss = [None] * depth
    for l in reversed(range(depth)):
        def post(n, g, l=l):
            if isinstance(n, tuple):
                key, n, g = ("g", "w_in", l, n[1]), "w_in", lay.unpermute_w(g)
            else:
                key = ("g", n, l)
            sched.post(key, _full_to_slots(n, g).astype(BF16), scatter=True)

        dx, gss[l] = _layer_bwd(dx, saves[l], layers[l], lay, tabs, batch, seq, post)
    grad_x = dx.reshape(batch, seq, d)

    small_parts = [jnp.stack([gss[l][n] for l in range(depth)]) for n in _SMALL]
    small_parts += [dgf, jnp.stack([gss[l]["conv_w"] for l in range(depth)]), loss[0, :1]]
    small_shapes = [p.shape for p in small_parts]
    (rs,) = _exchange([_pack(small_parts)], scatter=False, name="gather_small")
    red = _unpack(_sum_slots(rs, name="sum_small"), small_shapes)
    gsmall = dict(zip(list(_SMALL) + ["g_final"], red[:len(_SMALL) + 1]))
    gconv_full, loss_sum = red[-2], red[-1]
    cshard = conv_w.shape[2]
    gsmall["conv_w"] = lax.dynamic_slice_in_dim(gconv_full, me * cshard, cshard, axis=2)

    out = {}
    rep_names = list(_SMALL) + ["g_final"]
    rep_shapes = [wts[n].shape for n in rep_names]
    res = _adamw(_pack([wts[n] for n in rep_names]), [_pack([gsmall[n] for n in rep_names])[None]],
                 _pack([mom[n] for n in rep_names]), _pack([var[n] for n in rep_names]), name="adamw_small")
    unp = [_unpack(a, rep_shapes) for a in res]
    for i, n in enumerate(rep_names):
        out[n] = [unp[k][i] for k in range(4)]
    cs2 = (depth * CONV_WIDTH, cshard)
    res = _adamw(conv_w.reshape(cs2), [gsmall["conv_w"].reshape((1,) + cs2)], m_conv_w.reshape(cs2),
                 v_conv_w.reshape(cs2), name="adamw_conv_w")
    out["conv_w"] = [a.reshape(conv_w.shape) for a in res]
    for n in ("w_down", "w_up", "w_out", "w_a", "w_b", "w_c", "w_in"):
        shp = wts[n].shape
        r2 = (shp[0] * shp[1], shp[2])
        if n == "w_in":
            recvs = [sched.get(("g", n, l, part)) for l in range(depth) for part in range(W_IN_PARTS)]
        else:
            recvs = [sched.get(("g", n, l)) for l in range(depth)]
        res = _adamw(wts[n].reshape(r2), recvs, mom[n].reshape(r2), var[n].reshape(r2), name="adamw_" + n)
        out[n] = [a.reshape(shp) for a in res]

    order = ["g_mix", "w_in", "conv_w", "conv_b", "dt_bias", "a_log", "d_skip", "ssd_norm", "w_a", "w_b", "w_c",
             "sink", "w_out", "g_mlp", "w_up", "w_down", "g_final"]
    outs = [loss_sum.reshape(()), grad_x]
    for k in range(4):
        outs += [out[n][k] for n in order]
    return tuple(outs)
```

```python
import functools
import math

import jax
import jax.numpy as jnp
from jax import lax
from jax.experimental import pallas as pl
from jax.experimental.pallas import tpu as pltpu

F32 = jnp.float32
BF16 = jnp.bfloat16
HI = lax.Precision.HIGHEST
MESH = pl.DeviceIdType.MESH
N_DEV = 8

SSD_HEADS = 32
SSD_HEAD_DIM = 64
SSD_GROUPS = 8
SSD_STATE = 128
SSD_CHUNK = 128
CONV_WIDTH = 5
HEAD_DIM = 128
ROPE_DIM = 32
ROPE_THETA = 500000.0
DIL_PATTERNS = ((128, 1), (512, 4), (2048, 16))
DIL_HEADS = 8
WIN_Q_HEADS = 16
WIN_KV_HEADS = 4
WIN_HALF = 128
N_BRANCH = 3
EPS = 1e-6
NEG_INF = -1e30
ADAM_LR = 0.001
ADAM_B1 = 0.9
ADAM_B2 = 0.999
ADAM_EPS = 1e-08
ADAM_WD = 0.01
ADAM_STEP = 10

LANE = 128
QBLK = 128
VMEM_LIMIT = 56 * 1024 * 1024
PAD_TO = 512
MM_VMEM_BUDGET = 44 * 1024 * 1024
W_IN_PARTS = 4


def _cparams(sem=None):
    return pltpu.CompilerParams(dimension_semantics=sem, vmem_limit_bytes=VMEM_LIMIT)


PIECE_BYTES = 400 * 1024
US_PER_PIECE_BYTE = 8.8e-5
MAX_PIECES = 8
CARRIER_US = {
    "proj_in": 450, "proj_in_dx": 560, "proj_in_dw": 140, "scan_fwd0": 200, "scan_fwd1": 200, "scan_bwd0": 480,
    "scan_bwd1": 480, "win_fwd": 150, "win_bwd": 390, "dil_fwd0": 85, "dil_fwd1": 50, "dil_fwd2": 65,
    "dil_bwd0": 165, "dil_bwd1": 155, "dil_bwd2": 110, "mlp_up": 155, "mlp_down": 170, "mlp_up_dx": 190,
    "mlp_up_dw": 190, "mlp_down_dx": 165, "mlp_down_dw": 188, "conv_bwd_x": 75, "rope_dil": 48,
    "adamw_w_up": 63, "adamw_w_down": 62, "proj_a": 42, "proj_c": 42, "proj_out": 42, "proj_a_dx": 42,
    "proj_c_dx": 42, "proj_out_dx": 42, "proj_a_dw": 42, "proj_c_dw": 42, "proj_out_dw": 42, "conv_fwd": 65,
    "merge": 65, "merge_bwd": 100, "ssd_post": 50, "ssd_post_bwd": 90, "rope_win": 90, "norm_mix_bwd": 50,
    "norm_mlp_bwd": 50, "dil_combine_bwd": 70,
}


class _Piece:
    def __init__(self, key, row0, rows, scatter, est):
        self.key, self.row0, self.rows, self.scatter, self.est = key, row0, rows, scatter, est


def _coalesce(pieces):
    out = []
    for p in pieces:
        q = out[-1] if out else None
        if q is not None and q.key == p.key and q.scatter == p.scatter and q.row0 + q.rows == p.row0:
            out[-1] = _Piece(q.key, q.row0, q.rows + p.rows, q.scatter, q.est + p.est)
        else:
            out.append(p)
    return out


class _Sched:
    def __init__(self):
        self.queue, self.src, self.dst = [], {}, {}

    def post(self, key, src, scatter):
        r, c = src.shape[-2:]
        self.src[key] = src
        self.dst[key] = lax.empty((N_DEV, r, c), src.dtype)
        row_bytes = c * src.dtype.itemsize
        pr = r
        while pr * row_bytes > PIECE_BYTES and pr % 32 == 0:
            pr //= 2
        for row0 in range(0, r, pr):
            self.queue.append(_Piece(key, row0, pr, scatter, pr * row_bytes * US_PER_PIECE_BYTE))

    def take(self, name):
        budget = CARRIER_US.get(name)
        out, used = [], 0.0
        while budget and self.queue and used + self.queue[0].est <= 1.1 * budget:
            used += self.queue[0].est
            out.append(self.queue.pop(0))
        return _coalesce(out)

    def get(self, key):
        last = max([i for i, p in enumerate(self.queue) if p.key == key], default=-1)
        if last >= 0:
            pieces, self.queue = _coalesce(self.queue[:last + 1]), self.queue[last + 1:]
            for i in range(0, len(pieces), MAX_PIECES):
                _exchange_pieces(self, pieces[i:i + MAX_PIECES], name="exchange_flush")
        return self.dst[key]


_SCHED = None


def _piece_copies(pieces, keys, src_refs, dst_refs, send_sems, recv_sems, loc_sems):
    x, y, c = lax.axis_index("x"), lax.axis_index("y"), lax.axis_index("c")
    me = 4 * x + 2 * y + c
    cps = []
    for t, p in enumerate(pieces):
        ki = keys.index(p.key)
        rows = pl.ds(p.row0, p.rows)
        for j in range(1, N_DEV):
            px = (1 - x) if (j >> 2) & 1 else x
            py = (1 - y) if (j >> 1) & 1 else y
            pc = (1 - c) if j & 1 else c
            src = src_refs[ki].at[4 * px + 2 * py + pc, rows] if p.scatter else src_refs[ki].at[rows]
            cps.append(pltpu.make_async_remote_copy(
                src_ref=src, dst_ref=dst_refs[ki].at[me, rows], send_sem=send_sems.at[t * 7 + j - 1],
                recv_sem=recv_sems.at[t * 7 + j - 1], device_id=(px, py, pc), device_id_type=MESH))
        src = src_refs[ki].at[me, rows] if p.scatter else src_refs[ki].at[rows]
        cps.append(pltpu.make_async_copy(src, dst_refs[ki].at[me, rows], loc_sems.at[t]))
    return cps


def _two_level_gather(pieces, keys, src_refs, dst_refs, send_sems, recv_sems, loc_sems):
    x, y, c = lax.axis_index("x"), lax.axis_index("y"), lax.axis_index("c")
    me = 4 * x + 2 * y + c
    sib = (x, y, 1 - c)
    chips = [(1 - x, y), (x, 1 - y), (1 - x, 1 - y)]

    def copy(t, k, src, dst, to):
        return pltpu.make_async_remote_copy(src_ref=src, dst_ref=dst, send_sem=send_sems.at[7 * t + k],
                                            recv_sem=recv_sems.at[7 * t + k], device_id=to, device_id_type=MESH)

    def landing(p, px, py, pc):
        return dst_refs[keys.index(p.key)].at[4 * px + 2 * py + pc, pl.ds(p.row0, p.rows)]

    sends, local = [], []
    for t, p in enumerate(pieces):
        src = src_refs[keys.index(p.key)].at[pl.ds(p.row0, p.rows)]
        mine = landing(p, x, y, c)
        sends.append(copy(t, 0, src, mine, sib))
        sends += [copy(t, 1 + j, src, mine, (px, py, c)) for j, (px, py) in enumerate(chips)]
        local.append(pltpu.make_async_copy(src, mine, loc_sems.at[t]))
    for cp in sends + local:
        cp.start()
    passed = []
    for t, p in enumerate(pieces):
        for j, (px, py) in enumerate(chips):
            blk = landing(p, px, py, c)
            copy(t, 1 + j, blk, blk, (x, y, c)).wait_recv()
            fwd = copy(t, 4 + j, blk, blk, sib)
            fwd.start()
            passed.append(fwd)
    for t, p in enumerate(pieces):
        blk = landing(p, x, y, 1 - c)
        copy(t, 0, blk, blk, (x, y, c)).wait_recv()
        for j, (px, py) in enumerate(chips):
            blk = landing(p, px, py, 1 - c)
            copy(t, 4 + j, blk, blk, (x, y, c)).wait_recv()
    for cp in sends + passed:
        cp.wait_send()
    for cp in local:
        cp.wait()


def _carry_call(sched, pieces, body, *, name, grid, in_specs, out_specs, out_shape, scratch_shapes, ins, aliases=None):
    keys = []
    for p in pieces:
        if p.key not in keys:
            keys.append(p.key)
    n_in, n_out, nk, npc = len(ins), len(out_shape), len(keys), len(pieces)
    n_scr = len(scratch_shapes)

    def wrapped(*refs):
        in_refs = refs[:n_in]
        src_refs = refs[n_in:n_in + nk]
        out_refs = refs[n_in + 2 * nk:n_in + 2 * nk + n_out]
        dst_refs = refs[n_in + 2 * nk + n_out:n_in + 3 * nk + n_out]
        scr = refs[n_in + 3 * nk + n_out:]
        inner_scr, sems = scr[:n_scr], scr[n_scr:]
        if grid:
            pids = [pl.program_id(a) for a in range(len(grid))]
            first = functools.reduce(lambda u, v: u & v, [q == 0 for q in pids])
            last = functools.reduce(lambda u, v: u & v, [q == g - 1 for q, g in zip(pids, grid)])

            @pl.when(first)
            def _():
                for cp in _piece_copies(pieces, keys, src_refs, dst_refs, *sems):
                    cp.start()

            body(*in_refs, *out_refs, *inner_scr)

            @pl.when(last)
            def _():
                for cp in _piece_copies(pieces, keys, src_refs, dst_refs, *sems):
                    cp.wait()
        elif all(not p.scatter for p in pieces):
            _two_level_gather(pieces, keys, src_refs, dst_refs, *sems)
        else:
            cps = _piece_copies(pieces, keys, src_refs, dst_refs, *sems)
            for cp in cps:
                cp.start()
            for cp in cps:
                cp.wait()

    anyspec = pl.BlockSpec(memory_space=pl.ANY)
    dsts = [sched.dst[k] for k in keys]
    kwargs = dict(grid=grid) if grid else {}
    res = pl.pallas_call(
        wrapped, name=name, in_specs=list(in_specs) + [anyspec] * (2 * nk), out_specs=list(out_specs) + [anyspec] * nk,
        out_shape=list(out_shape) + [jax.ShapeDtypeStruct(d.shape, d.dtype) for d in dsts],
        input_output_aliases={**(aliases or {}), **{n_in + nk + i: n_out + i for i in range(nk)}},
        scratch_shapes=list(scratch_shapes) + [pltpu.SemaphoreType.DMA((7 * npc,)), pltpu.SemaphoreType.DMA((7 * npc,)),
                                               pltpu.SemaphoreType.DMA((npc,))],
        compiler_params=pltpu.CompilerParams(dimension_semantics=("arbitrary",) * len(grid) if grid else None,
                                             vmem_limit_bytes=VMEM_LIMIT, has_side_effects=True),
        **kwargs,
    )(*ins, *[sched.src[k] for k in keys], *dsts)
    for i, k in enumerate(keys):
        sched.dst[k] = res[n_out + i]
    return list(res[:n_out])


def _exchange_pieces(sched, pieces, *, name):
    _carry_call(sched, pieces, None, name=name, grid=(), in_specs=[], out_specs=[], out_shape=[], scratch_shapes=[],
                ins=[])


def _pcall(body, *, name, grid, in_specs, out_specs, out_shape, scratch_shapes=(), sem=None, into=None):
    single = not isinstance(out_shape, (list, tuple))
    out_shape_l = [out_shape] if single else list(out_shape)
    out_specs_l = [out_specs] if single else list(out_specs)
    into = into or {}

    def run(*ins):
        n0, nb = len(ins), len(into)
        specs = list(in_specs) + [pl.BlockSpec(memory_space=pl.ANY)] * nb
        aliases = {n0 + k: oi for k, oi in enumerate(into)}
        for oi, buf in into.items():
            out_shape_l[oi] = jax.ShapeDtypeStruct(buf.shape, buf.dtype)
        kbody = (lambda *refs: body(*refs[:n0], *refs[n0 + nb:])) if nb else body
        args = list(ins) + list(into.values())
        pieces = _SCHED.take(name) if _SCHED is not None else []
        if pieces:
            res = _carry_call(_SCHED, pieces, kbody, name=name, grid=grid, in_specs=specs, out_specs=out_specs_l,
                              out_shape=out_shape_l, scratch_shapes=list(scratch_shapes), ins=args, aliases=aliases)
        else:
            res = pl.pallas_call(kbody, name=name, grid=grid, in_specs=specs, out_specs=out_specs_l,
                                 out_shape=out_shape_l, scratch_shapes=list(scratch_shapes),
                                 input_output_aliases=aliases, compiler_params=_cparams(sem))(*args)
        return res[0] if single else list(res)

    return run


def _pick(dim, cands):
    for c in cands:
        if dim % c == 0:
            return c
    return dim


def _mm_tiles(m, n, k, a_bytes, b_bytes, o_bytes, has_add):
    tm = _pick(m, (1024, 512, 256, 128))
    tn = _pick(n, (1024, 1792, 512, 256, 128))
    for tk in (3584, 2048, 1792, 1024, 896, 512, 256, 128):
        if k % tk:
            continue
        need = 2 * (tm * tk * a_bytes + tk * tn * b_bytes + tm * tn * (o_bytes + (4 if has_add else 0)))
        need += tm * tn * 4 if k // tk > 1 else 0
        if need <= MM_VMEM_BUDGET:
            return tm, tn, tk
    return tm, tn, _pick(k, (128,))


def _mm(a, b, *, ta=False, tb=False, out_dtype=F32, add=None, post=None, name):
    m, k = (a.shape[1], a.shape[0]) if ta else a.shape
    k2, n = (b.shape[1], b.shape[0]) if tb else b.shape
    assert k == k2, (a.shape, b.shape, ta, tb)
    pfn, pins, pdts = post if post is not None else (None, [], [out_dtype])
    o_bytes = sum(jnp.dtype(dt).itemsize for dt in pdts) + sum(e.dtype.itemsize for e in pins)
    tm, tn, tk = _mm_tiles(m, n, k, a.dtype.itemsize, b.dtype.itemsize, o_bytes, add is not None)
    assert m % tm == 0 and n % tn == 0 and k % tk == 0, (m, n, k, tm, tn, tk)
    nk = k // tk
    a_spec = pl.BlockSpec((tk, tm), lambda i, j, kk: (kk, i)) if ta else pl.BlockSpec((tm, tk), lambda i, j, kk: (i, kk))
    b_spec = pl.BlockSpec((tn, tk), lambda i, j, kk: (j, kk)) if tb else pl.BlockSpec((tk, tn), lambda i, j, kk: (kk, j))
    o_spec = pl.BlockSpec((tm, tn), lambda i, j, kk: (i, j))
    dims = (((0 if ta else 1,), (1 if tb else 0,)), ((), ()))
    has_add = add is not None

    nx, no = (1 if has_add else 0) + len(pins), len(pdts)

    def body(*refs):
        a_ref, b_ref = refs[:2]
        x_refs = refs[2:2 + nx]
        o_refs = refs[2 + nx:2 + nx + no]
        part = lax.dot_general(a_ref[...].astype(BF16), b_ref[...].astype(BF16), dims, preferred_element_type=F32)

        def finish(r):
            if has_add:
                r = r + x_refs[0][...]
            outs = pfn(r, *[x[...] for x in x_refs[1 if has_add else 0:]]) if pfn else (r,)
            for o_ref, v in zip(o_refs, outs):
                o_ref[...] = v.astype(o_ref.dtype)

        if nk == 1:
            finish(part)
            return
        acc_ref = refs[-1]
        kk = pl.program_id(2)

        @pl.when(kk == 0)
        def _():
            acc_ref[...] = part

        @pl.when(kk > 0)
        def _():
            acc_ref[...] += part

        @pl.when(kk == nk - 1)
        def _():
            finish(acc_ref[...])

    ins = [a, b] + ([add] if has_add else []) + list(pins)
    specs = [a_spec, b_spec] + [o_spec] * nx
    res = _pcall(
        body, name=name, grid=(m // tm, n // tn, nk), in_specs=specs, out_specs=[o_spec] * no,
        out_shape=[jax.ShapeDtypeStruct((m, n), dt) for dt in pdts],
        scratch_shapes=[pltpu.VMEM((tm, tn), F32)] if nk > 1 else [],
        sem=("parallel", "parallel", "arbitrary"),
    )(*ins)
    return res if post is not None else res[0]


def _rowcall(fn, rows, pars, row_outs, par_outs, *, tile, name):
    t = rows[0][0].shape[0]
    tile = min(tile, t)
    assert t % tile == 0
    nr, npar, nro, npo = len(rows), len(pars), len(row_outs), len(par_outs)
    in_specs = [pl.BlockSpec((tile, c), functools.partial(lambda i, cb: (i, cb), cb=cb)) for (_, c, cb) in rows]
    in_specs += [pl.BlockSpec(p.shape, lambda i: (0, 0)) for p in pars]
    into = {k: ro[2] for k, ro in enumerate(row_outs) if len(ro) == 4}
    out_specs = [pl.BlockSpec((tile, ro[0]), functools.partial(lambda i, cb: (i, cb), cb=ro[3] if len(ro) == 4 else 0))
                 for ro in row_outs]
    out_specs += [pl.BlockSpec(s, lambda i: (0, 0)) for s in par_outs]
    out_shape = [jax.ShapeDtypeStruct((t, ro[0]), ro[1]) for ro in row_outs]
    out_shape += [jax.ShapeDtypeStruct(s, F32) for s in par_outs]

    def body(*refs):
        rv = [r[...] for r in refs[:nr]]
        pv = [r[...] for r in refs[nr:nr + npar]]
        ro_refs = refs[nr + npar:nr + npar + nro]
        po_refs = refs[nr + npar + nro:]
        ro, po = fn(rv, pv)
        for ref, v in zip(ro_refs, ro):
            ref[...] = v.astype(ref.dtype)
        if npo:
            @pl.when(pl.program_id(0) == 0)
            def _():
                for ref in po_refs:
                    ref[...] = jnp.zeros_like(ref)
            for ref, v in zip(po_refs, po):
                ref[...] += v

    res = _pcall(
        body, name=name, grid=(t // tile,), in_specs=in_specs, out_specs=out_specs, out_shape=out_shape,
        sem=("arbitrary",), into=into,
    )(*[r[0] for r in rows], *pars)
    return list(res[:nro]), list(res[nro:])


def _map2d(fn, ins, out_dtype, *, name, tile=256, cw=2048):
    t, w = ins[0].shape
    tile, cw = min(tile, t), min(cw, w)
    assert t % tile == 0 and w % cw == 0

    def body(*refs):
        refs[-1][...] = fn(*[r[...] for r in refs[:-1]]).astype(out_dtype)

    spec = pl.BlockSpec((tile, cw), lambda i, j: (i, j))
    return pl.pallas_call(
        body, name=name, grid=(t // tile, w // cw), in_specs=[spec] * len(ins), out_specs=spec,
        out_shape=jax.ShapeDtypeStruct((t, w), out_dtype), compiler_params=_cparams(("parallel", "parallel")),
    )(*ins)


def _exchange(srcs, *, scatter, name):
    n = len(srcs)
    out_shape = [jax.ShapeDtypeStruct(s.shape if scatter else (N_DEV,) + s.shape, s.dtype) for s in srcs]

    def body(*refs):
        src_refs, out_refs = refs[:n], refs[n:2 * n]
        send_sems, recv_sems, loc_sems = refs[2 * n:]
        x, y, c = lax.axis_index("x"), lax.axis_index("y"), lax.axis_index("c")
        me = 4 * x + 2 * y + c
        copies = []
        for a in range(n):
            for j in range(1, N_DEV):
                px = (1 - x) if (j >> 2) & 1 else x
                py = (1 - y) if (j >> 1) & 1 else y
                pc = (1 - c) if j & 1 else c
                src = src_refs[a].at[4 * px + 2 * py + pc] if scatter else src_refs[a]
                cp = pltpu.make_async_remote_copy(
                    src_ref=src, dst_ref=out_refs[a].at[me], send_sem=send_sems.at[a * 7 + j - 1],
                    recv_sem=recv_sems.at[a * 7 + j - 1], device_id=(px, py, pc), device_id_type=MESH)
                cp.start()
                copies.append(cp)
            src = src_refs[a].at[me] if scatter else src_refs[a]
            cp = pltpu.make_async_copy(src, out_refs[a].at[me], loc_sems.at[a])
            cp.start()
            copies.append(cp)
        for cp in copies:
            cp.wait()

    anyspec = pl.BlockSpec(memory_space=pl.ANY)
    return pl.pallas_call(
        body, name=name, in_specs=[anyspec] * n, out_specs=[anyspec] * n, out_shape=out_shape,
        scratch_shapes=[pltpu.SemaphoreType.DMA((7 * n,)), pltpu.SemaphoreType.DMA((7 * n,)),
                        pltpu.SemaphoreType.DMA((n,))],
        compiler_params=pltpu.CompilerParams(has_side_effects=True),
    )(*srcs)


def _row_tile(r, c, budget_elems=256 * 1024):
    tr = r
    while tr * c > budget_elems and tr % 16 == 0:
        tr //= 2
    return tr


def _adamw(w, recvs, m, v, *, name):
    r, c = w.shape
    nl = len(recvs)
    ns, rl = recvs[0].shape[:2]
    assert rl * nl == r
    tr = _row_tile(rl, c, budget_elems=(1024 * 1024) // max(nl, 4))
    nt = rl // tr
    bc1 = 1.0 / (1.0 - ADAM_B1 ** ADAM_STEP)
    bc2 = 1.0 / (1.0 - ADAM_B2 ** ADAM_STEP)

    def body(*refs):
        w_ref, m_ref, v_ref = refs[:3]
        r_refs = refs[3:3 + nl]
        g_ref, d_ref, mo_ref, vo_ref = refs[3 + nl:]
        i = pl.program_id(0)
        for k in range(nl):
            @pl.when(i // nt == k)
            def _(k=k):
                g = r_refs[k][0].astype(F32)
                for s in range(1, ns):
                    g = g + r_refs[k][s].astype(F32)
                g_ref[...] = g
        g = g_ref[...]
        mn = ADAM_B1 * m_ref[...] + (1.0 - ADAM_B1) * g
        vn = ADAM_B2 * v_ref[...] + (1.0 - ADAM_B2) * (g * g)
        mo_ref[...] = mn
        vo_ref[...] = vn
        d_ref[...] = -ADAM_LR * ((mn * bc1) / (jnp.sqrt(vn * bc2) + ADAM_EPS) + ADAM_WD * w_ref[...])

    spec = pl.BlockSpec((tr, c), lambda i: (i, 0))
    rspecs = [pl.BlockSpec((ns, tr, c), functools.partial(lambda i, k: (0, jnp.clip(i - k * nt, 0, nt - 1), 0), k=k))
              for k in range(nl)]
    return _pcall(
        body, name=name, grid=(r // tr,), in_specs=[spec, spec, spec] + rspecs,
        out_specs=[spec] * 4, out_shape=[jax.ShapeDtypeStruct((r, c), F32)] * 4, sem=("arbitrary",),
    )(w, m, v, *recvs)


def _sum_slots(recv, *, name):
    ns, r, c = recv.shape

    def body(r_ref, o_ref):
        g = r_ref[0]
        for s in range(1, ns):
            g = g + r_ref[s]
        o_ref[...] = g

    return pl.pallas_call(body, name=name, out_shape=jax.ShapeDtypeStruct((r, c), F32))(recv)


def _rms(x, g):
    return x * lax.rsqrt(jnp.mean(x * x, axis=-1, keepdims=True) + EPS) * g


def _silu(x):
    return x * jax.nn.sigmoid(x)


def _merge_f(a, b, c, g0, g1, g2):
    return jax.nn.sigmoid(g0) * a + jax.nn.sigmoid(g1) * b + jax.nn.sigmoid(g2) * c


def _ssd_post_f(yf, yb, xs, z, dskip, gnorm):
    y = (yf + yb + dskip * xs) * _silu(z)
    return _rms(y, gnorm)


def _combine_f(o0, o1, o2, l0, l1, l2):
    m = jnp.maximum(jnp.maximum(l0, l1), l2)
    e0, e1, e2 = jnp.exp(l0 - m), jnp.exp(l1 - m), jnp.exp(l2 - m)
    return (e0 * o0 + e1 * o1 + e2 * o2) / (e0 + e1 + e2)


def _key_window(ln, hw):
    return min(ln, QBLK + 2 * hw)


def _key_start(i, ln, hw):
    return pl.multiple_of(jnp.clip(i * QBLK - hw, 0, ln - _key_window(ln, hw)), 64)


def _band_mask(qs, ks, kwin, hw):
    qpos = qs + lax.broadcasted_iota(jnp.int32, (QBLK, kwin), 0)
    kpos = ks + lax.broadcasted_iota(jnp.int32, (QBLK, kwin), 1)
    return jnp.abs(qpos - kpos) <= hw


def _attn_block(q, k3, v3, sk, valid, *, has_sink):
    s = lax.dot_general(q.astype(BF16), k3.astype(BF16), (((1,), (1,)), ((), ())),
                        preferred_element_type=F32) * (HEAD_DIM ** -0.5)
    s = jnp.where(valid, s, NEG_INF)
    m = jnp.max(s, axis=-1, keepdims=True)
    if has_sink:
        m = jnp.maximum(m, sk)
    m = lax.stop_gradient(m)
    e = jnp.exp(s - m)
    l = jnp.sum(e, axis=-1, keepdims=True)
    if has_sink:
        l = l + jnp.exp(sk - m)
    o = jnp.dot(e.astype(BF16), v3.astype(BF16), preferred_element_type=F32) / l
    return o, m + jnp.log(l)


def _ssd_chunk(state, xs, bm, cm, dtr, dtr_t, bias, bias_t, alog, alog_t, *, reverse):
    t = xs.shape[0]
    hg = dtr.shape[1]
    hp = xs.shape[1]
    p = hp // hg
    dt = jax.nn.softplus(dtr + bias)
    dt_t = jax.nn.softplus(dtr_t + bias_t)
    dta = dt * (-jnp.exp(alog))
    dta_t = dt_t * (-jnp.exp(alog_t))
    li = lax.broadcasted_iota(jnp.int32, (t, t), 0)
    si = lax.broadcasted_iota(jnp.int32, (t, t), 1)
    tri = (li <= si) if reverse else (li >= si)
    trif = tri.astype(F32)
    cs = jnp.dot(trif, dta, precision=HI, preferred_element_type=F32)
    cs_t = lax.dot_general(dta_t, trif, (((1,), (1,)), ((), ())), precision=HI,
                           preferred_element_type=F32)
    total = jnp.sum(dta, axis=0, keepdims=True)
    cb = lax.dot_general(cm.astype(BF16), bm.astype(BF16), (((1,), (1,)), ((), ())),
                         preferred_element_type=F32)
    lane_h = lax.broadcasted_iota(jnp.int32, (1, hp), 1) // p
    col_h = lax.broadcasted_iota(jnp.int32, (1, hg), 1)
    row_h = lax.broadcasted_iota(jnp.int32, (hg, 1), 0)
    dt_x = jnp.zeros((t, hp), F32)
    ecs_x = jnp.zeros((t, hp), F32)
    ds_x = jnp.zeros((t, hp), F32)
    etot_x = jnp.zeros((1, hp), F32)
    decays, masks = [], []
    for h in range(hg):
        oh = (col_h == h).astype(F32)
        oh_t = (row_h == h).astype(F32)
        mk = (lane_h == h).astype(F32)
        dt_h = jnp.sum(dt * oh, axis=1, keepdims=True)
        cs_h = jnp.sum(cs * oh, axis=1, keepdims=True)
        cst_h = jnp.sum(cs_t * oh_t, axis=0, keepdims=True)
        tot_h = jnp.sum(total * oh, axis=1, keepdims=True)
        dt_x = dt_x + dt_h * mk
        ecs_x = ecs_x + jnp.exp(cs_h) * mk
        ds_x = ds_x + jnp.exp(tot_h - cs_h) * mk
        etot_x = etot_x + jnp.exp(tot_h) * mk
        decays.append(jnp.exp(jnp.where(tri, cs_h - cst_h, -jnp.inf)))
        masks.append(mk)
    xdt = xs * dt_x
    y = jnp.dot(cm.astype(BF16), state.astype(BF16), preferred_element_type=F32) * ecs_x
    for h in range(hg):
        y = y + jnp.dot((cb * decays[h]).astype(BF16), (xdt * masks[h]).astype(BF16),
                        preferred_element_type=F32)
    st_new = lax.dot_general(bm.astype(BF16), (xdt * ds_x).astype(BF16), (((0,), (0,)), ((), ())),
                             preferred_element_type=F32)
    return y, state * etot_x + st_new


def _rope_tables(seq):
    half = ROPE_DIM // 2
    inv = ROPE_THETA ** (-jnp.arange(0, ROPE_DIM, 2, dtype=F32) / ROPE_DIM)
    ang = jnp.arange(seq, dtype=F32)[:, None] * inv[None, :]
    cos, sin = jnp.cos(ang), jnp.sin(ang)
    rest = HEAD_DIM - ROPE_DIM
    c = jnp.concatenate([cos, cos, jnp.ones((seq, rest), F32)], axis=1)
    a = jnp.concatenate([-sin, jnp.zeros((seq, HEAD_DIM - half), F32)], axis=1)
    b = jnp.concatenate([jnp.zeros((seq, half), F32), sin, jnp.zeros((seq, rest), F32)], axis=1)
    return c, a, b


def _rope(src, tabs, *, col0, width, seq, group, inverse, out_dtype, name, into=None, out_col0=0):
    t = src.shape[0]
    half = ROPE_DIM // 2
    nhb = 6 if all(v % (6 * HEAD_DIM) == 0 for v in (width, col0, out_col0)) else 3
    cw, tile = nhb * HEAD_DIM, 512
    assert width % cw == 0 and col0 % cw == 0 and out_col0 % cw == 0 and seq % tile == 0 and t % tile == 0
    ns = seq // tile

    def body(x_ref, c_ref, a_ref, b_ref, o_ref):
        jb = pl.program_id(1)
        c, a, b = c_ref[...], a_ref[...], b_ref[...]
        for hh in range(nhb):
            xv = x_ref[:, hh * HEAD_DIM:(hh + 1) * HEAD_DIM].astype(F32)
            if inverse:
                yv = xv * c + pltpu.roll(xv * a, half, 1) + pltpu.roll(xv * b, HEAD_DIM - half, 1)
            else:
                yv = xv * c + pltpu.roll(xv, HEAD_DIM - half, 1) * a + pltpu.roll(xv, half, 1) * b
            if group:
                keep = ((jb * nhb + hh) % group) == (group - 1)
                yv = jnp.where(keep, xv, yv)
            o_ref[:, hh * HEAD_DIM:(hh + 1) * HEAD_DIM] = yv.astype(o_ref.dtype)

    tspec = pl.BlockSpec((tile, HEAD_DIM), lambda i, j: (i % ns, 0))
    return _pcall(
        body, name=name, grid=(t // tile, width // cw),
        in_specs=[pl.BlockSpec((tile, cw), lambda i, j: (i, col0 // cw + j)), tspec, tspec, tspec],
        out_specs=pl.BlockSpec((tile, cw), lambda i, j: (i, out_col0 // cw + j)),
        out_shape=jax.ShapeDtypeStruct((t, width), out_dtype),
        sem=("parallel", "parallel"), into=None if into is None else {0: into},
    )(src, *tabs)


def _shift_rows(x, d, tpos):
    if d == 0:
        return x
    s = x.shape[0]
    y = pltpu.roll(x, (-d) % s, 0)
    ok = (tpos + d >= 0) & (tpos + d < s)
    return jnp.where(ok, y, 0.0)


def _conv_fwd(p, w8, bias, *, col0, chans, batch, seq, name):
    cb = 256
    assert chans % cb == 0 and col0 % cb == 0
    pad = (CONV_WIDTH - 1) // 2

    def body(x_ref, w_ref, b_ref, o_ref):
        x = x_ref[...]
        tpos = lax.broadcasted_iota(jnp.int32, x.shape, 0)
        acc = jnp.broadcast_to(b_ref[...], x.shape)
        for k in range(CONV_WIDTH):
            acc = acc + w_ref[k:k + 1, :] * _shift_rows(x, k - pad, tpos)
        o_ref[...] = _silu(acc)

    return _pcall(
        body, name=name, grid=(chans // cb, batch),
        in_specs=[pl.BlockSpec((seq, cb), lambda j, b: (b, col0 // cb + j)),
                  pl.BlockSpec((8, cb), lambda j, b: (0, j)), pl.BlockSpec((1, cb), lambda j, b: (0, j))],
        out_specs=pl.BlockSpec((seq, cb), lambda j, b: (b, j)),
        out_shape=jax.ShapeDtypeStruct((batch * seq, chans), F32),
        sem=("parallel", "arbitrary"),
    )(p, w8, bias)


def _conv_bwd(p, w8, bias, dus, *, col0, ch0, chans, batch, seq, into, out_col0, name):
    cb = 256 if chans % 256 == 0 and ch0 % 256 == 0 else 128
    assert chans % cb == 0 and ch0 % cb == 0 and col0 % cb == 0 and out_col0 % cb == 0
    pad = (CONV_WIDTH - 1) // 2
    ndu = len(dus)

    def body(*refs):
        x_ref, w_ref, b_ref = refs[:3]
        du_refs = refs[3:3 + ndu]
        dx_ref, dw_ref, db_ref = refs[3 + ndu:]
        du = du_refs[0][...]
        for r in du_refs[1:]:
            du = du + r[...]
        _conv_bwd_block(x_ref, w_ref, b_ref, du, dx_ref, dw_ref, db_ref)

    c0 = (col0 + ch0) // cb
    return _pcall(
        body, name=name, grid=(chans // cb, batch),
        in_specs=[pl.BlockSpec((seq, cb), lambda j, b: (b, c0 + j)),
                  pl.BlockSpec((8, cb), lambda j, b: (0, ch0 // cb + j)),
                  pl.BlockSpec((1, cb), lambda j, b: (0, ch0 // cb + j))]
        + [pl.BlockSpec((seq, cb), lambda j, b: (b, j))] * ndu,
        out_specs=[pl.BlockSpec((seq, cb), lambda j, b: (b, (out_col0 + ch0) // cb + j)),
                   pl.BlockSpec((8, cb), lambda j, b: (0, j)), pl.BlockSpec((1, cb), lambda j, b: (0, j))],
        out_shape=[jax.ShapeDtypeStruct(into.shape, into.dtype), jax.ShapeDtypeStruct((8, chans), F32),
                   jax.ShapeDtypeStruct((1, chans), F32)],
        sem=("parallel", "arbitrary"), into={0: into},
    )(p, w8, bias, *dus)


def _conv_bwd_block(x_ref, w_ref, b_ref, du, dx_ref, dw_ref, db_ref):
    pad = (CONV_WIDTH - 1) // 2
    x = x_ref[...]
    tpos = lax.broadcasted_iota(jnp.int32, x.shape, 0)
    acc = jnp.broadcast_to(b_ref[...], x.shape)
    xs = []
    for k in range(CONV_WIDTH):
        xs.append(_shift_rows(x, k - pad, tpos))
        acc = acc + w_ref[k:k + 1, :] * xs[k]
    sg = jax.nn.sigmoid(acc)
    dacc = du * (sg * (1.0 + acc * (1.0 - sg)))
    dx = jnp.zeros_like(x)
    for k in range(CONV_WIDTH):
        dx = dx + w_ref[k:k + 1, :] * _shift_rows(dacc, pad - k, tpos)
    dx_ref[...] = dx.astype(dx_ref.dtype)

    @pl.when(pl.program_id(1) == 0)
    def _():
        dw_ref[...] = jnp.zeros_like(dw_ref)
        db_ref[...] = jnp.zeros_like(db_ref)

    for k in range(CONV_WIDTH):
        dw_ref[k:k + 1, :] += jnp.sum(dacc * xs[k], axis=0, keepdims=True)
    db_ref[...] += jnp.sum(dacc, axis=0, keepdims=True)


def _scan_gpb(groups):
    return 4 if groups % 4 == 0 else 2 if groups % 2 == 0 else 1


def _scan_specs(batch, nc, groups, hg, inner, reverse_order):
    gpb = _scan_gpb(groups)
    t, n, hp = SSD_CHUNK, SSD_STATE, hg * SSD_HEAD_DIM
    ncb = inner // (gpb * n)
    ngb = groups // gpb

    def row(b, c):
        return b * nc + ((nc - 1 - c) if reverse_order else c)

    return dict(
        xs=pl.BlockSpec((t, gpb * hp), lambda g, b, c: (row(b, c), g)),
        bm=pl.BlockSpec((t, gpb * n), lambda g, b, c: (row(b, c), ncb + g)),
        cm=pl.BlockSpec((t, gpb * n), lambda g, b, c: (row(b, c), ncb + ngb + g)),
        dtr=pl.BlockSpec((gpb, t, hg), lambda g, b, c: (g, row(b, c), 0)),
        dtr_t=pl.BlockSpec((gpb, hg, t), lambda g, b, c: (g, 0, row(b, c))),
        par=pl.BlockSpec((gpb, 1, hg), lambda g, b, c: (g, 0, 0)),
        par_t=pl.BlockSpec((gpb, hg, 1), lambda g, b, c: (g, 0, 0)),
        y=pl.BlockSpec((t, gpb * hp), lambda g, b, c: (row(b, c), g)),
        nrow=pl.BlockSpec((t, gpb * n), lambda g, b, c: (row(b, c), g)),
        st=pl.BlockSpec((gpb, None, n, hp), lambda g, b, c: (g, row(b, c), 0, 0)),
    )


def _scan_fwd(u, dtr, dtr_t, bias, bias_t, alog, alog_t, *, batch, seq, inner, reverse, name):
    groups, hg = dtr.shape[0], dtr.shape[2]
    nc = seq // SSD_CHUNK
    hp = hg * SSD_HEAD_DIM
    sp = _scan_specs(batch, nc, groups, hg, inner, reverse)
    gpb, n = _scan_gpb(groups), SSD_STATE

    def body(xs_ref, bm_ref, cm_ref, dtr_ref, dtrt_ref, b_ref, bt_ref, a_ref, at_ref, y_ref, st_ref, state):
        @pl.when(pl.program_id(2) == 0)
        def _():
            state[...] = jnp.zeros_like(state)

        for k in range(gpb):
            xc, nc_ = slice(k * hp, (k + 1) * hp), slice(k * n, (k + 1) * n)
            st_in = state[k]
            st_ref[k] = st_in
            y, st_out = _ssd_chunk(st_in, xs_ref[:, xc], bm_ref[:, nc_], cm_ref[:, nc_], dtr_ref[k], dtrt_ref[k],
                                   b_ref[k], bt_ref[k], a_ref[k], at_ref[k], reverse=reverse)
            y_ref[:, xc] = y
            state[k] = st_out

    return _pcall(
        body, name=name, grid=(groups // gpb, batch, nc),
        in_specs=[sp["xs"], sp["bm"], sp["cm"], sp["dtr"], sp["dtr_t"], sp["par"], sp["par_t"], sp["par"], sp["par_t"]],
        out_specs=[sp["y"], sp["st"]],
        out_shape=[jax.ShapeDtypeStruct((batch * seq, inner), F32),
                   jax.ShapeDtypeStruct((groups, batch * nc, SSD_STATE, hp), F32)],
        scratch_shapes=[pltpu.VMEM((gpb, SSD_STATE, hp), F32)],
        sem=("parallel", "arbitrary", "arbitrary"),
    )(u, u, u, dtr, dtr_t, bias, bias_t, alog, alog_t)


def _scan_bwd(u, dtr, dtr_t, bias, bias_t, alog, alog_t, st, dy, *, batch, seq, inner, reverse, name):
    groups, hg = dtr.shape[0], dtr.shape[2]
    nc = seq // SSD_CHUNK
    hp = hg * SSD_HEAD_DIM
    t = batch * seq
    sp = _scan_specs(batch, nc, groups, hg, inner, not reverse)
    f = functools.partial(_ssd_chunk, reverse=reverse)
    gpb, n = _scan_gpb(groups), SSD_STATE

    def body(xs_ref, bm_ref, cm_ref, dtr_ref, dtrt_ref, b_ref, bt_ref, a_ref, at_ref, st_ref, dy_ref,
             dxs_ref, dbm_ref, dcm_ref, ddtr_ref, ddtrt_ref, db_ref, dbt_ref, da_ref, dat_ref, dstate):
        first = (pl.program_id(1) == 0) & (pl.program_id(2) == 0)

        @pl.when(pl.program_id(2) == 0)
        def _():
            dstate[...] = jnp.zeros_like(dstate)

        @pl.when(first)
        def _():
            for r in (db_ref, dbt_ref, da_ref, dat_ref):
                r[...] = jnp.zeros_like(r)

        for k in range(gpb):
            xc, nc_ = slice(k * hp, (k + 1) * hp), slice(k * n, (k + 1) * n)
            _, vjp = jax.vjp(f, st_ref[k], xs_ref[:, xc], bm_ref[:, nc_], cm_ref[:, nc_], dtr_ref[k], dtrt_ref[k],
                             b_ref[k], bt_ref[k], a_ref[k], at_ref[k])
            dst, dxs, dbm, dcm, ddtr, ddtrt, db, dbt, da, dat = vjp((dy_ref[:, xc], dstate[k]))
            dstate[k] = dst
            dxs_ref[:, xc] = dxs
            dbm_ref[:, nc_] = dbm
            dcm_ref[:, nc_] = dcm
            ddtr_ref[k] = ddtr
            ddtrt_ref[k] = ddtrt
            db_ref[k] += db
            dbt_ref[k] += dbt
            da_ref[k] += da
            dat_ref[k] += dat

    gn = groups * SSD_STATE
    return _pcall(
        body, name=name, grid=(groups // gpb, batch, nc),
        in_specs=[sp["xs"], sp["bm"], sp["cm"], sp["dtr"], sp["dtr_t"], sp["par"], sp["par_t"], sp["par"], sp["par_t"],
                  sp["st"], sp["y"]],
        out_specs=[sp["y"], sp["nrow"], sp["nrow"], sp["dtr"], sp["dtr_t"], sp["par"], sp["par_t"], sp["par"], sp["par_t"]],
        out_shape=[jax.ShapeDtypeStruct((t, inner), F32), jax.ShapeDtypeStruct((t, gn), F32),
                   jax.ShapeDtypeStruct((t, gn), F32), jax.ShapeDtypeStruct(dtr.shape, F32),
                   jax.ShapeDtypeStruct(dtr_t.shape, F32), jax.ShapeDtypeStruct(bias.shape, F32),
                   jax.ShapeDtypeStruct(bias_t.shape, F32), jax.ShapeDtypeStruct(alog.shape, F32),
                   jax.ShapeDtypeStruct(alog_t.shape, F32)],
        scratch_shapes=[pltpu.VMEM((gpb, SSD_STATE, hp), F32)],
        sem=("parallel", "arbitrary", "arbitrary"),
    )(u, u, u, dtr, dtr_t, bias, bias_t, alog, alog_t, st, dy)


def _heads_per_step(nh, nb, nbw, cb0):
    hps = min(nh, max(1, 8 // nb))
    while nh % hps or nbw % hps or cb0 % hps:
        hps -= 1
    return hps


def _attn_load(ref, col, blk):
    return ref[pl.ds(pl.multiple_of(blk * QBLK, QBLK), QBLK), col * HEAD_DIM:(col + 1) * HEAD_DIM].astype(F32)


def _lane0(row):
    lane = lax.broadcasted_iota(jnp.int32, row.shape, 1)
    return jnp.sum(jnp.where(lane == 0, row, 0.0), axis=1, keepdims=True)


def _attn_fwd(rq, sinkx, *, batch, seq, dil, nbw, cb0, nh, rep, hw, want_lse, out_dtype, name):
    t, w = rq.shape
    ln = seq // dil
    nb = ln // QBLK
    bw = (rep + 2) * HEAD_DIM
    ow = nh * rep * HEAD_DIM
    has_sink = sinkx is not None
    rq3 = rq.reshape(batch, ln, dil * w)
    kwin = _key_window(ln, hw)
    hps = _heads_per_step(nh, nb, nbw, cb0)
    f = functools.partial(_attn_block, has_sink=has_sink)

    def body(*refs):
        if has_sink:
            blk_ref, sink_ref = refs[:2]
            outs = refs[2:]
        else:
            blk_ref, sink_ref = refs[0], None
            outs = refs[1:]
        o_ref = outs[0]
        lse_ref = outs[1] if want_lse else None
        g = pl.program_id(2)

        def qblock(i, carry):
            ks = _key_start(i, ln, hw)
            valid = _band_mask(i * QBLK, ks, kwin, hw)
            rows = pl.ds(pl.multiple_of(i * QBLK, QBLK), QBLK)
            for hh in range(hps):
                hb = hh * (rep + 2)
                k3 = blk_ref[pl.ds(ks, kwin), (hb + rep) * HEAD_DIM:(hb + rep + 1) * HEAD_DIM].astype(F32)
                v3 = blk_ref[pl.ds(ks, kwin), (hb + rep + 1) * HEAD_DIM:(hb + rep + 2) * HEAD_DIM].astype(F32)
                for r in range(rep):
                    sk = _lane0(sink_ref[pl.ds((g * hps + hh) * rep + r, 1), :]) if has_sink else None
                    o, lse = f(_attn_load(blk_ref, hb + r, i), k3, v3, sk, valid)
                    oc = slice((hh * rep + r) * HEAD_DIM, (hh * rep + r + 1) * HEAD_DIM)
                    o_ref[rows, oc] = o.astype(o_ref.dtype)
                    if want_lse:
                        lse_ref[rows, oc] = jnp.broadcast_to(lse, o.shape)
            return carry

        if nb == 1:
            qblock(0, 0)
        else:
            lax.fori_loop(0, nb, qblock, 0, unroll=2 if nb % 2 == 0 else 1)

    in_specs = [pl.BlockSpec((None, ln, hps * bw), lambda b, r, h: (b, 0, (r * nbw + cb0) // hps + h))]
    ins = [rq3]
    if has_sink:
        in_specs.append(pl.BlockSpec(sinkx.shape, lambda b, r, h: (0, 0)))
        ins.append(sinkx)
    ospec = pl.BlockSpec((None, ln, hps * rep * HEAD_DIM), lambda b, r, h: (b, 0, r * (nh // hps) + h))
    out_shape = [jax.ShapeDtypeStruct((batch, ln, dil * ow), out_dtype)]
    out_specs = [ospec]
    if want_lse:
        out_shape.append(jax.ShapeDtypeStruct((batch, ln, dil * ow), F32))
        out_specs.append(ospec)
    res = _pcall(
        body, name=name, grid=(batch, dil, nh // hps), in_specs=in_specs, out_specs=out_specs, out_shape=out_shape,
        sem=("parallel", "parallel", "parallel"),
    )(*ins)
    return [r.reshape(t, ow) for r in res]


def _attn_bwd(rq, sinkx, do, dlse, *, batch, seq, dil, nbw, cb0, nh, rep, hw, name):
    t, w = rq.shape
    ln = seq // dil
    nb = ln // QBLK
    bw = (rep + 2) * HEAD_DIM
    ow = nh * rep * HEAD_DIM
    has_sink = sinkx is not None
    has_lse = dlse is not None
    kwin = _key_window(ln, hw)
    hps = _heads_per_step(nh, nb, nbw, cb0)
    f = functools.partial(_attn_block, has_sink=has_sink)

    def body(*refs):
        refs = list(refs)
        blk_ref = refs.pop(0)
        sink_ref = refs.pop(0) if has_sink else None
        do_ref = refs.pop(0)
        dlse_ref = refs.pop(0) if has_lse else None
        d_ref = refs.pop(0)
        dsink_ref = refs.pop(0) if has_sink else None
        g = pl.program_id(2)
        for hh in range(hps):
            kv0 = (hh * (rep + 2) + rep) * HEAD_DIM
            d_ref[:, kv0:kv0 + 2 * HEAD_DIM] = jnp.zeros((ln, 2 * HEAD_DIM), F32)
        if has_sink:
            @pl.when((pl.program_id(0) == 0) & (pl.program_id(1) == 0) & (g == 0))
            def _():
                dsink_ref[...] = jnp.zeros_like(dsink_ref)

        def qblock(i, carry):
            ks = _key_start(i, ln, hw)
            valid = _band_mask(i * QBLK, ks, kwin, hw)
            krows = pl.ds(ks, kwin)
            rows = pl.ds(pl.multiple_of(i * QBLK, QBLK), QBLK)
            for hh in range(hps):
                hb = hh * (rep + 2)
                kc = slice((hb + rep) * HEAD_DIM, (hb + rep + 1) * HEAD_DIM)
                vc = slice((hb + rep + 1) * HEAD_DIM, (hb + rep + 2) * HEAD_DIM)
                k3 = blk_ref[krows, kc].astype(F32)
                v3 = blk_ref[krows, vc].astype(F32)
                dk3 = jnp.zeros_like(k3)
                dv3 = jnp.zeros_like(v3)
                for r in range(rep):
                    oc = slice((hh * rep + r) * HEAD_DIM, (hh * rep + r + 1) * HEAD_DIM)
                    q = _attn_load(blk_ref, hb + r, i)
                    dov = do_ref[rows, oc]
                    dl = dlse_ref[rows, oc] if has_lse else jnp.zeros_like(dov)
                    if has_sink:
                        srow_i = (g * hps + hh) * rep + r
                        srow = sink_ref[pl.ds(srow_i, 1), :]
                        _, vjp = jax.vjp(lambda q_, k_, v_, s_: f(q_, k_, v_, _lane0(s_), valid), q, k3, v3, srow)
                        dq, dk, dv, ds = vjp((dov, jnp.sum(dl, axis=1, keepdims=True)))
                        dsink_ref[pl.ds(srow_i, 1), :] += ds
                    else:
                        _, vjp = jax.vjp(lambda q_, k_, v_: f(q_, k_, v_, None, valid), q, k3, v3)
                        dq, dk, dv = vjp((dov, jnp.sum(dl, axis=1, keepdims=True)))
                    d_ref[rows, (hb + r) * HEAD_DIM:(hb + r + 1) * HEAD_DIM] = dq
                    dk3 = dk3 + dk
                    dv3 = dv3 + dv
                d_ref[krows, kc] += dk3
                d_ref[krows, vc] += dv3
            return carry

        if nb == 1:
            qblock(0, 0)
        else:
            lax.fori_loop(0, nb, qblock, 0, unroll=2 if nb % 2 == 0 else 1)

    ospec = pl.BlockSpec((None, ln, hps * rep * HEAD_DIM), lambda b, r, h: (b, 0, r * (nh // hps) + h))
    in_specs = [pl.BlockSpec((None, ln, hps * bw), lambda b, r, h: (b, 0, (r * nbw + cb0) // hps + h))]
    ins = [rq.reshape(batch, ln, dil * w)]
    if has_sink:
        in_specs.append(pl.BlockSpec(sinkx.shape, lambda b, r, h: (0, 0)))
        ins.append(sinkx)
    in_specs.append(ospec)
    ins.append(do.reshape(batch, ln, dil * ow))
    if has_lse:
        in_specs.append(ospec)
        ins.append(dlse.reshape(batch, ln, dil * ow))
    dw = nh * bw
    out_specs = [pl.BlockSpec((None, ln, hps * bw), lambda b, r, h: (b, 0, r * (nh // hps) + h))]
    out_shape = [jax.ShapeDtypeStruct((batch, ln, dil * dw), F32)]
    if has_sink:
        out_specs.append(pl.BlockSpec(sinkx.shape, lambda b, r, h: (0, 0)))
        out_shape.append(jax.ShapeDtypeStruct(sinkx.shape, F32))
    res = _pcall(
        body, name=name, grid=(batch, dil, nh // hps), in_specs=in_specs, out_specs=out_specs, out_shape=out_shape,
        sem=("arbitrary", "arbitrary", "arbitrary"),
    )(*ins)
    return [res[0].reshape(t, dw)] + list(res[1:])


def _final_loss(x, g, target, *, name):
    d = x.shape[1]

    def fn(rv, pv):
        xv, tg = rv
        y, vjp = jax.vjp(_rms, xv, pv[0])
        err = y - tg
        dx, dg = vjp(err * (1.0 / d))
        loss = 0.5 * jnp.sum(err * err) * (1.0 / d)
        return [dx], [dg, jnp.full((1, LANE), loss, F32)]

    (dx,), (dg, loss) = _rowcall(fn, [(x, d, 0), (target, d, 0)], [g], [(d, F32)], [(1, d), (1, LANE)],
                                 tile=256, name=name)
    return dx, dg, loss


class _Layout:
    def __init__(self, d_model):
        self.d = d_model
        self.inner = SSD_HEADS * SSD_HEAD_DIM
        self.gn = SSD_GROUPS * SSD_STATE
        self.xbc = self.inner + 2 * self.gn
        self.ndt = 2 * SSD_HEADS
        self.ngrp = len(DIL_PATTERNS)
        self.dilw = DIL_HEADS * HEAD_DIM
        self.rqd = 3 * self.ngrp * self.dilw
        self.rep = WIN_Q_HEADS // WIN_KV_HEADS
        self.rqw = WIN_KV_HEADS * (self.rep + 2) * HEAD_DIM
        self.qw = WIN_Q_HEADS * HEAD_DIM
        self.kw = WIN_KV_HEADS * HEAD_DIM
        self.gates = N_BRANCH * d_model
        self.n_in = self.inner + self.xbc + self.ndt + self.rqd + self.qw + 2 * self.kw + self.gates
        self.o_gates = 0
        self.o_z = self.gates
        self.o_xbc = self.o_z + self.inner
        self.o_rqd = self.o_xbc + self.xbc
        self.o_rqw = self.o_rqd + self.rqd
        self.o_dt = self.o_rqw + self.rqw
        self.dtw = -(-(self.o_dt + self.ndt) // PAD_TO) * PAD_TO - self.o_dt
        self.width = self.o_dt + self.dtw
        assert self.o_z % self.inner == 0 and self.o_xbc % 256 == 0
        assert self.o_rqd % (3 * HEAD_DIM) == 0 and self.o_rqw % (3 * HEAD_DIM) == 0 and self.dtw % LANE == 0
        assert self.o_dt % self.dtw == 0

    def split_points(self):
        sizes = (self.inner, self.xbc, self.ndt, self.rqd, self.qw, self.kw, self.kw, self.gates)
        pts, acc = [], 0
        for s in sizes:
            pts.append((acc, acc + s))
            acc += s
        return pts

    def permute_w(self, w):
        d = w.shape[0]
        z, xbc, dt, qkvd, qw, kw, vw, gates = [w[:, a:b] for a, b in self.split_points()]
        nhd = self.ngrp * DIL_HEADS
        qkvd = qkvd.reshape(d, 3, nhd, HEAD_DIM).transpose(0, 2, 1, 3).reshape(d, self.rqd)
        win = jnp.concatenate([qw.reshape(d, WIN_KV_HEADS, self.rep, HEAD_DIM),
                               kw.reshape(d, WIN_KV_HEADS, 1, HEAD_DIM),
                               vw.reshape(d, WIN_KV_HEADS, 1, HEAD_DIM)], axis=2).reshape(d, self.rqw)
        pad = jnp.zeros((d, self.dtw - self.ndt), w.dtype)
        return jnp.concatenate([gates, z, xbc, qkvd, win, dt, pad], axis=1)

    def unpermute_w(self, wp):
        d = wp.shape[0]
        gates = wp[:, :self.o_z]
        z = wp[:, self.o_z:self.o_xbc]
        xbc = wp[:, self.o_xbc:self.o_rqd]
        qkvd = wp[:, self.o_rqd:self.o_rqw]
        win = wp[:, self.o_rqw:self.o_dt].reshape(d, WIN_KV_HEADS, self.rep + 2, HEAD_DIM)
        dt = wp[:, self.o_dt:self.o_dt + self.ndt]
        nhd = self.ngrp * DIL_HEADS
        qkvd = qkvd.reshape(d, nhd, 3, HEAD_DIM).transpose(0, 2, 1, 3).reshape(d, self.rqd)
        qw = win[:, :, :self.rep].reshape(d, self.qw)
        kw = win[:, :, self.rep].reshape(d, self.kw)
        vw = win[:, :, self.rep + 1].reshape(d, self.kw)
        return jnp.concatenate([z, xbc, dt, qkvd, qw, kw, vw, gates], axis=1)


def _dt_layouts(pdt, dirn, batch_seq):
    hg = SSD_HEADS // SSD_GROUPS
    v = pdt[:, dirn * SSD_HEADS:(dirn + 1) * SSD_HEADS].reshape(batch_seq, SSD_GROUPS, hg)
    return v.transpose(1, 0, 2), v.transpose(1, 2, 0)


def _par_layouts(p):
    hg = SSD_HEADS // SSD_GROUPS
    v = p.reshape(SSD_GROUPS, hg)
    return v[:, None, :], v[:, :, None]


def _layer_fwd(x, lw, lay, tabs, batch, seq):
    d = lay.d
    sv = {"x": x}
    (h,), _ = _rowcall(lambda rv, pv: ([_rms(rv[0], pv[0])], []), [(x, d, 0)], [lw["g_mix"]], [(d, BF16)], [],
                       tile=256, name="norm_mix")
    p = _mm(h, lw["w_in"], name="proj_in")
    sv["h"], sv["p"] = h, p
    u = _conv_fwd(p, lw["conv_w8"], lw["conv_b"], col0=lay.o_xbc, chans=lay.xbc, batch=batch, seq=seq, name="conv_fwd")
    sv["u"] = u
    pdt = p[:, lay.o_dt:lay.o_dt + lay.ndt]
    ys, sv["st"], sv["dtl"] = [], [], []
    for dirn in range(2):
        dtr, dtr_t = _dt_layouts(pdt, dirn, batch * seq)
        bias, bias_t = _par_layouts(lw["dt_bias"][dirn])
        alog, alog_t = _par_layouts(lw["a_log"][dirn])
        y, st = _scan_fwd(u, dtr, dtr_t, bias, bias_t, alog, alog_t, batch=batch, seq=seq, inner=lay.inner,
                          reverse=bool(dirn), name="scan_fwd%d" % dirn)
        ys.append(y)
        sv["st"].append(st)
        sv["dtl"].append((dtr, dtr_t, bias, bias_t, alog, alog_t))
    sv["ys"] = ys
    inner = lay.inner
    (ya,), _ = _rowcall(lambda rv, pv: ([_ssd_post_f(*rv, *pv)], []),
                        [(ys[0], inner, 0), (ys[1], inner, 0), (u, inner, 0), (p, inner, lay.o_z // inner)],
                        [lw["d_skip_x"], lw["ssd_norm"]], [(inner, BF16)], [], tile=128, name="ssd_post")
    sv["ya"] = ya
    gw = 3 * lay.dilw
    rqd = [_rope(p, tabs, col0=lay.o_rqd + gi * gw, width=gw, seq=seq, group=0, inverse=False, out_dtype=BF16,
                 name="rope_dil") for gi in range(lay.ngrp)]
    rqw = _rope(p, tabs, col0=lay.o_rqw, width=lay.rqw, seq=seq, group=lay.rep + 2, inverse=False, out_dtype=BF16,
                name="rope_win")
    sv["rqd"], sv["rqw"] = rqd, rqw
    os_, ls_ = [], []
    for gi, (window, dil) in enumerate(DIL_PATTERNS):
        o, l = _attn_fwd(rqd[gi], None, batch=batch, seq=seq, dil=dil, nbw=DIL_HEADS, cb0=0,
                         nh=DIL_HEADS, rep=1, hw=window // (2 * dil), want_lse=True, out_dtype=F32,
                         name="dil_fwd%d" % gi)
        os_.append(o)
        ls_.append(l)
    sv["os"], sv["ls"] = os_, ls_
    dw = lay.dilw
    (yb,), _ = _rowcall(lambda rv, pv: ([_combine_f(*rv)], []), [(a, dw, 0) for a in os_ + ls_], [], [(dw, BF16)], [],
                        tile=256, name="dil_combine")
    sv["yb"] = yb
    (yc,) = _attn_fwd(rqw, lw["sink_x"], batch=batch, seq=seq, dil=1, nbw=WIN_KV_HEADS, cb0=0, nh=WIN_KV_HEADS,
                      rep=lay.rep, hw=WIN_HALF, want_lse=False, out_dtype=BF16, name="win_fwd")
    sv["yc"] = yc
    ma = _mm(ya, lw["w_a"], name="proj_a")
    mb = _mm(yb, lw["w_b"], name="proj_b")
    mc = _mm(yc, lw["w_c"], name="proj_c")
    sv["mabc"] = (ma, mb, mc)
    (mg,), _ = _rowcall(lambda rv, pv: ([_merge_f(*rv)], []),
                        [(ma, d, 0), (mb, d, 0), (mc, d, 0), (p, d, 0), (p, d, 1), (p, d, 2)], [], [(d, BF16)], [],
                        tile=256, name="merge")
    sv["mg"] = mg
    x1 = _mm(mg, lw["w_out"], add=x, name="proj_out")
    sv["x1"] = x1
    (hm,), _ = _rowcall(lambda rv, pv: ([_rms(rv[0], pv[0])], []), [(x1, d, 0)], [lw["g_mlp"]], [(d, BF16)], [],
                        tile=256, name="norm_mlp")
    up, act = _mm(hm, lw["w_up"], post=(lambda r: (r, jnp.square(jnp.maximum(r, 0.0))), [], [F32, BF16]),
                  name="mlp_up")
    sv["hm"], sv["up"], sv["act"] = hm, up, act
    x2 = _mm(act, lw["w_down"], add=x1, name="mlp_down")
    return x2, sv


def _layer_bwd(dxo, sv, lw, lay, tabs, batch, seq, post):
    d = lay.d
    inner = lay.inner
    gs = {}
    (dup,) = _mm(dxo, lw["w_down"], tb=True, name="mlp_down_dx",
                 post=(lambda r, a: (r * (2.0 * jnp.maximum(a, 0.0)),), [sv["up"]], [BF16]))
    post("w_down", _mm(sv["act"], dxo, ta=True, name="mlp_down_dw"))
    dhm = _mm(dup, lw["w_up"], tb=True, name="mlp_up_dx")
    post("w_up", _mm(sv["hm"], dup, ta=True, name="mlp_up_dw"))

    def norm_bwd(rv, pv):
        xv, dh, dres = rv
        _, vjp = jax.vjp(_rms, xv, pv[0])
        dx, dg = vjp(dh)
        return [dx + dres], [dg]

    (dx1,), (gs["g_mlp"],) = _rowcall(norm_bwd, [(sv["x1"], d, 0), (dhm, d, 0), (dxo, d, 0)], [lw["g_mlp"]],
                                      [(d, F32)], [(1, d)], tile=128, name="norm_mlp_bwd")
    dmg = _mm(dx1, lw["w_out"], tb=True, name="proj_out_dx")
    post("w_out", _mm(sv["mg"], dx1, ta=True, name="proj_out_dw"))
    ma, mb, mc = sv["mabc"]
    p = sv["p"]

    def merge_bwd(rv, pv):
        _, vjp = jax.vjp(_merge_f, *rv[:6])
        da, db, dc, d0, d1, d2 = vjp(rv[6])
        return [da, db, dc, jnp.concatenate([d0, d1, d2], axis=1)], []

    dp = lax.empty((batch * seq, lay.width), BF16)
    (dma, dmb, dmc, dp), _ = _rowcall(
        merge_bwd, [(ma, d, 0), (mb, d, 0), (mc, d, 0), (p, d, 0), (p, d, 1), (p, d, 2), (dmg, d, 0)], [],
        [(d, BF16), (d, BF16), (d, BF16), (lay.gates, BF16, dp, 0)], [], tile=128, name="merge_bwd")
    dya = _mm(dma, lw["w_a"], tb=True, name="proj_a_dx")
    post("w_a", _mm(sv["ya"], dma, ta=True, name="proj_a_dw"))
    dyb = _mm(dmb, lw["w_b"], tb=True, name="proj_b_dx")
    post("w_b", _mm(sv["yb"], dmb, ta=True, name="proj_b_dw"))
    dyc = _mm(dmc, lw["w_c"], tb=True, name="proj_c_dx")
    post("w_c", _mm(sv["yc"], dmc, ta=True, name="proj_c_dw"))
    drqw, dsink = _attn_bwd(sv["rqw"], lw["sink_x"], dyc, None, batch=batch, seq=seq, dil=1, nbw=WIN_KV_HEADS, cb0=0,
                            nh=WIN_KV_HEADS, rep=lay.rep, hw=WIN_HALF, name="win_bwd")
    gs["sink"] = jnp.sum(dsink, axis=1)
    dw = lay.dilw

    def combine_bwd(rv, pv):
        _, vjp = jax.vjp(_combine_f, *rv[:6])
        return list(vjp(rv[6])), []

    dol, _ = _rowcall(combine_bwd, [(a, dw, 0) for a in sv["os"] + sv["ls"]] + [(dyb, dw, 0)], [],
                      [(dw, F32)] * 6, [], tile=256, name="dil_combine_bwd")
    for gi, (window, dil) in enumerate(DIL_PATTERNS):
        (dg_,) = _attn_bwd(sv["rqd"][gi], None, dol[gi], dol[3 + gi], batch=batch, seq=seq, dil=dil,
                           nbw=DIL_HEADS, cb0=0, nh=DIL_HEADS, rep=1,
                           hw=window // (2 * dil), name="dil_bwd%d" % gi)
        dp = _rope(dg_, tabs, col0=0, width=dg_.shape[1], seq=seq, group=0, inverse=True, out_dtype=BF16,
                   name="rope_dil_bwd%d" % gi, into=dp, out_col0=lay.o_rqd + gi * dg_.shape[1])
    dp = _rope(drqw, tabs, col0=0, width=lay.rqw, seq=seq, group=lay.rep + 2, inverse=True, out_dtype=BF16,
               name="rope_win_bwd", into=dp, out_col0=lay.o_rqw)
    u, ys = sv["u"], sv["ys"]

    def post_bwd(rv, pv):
        _, vjp = jax.vjp(_ssd_post_f, *rv[:4], *pv)
        dyf, _, dxs, dz, dsk, dgn = vjp(rv[4])
        return [dyf, dxs, dz], [dsk, dgn]

    (dy, dxs_post, dp), (dsk, gs["ssd_norm"]) = _rowcall(
        post_bwd, [(ys[0], inner, 0), (ys[1], inner, 0), (u, inner, 0), (p, inner, lay.o_z // inner), (dya, inner, 0)],
        [lw["d_skip_x"], lw["ssd_norm"]], [(inner, F32), (inner, F32), (inner, BF16, dp, lay.o_z // inner)],
        [(1, inner), (1, inner)], tile=128, name="ssd_post_bwd")
    gs["d_skip"] = jnp.sum(dsk.reshape(SSD_HEADS, SSD_HEAD_DIM), axis=1)
    rs = []
    ddt, gdb, gda = [], [], []
    for dirn in range(2):
        dtr, dtr_t, bias, bias_t, alog, alog_t = sv["dtl"][dirn]
        r = _scan_bwd(u, dtr, dtr_t, bias, bias_t, alog, alog_t, sv["st"][dirn], dy, batch=batch, seq=seq,
                      inner=inner, reverse=bool(dirn), name="scan_bwd%d" % dirn)
        rs.append(r)
        ddt.append((r[3] + r[4].transpose(0, 2, 1)).transpose(1, 0, 2).reshape(batch * seq, SSD_HEADS))
        gdb.append((r[5][:, 0, :] + r[6][:, :, 0]).reshape(SSD_HEADS))
        gda.append((r[7][:, 0, :] + r[8][:, :, 0]).reshape(SSD_HEADS))
    gs["dt_bias"] = jnp.stack(gdb)
    gs["a_log"] = jnp.stack(gda)
    dcws, dcbs = [], []
    for tag, ch0, chans, dus in (("x", 0, inner, [dxs_post, rs[0][0], rs[1][0]]),
                                 ("b", inner, lay.gn, [rs[0][1], rs[1][1]]),
                                 ("c", inner + lay.gn, lay.gn, [rs[0][2], rs[1][2]])):
        dp, dcw, dcb = _conv_bwd(p, lw["conv_w8"], lw["conv_b"], dus, col0=lay.o_xbc, ch0=ch0, chans=chans,
                                 batch=batch, seq=seq, into=dp, out_col0=lay.o_xbc, name="conv_bwd_" + tag)
        dcws.append(dcw)
        dcbs.append(dcb)
    gs["conv_w"] = jnp.concatenate(dcws, axis=1)[:CONV_WIDTH]
    gs["conv_b"] = jnp.concatenate(dcbs, axis=1)[0]
    ddtp = jnp.concatenate(ddt + [jnp.zeros((batch * seq, lay.dtw - lay.ndt), F32)], axis=1)
    (dp,), _ = _rowcall(lambda rv, pv: ([rv[0]], []), [(ddtp, lay.dtw, 0)], [],
                        [(lay.dtw, BF16, dp, lay.o_dt // lay.dtw)], [], tile=512, name="ddt_store")
    hd = d // W_IN_PARTS
    for part in range(W_IN_PARTS):
        post(("w_in", part), _mm(sv["h"][:, part * hd:(part + 1) * hd], dp, ta=True, name="proj_in_dw"))
    dh = _mm(dp, lw["w_in"], tb=True, name="proj_in_dx")
    (dx,), (gs["g_mix"],) = _rowcall(norm_bwd, [(sv["x"], d, 0), (dh, d, 0), (dx1, d, 0)], [lw["g_mix"]],
                                     [(d, F32)], [(1, d)], tile=128, name="norm_mix_bwd")
    return dx, gs


_SHARDED = ("w_in", "w_a", "w_b", "w_c", "w_out", "w_up", "w_down")
_COL_SHARDED = ("w_in", "w_b", "w_up")
_SMALL = ("g_mix", "conv_b", "dt_bias", "a_log", "d_skip", "ssd_norm", "sink", "g_mlp")


def _gathered_to_full(name, g):
    n, r, c = g.shape
    if name in _COL_SHARDED:
        return g.transpose(1, 0, 2).reshape(r, n * c)
    return g.reshape(n * r, c)


def _full_to_slots(name, w):
    r, c = w.shape
    if name in _COL_SHARDED:
        return w.reshape(r, N_DEV, c // N_DEV).transpose(1, 0, 2)
    return w.reshape(N_DEV, r // N_DEV, c)


class _LayerWeights:
    def __init__(self, sched, layer, lay, small):
        self.sched, self.layer, self.lay, self.vals = sched, layer, lay, dict(small)

    def __getitem__(self, name):
        if name not in self.vals:
            full = _gathered_to_full(name, self.sched.get(("w", name, self.layer)))
            self.vals[name] = self.lay.permute_w(full) if name == "w_in" else full
        return self.vals[name]


def _pack(parts):
    flat = jnp.concatenate([p.reshape(-1).astype(F32) for p in parts])
    n = flat.shape[0]
    rows = -(-n // (8 * LANE)) * 8
    return jnp.pad(flat, (0, rows * LANE - n)).reshape(rows, LANE)


def _unpack(buf, shapes):
    flat = buf.reshape(-1)
    out, off = [], 0
    for s in shapes:
        n = math.prod(s)
        out.append(flat[off:off + n].reshape(s))
        off += n
    return out


def kernel(x, g_mix, w_in, conv_w, conv_b, dt_bias, a_log, d_skip, ssd_norm, w_a, w_b, w_c, sink, w_out, g_mlp, w_up, w_down, g_final, loss_target, m_g_mix, m_w_in, m_conv_w, m_conv_b, m_dt_bias, m_a_log, m_d_skip, m_ssd_norm, m_w_a, m_w_b, m_w_c, m_sink, m_w_out, m_g_mlp, m_w_up, m_w_down, m_g_final, v_g_mix, v_w_in, v_conv_w, v_conv_b, v_dt_bias, v_a_log, v_d_skip, v_ssd_norm, v_w_a, v_w_b, v_w_c, v_sink, v_w_out, v_g_mlp, v_w_up, v_w_down, v_g_final):
    batch, seq, d = x.shape
    depth = g_mix.shape[0]
    lay = _Layout(d)
    assert lay.n_in == w_in.shape[2] * N_DEV
    wts = dict(g_mix=g_mix, w_in=w_in, conv_w=conv_w, conv_b=conv_b, dt_bias=dt_bias, a_log=a_log, d_skip=d_skip,
               ssd_norm=ssd_norm, w_a=w_a, w_b=w_b, w_c=w_c, sink=sink, w_out=w_out, g_mlp=g_mlp, w_up=w_up,
               w_down=w_down, g_final=g_final)
    mom = dict(g_mix=m_g_mix, w_in=m_w_in, conv_w=m_conv_w, conv_b=m_conv_b, dt_bias=m_dt_bias, a_log=m_a_log,
               d_skip=m_d_skip, ssd_norm=m_ssd_norm, w_a=m_w_a, w_b=m_w_b, w_c=m_w_c, sink=m_sink, w_out=m_w_out,
               g_mlp=m_g_mlp, w_up=m_w_up, w_down=m_w_down, g_final=m_g_final)
    var = dict(g_mix=v_g_mix, w_in=v_w_in, conv_w=v_conv_w, conv_b=v_conv_b, dt_bias=v_dt_bias, a_log=v_a_log,
               d_skip=v_d_skip, ssd_norm=v_ssd_norm, w_a=v_w_a, w_b=v_w_b, w_c=v_w_c, sink=v_sink, w_out=v_w_out,
               g_mlp=v_g_mlp, w_up=v_w_up, w_down=v_w_down, g_final=v_g_final)
    me = 4 * lax.axis_index("x") + 2 * lax.axis_index("y") + lax.axis_index("c")

    global _SCHED
    sched = _SCHED = _Sched()
    (gconv,) = _exchange([conv_w], scatter=False, name="gather_conv_w")
    conv_full = gconv.transpose(1, 2, 0, 3).reshape(depth, CONV_WIDTH, -1)
    for l in range(depth):
        for n in _SHARDED:
            sched.post(("w", n, l), wts[n][l].astype(BF16), scatter=False)

    tabs = _rope_tables(seq)
    t = batch * seq
    xf = x.reshape(t, d)
    layers = []
    for l in range(depth):
        small = dict(
            g_mix=g_mix[l][None], g_mlp=g_mlp[l][None], ssd_norm=ssd_norm[l][None], conv_b=conv_b[l][None],
            dt_bias=dt_bias[l], a_log=a_log[l],
            d_skip_x=jnp.repeat(d_skip[l], SSD_HEAD_DIM)[None],
            sink_x=jnp.broadcast_to(sink[l][:, None], (WIN_Q_HEADS, LANE)),
            conv_w8=jnp.pad(conv_full[l], ((0, 8 - CONV_WIDTH), (0, 0))))
        layers.append(_LayerWeights(sched, l, lay, small))

    saves = []
    h = xf
    for l in range(depth):
        h, sv = _layer_fwd(h, layers[l], lay, tabs, batch, seq)
        saves.append(sv)
    dx, dgf, loss = _final_loss(h, g_final[None], loss_target.reshape(t, d), name="final_loss")

    gss = [None] * depth
    for l in reversed(range(depth)):
        def post(n, g, l=l):
            if isinstance(n, tuple):
                key, n, g = ("g", "w_in", l, n[1]), "w_in", lay.unpermute_w(g)
            else:
                key = ("g", n, l)
            sched.post(key, _full_to_slots(n, g).astype(BF16), scatter=True)

        dx, gss[l] = _layer_bwd(dx, saves[l], layers[l], lay, tabs, batch, seq, post)
    grad_x = dx.reshape(batch, seq, d)

    small_parts = [jnp.stack([gss[l][n] for l in range(depth)]) for n in _SMALL]
    small_parts += [dgf, jnp.stack([gss[l]["conv_w"] for l in range(depth)]), loss[0, :1]]
    small_shapes = [p.shape for p in small_parts]
    (rs,) = _exchange([_pack(small_parts)], scatter=False, name="gather_small")
    red = _unpack(_sum_slots(rs, name="sum_small"), small_shapes)
    gsmall = dict(zip(list(_SMALL) + ["g_final"], red[:len(_SMALL) + 1]))
    gconv_full, loss_sum = red[-2], red[-1]
    cshard = conv_w.shape[2]
    gsmall["conv_w"] = lax.dynamic_slice_in_dim(gconv_full, me * cshard, cshard, axis=2)

    out = {}
    rep_names = list(_SMALL) + ["g_final"]
    rep_shapes = [wts[n].shape for n in rep_names]
    res = _adamw(_pack([wts[n] for n in rep_names]), [_pack([gsmall[n] for n in rep_names])[None]],
                 _pack([mom[n] for n in rep_names]), _pack([var[n] for n in rep_names]), name="adamw_small")
    unp = [_unpack(a, rep_shapes) for a in res]
    for i, n in enumerate(rep_names):
        out[n] = [unp[k][i] for k in range(4)]
    cs2 = (depth * CONV_WIDTH, cshard)
    res = _adamw(conv_w.reshape(cs2), [gsmall["conv_w"].reshape((1,) + cs2)], m_conv_w.reshape(cs2),
                 v_conv_w.reshape(cs2), name="adamw_conv_w")
    out["conv_w"] = [a.reshape(conv_w.shape) for a in res]
    for n in ("w_down", "w_up", "w_out", "w_a", "w_b", "w_c", "w_in"):
        shp = wts[n].shape
        r2 = (shp[0] * shp[1], shp[2])
        if n == "w_in":
            recvs = [sched.get(("g", n, l, part)) for l in range(depth) for part in range(W_IN_PARTS)]
        else:
            recvs = [sched.get(("g", n, l)) for l in range(depth)]
        res = _adamw(wts[n].reshape(r2), recvs, mom[n].reshape(r2), var[n].reshape(r2), name="adamw_" + n)
        out[n] = [a.reshape(shp) for a in res]

    order = ["g_mix", "w_in", "conv_w", "conv_b", "dt_bias", "a_log", "d_skip", "ssd_norm", "w_a", "w_b", "w_c",
             "sink", "w_out", "g_mlp", "w_up", "w_down", "g_final"]
    outs = [loss_sum.reshape(()), grad_x]
    for k in range(4):
        outs += [out[n][k] for n in order]
    return tuple(outs)
```

```python
import functools
import math

import jax
import jax.numpy as jnp
from jax import lax
from jax.experimental import pallas as pl
from jax.experimental.pallas import tpu as pltpu

F32 = jnp.float32
BF16 = jnp.bfloat16
HI = lax.Precision.HIGHEST
MESH = pl.DeviceIdType.MESH
N_DEV = 8

SSD_HEADS = 32
SSD_HEAD_DIM = 64
SSD_GROUPS = 8
SSD_STATE = 128
SSD_CHUNK = 128
CONV_WIDTH = 5
HEAD_DIM = 128
ROPE_DIM = 32
ROPE_THETA = 500000.0
DIL_PATTERNS = ((128, 1), (512, 4), (2048, 16))
DIL_HEADS = 8
WIN_Q_HEADS = 16
WIN_KV_HEADS = 4
WIN_HALF = 128
N_BRANCH = 3
EPS = 1e-6
NEG_INF = -1e30
ADAM_LR = 0.001
ADAM_B1 = 0.9
ADAM_B2 = 0.999
ADAM_EPS = 1e-08
ADAM_WD = 0.01
ADAM_STEP = 10

LANE = 128
QBLK = 128
VMEM_LIMIT = 56 * 1024 * 1024
PAD_TO = 512
MM_VMEM_BUDGET = 44 * 1024 * 1024
W_IN_PARTS = 4


def _cparams(sem=None):
    return pltpu.CompilerParams(dimension_semantics=sem, vmem_limit_bytes=VMEM_LIMIT)


PIECE_BYTES = 400 * 1024
US_PER_PIECE_BYTE = 8.8e-5
MAX_PIECES = 8
CARRIER_US = {
    "proj_in": 450, "proj_in_dx": 560, "proj_in_dw": 140, "scan_fwd0": 200, "scan_fwd1": 200, "scan_bwd0": 480,
    "scan_bwd1": 480, "win_fwd": 150, "win_bwd": 390, "dil_fwd0": 85, "dil_fwd1": 50, "dil_fwd2": 65,
    "dil_bwd0": 165, "dil_bwd1": 155, "dil_bwd2": 110, "mlp_up": 155, "mlp_down": 170, "mlp_up_dx": 190,
    "mlp_up_dw": 190, "mlp_down_dx": 165, "mlp_down_dw": 188, "conv_bwd_x": 75, "rope_dil": 48,
    "adamw_w_up": 63, "adamw_w_down": 62, "proj_a": 42, "proj_c": 42, "proj_out": 42, "proj_a_dx": 42,
    "proj_c_dx": 42, "proj_out_dx": 42, "proj_a_dw": 42, "proj_c_dw": 42, "proj_out_dw": 42, "conv_fwd": 65,
    "merge": 65, "merge_bwd": 100, "ssd_post": 50, "ssd_post_bwd": 90, "rope_win": 90, "norm_mix_bwd": 50,
    "norm_mlp_bwd": 50, "dil_combine_bwd": 70, "norm_mix": 40, "norm_mlp": 40, "rope_dil_bwd0": 45,
    "rope_dil_bwd1": 45, "rope_dil_bwd2": 45, "rope_win_bwd": 85, "conv_bwd_b": 35, "conv_bwd_c": 35,
}


class _Piece:
    def __init__(self, key, row0, rows, scatter, est):
        self.key, self.row0, self.rows, self.scatter, self.est = key, row0, rows, scatter, est


def _coalesce(pieces):
    out = []
    for p in pieces:
        q = out[-1] if out else None
        if q is not None and q.key == p.key and q.scatter == p.scatter and q.row0 + q.rows == p.row0:
            out[-1] = _Piece(q.key, q.row0, q.rows + p.rows, q.scatter, q.est + p.est)
        else:
            out.append(p)
    return out


class _Sched:
    def __init__(self):
        self.queue, self.src, self.dst = [], {}, {}

    def post(self, key, src, scatter):
        r, c = src.shape[-2:]
        self.src[key] = src
        self.dst[key] = lax.empty((N_DEV, r, c), src.dtype)
        row_bytes = c * src.dtype.itemsize
        pr = r
        while pr * row_bytes > PIECE_BYTES and pr % 32 == 0:
            pr //= 2
        for row0 in range(0, r, pr):
            self.queue.append(_Piece(key, row0, pr, scatter, pr * row_bytes * US_PER_PIECE_BYTE))

    def take(self, name):
        budget = CARRIER_US.get(name)
        out, used = [], 0.0
        while budget and self.queue and used + self.queue[0].est <= 1.1 * budget:
            used += self.queue[0].est
            out.append(self.queue.pop(0))
        return _coalesce(out)

    def get(self, key):
        last = max([i for i, p in enumerate(self.queue) if p.key == key], default=-1)
        if last >= 0:
            pieces, self.queue = _coalesce(self.queue[:last + 1]), self.queue[last + 1:]
            for i in range(0, len(pieces), MAX_PIECES):
                _exchange_pieces(self, pieces[i:i + MAX_PIECES], name="exchange_flush")
        return self.dst[key]


_SCHED = None


def _piece_copies(pieces, keys, src_refs, dst_refs, send_sems, recv_sems, loc_sems):
    x, y, c = lax.axis_index("x"), lax.axis_index("y"), lax.axis_index("c")
    me = 4 * x + 2 * y + c
    cps = []
    for t, p in enumerate(pieces):
        ki = keys.index(p.key)
        rows = pl.ds(p.row0, p.rows)
        for j in range(1, N_DEV):
            px = (1 - x) if (j >> 2) & 1 else x
            py = (1 - y) if (j >> 1) & 1 else y
            pc = (1 - c) if j & 1 else c
            src = src_refs[ki].at[4 * px + 2 * py + pc, rows] if p.scatter else src_refs[ki].at[rows]
            cps.append(pltpu.make_async_remote_copy(
                src_ref=src, dst_ref=dst_refs[ki].at[me, rows], send_sem=send_sems.at[t * 7 + j - 1],
                recv_sem=recv_sems.at[t * 7 + j - 1], device_id=(px, py, pc), device_id_type=MESH))
        src = src_refs[ki].at[me, rows] if p.scatter else src_refs[ki].at[rows]
        cps.append(pltpu.make_async_copy(src, dst_refs[ki].at[me, rows], loc_sems.at[t]))
    return cps


def _two_level_gather(pieces, keys, src_refs, dst_refs, send_sems, recv_sems, loc_sems):
    x, y, c = lax.axis_index("x"), lax.axis_index("y"), lax.axis_index("c")
    me = 4 * x + 2 * y + c
    sib = (x, y, 1 - c)
    chips = [(1 - x, y), (x, 1 - y), (1 - x, 1 - y)]

    def copy(t, k, src, dst, to):
        return pltpu.make_async_remote_copy(src_ref=src, dst_ref=dst, send_sem=send_sems.at[7 * t + k],
                                            recv_sem=recv_sems.at[7 * t + k], device_id=to, device_id_type=MESH)

    def landing(p, px, py, pc):
        return dst_refs[keys.index(p.key)].at[4 * px + 2 * py + pc, pl.ds(p.row0, p.rows)]

    sends, local = [], []
    for t, p in enumerate(pieces):
        src = src_refs[keys.index(p.key)].at[pl.ds(p.row0, p.rows)]
        mine = landing(p, x, y, c)
        sends.append(copy(t, 0, src, mine, sib))
        sends += [copy(t, 1 + j, src, mine, (px, py, c)) for j, (px, py) in enumerate(chips)]
        local.append(pltpu.make_async_copy(src, mine, loc_sems.at[t]))
    for cp in sends + local:
        cp.start()
    passed = []
    for t, p in enumerate(pieces):
        for j, (px, py) in enumerate(chips):
            blk = landing(p, px, py, c)
            copy(t, 1 + j, blk, blk, (x, y, c)).wait_recv()
            fwd = copy(t, 4 + j, blk, blk, sib)
            fwd.start()
            passed.append(fwd)
    for t, p in enumerate(pieces):
        blk = landing(p, x, y, 1 - c)
        copy(t, 0, blk, blk, (x, y, c)).wait_recv()
        for j, (px, py) in enumerate(chips):
            blk = landing(p, px, py, 1 - c)
            copy(t, 4 + j, blk, blk, (x, y, c)).wait_recv()
    for cp in sends + passed:
        cp.wait_send()
    for cp in local:
        cp.wait()


def _carry_call(sched, pieces, body, *, name, grid, in_specs, out_specs, out_shape, scratch_shapes, ins, aliases=None):
    keys = []
    for p in pieces:
        if p.key not in keys:
            keys.append(p.key)
    n_in, n_out, nk, npc = len(ins), len(out_shape), len(keys), len(pieces)
    n_scr = len(scratch_shapes)

    def wrapped(*refs):
        in_refs = refs[:n_in]
        src_refs = refs[n_in:n_in + nk]
        out_refs = refs[n_in + 2 * nk:n_in + 2 * nk + n_out]
        dst_refs = refs[n_in + 2 * nk + n_out:n_in + 3 * nk + n_out]
        scr = refs[n_in + 3 * nk + n_out:]
        inner_scr, sems = scr[:n_scr], scr[n_scr:]
        if grid:
            pids = [pl.program_id(a) for a in range(len(grid))]
            first = functools.reduce(lambda u, v: u & v, [q == 0 for q in pids])
            last = functools.reduce(lambda u, v: u & v, [q == g - 1 for q, g in zip(pids, grid)])

            @pl.when(first)
            def _():
                for cp in _piece_copies(pieces, keys, src_refs, dst_refs, *sems):
                    cp.start()

            body(*in_refs, *out_refs, *inner_scr)

            @pl.when(last)
            def _():
                for cp in _piece_copies(pieces, keys, src_refs, dst_refs, *sems):
                    cp.wait()
        elif all(not p.scatter for p in pieces):
            _two_level_gather(pieces, keys, src_refs, dst_refs, *sems)
        else:
            cps = _piece_copies(pieces, keys, src_refs, dst_refs, *sems)
            for cp in cps:
                cp.start()
            for cp in cps:
                cp.wait()

    anyspec = pl.BlockSpec(memory_space=pl.ANY)
    dsts = [sched.dst[k] for k in keys]
    kwargs = dict(grid=grid) if grid else {}
    res = pl.pallas_call(
        wrapped, name=name, in_specs=list(in_specs) + [anyspec] * (2 * nk), out_specs=list(out_specs) + [anyspec] * nk,
        out_shape=list(out_shape) + [jax.ShapeDtypeStruct(d.shape, d.dtype) for d in dsts],
        input_output_aliases={**(aliases or {}), **{n_in + nk + i: n_out + i for i in range(nk)}},
        scratch_shapes=list(scratch_shapes) + [pltpu.SemaphoreType.DMA((7 * npc,)), pltpu.SemaphoreType.DMA((7 * npc,)),
                                               pltpu.SemaphoreType.DMA((npc,))],
        compiler_params=pltpu.CompilerParams(dimension_semantics=("arbitrary",) * len(grid) if grid else None,
                                             vmem_limit_bytes=VMEM_LIMIT, has_side_effects=True),
        **kwargs,
    )(*ins, *[sched.src[k] for k in keys], *dsts)
    for i, k in enumerate(keys):
        sched.dst[k] = res[n_out + i]
    return list(res[:n_out])


def _exchange_pieces(sched, pieces, *, name):
    _carry_call(sched, pieces, None, name=name, grid=(), in_specs=[], out_specs=[], out_shape=[], scratch_shapes=[],
                ins=[])


def _pcall(body, *, name, grid, in_specs, out_specs, out_shape, scratch_shapes=(), sem=None, into=None):
    single = not isinstance(out_shape, (list, tuple))
    out_shape_l = [out_shape] if single else list(out_shape)
    out_specs_l = [out_specs] if single else list(out_specs)
    into = into or {}

    def run(*ins):
        n0, nb = len(ins), len(into)
        specs = list(in_specs) + [pl.BlockSpec(memory_space=pl.ANY)] * nb
        aliases = {n0 + k: oi for k, oi in enumerate(into)}
        for oi, buf in into.items():
            out_shape_l[oi] = jax.ShapeDtypeStruct(buf.shape, buf.dtype)
        kbody = (lambda *refs: body(*refs[:n0], *refs[n0 + nb:])) if nb else body
        args = list(ins) + list(into.values())
        pieces = _SCHED.take(name) if _SCHED is not None else []
        if pieces:
            res = _carry_call(_SCHED, pieces, kbody, name=name, grid=grid, in_specs=specs, out_specs=out_specs_l,
                              out_shape=out_shape_l, scratch_shapes=list(scratch_shapes), ins=args, aliases=aliases)
        else:
            res = pl.pallas_call(kbody, name=name, grid=grid, in_specs=specs, out_specs=out_specs_l,
                                 out_shape=out_shape_l, scratch_shapes=list(scratch_shapes),
                                 input_output_aliases=aliases, compiler_params=_cparams(sem))(*args)
        return res[0] if single else list(res)

    return run


def _pick(dim, cands):
    for c in cands:
        if dim % c == 0:
            return c
    return dim


def _mm_tiles(m, n, k, a_bytes, b_bytes, o_bytes, has_add):
    tm = _pick(m, (1024, 512, 256, 128))
    tn = _pick(n, (1024, 1792, 512, 256, 128))
    for tk in (3584, 2048, 1792, 1024, 896, 512, 256, 128):
        if k % tk:
            continue
        need = 2 * (tm * tk * a_bytes + tk * tn * b_bytes + tm * tn * (o_bytes + (4 if has_add else 0)))
        need += tm * tn * 4 if k // tk > 1 else 0
        if need <= MM_VMEM_BUDGET:
            return tm, tn, tk
    return tm, tn, _pick(k, (128,))


def _mm(a, b, *, ta=False, tb=False, out_dtype=F32, add=None, post=None, name):
    m, k = (a.shape[1], a.shape[0]) if ta else a.shape
    k2, n = (b.shape[1], b.shape[0]) if tb else b.shape
    assert k == k2, (a.shape, b.shape, ta, tb)
    pfn, pins, pdts = post if post is not None else (None, [], [out_dtype])
    o_bytes = sum(jnp.dtype(dt).itemsize for dt in pdts) + sum(e.dtype.itemsize for e in pins)
    tm, tn, tk = _mm_tiles(m, n, k, a.dtype.itemsize, b.dtype.itemsize, o_bytes, add is not None)
    assert m % tm == 0 and n % tn == 0 and k % tk == 0, (m, n, k, tm, tn, tk)
    nk = k // tk
    a_spec = pl.BlockSpec((tk, tm), lambda i, j, kk: (kk, i)) if ta else pl.BlockSpec((tm, tk), lambda i, j, kk: (i, kk))
    b_spec = pl.BlockSpec((tn, tk), lambda i, j, kk: (j, kk)) if tb else pl.BlockSpec((tk, tn), lambda i, j, kk: (kk, j))
    o_spec = pl.BlockSpec((tm, tn), lambda i, j, kk: (i, j))
    dims = (((0 if ta else 1,), (1 if tb else 0,)), ((), ()))
    has_add = add is not None

    nx, no = (1 if has_add else 0) + len(pins), len(pdts)

    def body(*refs):
        a_ref, b_ref = refs[:2]
        x_refs = refs[2:2 + nx]
        o_refs = refs[2 + nx:2 + nx + no]
        part = lax.dot_general(a_ref[...].astype(BF16), b_ref[...].astype(BF16), dims, preferred_element_type=F32)

        def finish(r):
            if has_add:
                r = r + x_refs[0][...]
            outs = pfn(r, *[x[...] for x in x_refs[1 if has_add else 0:]]) if pfn else (r,)
            for o_ref, v in zip(o_refs, outs):
                o_ref[...] = v.astype(o_ref.dtype)

        if nk == 1:
            finish(part)
            return
        acc_ref = refs[-1]
        kk = pl.program_id(2)

        @pl.when(kk == 0)
        def _():
            acc_ref[...] = part

        @pl.when(kk > 0)
        def _():
            acc_ref[...] += part

        @pl.when(kk == nk - 1)
        def _():
            finish(acc_ref[...])

    ins = [a, b] + ([add] if has_add else []) + list(pins)
    specs = [a_spec, b_spec] + [o_spec] * nx
    res = _pcall(
        body, name=name, grid=(m // tm, n // tn, nk), in_specs=specs, out_specs=[o_spec] * no,
        out_shape=[jax.ShapeDtypeStruct((m, n), dt) for dt in pdts],
        scratch_shapes=[pltpu.VMEM((tm, tn), F32)] if nk > 1 else [],
        sem=("parallel", "parallel", "arbitrary"),
    )(*ins)
    return res if post is not None else res[0]


def _rowcall(fn, rows, pars, row_outs, par_outs, *, tile, name):
    t = rows[0][0].shape[0]
    tile = min(tile, t)
    assert t % tile == 0
    nr, npar, nro, npo = len(rows), len(pars), len(row_outs), len(par_outs)
    in_specs = [pl.BlockSpec((tile, c), functools.partial(lambda i, cb: (i, cb), cb=cb)) for (_, c, cb) in rows]
    in_specs += [pl.BlockSpec(p.shape, lambda i: (0, 0)) for p in pars]
    into = {k: ro[2] for k, ro in enumerate(row_outs) if len(ro) == 4}
    out_specs = [pl.BlockSpec((tile, ro[0]), functools.partial(lambda i, cb: (i, cb), cb=ro[3] if len(ro) == 4 else 0))
                 for ro in row_outs]
    out_specs += [pl.BlockSpec(s, lambda i: (0, 0)) for s in par_outs]
    out_shape = [jax.ShapeDtypeStruct((t, ro[0]), ro[1]) for ro in row_outs]
    out_shape += [jax.ShapeDtypeStruct(s, F32) for s in par_outs]

    def body(*refs):
        rv = [r[...] for r in refs[:nr]]
        pv = [r[...] for r in refs[nr:nr + npar]]
        ro_refs = refs[nr + npar:nr + npar + nro]
        po_refs = refs[nr + npar + nro:]
        ro, po = fn(rv, pv)
        for ref, v in zip(ro_refs, ro):
            ref[...] = v.astype(ref.dtype)
        if npo:
            @pl.when(pl.program_id(0) == 0)
            def _():
                for ref in po_refs:
                    ref[...] = jnp.zeros_like(ref)
            for ref, v in zip(po_refs, po):
                ref[...] += v

    res = _pcall(
        body, name=name, grid=(t // tile,), in_specs=in_specs, out_specs=out_specs, out_shape=out_shape,
        sem=("arbitrary",), into=into,
    )(*[r[0] for r in rows], *pars)
    return list(res[:nro]), list(res[nro:])


def _map2d(fn, ins, out_dtype, *, name, tile=256, cw=2048):
    t, w = ins[0].shape
    tile, cw = min(tile, t), min(cw, w)
    assert t % tile == 0 and w % cw == 0

    def body(*refs):
        refs[-1][...] = fn(*[r[...] for r in refs[:-1]]).astype(out_dtype)

    spec = pl.BlockSpec((tile, cw), lambda i, j: (i, j))
    return pl.pallas_call(
        body, name=name, grid=(t // tile, w // cw), in_specs=[spec] * len(ins), out_specs=spec,
        out_shape=jax.ShapeDtypeStruct((t, w), out_dtype), compiler_params=_cparams(("parallel", "parallel")),
    )(*ins)


def _exchange(srcs, *, scatter, name):
    n = len(srcs)
    out_shape = [jax.ShapeDtypeStruct(s.shape if scatter else (N_DEV,) + s.shape, s.dtype) for s in srcs]

    def body(*refs):
        src_refs, out_refs = refs[:n], refs[n:2 * n]
        send_sems, recv_sems, loc_sems = refs[2 * n:]
        x, y, c = lax.axis_index("x"), lax.axis_index("y"), lax.axis_index("c")
        me = 4 * x + 2 * y + c
        copies = []
        for a in range(n):
            for j in range(1, N_DEV):
                px = (1 - x) if (j >> 2) & 1 else x
                py = (1 - y) if (j >> 1) & 1 else y
                pc = (1 - c) if j & 1 else c
                src = src_refs[a].at[4 * px + 2 * py + pc] if scatter else src_refs[a]
                cp = pltpu.make_async_remote_copy(
                    src_ref=src, dst_ref=out_refs[a].at[me], send_sem=send_sems.at[a * 7 + j - 1],
                    recv_sem=recv_sems.at[a * 7 + j - 1], device_id=(px, py, pc), device_id_type=MESH)
                cp.start()
                copies.append(cp)
            src = src_refs[a].at[me] if scatter else src_refs[a]
            cp = pltpu.make_async_copy(src, out_refs[a].at[me], loc_sems.at[a])
            cp.start()
            copies.append(cp)
        for cp in copies:
            cp.wait()

    anyspec = pl.BlockSpec(memory_space=pl.ANY)
    return pl.pallas_call(
        body, name=name, in_specs=[anyspec] * n, out_specs=[anyspec] * n, out_shape=out_shape,
        scratch_shapes=[pltpu.SemaphoreType.DMA((7 * n,)), pltpu.SemaphoreType.DMA((7 * n,)),
                        pltpu.SemaphoreType.DMA((n,))],
        compiler_params=pltpu.CompilerParams(has_side_effects=True),
    )(*srcs)


def _row_tile(r, c, budget_elems=256 * 1024):
    tr = r
    while tr * c > budget_elems and tr % 16 == 0:
        tr //= 2
    return tr


def _adamw(w, recvs, m, v, *, name):
    r, c = w.shape
    nl = len(recvs)
    ns, rl = recvs[0].shape[:2]
    assert rl * nl == r
    tr = _row_tile(rl, c, budget_elems=(1024 * 1024) // max(nl, 4))
    nt = rl // tr
    bc1 = 1.0 / (1.0 - ADAM_B1 ** ADAM_STEP)
    bc2 = 1.0 / (1.0 - ADAM_B2 ** ADAM_STEP)

    def body(*refs):
        w_ref, m_ref, v_ref = refs[:3]
        r_refs = refs[3:3 + nl]
        g_ref, d_ref, mo_ref, vo_ref = refs[3 + nl:]
        i = pl.program_id(0)
        for k in range(nl):
            @pl.when(i // nt == k)
            def _(k=k):
                g = r_refs[k][0].astype(F32)
                for s in range(1, ns):
                    g = g + r_refs[k][s].astype(F32)
                g_ref[...] = g
        g = g_ref[...]
        mn = ADAM_B1 * m_ref[...] + (1.0 - ADAM_B1) * g
        vn = ADAM_B2 * v_ref[...] + (1.0 - ADAM_B2) * (g * g)
        mo_ref[...] = mn
        vo_ref[...] = vn
        d_ref[...] = -ADAM_LR * ((mn * bc1) / (jnp.sqrt(vn * bc2) + ADAM_EPS) + ADAM_WD * w_ref[...])

    spec = pl.BlockSpec((tr, c), lambda i: (i, 0))
    rspecs = [pl.BlockSpec((ns, tr, c), functools.partial(lambda i, k: (0, jnp.clip(i - k * nt, 0, nt - 1), 0), k=k))
              for k in range(nl)]
    return _pcall(
        body, name=name, grid=(r // tr,), in_specs=[spec, spec, spec] + rspecs,
        out_specs=[spec] * 4, out_shape=[jax.ShapeDtypeStruct((r, c), F32)] * 4, sem=("arbitrary",),
    )(w, m, v, *recvs)


def _sum_slots(recv, *, name):
    ns, r, c = recv.shape

    def body(r_ref, o_ref):
        g = r_ref[0]
        for s in range(1, ns):
            g = g + r_ref[s]
        o_ref[...] = g

    return pl.pallas_call(body, name=name, out_shape=jax.ShapeDtypeStruct((r, c), F32))(recv)


def _rms(x, g):
    return x * lax.rsqrt(jnp.mean(x * x, axis=-1, keepdims=True) + EPS) * g


def _silu(x):
    return x * jax.nn.sigmoid(x)


def _merge_f(a, b, c, g0, g1, g2):
    return jax.nn.sigmoid(g0) * a + jax.nn.sigmoid(g1) * b + jax.nn.sigmoid(g2) * c


def _ssd_post_f(yf, yb, xs, z, dskip, gnorm):
    y = (yf + yb + dskip * xs) * _silu(z)
    return _rms(y, gnorm)


def _combine_f(o0, o1, o2, l0, l1, l2):
    m = jnp.maximum(jnp.maximum(l0, l1), l2)
    e0, e1, e2 = jnp.exp(l0 - m), jnp.exp(l1 - m), jnp.exp(l2 - m)
    return (e0 * o0 + e1 * o1 + e2 * o2) / (e0 + e1 + e2)


def _key_window(ln, hw):
    return min(ln, QBLK + 2 * hw)


def _key_start(i, ln, hw):
    return pl.multiple_of(jnp.clip(i * QBLK - hw, 0, ln - _key_window(ln, hw)), 64)


def _band_mask(qs, ks, kwin, hw):
    qpos = qs + lax.broadcasted_iota(jnp.int32, (QBLK, kwin), 0)
    kpos = ks + lax.broadcasted_iota(jnp.int32, (QBLK, kwin), 1)
    return jnp.abs(qpos - kpos) <= hw


def _attn_block(q, k3, v3, sk, valid, *, has_sink):
    s = lax.dot_general(q.astype(BF16), k3.astype(BF16), (((1,), (1,)), ((), ())),
                        preferred_element_type=F32) * (HEAD_DIM ** -0.5)
    s = jnp.where(valid, s, NEG_INF)
    m = jnp.max(s, axis=-1, keepdims=True)
    if has_sink:
        m = jnp.maximum(m, sk)
    m = lax.stop_gradient(m)
    e = jnp.exp(s - m)
    l = jnp.sum(e, axis=-1, keepdims=True)
    if has_sink:
        l = l + jnp.exp(sk - m)
    o = jnp.dot(e.astype(BF16), v3.astype(BF16), preferred_element_type=F32) / l
    return o, m + jnp.log(l)


def _ssd_chunk(state, xs, bm, cm, dtr, dtr_t, bias, bias_t, alog, alog_t, *, reverse):
    t = xs.shape[0]
    hg = dtr.shape[1]
    hp = xs.shape[1]
    p = hp // hg
    dt = jax.nn.softplus(dtr + bias)
    dt_t = jax.nn.softplus(dtr_t + bias_t)
    dta = dt * (-jnp.exp(alog))
    dta_t = dt_t * (-jnp.exp(alog_t))
    li = lax.broadcasted_iota(jnp.int32, (t, t), 0)
    si = lax.broadcasted_iota(jnp.int32, (t, t), 1)
    tri = (li <= si) if reverse else (li >= si)
    trif = tri.astype(F32)
    cs = jnp.dot(trif, dta, precision=HI, preferred_element_type=F32)
    cs_t = lax.dot_general(dta_t, trif, (((1,), (1,)), ((), ())), precision=HI,
                           preferred_element_type=F32)
    total = jnp.sum(dta, axis=0, keepdims=True)
    cb = lax.dot_general(cm.astype(BF16), bm.astype(BF16), (((1,), (1,)), ((), ())),
                         preferred_element_type=F32)
    lane_h = lax.broadcasted_iota(jnp.int32, (1, hp), 1) // p
    col_h = lax.broadcasted_iota(jnp.int32, (1, hg), 1)
    row_h = lax.broadcasted_iota(jnp.int32, (hg, 1), 0)
    dt_x = jnp.zeros((t, hp), F32)
    ecs_x = jnp.zeros((t, hp), F32)
    ds_x = jnp.zeros((t, hp), F32)
    etot_x = jnp.zeros((1, hp), F32)
    decays, masks = [], []
    for h in range(hg):
        oh = (col_h == h).astype(F32)
        oh_t = (row_h == h).astype(F32)
        mk = (lane_h == h).astype(F32)
        dt_h = jnp.sum(dt * oh, axis=1, keepdims=True)
        cs_h = jnp.sum(cs * oh, axis=1, keepdims=True)
        cst_h = jnp.sum(cs_t * oh_t, axis=0, keepdims=True)
        tot_h = jnp.sum(total * oh, axis=1, keepdims=True)
        dt_x = dt_x + dt_h * mk
        ecs_x = ecs_x + jnp.exp(cs_h) * mk
        ds_x = ds_x + jnp.exp(tot_h - cs_h) * mk
        etot_x = etot_x + jnp.exp(tot_h) * mk
        decays.append(jnp.exp(jnp.where(tri, cs_h - cst_h, -jnp.inf)))
        masks.append(mk)
    xdt = xs * dt_x
    y = jnp.dot(cm.astype(BF16), state.astype(BF16), preferred_element_type=F32) * ecs_x
    for h in range(hg):
        y = y + jnp.dot((cb * decays[h]).astype(BF16), (xdt * masks[h]).astype(BF16),
                        preferred_element_type=F32)
    st_new = lax.dot_general(bm.astype(BF16), (xdt * ds_x).astype(BF16), (((0,), (0,)), ((), ())),
                             preferred_element_type=F32)
    return y, state * etot_x + st_new


def _rope_tables(seq):
    half = ROPE_DIM // 2
    inv = ROPE_THETA ** (-jnp.arange(0, ROPE_DIM, 2, dtype=F32) / ROPE_DIM)
    ang = jnp.arange(seq, dtype=F32)[:, None] * inv[None, :]
    cos, sin = jnp.cos(ang), jnp.sin(ang)
    rest = HEAD_DIM - ROPE_DIM
    c = jnp.concatenate([cos, cos, jnp.ones((seq, rest), F32)], axis=1)
    a = jnp.concatenate([-sin, jnp.zeros((seq, HEAD_DIM - half), F32)], axis=1)
    b = jnp.concatenate([jnp.zeros((seq, half), F32), sin, jnp.zeros((seq, rest), F32)], axis=1)
    return c, a, b


def _rope(src, tabs, *, col0, width, seq, group, inverse, out_dtype, name, into=None, out_col0=0):
    t = src.shape[0]
    half = ROPE_DIM // 2
    nhb = 6 if all(v % (6 * HEAD_DIM) == 0 for v in (width, col0, out_col0)) else 3
    cw, tile = nhb * HEAD_DIM, 512
    assert width % cw == 0 and col0 % cw == 0 and out_col0 % cw == 0 and seq % tile == 0 and t % tile == 0
    ns = seq // tile

    def body(x_ref, c_ref, a_ref, b_ref, o_ref):
        jb = pl.program_id(1)
        c, a, b = c_ref[...], a_ref[...], b_ref[...]
        for hh in range(nhb):
            xv = x_ref[:, hh * HEAD_DIM:(hh + 1) * HEAD_DIM].astype(F32)
            if inverse:
                yv = xv * c + pltpu.roll(xv * a, half, 1) + pltpu.roll(xv * b, HEAD_DIM - half, 1)
            else:
                yv = xv * c + pltpu.roll(xv, HEAD_DIM - half, 1) * a + pltpu.roll(xv, half, 1) * b
            if group:
                keep = ((jb * nhb + hh) % group) == (group - 1)
                yv = jnp.where(keep, xv, yv)
            o_ref[:, hh * HEAD_DIM:(hh + 1) * HEAD_DIM] = yv.astype(o_ref.dtype)

    tspec = pl.BlockSpec((tile, HEAD_DIM), lambda i, j: (i % ns, 0))
    return _pcall(
        body, name=name, grid=(t // tile, width // cw),
        in_specs=[pl.BlockSpec((tile, cw), lambda i, j: (i, col0 // cw + j)), tspec, tspec, tspec],
        out_specs=pl.BlockSpec((tile, cw), lambda i, j: (i, out_col0 // cw + j)),
        out_shape=jax.ShapeDtypeStruct((t, width), out_dtype),
        sem=("parallel", "parallel"), into=None if into is None else {0: into},
    )(src, *tabs)


def _shift_rows(x, d, tpos):
    if d == 0:
        return x
    s = x.shape[0]
    y = pltpu.roll(x, (-d) % s, 0)
    ok = (tpos + d >= 0) & (tpos + d < s)
    return jnp.where(ok, y, 0.0)


def _conv_fwd(p, w8, bias, *, col0, chans, batch, seq, name):
    cb = 256
    assert chans % cb == 0 and col0 % cb == 0
    pad = (CONV_WIDTH - 1) // 2

    def body(x_ref, w_ref, b_ref, o_ref):
        x = x_ref[...]
        tpos = lax.broadcasted_iota(jnp.int32, x.shape, 0)
        acc = jnp.broadcast_to(b_ref[...], x.shape)
        for k in range(CONV_WIDTH):
            acc = acc + w_ref[k:k + 1, :] * _shift_rows(x, k - pad, tpos)
        o_ref[...] = _silu(acc)

    return _pcall(
        body, name=name, grid=(chans // cb, batch),
        in_specs=[pl.BlockSpec((seq, cb), lambda j, b: (b, col0 // cb + j)),
                  pl.BlockSpec((8, cb), lambda j, b: (0, j)), pl.BlockSpec((1, cb), lambda j, b: (0, j))],
        out_specs=pl.BlockSpec((seq, cb), lambda j, b: (b, j)),
        out_shape=jax.ShapeDtypeStruct((batch * seq, chans), F32),
        sem=("parallel", "arbitrary"),
    )(p, w8, bias)


def _conv_bwd(p, w8, bias, dus, *, col0, ch0, chans, batch, seq, into, out_col0, name):
    cb = 256 if chans % 256 == 0 and ch0 % 256 == 0 else 128
    assert chans % cb == 0 and ch0 % cb == 0 and col0 % cb == 0 and out_col0 % cb == 0
    pad = (CONV_WIDTH - 1) // 2
    ndu = len(dus)

    def body(*refs):
        x_ref, w_ref, b_ref = refs[:3]
        du_refs = refs[3:3 + ndu]
        dx_ref, dw_ref, db_ref = refs[3 + ndu:]
        du = du_refs[0][...]
        for r in du_refs[1:]:
            du = du + r[...]
        _conv_bwd_block(x_ref, w_ref, b_ref, du, dx_ref, dw_ref, db_ref)

    c0 = (col0 + ch0) // cb
    return _pcall(
        body, name=name, grid=(chans // cb, batch),
        in_specs=[pl.BlockSpec((seq, cb), lambda j, b: (b, c0 + j)),
                  pl.BlockSpec((8, cb), lambda j, b: (0, ch0 // cb + j)),
                  pl.BlockSpec((1, cb), lambda j, b: (0, ch0 // cb + j))]
        + [pl.BlockSpec((seq, cb), lambda j, b: (b, j))] * ndu,
        out_specs=[pl.BlockSpec((seq, cb), lambda j, b: (b, (out_col0 + ch0) // cb + j)),
                   pl.BlockSpec((8, cb), lambda j, b: (0, j)), pl.BlockSpec((1, cb), lambda j, b: (0, j))],
        out_shape=[jax.ShapeDtypeStruct(into.shape, into.dtype), jax.ShapeDtypeStruct((8, chans), F32),
                   jax.ShapeDtypeStruct((1, chans), F32)],
        sem=("parallel", "arbitrary"), into={0: into},
    )(p, w8, bias, *dus)


def _conv_bwd_block(x_ref, w_ref, b_ref, du, dx_ref, dw_ref, db_ref):
    pad = (CONV_WIDTH - 1) // 2
    x = x_ref[...]
    tpos = lax.broadcasted_iota(jnp.int32, x.shape, 0)
    acc = jnp.broadcast_to(b_ref[...], x.shape)
    xs = []
    for k in range(CONV_WIDTH):
        xs.append(_shift_rows(x, k - pad, tpos))
        acc = acc + w_ref[k:k + 1, :] * xs[k]
    sg = jax.nn.sigmoid(acc)
    dacc = du * (sg * (1.0 + acc * (1.0 - sg)))
    dx = jnp.zeros_like(x)
    for k in range(CONV_WIDTH):
        dx = dx + w_ref[k:k + 1, :] * _shift_rows(dacc, pad - k, tpos)
    dx_ref[...] = dx.astype(dx_ref.dtype)

    @pl.when(pl.program_id(1) == 0)
    def _():
        dw_ref[...] = jnp.zeros_like(dw_ref)
        db_ref[...] = jnp.zeros_like(db_ref)

    for k in range(CONV_WIDTH):
        dw_ref[k:k + 1, :] += jnp.sum(dacc * xs[k], axis=0, keepdims=True)
    db_ref[...] += jnp.sum(dacc, axis=0, keepdims=True)


def _scan_gpb(groups):
    return 4 if groups % 4 == 0 else 2 if groups % 2 == 0 else 1


def _scan_specs(batch, nc, groups, hg, inner, reverse_order):
    gpb = _scan_gpb(groups)
    t, n, hp = SSD_CHUNK, SSD_STATE, hg * SSD_HEAD_DIM
    ncb = inner // (gpb * n)
    ngb = groups // gpb

    def row(b, c):
        return b * nc + ((nc - 1 - c) if reverse_order else c)

    return dict(
        xs=pl.BlockSpec((t, gpb * hp), lambda g, b, c: (row(b, c), g)),
        bm=pl.BlockSpec((t, gpb * n), lambda g, b, c: (row(b, c), ncb + g)),
        cm=pl.BlockSpec((t, gpb * n), lambda g, b, c: (row(b, c), ncb + ngb + g)),
        dtr=pl.BlockSpec((gpb, t, hg), lambda g, b, c: (g, row(b, c), 0)),
        dtr_t=pl.BlockSpec((gpb, hg, t), lambda g, b, c: (g, 0, row(b, c))),
        par=pl.BlockSpec((gpb, 1, hg), lambda g, b, c: (g, 0, 0)),
        par_t=pl.BlockSpec((gpb, hg, 1), lambda g, b, c: (g, 0, 0)),
        y=pl.BlockSpec((t, gpb * hp), lambda g, b, c: (row(b, c), g)),
        nrow=pl.BlockSpec((t, gpb * n), lambda g, b, c: (row(b, c), g)),
        st=pl.BlockSpec((gpb, None, n, hp), lambda g, b, c: (g, row(b, c), 0, 0)),
    )


def _scan_fwd(u, dtr, dtr_t, bias, bias_t, alog, alog_t, *, batch, seq, inner, reverse, name):
    groups, hg = dtr.shape[0], dtr.shape[2]
    nc = seq // SSD_CHUNK
    hp = hg * SSD_HEAD_DIM
    sp = _scan_specs(batch, nc, groups, hg, inner, reverse)
    gpb, n = _scan_gpb(groups), SSD_STATE

    def body(xs_ref, bm_ref, cm_ref, dtr_ref, dtrt_ref, b_ref, bt_ref, a_ref, at_ref, y_ref, st_ref, state):
        @pl.when(pl.program_id(2) == 0)
        def _():
            state[...] = jnp.zeros_like(state)

        for k in range(gpb):
            xc, nc_ = slice(k * hp, (k + 1) * hp), slice(k * n, (k + 1) * n)
            st_in = state[k]
            st_ref[k] = st_in
            y, st_out = _ssd_chunk(st_in, xs_ref[:, xc], bm_ref[:, nc_], cm_ref[:, nc_], dtr_ref[k], dtrt_ref[k],
                                   b_ref[k], bt_ref[k], a_ref[k], at_ref[k], reverse=reverse)
            y_ref[:, xc] = y
            state[k] = st_out

    return _pcall(
        body, name=name, grid=(groups // gpb, batch, nc),
        in_specs=[sp["xs"], sp["bm"], sp["cm"], sp["dtr"], sp["dtr_t"], sp["par"], sp["par_t"], sp["par"], sp["par_t"]],
        out_specs=[sp["y"], sp["st"]],
        out_shape=[jax.ShapeDtypeStruct((batch * seq, inner), F32),
                   jax.ShapeDtypeStruct((groups, batch * nc, SSD_STATE, hp), F32)],
        scratch_shapes=[pltpu.VMEM((gpb, SSD_STATE, hp), F32)],
        sem=("parallel", "arbitrary", "arbitrary"),
    )(u, u, u, dtr, dtr_t, bias, bias_t, alog, alog_t)


def _scan_bwd(u, dtr, dtr_t, bias, bias_t, alog, alog_t, st, dy, *, batch, seq, inner, reverse, name):
    groups, hg = dtr.shape[0], dtr.shape[2]
    nc = seq // SSD_CHUNK
    hp = hg * SSD_HEAD_DIM
    t = batch * seq
    sp = _scan_specs(batch, nc, groups, hg, inner, not reverse)
    f = functools.partial(_ssd_chunk, reverse=reverse)
    gpb, n = _scan_gpb(groups), SSD_STATE

    def body(xs_ref, bm_ref, cm_ref, dtr_ref, dtrt_ref, b_ref, bt_ref, a_ref, at_ref, st_ref, dy_ref,
             dxs_ref, dbm_ref, dcm_ref, ddtr_ref, ddtrt_ref, db_ref, dbt_ref, da_ref, dat_ref, dstate):
        first = (pl.program_id(1) == 0) & (pl.program_id(2) == 0)

        @pl.when(pl.program_id(2) == 0)
        def _():
            dstate[...] = jnp.zeros_like(dstate)

        @pl.when(first)
        def _():
            for r in (db_ref, dbt_ref, da_ref, dat_ref):
                r[...] = jnp.zeros_like(r)

        for k in range(gpb):
            xc, nc_ = slice(k * hp, (k + 1) * hp), slice(k * n, (k + 1) * n)
            _, vjp = jax.vjp(f, st_ref[k], xs_ref[:, xc], bm_ref[:, nc_], cm_ref[:, nc_], dtr_ref[k], dtrt_ref[k],
                             b_ref[k], bt_ref[k], a_ref[k], at_ref[k])
            dst, dxs, dbm, dcm, ddtr, ddtrt, db, dbt, da, dat = vjp((dy_ref[:, xc], dstate[k]))
            dstate[k] = dst
            dxs_ref[:, xc] = dxs
            dbm_ref[:, nc_] = dbm
            dcm_ref[:, nc_] = dcm
            ddtr_ref[k] = ddtr
            ddtrt_ref[k] = ddtrt
            db_ref[k] += db
            dbt_ref[k] += dbt
            da_ref[k] += da
            dat_ref[k] += dat

    gn = groups * SSD_STATE
    return _pcall(
        body, name=name, grid=(groups // gpb, batch, nc),
        in_specs=[sp["xs"], sp["bm"], sp["cm"], sp["dtr"], sp["dtr_t"], sp["par"], sp["par_t"], sp["par"], sp["par_t"],
                  sp["st"], sp["y"]],
        out_specs=[sp["y"], sp["nrow"], sp["nrow"], sp["dtr"], sp["dtr_t"], sp["par"], sp["par_t"], sp["par"], sp["par_t"]],
        out_shape=[jax.ShapeDtypeStruct((t, inner), F32), jax.ShapeDtypeStruct((t, gn), F32),
                   jax.ShapeDtypeStruct((t, gn), F32), jax.ShapeDtypeStruct(dtr.shape, F32),
                   jax.ShapeDtypeStruct(dtr_t.shape, F32), jax.ShapeDtypeStruct(bias.shape, F32),
                   jax.ShapeDtypeStruct(bias_t.shape, F32), jax.ShapeDtypeStruct(alog.shape, F32),
                   jax.ShapeDtypeStruct(alog_t.shape, F32)],
        scratch_shapes=[pltpu.VMEM((gpb, SSD_STATE, hp), F32)],
        sem=("parallel", "arbitrary", "arbitrary"),
    )(u, u, u, dtr, dtr_t, bias, bias_t, alog, alog_t, st, dy)


def _heads_per_step(nh, nb, nbw, cb0):
    hps = min(nh, max(1, 8 // nb))
    while nh % hps or nbw % hps or cb0 % hps:
        hps -= 1
    return hps


def _attn_load(ref, col, blk):
    return ref[pl.ds(pl.multiple_of(blk * QBLK, QBLK), QBLK), col * HEAD_DIM:(col + 1) * HEAD_DIM].astype(F32)


def _lane0(row):
    lane = lax.broadcasted_iota(jnp.int32, row.shape, 1)
    return jnp.sum(jnp.where(lane == 0, row, 0.0), axis=1, keepdims=True)


def _attn_fwd(rq, sinkx, *, batch, seq, dil, nbw, cb0, nh, rep, hw, want_lse, out_dtype, name):
    t, w = rq.shape
    ln = seq // dil
    nb = ln // QBLK
    bw = (rep + 2) * HEAD_DIM
    ow = nh * rep * HEAD_DIM
    has_sink = sinkx is not None
    rq3 = rq.reshape(batch, ln, dil * w)
    kwin = _key_window(ln, hw)
    hps = _heads_per_step(nh, nb, nbw, cb0)
    f = functools.partial(_attn_block, has_sink=has_sink)

    def body(*refs):
        if has_sink:
            blk_ref, sink_ref = refs[:2]
            outs = refs[2:]
        else:
            blk_ref, sink_ref = refs[0], None
            outs = refs[1:]
        o_ref = outs[0]
        lse_ref = outs[1] if want_lse else None
        g = pl.program_id(2)

        def qblock(i, carry):
            ks = _key_start(i, ln, hw)
            valid = _band_mask(i * QBLK, ks, kwin, hw)
            rows = pl.ds(pl.multiple_of(i * QBLK, QBLK), QBLK)
            for hh in range(hps):
                hb = hh * (rep + 2)
                k3 = blk_ref[pl.ds(ks, kwin), (hb + rep) * HEAD_DIM:(hb + rep + 1) * HEAD_DIM].astype(F32)
                v3 = blk_ref[pl.ds(ks, kwin), (hb + rep + 1) * HEAD_DIM:(hb + rep + 2) * HEAD_DIM].astype(F32)
                for r in range(rep):
                    sk = _lane0(sink_ref[pl.ds((g * hps + hh) * rep + r, 1), :]) if has_sink else None
                    o, lse = f(_attn_load(blk_ref, hb + r, i), k3, v3, sk, valid)
                    oc = slice((hh * rep + r) * HEAD_DIM, (hh * rep + r + 1) * HEAD_DIM)
                    o_ref[rows, oc] = o.astype(o_ref.dtype)
                    if want_lse:
                        lse_ref[rows, oc] = jnp.broadcast_to(lse, o.shape)
            return carry

        if nb == 1:
            qblock(0, 0)
        else:
            lax.fori_loop(0, nb, qblock, 0, unroll=2 if nb % 2 == 0 else 1)

    in_specs = [pl.BlockSpec((None, ln, hps * bw), lambda b, r, h: (b, 0, (r * nbw + cb0) // hps + h))]
    ins = [rq3]
    if has_sink:
        in_specs.append(pl.BlockSpec(sinkx.shape, lambda b, r, h: (0, 0)))
        ins.append(sinkx)
    ospec = pl.BlockSpec((None, ln, hps * rep * HEAD_DIM), lambda b, r, h: (b, 0, r * (nh // hps) + h))
    out_shape = [jax.ShapeDtypeStruct((batch, ln, dil * ow), out_dtype)]
    out_specs = [ospec]
    if want_lse:
        out_shape.append(jax.ShapeDtypeStruct((batch, ln, dil * ow), F32))
        out_specs.append(ospec)
    res = _pcall(
        body, name=name, grid=(batch, dil, nh // hps), in_specs=in_specs, out_specs=out_specs, out_shape=out_shape,
        sem=("parallel", "parallel", "parallel"),
    )(*ins)
    return [r.reshape(t, ow) for r in res]


def _attn_bwd(rq, sinkx, do, dlse, *, batch, seq, dil, nbw, cb0, nh, rep, hw, name):
    t, w = rq.shape
    ln = seq // dil
    nb = ln // QBLK
    bw = (rep + 2) * HEAD_DIM
    ow = nh * rep * HEAD_DIM
    has_sink = sinkx is not None
    has_lse = dlse is not None
    kwin = _key_window(ln, hw)
    hps = _heads_per_step(nh, nb, nbw, cb0)
    f = functools.partial(_attn_block, has_sink=has_sink)

    def body(*refs):
        refs = list(refs)
        blk_ref = refs.pop(0)
        sink_ref = refs.pop(0) if has_sink else None
        do_ref = refs.pop(0)
        dlse_ref = refs.pop(0) if has_lse else None
        d_ref = refs.pop(0)
        dsink_ref = refs.pop(0) if has_sink else None
        g = pl.program_id(2)
        for hh in range(hps):
            kv0 = (hh * (rep + 2) + rep) * HEAD_DIM
            d_ref[:, kv0:kv0 + 2 * HEAD_DIM] = jnp.zeros((ln, 2 * HEAD_DIM), F32)
        if has_sink:
            @pl.when((pl.program_id(0) == 0) & (pl.program_id(1) == 0) & (g == 0))
            def _():
                dsink_ref[...] = jnp.zeros_like(dsink_ref)

        def qblock(i, carry):
            ks = _key_start(i, ln, hw)
            valid = _band_mask(i * QBLK, ks, kwin, hw)
            krows = pl.ds(ks, kwin)
            rows = pl.ds(pl.multiple_of(i * QBLK, QBLK), QBLK)
            for hh in range(hps):
                hb = hh * (rep + 2)
                kc = slice((hb + rep) * HEAD_DIM, (hb + rep + 1) * HEAD_DIM)
                vc = slice((hb + rep + 1) * HEAD_DIM, (hb + rep + 2) * HEAD_DIM)
                k3 = blk_ref[krows, kc].astype(F32)
                v3 = blk_ref[krows, vc].astype(F32)
                dk3 = jnp.zeros_like(k3)
                dv3 = jnp.zeros_like(v3)
                for r in range(rep):
                    oc = slice((hh * rep + r) * HEAD_DIM, (hh * rep + r + 1) * HEAD_DIM)
                    q = _attn_load(blk_ref, hb + r, i)
                    dov = do_ref[rows, oc]
                    dl = dlse_ref[rows, oc] if has_lse else jnp.zeros_like(dov)
                    if has_sink:
                        srow_i = (g * hps + hh) * rep + r
                        srow = sink_ref[pl.ds(srow_i, 1), :]
                        _, vjp = jax.vjp(lambda q_, k_, v_, s_: f(q_, k_, v_, _lane0(s_), valid), q, k3, v3, srow)
                        dq, dk, dv, ds = vjp((dov, jnp.sum(dl, axis=1, keepdims=True)))
                        dsink_ref[pl.ds(srow_i, 1), :] += ds
                    else:
                        _, vjp = jax.vjp(lambda q_, k_, v_: f(q_, k_, v_, None, valid), q, k3, v3)
                        dq, dk, dv = vjp((dov, jnp.sum(dl, axis=1, keepdims=True)))
                    d_ref[rows, (hb + r) * HEAD_DIM:(hb + r + 1) * HEAD_DIM] = dq
                    dk3 = dk3 + dk
                    dv3 = dv3 + dv
                d_ref[krows, kc] += dk3
                d_ref[krows, vc] += dv3
            return carry

        if nb == 1:
            qblock(0, 0)
        else:
            lax.fori_loop(0, nb, qblock, 0, unroll=2 if nb % 2 == 0 else 1)

    ospec = pl.BlockSpec((None, ln, hps * rep * HEAD_DIM), lambda b, r, h: (b, 0, r * (nh // hps) + h))
    in_specs = [pl.BlockSpec((None, ln, hps * bw), lambda b, r, h: (b, 0, (r * nbw + cb0) // hps + h))]
    ins = [rq.reshape(batch, ln, dil * w)]
    if has_sink:
        in_specs.append(pl.BlockSpec(sinkx.shape, lambda b, r, h: (0, 0)))
        ins.append(sinkx)
    in_specs.append(ospec)
    ins.append(do.reshape(batch, ln, dil * ow))
    if has_lse:
        in_specs.append(ospec)
        ins.append(dlse.reshape(batch, ln, dil * ow))
    dw = nh * bw
    out_specs = [pl.BlockSpec((None, ln, hps * bw), lambda b, r, h: (b, 0, r * (nh // hps) + h))]
    out_shape = [jax.ShapeDtypeStruct((batch, ln, dil * dw), F32)]
    if has_sink:
        out_specs.append(pl.BlockSpec(sinkx.shape, lambda b, r, h: (0, 0)))
        out_shape.append(jax.ShapeDtypeStruct(sinkx.shape, F32))
    res = _pcall(
        body, name=name, grid=(batch, dil, nh // hps), in_specs=in_specs, out_specs=out_specs, out_shape=out_shape,
        sem=("arbitrary", "arbitrary", "arbitrary"),
    )(*ins)
    return [res[0].reshape(t, dw)] + list(res[1:])


def _final_loss(x, g, target, *, name):
    d = x.shape[1]

    def fn(rv, pv):
        xv, tg = rv
        y, vjp = jax.vjp(_rms, xv, pv[0])
        err = y - tg
        dx, dg = vjp(err * (1.0 / d))
        loss = 0.5 * jnp.sum(err * err) * (1.0 / d)
        return [dx], [dg, jnp.full((1, LANE), loss, F32)]

    (dx,), (dg, loss) = _rowcall(fn, [(x, d, 0), (target, d, 0)], [g], [(d, F32)], [(1, d), (1, LANE)],
                                 tile=256, name=name)
    return dx, dg, loss


class _Layout:
    def __init__(self, d_model):
        self.d = d_model
        self.inner = SSD_HEADS * SSD_HEAD_DIM
        self.gn = SSD_GROUPS * SSD_STATE
        self.xbc = self.inner + 2 * self.gn
        self.ndt = 2 * SSD_HEADS
        self.ngrp = len(DIL_PATTERNS)
        self.dilw = DIL_HEADS * HEAD_DIM
        self.rqd = 3 * self.ngrp * self.dilw
        self.rep = WIN_Q_HEADS // WIN_KV_HEADS
        self.rqw = WIN_KV_HEADS * (self.rep + 2) * HEAD_DIM
        self.qw = WIN_Q_HEADS * HEAD_DIM
        self.kw = WIN_KV_HEADS * HEAD_DIM
        self.gates = N_BRANCH * d_model
        self.n_in = self.inner + self.xbc + self.ndt + self.rqd + self.qw + 2 * self.kw + self.gates
        self.o_gates = 0
        self.o_z = self.gates
        self.o_xbc = self.o_z + self.inner
        self.o_rqd = self.o_xbc + self.xbc
        self.o_rqw = self.o_rqd + self.rqd
        self.o_dt = self.o_rqw + self.rqw
        self.dtw = -(-(self.o_dt + self.ndt) // PAD_TO) * PAD_TO - self.o_dt
        self.width = self.o_dt + self.dtw
        assert self.o_z % self.inner == 0 and self.o_xbc % 256 == 0
        assert self.o_rqd % (3 * HEAD_DIM) == 0 and self.o_rqw % (3 * HEAD_DIM) == 0 and self.dtw % LANE == 0
        assert self.o_dt % self.dtw == 0

    def split_points(self):
        sizes = (self.inner, self.xbc, self.ndt, self.rqd, self.qw, self.kw, self.kw, self.gates)
        pts, acc = [], 0
        for s in sizes:
            pts.append((acc, acc + s))
            acc += s
        return pts

    def permute_w(self, w):
        d = w.shape[0]
        z, xbc, dt, qkvd, qw, kw, vw, gates = [w[:, a:b] for a, b in self.split_points()]
        nhd = self.ngrp * DIL_HEADS
        qkvd = qkvd.reshape(d, 3, nhd, HEAD_DIM).transpose(0, 2, 1, 3).reshape(d, self.rqd)
        win = jnp.concatenate([qw.reshape(d, WIN_KV_HEADS, self.rep, HEAD_DIM),
                               kw.reshape(d, WIN_KV_HEADS, 1, HEAD_DIM),
                               vw.reshape(d, WIN_KV_HEADS, 1, HEAD_DIM)], axis=2).reshape(d, self.rqw)
        pad = jnp.zeros((d, self.dtw - self.ndt), w.dtype)
        return jnp.concatenate([gates, z, xbc, qkvd, win, dt, pad], axis=1)

    def unpermute_w(self, wp):
        d = wp.shape[0]
        gates = wp[:, :self.o_z]
        z = wp[:, self.o_z:self.o_xbc]
        xbc = wp[:, self.o_xbc:self.o_rqd]
        qkvd = wp[:, self.o_rqd:self.o_rqw]
        win = wp[:, self.o_rqw:self.o_dt].reshape(d, WIN_KV_HEADS, self.rep + 2, HEAD_DIM)
        dt = wp[:, self.o_dt:self.o_dt + self.ndt]
        nhd = self.ngrp * DIL_HEADS
        qkvd = qkvd.reshape(d, nhd, 3, HEAD_DIM).transpose(0, 2, 1, 3).reshape(d, self.rqd)
        qw = win[:, :, :self.rep].reshape(d, self.qw)
        kw = win[:, :, self.rep].reshape(d, self.kw)
        vw = win[:, :, self.rep + 1].reshape(d, self.kw)
        return jnp.concatenate([z, xbc, dt, qkvd, qw, kw, vw, gates], axis=1)


def _dt_layouts(pdt, dirn, batch_seq):
    hg = SSD_HEADS // SSD_GROUPS
    v = pdt[:, dirn * SSD_HEADS:(dirn + 1) * SSD_HEADS].reshape(batch_seq, SSD_GROUPS, hg)
    return v.transpose(1, 0, 2), v.transpose(1, 2, 0)


def _par_layouts(p):
    hg = SSD_HEADS // SSD_GROUPS
    v = p.reshape(SSD_GROUPS, hg)
    return v[:, None, :], v[:, :, None]


def _layer_fwd(x, lw, lay, tabs, batch, seq):
    d = lay.d
    sv = {"x": x}
    (h,), _ = _rowcall(lambda rv, pv: ([_rms(rv[0], pv[0])], []), [(x, d, 0)], [lw["g_mix"]], [(d, BF16)], [],
                       tile=256, name="norm_mix")
    p = _mm(h, lw["w_in"], name="proj_in")
    sv["h"], sv["p"] = h, p
    u = _conv_fwd(p, lw["conv_w8"], lw["conv_b"], col0=lay.o_xbc, chans=lay.xbc, batch=batch, seq=seq, name="conv_fwd")
    sv["u"] = u
    pdt = p[:, lay.o_dt:lay.o_dt + lay.ndt]
    ys, sv["st"], sv["dtl"] = [], [], []
    for dirn in range(2):
        dtr, dtr_t = _dt_layouts(pdt, dirn, batch * seq)
        bias, bias_t = _par_layouts(lw["dt_bias"][dirn])
        alog, alog_t = _par_layouts(lw["a_log"][dirn])
        y, st = _scan_fwd(u, dtr, dtr_t, bias, bias_t, alog, alog_t, batch=batch, seq=seq, inner=lay.inner,
                          reverse=bool(dirn), name="scan_fwd%d" % dirn)
        ys.append(y)
        sv["st"].append(st)
        sv["dtl"].append((dtr, dtr_t, bias, bias_t, alog, alog_t))
    sv["ys"] = ys
    inner = lay.inner
    (ya,), _ = _rowcall(lambda rv, pv: ([_ssd_post_f(*rv, *pv)], []),
                        [(ys[0], inner, 0), (ys[1], inner, 0), (u, inner, 0), (p, inner, lay.o_z // inner)],
                        [lw["d_skip_x"], lw["ssd_norm"]], [(inner, BF16)], [], tile=128, name="ssd_post")
    sv["ya"] = ya
    gw = 3 * lay.dilw
    rqd = [_rope(p, tabs, col0=lay.o_rqd + gi * gw, width=gw, seq=seq, group=0, inverse=False, out_dtype=BF16,
                 name="rope_dil") for gi in range(lay.ngrp)]
    rqw = _rope(p, tabs, col0=lay.o_rqw, width=lay.rqw, seq=seq, group=lay.rep + 2, inverse=False, out_dtype=BF16,
                name="rope_win")
    sv["rqd"], sv["rqw"] = rqd, rqw
    os_, ls_ = [], []
    for gi, (window, dil) in enumerate(DIL_PATTERNS):
        o, l = _attn_fwd(rqd[gi], None, batch=batch, seq=seq, dil=dil, nbw=DIL_HEADS, cb0=0,
                         nh=DIL_HEADS, rep=1, hw=window // (2 * dil), want_lse=True, out_dtype=F32,
                         name="dil_fwd%d" % gi)
        os_.append(o)
        ls_.append(l)
    sv["os"], sv["ls"] = os_, ls_
    dw = lay.dilw
    (yb,), _ = _rowcall(lambda rv, pv: ([_combine_f(*rv)], []), [(a, dw, 0) for a in os_ + ls_], [], [(dw, BF16)], [],
                        tile=256, name="dil_combine")
    sv["yb"] = yb
    (yc,) = _attn_fwd(rqw, lw["sink_x"], batch=batch, seq=seq, dil=1, nbw=WIN_KV_HEADS, cb0=0, nh=WIN_KV_HEADS,
                      rep=lay.rep, hw=WIN_HALF, want_lse=False, out_dtype=BF16, name="win_fwd")
    sv["yc"] = yc
    ma = _mm(ya, lw["w_a"], name="proj_a")
    mb = _mm(yb, lw["w_b"], name="proj_b")
    mc = _mm(yc, lw["w_c"], name="proj_c")
    sv["mabc"] = (ma, mb, mc)
    (mg,), _ = _rowcall(lambda rv, pv: ([_merge_f(*rv)], []),
                        [(ma, d, 0), (mb, d, 0), (mc, d, 0), (p, d, 0), (p, d, 1), (p, d, 2)], [], [(d, BF16)], [],
                        tile=256, name="merge")
    sv["mg"] = mg
    x1 = _mm(mg, lw["w_out"], add=x, name="proj_out")
    sv["x1"] = x1
    (hm,), _ = _rowcall(lambda rv, pv: ([_rms(rv[0], pv[0])], []), [(x1, d, 0)], [lw["g_mlp"]], [(d, BF16)], [],
                        tile=256, name="norm_mlp")
    up, act = _mm(hm, lw["w_up"], post=(lambda r: (r, jnp.square(jnp.maximum(r, 0.0))), [], [F32, BF16]),
                  name="mlp_up")
    sv["hm"], sv["up"], sv["act"] = hm, up, act
    x2 = _mm(act, lw["w_down"], add=x1, name="mlp_down")
    return x2, sv


def _layer_bwd(dxo, sv, lw, lay, tabs, batch, seq, post):
    d = lay.d
    inner = lay.inner
    gs = {}
    (dup,) = _mm(dxo, lw["w_down"], tb=True, name="mlp_down_dx",
                 post=(lambda r, a: (r * (2.0 * jnp.maximum(a, 0.0)),), [sv["up"]], [BF16]))
    post("w_down", _mm(sv["act"], dxo, ta=True, out_dtype=BF16, name="mlp_down_dw"))
    dhm = _mm(dup, lw["w_up"], tb=True, name="mlp_up_dx")
    post("w_up", _mm(sv["hm"], dup, ta=True, out_dtype=BF16, name="mlp_up_dw"))

    def norm_bwd(rv, pv):
        xv, dh, dres = rv
        _, vjp = jax.vjp(_rms, xv, pv[0])
        dx, dg = vjp(dh)
        return [dx + dres], [dg]

    (dx1,), (gs["g_mlp"],) = _rowcall(norm_bwd, [(sv["x1"], d, 0), (dhm, d, 0), (dxo, d, 0)], [lw["g_mlp"]],
                                      [(d, F32)], [(1, d)], tile=128, name="norm_mlp_bwd")
    dmg = _mm(dx1, lw["w_out"], tb=True, name="proj_out_dx")
    post("w_out", _mm(sv["mg"], dx1, ta=True, out_dtype=BF16, name="proj_out_dw"))
    ma, mb, mc = sv["mabc"]
    p = sv["p"]

    def merge_bwd(rv, pv):
        _, vjp = jax.vjp(_merge_f, *rv[:6])
        da, db, dc, d0, d1, d2 = vjp(rv[6])
        return [da, db, dc, jnp.concatenate([d0, d1, d2], axis=1)], []

    dp = lax.empty((batch * seq, lay.width), BF16)
    (dma, dmb, dmc, dp), _ = _rowcall(
        merge_bwd, [(ma, d, 0), (mb, d, 0), (mc, d, 0), (p, d, 0), (p, d, 1), (p, d, 2), (dmg, d, 0)], [],
        [(d, BF16), (d, BF16), (d, BF16), (lay.gates, BF16, dp, 0)], [], tile=128, name="merge_bwd")
    dya = _mm(dma, lw["w_a"], tb=True, name="proj_a_dx")
    post("w_a", _mm(sv["ya"], dma, ta=True, out_dtype=BF16, name="proj_a_dw"))
    dyb = _mm(dmb, lw["w_b"], tb=True, name="proj_b_dx")
    post("w_b", _mm(sv["yb"], dmb, ta=True, out_dtype=BF16, name="proj_b_dw"))
    dyc = _mm(dmc, lw["w_c"], tb=True, name="proj_c_dx")
    post("w_c", _mm(sv["yc"], dmc, ta=True, out_dtype=BF16, name="proj_c_dw"))
    drqw, dsink = _attn_bwd(sv["rqw"], lw["sink_x"], dyc, None, batch=batch, seq=seq, dil=1, nbw=WIN_KV_HEADS, cb0=0,
                            nh=WIN_KV_HEADS, rep=lay.rep, hw=WIN_HALF, name="win_bwd")
    gs["sink"] = jnp.sum(dsink, axis=1)
    dw = lay.dilw

    def combine_bwd(rv, pv):
        _, vjp = jax.vjp(_combine_f, *rv[:6])
        return list(vjp(rv[6])), []

    dol, _ = _rowcall(combine_bwd, [(a, dw, 0) for a in sv["os"] + sv["ls"]] + [(dyb, dw, 0)], [],
                      [(dw, F32)] * 6, [], tile=256, name="dil_combine_bwd")
    for gi, (window, dil) in enumerate(DIL_PATTERNS):
        (dg_,) = _attn_bwd(sv["rqd"][gi], None, dol[gi], dol[3 + gi], batch=batch, seq=seq, dil=dil,
                           nbw=DIL_HEADS, cb0=0, nh=DIL_HEADS, rep=1,
                           hw=window // (2 * dil), name="dil_bwd%d" % gi)
        dp = _rope(dg_, tabs, col0=0, width=dg_.shape[1], seq=seq, group=0, inverse=True, out_dtype=BF16,
                   name="rope_dil_bwd%d" % gi, into=dp, out_col0=lay.o_rqd + gi * dg_.shape[1])
    dp = _rope(drqw, tabs, col0=0, width=lay.rqw, seq=seq, group=lay.rep + 2, inverse=True, out_dtype=BF16,
               name="rope_win_bwd", into=dp, out_col0=lay.o_rqw)
    u, ys = sv["u"], sv["ys"]

    def post_bwd(rv, pv):
        _, vjp = jax.vjp(_ssd_post_f, *rv[:4], *pv)
        dyf, _, dxs, dz, dsk, dgn = vjp(rv[4])
        return [dyf, dxs, dz], [dsk, dgn]

    (dy, dxs_post, dp), (dsk, gs["ssd_norm"]) = _rowcall(
        post_bwd, [(ys[0], inner, 0), (ys[1], inner, 0), (u, inner, 0), (p, inner, lay.o_z // inner), (dya, inner, 0)],
        [lw["d_skip_x"], lw["ssd_norm"]], [(inner, F32), (inner, F32), (inner, BF16, dp, lay.o_z // inner)],
        [(1, inner), (1, inner)], tile=128, name="ssd_post_bwd")
    gs["d_skip"] = jnp.sum(dsk.reshape(SSD_HEADS, SSD_HEAD_DIM), axis=1)
    rs = []
    ddt, gdb, gda = [], [], []
    for dirn in range(2):
        dtr, dtr_t, bias, bias_t, alog, alog_t = sv["dtl"][dirn]
        r = _scan_bwd(u, dtr, dtr_t, bias, bias_t, alog, alog_t, sv["st"][dirn], dy, batch=batch, seq=seq,
                      inner=inner, reverse=bool(dirn), name="scan_bwd%d" % dirn)
        rs.append(r)
        ddt.append((r[3] + r[4].transpose(0, 2, 1)).transpose(1, 0, 2).reshape(batch * seq, SSD_HEADS))
        gdb.append((r[5][:, 0, :] + r[6][:, :, 0]).reshape(SSD_HEADS))
        gda.append((r[7][:, 0, :] + r[8][:, :, 0]).reshape(SSD_HEADS))
    gs["dt_bias"] = jnp.stack(gdb)
    gs["a_log"] = jnp.stack(gda)
    dcws, dcbs = [], []
    for tag, ch0, chans, dus in (("x", 0, inner, [dxs_post, rs[0][0], rs[1][0]]),
                                 ("b", inner, lay.gn, [rs[0][1], rs[1][1]]),
                                 ("c", inner + lay.gn, lay.gn, [rs[0][2], rs[1][2]])):
        dp, dcw, dcb = _conv_bwd(p, lw["conv_w8"], lw["conv_b"], dus, col0=lay.o_xbc, ch0=ch0, chans=chans,
                                 batch=batch, seq=seq, into=dp, out_col0=lay.o_xbc, name="conv_bwd_" + tag)
        dcws.append(dcw)
        dcbs.append(dcb)
    gs["conv_w"] = jnp.concatenate(dcws, axis=1)[:CONV_WIDTH]
    gs["conv_b"] = jnp.concatenate(dcbs, axis=1)[0]
    ddtp = jnp.concatenate(ddt + [jnp.zeros((batch * seq, lay.dtw - lay.ndt), F32)], axis=1)
    (dp,), _ = _rowcall(lambda rv, pv: ([rv[0]], []), [(ddtp, lay.dtw, 0)], [],
                        [(lay.dtw, BF16, dp, lay.o_dt // lay.dtw)], [], tile=512, name="ddt_store")
    hd = d // W_IN_PARTS
    for part in range(W_IN_PARTS):
        post(("w_in", part), _mm(sv["h"][:, part * hd:(part + 1) * hd], dp, ta=True, out_dtype=BF16,
                                 name="proj_in_dw"))
    dh = _mm(dp, lw["w_in"], tb=True, name="proj_in_dx")
    (dx,), (gs["g_mix"],) = _rowcall(norm_bwd, [(sv["x"], d, 0), (dh, d, 0), (dx1, d, 0)], [lw["g_mix"]],
                                     [(d, F32)], [(1, d)], tile=128, name="norm_mix_bwd")
    return dx, gs


_SHARDED = ("w_in", "w_a", "w_b", "w_c", "w_out", "w_up", "w_down")
_COL_SHARDED = ("w_in", "w_b", "w_up")
_SMALL = ("g_mix", "conv_b", "dt_bias", "a_log", "d_skip", "ssd_norm", "sink", "g_mlp")


def _gathered_to_full(name, g):
    n, r, c = g.shape
    if name in _COL_SHARDED:
        return g.transpose(1, 0, 2).reshape(r, n * c)
    return g.reshape(n * r, c)


def _full_to_slots(name, w):
    r, c = w.shape
    if name in _COL_SHARDED:
        return w.reshape(r, N_DEV, c // N_DEV).transpose(1, 0, 2)
    return w.reshape(N_DEV, r // N_DEV, c)


class _LayerWeights:
    def __init__(self, sched, layer, lay, small):
        self.sched, self.layer, self.lay, self.vals = sched, layer, lay, dict(small)

    def __getitem__(self, name):
        if name not in self.vals:
            full = _gathered_to_full(name, self.sched.get(("w", name, self.layer)))
            self.vals[name] = self.lay.permute_w(full) if name == "w_in" else full
        return self.vals[name]


def _pack(parts):
    flat = jnp.concatenate([p.reshape(-1).astype(F32) for p in parts])
    n = flat.shape[0]
    rows = -(-n // (8 * LANE)) * 8
    return jnp.pad(flat, (0, rows * LANE - n)).reshape(rows, LANE)


def _unpack(buf, shapes):
    flat = buf.reshape(-1)
    out, off = [], 0
    for s in shapes:
        n = math.prod(s)
        out.append(flat[off:off + n].reshape(s))
        off += n
    return out


def kernel(x, g_mix, w_in, conv_w, conv_b, dt_bias, a_log, d_skip, ssd_norm, w_a, w_b, w_c, sink, w_out, g_mlp, w_up, w_down, g_final, loss_target, m_g_mix, m_w_in, m_conv_w, m_conv_b, m_dt_bias, m_a_log, m_d_skip, m_ssd_norm, m_w_a, m_w_b, m_w_c, m_sink, m_w_out, m_g_mlp, m_w_up, m_w_down, m_g_final, v_g_mix, v_w_in, v_conv_w, v_conv_b, v_dt_bias, v_a_log, v_d_skip, v_ssd_norm, v_w_a, v_w_b, v_w_c, v_sink, v_w_out, v_g_mlp, v_w_up, v_w_down, v_g_final):
    batch, seq, d = x.shape
    depth = g_mix.shape[0]
    lay = _Layout(d)
    assert lay.n_in == w_in.shape[2] * N_DEV
    wts = dict(g_mix=g_mix, w_in=w_in, conv_w=conv_w, conv_b=conv_b, dt_bias=dt_bias, a_log=a_log, d_skip=d_skip,
               ssd_norm=ssd_norm, w_a=w_a, w_b=w_b, w_c=w_c, sink=sink, w_out=w_out, g_mlp=g_mlp, w_up=w_up,
               w_down=w_down, g_final=g_final)
    mom = dict(g_mix=m_g_mix, w_in=m_w_in, conv_w=m_conv_w, conv_b=m_conv_b, dt_bias=m_dt_bias, a_log=m_a_log,
               d_skip=m_d_skip, ssd_norm=m_ssd_norm, w_a=m_w_a, w_b=m_w_b, w_c=m_w_c, sink=m_sink, w_out=m_w_out,
               g_mlp=m_g_mlp, w_up=m_w_up, w_down=m_w_down, g_final=m_g_final)
    var = dict(g_mix=v_g_mix, w_in=v_w_in, conv_w=v_conv_w, conv_b=v_conv_b, dt_bias=v_dt_bias, a_log=v_a_log,
               d_skip=v_d_skip, ssd_norm=v_ssd_norm, w_a=v_w_a, w_b=v_w_b, w_c=v_w_c, sink=v_sink, w_out=v_w_out,
               g_mlp=v_g_mlp, w_up=v_w_up, w_down=v_w_down, g_final=v_g_final)
    me = 4 * lax.axis_index("x") + 2 * lax.axis_index("y") + lax.axis_index("c")

    global _SCHED
    sched = _SCHED = _Sched()
    (gconv,) = _exchange([conv_w], scatter=False, name="gather_conv_w")
    conv_full = gconv.transpose(1, 2, 0, 3).reshape(depth, CONV_WIDTH, -1)
    for l in range(depth):
        for n in _SHARDED:
            sched.post(("w", n, l), wts[n][l].astype(BF16), scatter=False)

    tabs = _rope_tables(seq)
    t = batch * seq
    xf = x.reshape(t, d)
    layers = []
    for l in range(depth):
        small = dict(
            g_mix=g_mix[l][None], g_mlp=g_mlp[l][None], ssd_norm=ssd_norm[l][None], conv_b=conv_b[l][None],
            dt_bias=dt_bias[l], a_log=a_log[l],
            d_skip_x=jnp.repeat(d_skip[l], SSD_HEAD_DIM)[None],
            sink_x=jnp.broadcast_to(sink[l][:, None], (WIN_Q_HEADS, LANE)),
            conv_w8=jnp.pad(conv_full[l], ((0, 8 - CONV_WIDTH), (0, 0))))
        layers.append(_LayerWeights(sched, l, lay, small))

    saves = []
    h = xf
    for l in range(depth):
        h, sv = _layer_fwd(h, layers[l], lay, tabs, batch, seq)
        saves.append(sv)
    dx, dgf, loss = _final_loss(h, g_final[None], loss_target.reshape(t, d), name="final_loss")

    gss = [None] * depth
    for l in reversed(range(depth)):
        def post(n, g, l=l):
            if isinstance(n, tuple):
                key, n, g = ("g", "w_in", l, n[1]), "w_in", lay.unpermute_w(g)
            else:
                key = ("g", n, l)
            sched.post(key, _full_to_slots(n, g).astype(BF16), scatter=True)

        dx, gss[l] = _layer_bwd(dx, saves[l], layers[l], lay, tabs, batch, seq, post)
    grad_x = dx.reshape(batch, seq, d)

    small_parts = [jnp.stack([gss[l][n] for l in range(depth)]) for n in _SMALL]
    small_parts += [dgf, jnp.stack([gss[l]["conv_w"] for l in range(depth)]), loss[0, :1]]
    small_shapes = [p.shape for p in small_parts]
    (rs,) = _exchange([_pack(small_parts)], scatter=False, name="gather_small")
    red = _unpack(_sum_slots(rs, name="sum_small"), small_shapes)
    gsmall = dict(zip(list(_SMALL) + ["g_final"], red[:len(_SMALL) + 1]))
    gconv_full, loss_sum = red[-2], red[-1]
    cshard = conv_w.shape[2]
    gsmall["conv_w"] = lax.dynamic_slice_in_dim(gconv_full, me * cshard, cshard, axis=2)

    out = {}
    rep_names = list(_SMALL) + ["g_final"]
    rep_shapes = [wts[n].shape for n in rep_names]
    res = _adamw(_pack([wts[n] for n in rep_names]), [_pack([gsmall[n] for n in rep_names])[None]],
                 _pack([mom[n] for n in rep_names]), _pack([var[n] for n in rep_names]), name="adamw_small")
    unp = [_unpack(a, rep_shapes) for a in res]
    for i, n in enumerate(rep_names):
        out[n] = [unp[k][i] for k in range(4)]
    cs2 = (depth * CONV_WIDTH, cshard)
    res = _adamw(conv_w.reshape(cs2), [gsmall["conv_w"].reshape((1,) + cs2)], m_conv_w.reshape(cs2),
                 v_conv_w.reshape(cs2), name="adamw_conv_w")
    out["conv_w"] = [a.reshape(conv_w.shape) for a in res]
    for n in ("w_down", "w_up", "w_out", "w_a", "w_b", "w_c", "w_in"):
        shp = wts[n].shape
        r2 = (shp[0] * shp[1], shp[2])
        if n == "w_in":
            recvs = [sched.get(("g", n, l, part)) for l in range(depth) for part in range(W_IN_PARTS)]
        else:
            recvs = [sched.get(("g", n, l)) for l in range(depth)]
        res = _adamw(wts[n].reshape(r2), recvs, mom[n].reshape(r2), var[n].reshape(r2), name="adamw_" + n)
        out[n] = [a.reshape(shp) for a in res]

    order = ["g_mix", "w_in", "conv_w", "conv_b", "dt_bias", "a_log", "d_skip", "ssd_norm", "w_a", "w_b", "w_c",
             "sink", "w_out", "g_mlp", "w_up", "w_down", "g_final"]
    outs = [loss_sum.reshape(()), grad_x]
    for k in range(4):
        outs += [out[n][k] for n in order]
    return tuple(outs)
```

```python
import functools
import math

import jax
import jax.numpy as jnp
from jax import lax
from jax.experimental import pallas as pl
from jax.experimental.pallas import tpu as pltpu

F32 = jnp.float32
BF16 = jnp.bfloat16
HI = lax.Precision.HIGHEST
MESH = pl.DeviceIdType.MESH
N_DEV = 8

SSD_HEADS = 32
SSD_HEAD_DIM = 64
SSD_GROUPS = 8
SSD_STATE = 128
SSD_CHUNK = 128
CONV_WIDTH = 5
HEAD_DIM = 128
ROPE_DIM = 32
ROPE_THETA = 500000.0
DIL_PATTERNS = ((128, 1), (512, 4), (2048, 16))
DIL_HEADS = 8
WIN_Q_HEADS = 16
WIN_KV_HEADS = 4
WIN_HALF = 128
N_BRANCH = 3
EPS = 1e-6
NEG_INF = -1e30
ADAM_LR = 0.001
ADAM_B1 = 0.9
ADAM_B2 = 0.999
ADAM_EPS = 1e-08
ADAM_WD = 0.01
ADAM_STEP = 10

LANE = 128
QBLK = 128
VMEM_LIMIT = 56 * 1024 * 1024
PAD_TO = 512
MM_VMEM_BUDGET = 44 * 1024 * 1024
W_IN_PARTS = 4


def _cparams(sem=None):
    return pltpu.CompilerParams(dimension_semantics=sem, vmem_limit_bytes=VMEM_LIMIT)


PIECE_BYTES = 400 * 1024
US_PER_PIECE_BYTE = 8.8e-5
MAX_PIECES = 8
CARRIER_US = {
    "proj_in": 450, "proj_in_dx": 560, "proj_in_dw": 140, "scan_fwd0": 200, "scan_fwd1": 200, "scan_bwd0": 480,
    "scan_bwd1": 480, "win_fwd": 150, "win_bwd": 390, "dil_fwd0": 85, "dil_fwd1": 50, "dil_fwd2": 65,
    "dil_bwd0": 165, "dil_bwd1": 155, "dil_bwd2": 110, "mlp_up": 155, "mlp_down": 170, "mlp_up_dx": 190,
    "mlp_up_dw": 190, "mlp_down_dx": 165, "mlp_down_dw": 188, "conv_bwd_x": 75, "rope_dil": 48,
    "adamw_w_up": 63, "adamw_w_down": 62, "proj_a": 42, "proj_c": 42, "proj_out": 42, "proj_a_dx": 42,
    "proj_c_dx": 42, "proj_out_dx": 42, "proj_a_dw": 42, "proj_c_dw": 42, "proj_out_dw": 42, "conv_fwd": 65,
    "merge": 65, "merge_bwd": 100, "ssd_post": 50, "ssd_post_bwd": 90, "rope_win": 90, "norm_mix_bwd": 50,
    "norm_mlp_bwd": 50, "dil_combine_bwd": 70, "norm_mix": 40, "norm_mlp": 40, "rope_dil_bwd0": 45,
    "rope_dil_bwd1": 45, "rope_dil_bwd2": 45, "rope_win_bwd": 85, "conv_bwd_b": 35, "conv_bwd_c": 35,
}


class _Piece:
    def __init__(self, key, row0, rows, scatter, est):
        self.key, self.row0, self.rows, self.scatter, self.est = key, row0, rows, scatter, est


def _coalesce(pieces):
    out = []
    for p in pieces:
        q = out[-1] if out else None
        if q is not None and q.key == p.key and q.scatter == p.scatter and q.row0 + q.rows == p.row0:
            out[-1] = _Piece(q.key, q.row0, q.rows + p.rows, q.scatter, q.est + p.est)
        else:
            out.append(p)
    return out


class _Sched:
    def __init__(self):
        self.queue, self.src, self.dst = [], {}, {}

    def post(self, key, src, scatter):
        r, c = src.shape[-2:]
        self.src[key] = src
        self.dst[key] = lax.empty((N_DEV, r, c), src.dtype)
        row_bytes = c * src.dtype.itemsize
        pr = r
        while pr * row_bytes > PIECE_BYTES and pr % 32 == 0:
            pr //= 2
        for row0 in range(0, r, pr):
            self.queue.append(_Piece(key, row0, pr, scatter, pr * row_bytes * US_PER_PIECE_BYTE))

    def take(self, name):
        budget = CARRIER_US.get(name)
        out, used = [], 0.0
        while budget and self.queue and used + self.queue[0].est <= 1.1 * budget:
            used += self.queue[0].est
            out.append(self.queue.pop(0))
        return _coalesce(out)

    def get(self, key):
        last = max([i for i, p in enumerate(self.queue) if p.key == key], default=-1)
        if last >= 0:
            pieces, self.queue = _coalesce(self.queue[:last + 1]), self.queue[last + 1:]
            for i in range(0, len(pieces), MAX_PIECES):
                _exchange_pieces(self, pieces[i:i + MAX_PIECES], name="exchange_flush")
        return self.dst[key]


_SCHED = None


def _piece_copies(pieces, keys, src_refs, dst_refs, send_sems, recv_sems, loc_sems):
    x, y, c = lax.axis_index("x"), lax.axis_index("y"), lax.axis_index("c")
    me = 4 * x + 2 * y + c
    cps = []
    for t, p in enumerate(pieces):
        ki = keys.index(p.key)
        rows = pl.ds(p.row0, p.rows)
        for j in range(1, N_DEV):
            px = (1 - x) if (j >> 2) & 1 else x
            py = (1 - y) if (j >> 1) & 1 else y
            pc = (1 - c) if j & 1 else c
            src = src_refs[ki].at[4 * px + 2 * py + pc, rows] if p.scatter else src_refs[ki].at[rows]
            cps.append(pltpu.make_async_remote_copy(
                src_ref=src, dst_ref=dst_refs[ki].at[me, rows], send_sem=send_sems.at[t * 7 + j - 1],
                recv_sem=recv_sems.at[t * 7 + j - 1], device_id=(px, py, pc), device_id_type=MESH))
        src = src_refs[ki].at[me, rows] if p.scatter else src_refs[ki].at[rows]
        cps.append(pltpu.make_async_copy(src, dst_refs[ki].at[me, rows], loc_sems.at[t]))
    return cps


def _two_level_gather(pieces, keys, src_refs, dst_refs, send_sems, recv_sems, loc_sems):
    x, y, c = lax.axis_index("x"), lax.axis_index("y"), lax.axis_index("c")
    me = 4 * x + 2 * y + c
    sib = (x, y, 1 - c)
    chips = [(1 - x, y), (x, 1 - y), (1 - x, 1 - y)]

    def copy(t, k, src, dst, to):
        return pltpu.make_async_remote_copy(src_ref=src, dst_ref=dst, send_sem=send_sems.at[7 * t + k],
                                            recv_sem=recv_sems.at[7 * t + k], device_id=to, device_id_type=MESH)

    def landing(p, px, py, pc):
        return dst_refs[keys.index(p.key)].at[4 * px + 2 * py + pc, pl.ds(p.row0, p.rows)]

    sends, local = [], []
    for t, p in enumerate(pieces):
        src = src_refs[keys.index(p.key)].at[pl.ds(p.row0, p.rows)]
        mine = landing(p, x, y, c)
        sends.append(copy(t, 0, src, mine, sib))
        sends += [copy(t, 1 + j, src, mine, (px, py, c)) for j, (px, py) in enumerate(chips)]
        local.append(pltpu.make_async_copy(src, mine, loc_sems.at[t]))
    for cp in sends + local:
        cp.start()
    passed = []
    for t, p in enumerate(pieces):
        for j, (px, py) in enumerate(chips):
            blk = landing(p, px, py, c)
            copy(t, 1 + j, blk, blk, (x, y, c)).wait_recv()
            fwd = copy(t, 4 + j, blk, blk, sib)
            fwd.start()
            passed.append(fwd)
    for t, p in enumerate(pieces):
        blk = landing(p, x, y, 1 - c)
        copy(t, 0, blk, blk, (x, y, c)).wait_recv()
        for j, (px, py) in enumerate(chips):
            blk = landing(p, px, py, 1 - c)
            copy(t, 4 + j, blk, blk, (x, y, c)).wait_recv()
    for cp in sends + passed:
        cp.wait_send()
    for cp in local:
        cp.wait()


def _carry_call(sched, pieces, body, *, name, grid, in_specs, out_specs, out_shape, scratch_shapes, ins, aliases=None):
    keys = []
    for p in pieces:
        if p.key not in keys:
            keys.append(p.key)
    n_in, n_out, nk, npc = len(ins), len(out_shape), len(keys), len(pieces)
    n_scr = len(scratch_shapes)

    def wrapped(*refs):
        in_refs = refs[:n_in]
        src_refs = refs[n_in:n_in + nk]
        out_refs = refs[n_in + 2 * nk:n_in + 2 * nk + n_out]
        dst_refs = refs[n_in + 2 * nk + n_out:n_in + 3 * nk + n_out]
        scr = refs[n_in + 3 * nk + n_out:]
        inner_scr, sems = scr[:n_scr], scr[n_scr:]
        if grid:
            pids = [pl.program_id(a) for a in range(len(grid))]
            first = functools.reduce(lambda u, v: u & v, [q == 0 for q in pids])
            last = functools.reduce(lambda u, v: u & v, [q == g - 1 for q, g in zip(pids, grid)])

            @pl.when(first)
            def _():
                for cp in _piece_copies(pieces, keys, src_refs, dst_refs, *sems):
                    cp.start()

            body(*in_refs, *out_refs, *inner_scr)

            @pl.when(last)
            def _():
                for cp in _piece_copies(pieces, keys, src_refs, dst_refs, *sems):
                    cp.wait()
        elif all(not p.scatter for p in pieces):
            _two_level_gather(pieces, keys, src_refs, dst_refs, *sems)
        else:
            cps = _piece_copies(pieces, keys, src_refs, dst_refs, *sems)
            for cp in cps:
                cp.start()
            for cp in cps:
                cp.wait()

    anyspec = pl.BlockSpec(memory_space=pl.ANY)
    dsts = [sched.dst[k] for k in keys]
    kwargs = dict(grid=grid) if grid else {}
    res = pl.pallas_call(
        wrapped, name=name, in_specs=list(in_specs) + [anyspec] * (2 * nk), out_specs=list(out_specs) + [anyspec] * nk,
        out_shape=list(out_shape) + [jax.ShapeDtypeStruct(d.shape, d.dtype) for d in dsts],
        input_output_aliases={**(aliases or {}), **{n_in + nk + i: n_out + i for i in range(nk)}},
        scratch_shapes=list(scratch_shapes) + [pltpu.SemaphoreType.DMA((7 * npc,)), pltpu.SemaphoreType.DMA((7 * npc,)),
                                               pltpu.SemaphoreType.DMA((npc,))],
        compiler_params=pltpu.CompilerParams(dimension_semantics=("arbitrary",) * len(grid) if grid else None,
                                             vmem_limit_bytes=VMEM_LIMIT, has_side_effects=True),
        **kwargs,
    )(*ins, *[sched.src[k] for k in keys], *dsts)
    for i, k in enumerate(keys):
        sched.dst[k] = res[n_out + i]
    return list(res[:n_out])


def _exchange_pieces(sched, pieces, *, name):
    _carry_call(sched, pieces, None, name=name, grid=(), in_specs=[], out_specs=[], out_shape=[], scratch_shapes=[],
                ins=[])


def _pcall(body, *, name, grid, in_specs, out_specs, out_shape, scratch_shapes=(), sem=None, into=None):
    single = not isinstance(out_shape, (list, tuple))
    out_shape_l = [out_shape] if single else list(out_shape)
    out_specs_l = [out_specs] if single else list(out_specs)
    into = into or {}

    def run(*ins):
        n0, nb = len(ins), len(into)
        specs = list(in_specs) + [pl.BlockSpec(memory_space=pl.ANY)] * nb
        aliases = {n0 + k: oi for k, oi in enumerate(into)}
        for oi, buf in into.items():
            out_shape_l[oi] = jax.ShapeDtypeStruct(buf.shape, buf.dtype)
        kbody = (lambda *refs: body(*refs[:n0], *refs[n0 + nb:])) if nb else body
        args = list(ins) + list(into.values())
        pieces = _SCHED.take(name) if _SCHED is not None else []
        if pieces:
            res = _carry_call(_SCHED, pieces, kbody, name=name, grid=grid, in_specs=specs, out_specs=out_specs_l,
                              out_shape=out_shape_l, scratch_shapes=list(scratch_shapes), ins=args, aliases=aliases)
        else:
            res = pl.pallas_call(kbody, name=name, grid=grid, in_specs=specs, out_specs=out_specs_l,
                                 out_shape=out_shape_l, scratch_shapes=list(scratch_shapes),
                                 input_output_aliases=aliases, compiler_params=_cparams(sem))(*args)
        return res[0] if single else list(res)

    return run


def _pick(dim, cands):
    for c in cands:
        if dim % c == 0:
            return c
    return dim


def _mm_tiles(m, n, k, a_bytes, b_bytes, o_bytes, has_add):
    tm = _pick(m, (1024, 512, 256, 128))
    tn = _pick(n, (1024, 1792, 512, 256, 128))
    for tk in (3584, 2048, 1792, 1024, 896, 512, 256, 128):
        if k % tk:
            continue
        need = 2 * (tm * tk * a_bytes + tk * tn * b_bytes + tm * tn * (o_bytes + (4 if has_add else 0)))
        need += tm * tn * 4 if k // tk > 1 else 0
        if need <= MM_VMEM_BUDGET:
            return tm, tn, tk
    return tm, tn, _pick(k, (128,))


def _mm(a, b, *, ta=False, tb=False, out_dtype=F32, add=None, post=None, name):
    m, k = (a.shape[1], a.shape[0]) if ta else a.shape
    k2, n = (b.shape[1], b.shape[0]) if tb else b.shape
    assert k == k2, (a.shape, b.shape, ta, tb)
    pfn, pins, pdts = post if post is not None else (None, [], [out_dtype])
    o_bytes = sum(jnp.dtype(dt).itemsize for dt in pdts) + sum(e.dtype.itemsize for e in pins)
    tm, tn, tk = _mm_tiles(m, n, k, a.dtype.itemsize, b.dtype.itemsize, o_bytes, add is not None)
    assert m % tm == 0 and n % tn == 0 and k % tk == 0, (m, n, k, tm, tn, tk)
    nk = k // tk
    a_spec = pl.BlockSpec((tk, tm), lambda i, j, kk: (kk, i)) if ta else pl.BlockSpec((tm, tk), lambda i, j, kk: (i, kk))
    b_spec = pl.BlockSpec((tn, tk), lambda i, j, kk: (j, kk)) if tb else pl.BlockSpec((tk, tn), lambda i, j, kk: (kk, j))
    o_spec = pl.BlockSpec((tm, tn), lambda i, j, kk: (i, j))
    dims = (((0 if ta else 1,), (1 if tb else 0,)), ((), ()))
    has_add = add is not None

    nx, no = (1 if has_add else 0) + len(pins), len(pdts)

    def body(*refs):
        a_ref, b_ref = refs[:2]
        x_refs = refs[2:2 + nx]
        o_refs = refs[2 + nx:2 + nx + no]
        part = lax.dot_general(a_ref[...].astype(BF16), b_ref[...].astype(BF16), dims, preferred_element_type=F32)

        def finish(r):
            if has_add:
                r = r + x_refs[0][...]
            outs = pfn(r, *[x[...] for x in x_refs[1 if has_add else 0:]]) if pfn else (r,)
            for o_ref, v in zip(o_refs, outs):
                o_ref[...] = v.astype(o_ref.dtype)

        if nk == 1:
            finish(part)
            return
        acc_ref = refs[-1]
        kk = pl.program_id(2)

        @pl.when(kk == 0)
        def _():
            acc_ref[...] = part

        @pl.when(kk > 0)
        def _():
            acc_ref[...] += part

        @pl.when(kk == nk - 1)
        def _():
            finish(acc_ref[...])

    ins = [a, b] + ([add] if has_add else []) + list(pins)
    specs = [a_spec, b_spec] + [o_spec] * nx
    res = _pcall(
        body, name=name, grid=(m // tm, n // tn, nk), in_specs=specs, out_specs=[o_spec] * no,
        out_shape=[jax.ShapeDtypeStruct((m, n), dt) for dt in pdts],
        scratch_shapes=[pltpu.VMEM((tm, tn), F32)] if nk > 1 else [],
        sem=("parallel", "parallel", "arbitrary"),
    )(*ins)
    return res if post is not None else res[0]


def _rowcall(fn, rows, pars, row_outs, par_outs, *, tile, name):
    t = rows[0][0].shape[0]
    tile = min(tile, t)
    assert t % tile == 0
    nr, npar, nro, npo = len(rows), len(pars), len(row_outs), len(par_outs)
    in_specs = [pl.BlockSpec((tile, c), functools.partial(lambda i, cb: (i, cb), cb=cb)) for (_, c, cb) in rows]
    in_specs += [pl.BlockSpec(p.shape, lambda i: (0, 0)) for p in pars]
    into = {k: ro[2] for k, ro in enumerate(row_outs) if len(ro) == 4}
    out_specs = [pl.BlockSpec((tile, ro[0]), functools.partial(lambda i, cb: (i, cb), cb=ro[3] if len(ro) == 4 else 0))
                 for ro in row_outs]
    out_specs += [pl.BlockSpec(s, lambda i: (0, 0)) for s in par_outs]
    out_shape = [jax.ShapeDtypeStruct((t, ro[0]), ro[1]) for ro in row_outs]
    out_shape += [jax.ShapeDtypeStruct(s, F32) for s in par_outs]

    def body(*refs):
        rv = [r[...] for r in refs[:nr]]
        pv = [r[...] for r in refs[nr:nr + npar]]
        ro_refs = refs[nr + npar:nr + npar + nro]
        po_refs = refs[nr + npar + nro:]
        ro, po = fn(rv, pv)
        for ref, v in zip(ro_refs, ro):
            ref[...] = v.astype(ref.dtype)
        if npo:
            @pl.when(pl.program_id(0) == 0)
            def _():
                for ref in po_refs:
                    ref[...] = jnp.zeros_like(ref)
            for ref, v in zip(po_refs, po):
                ref[...] += v

    res = _pcall(
        body, name=name, grid=(t // tile,), in_specs=in_specs, out_specs=out_specs, out_shape=out_shape,
        sem=("arbitrary",), into=into,
    )(*[r[0] for r in rows], *pars)
    return list(res[:nro]), list(res[nro:])


def _map2d(fn, ins, out_dtype, *, name, tile=256, cw=2048):
    t, w = ins[0].shape
    tile, cw = min(tile, t), min(cw, w)
    assert t % tile == 0 and w % cw == 0

    def body(*refs):
        refs[-1][...] = fn(*[r[...] for r in refs[:-1]]).astype(out_dtype)

    spec = pl.BlockSpec((tile, cw), lambda i, j: (i, j))
    return pl.pallas_call(
        body, name=name, grid=(t // tile, w // cw), in_specs=[spec] * len(ins), out_specs=spec,
        out_shape=jax.ShapeDtypeStruct((t, w), out_dtype), compiler_params=_cparams(("parallel", "parallel")),
    )(*ins)


def _exchange(srcs, *, scatter, name):
    n = len(srcs)
    out_shape = [jax.ShapeDtypeStruct(s.shape if scatter else (N_DEV,) + s.shape, s.dtype) for s in srcs]

    def body(*refs):
        src_refs, out_refs = refs[:n], refs[n:2 * n]
        send_sems, recv_sems, loc_sems = refs[2 * n:]
        x, y, c = lax.axis_index("x"), lax.axis_index("y"), lax.axis_index("c")
        me = 4 * x + 2 * y + c
        copies = []
        for a in range(n):
            for j in range(1, N_DEV):
                px = (1 - x) if (j >> 2) & 1 else x
                py = (1 - y) if (j >> 1) & 1 else y
                pc = (1 - c) if j & 1 else c
                src = src_refs[a].at[4 * px + 2 * py + pc] if scatter else src_refs[a]
                cp = pltpu.make_async_remote_copy(
                    src_ref=src, dst_ref=out_refs[a].at[me], send_sem=send_sems.at[a * 7 + j - 1],
                    recv_sem=recv_sems.at[a * 7 + j - 1], device_id=(px, py, pc), device_id_type=MESH)
                cp.start()
                copies.append(cp)
            src = src_refs[a].at[me] if scatter else src_refs[a]
            cp = pltpu.make_async_copy(src, out_refs[a].at[me], loc_sems.at[a])
            cp.start()
            copies.append(cp)
        for cp in copies:
            cp.wait()

    anyspec = pl.BlockSpec(memory_space=pl.ANY)
    return pl.pallas_call(
        body, name=name, in_specs=[anyspec] * n, out_specs=[anyspec] * n, out_shape=out_shape,
        scratch_shapes=[pltpu.SemaphoreType.DMA((7 * n,)), pltpu.SemaphoreType.DMA((7 * n,)),
                        pltpu.SemaphoreType.DMA((n,))],
        compiler_params=pltpu.CompilerParams(has_side_effects=True),
    )(*srcs)


def _row_tile(r, c, budget_elems=256 * 1024):
    tr = r
    while tr * c > budget_elems and tr % 16 == 0:
        tr //= 2
    return tr


def _adamw(w, recvs, m, v, *, name):
    r, c = w.shape
    nl = len(recvs)
    ns, rl = recvs[0].shape[:2]
    assert rl * nl == r
    tr = _row_tile(rl, c, budget_elems=(1024 * 1024) // max(nl, 4))
    nt = rl // tr
    bc1 = 1.0 / (1.0 - ADAM_B1 ** ADAM_STEP)
    bc2 = 1.0 / (1.0 - ADAM_B2 ** ADAM_STEP)

    def body(*refs):
        w_ref, m_ref, v_ref = refs[:3]
        r_refs = refs[3:3 + nl]
        g_ref, d_ref, mo_ref, vo_ref = refs[3 + nl:]
        i = pl.program_id(0)
        for k in range(nl):
            @pl.when(i // nt == k)
            def _(k=k):
                g = r_refs[k][0].astype(F32)
                for s in range(1, ns):
                    g = g + r_refs[k][s].astype(F32)
                g_ref[...] = g
        g = g_ref[...]
        mn = ADAM_B1 * m_ref[...] + (1.0 - ADAM_B1) * g
        vn = ADAM_B2 * v_ref[...] + (1.0 - ADAM_B2) * (g * g)
        mo_ref[...] = mn
        vo_ref[...] = vn
        d_ref[...] = -ADAM_LR * ((mn * bc1) / (jnp.sqrt(vn * bc2) + ADAM_EPS) + ADAM_WD * w_ref[...])

    spec = pl.BlockSpec((tr, c), lambda i: (i, 0))
    rspecs = [pl.BlockSpec((ns, tr, c), functools.partial(lambda i, k: (0, jnp.clip(i - k * nt, 0, nt - 1), 0), k=k))
              for k in range(nl)]
    return _pcall(
        body, name=name, grid=(r // tr,), in_specs=[spec, spec, spec] + rspecs,
        out_specs=[spec] * 4, out_shape=[jax.ShapeDtypeStruct((r, c), F32)] * 4, sem=("arbitrary",),
    )(w, m, v, *recvs)


def _sum_slots(recv, *, name):
    ns, r, c = recv.shape

    def body(r_ref, o_ref):
        g = r_ref[0]
        for s in range(1, ns):
            g = g + r_ref[s]
        o_ref[...] = g

    return pl.pallas_call(body, name=name, out_shape=jax.ShapeDtypeStruct((r, c), F32))(recv)


def _rms(x, g):
    return x * lax.rsqrt(jnp.mean(x * x, axis=-1, keepdims=True) + EPS) * g


def _silu(x):
    return x * jax.nn.sigmoid(x)


def _merge_f(a, b, c, g0, g1, g2):
    return jax.nn.sigmoid(g0) * a + jax.nn.sigmoid(g1) * b + jax.nn.sigmoid(g2) * c


def _ssd_post_f(yf, yb, xs, z, dskip, gnorm):
    y = (yf + yb + dskip * xs) * _silu(z)
    return _rms(y, gnorm)


def _combine_f(o0, o1, o2, l0, l1, l2):
    m = jnp.maximum(jnp.maximum(l0, l1), l2)
    e0, e1, e2 = jnp.exp(l0 - m), jnp.exp(l1 - m), jnp.exp(l2 - m)
    return (e0 * o0 + e1 * o1 + e2 * o2) / (e0 + e1 + e2)


def _key_window(ln, hw):
    return min(ln, QBLK + 2 * hw)


def _key_start(i, ln, hw):
    return pl.multiple_of(jnp.clip(i * QBLK - hw, 0, ln - _key_window(ln, hw)), 64)


def _band_mask(qs, ks, kwin, hw):
    qpos = qs + lax.broadcasted_iota(jnp.int32, (QBLK, kwin), 0)
    kpos = ks + lax.broadcasted_iota(jnp.int32, (QBLK, kwin), 1)
    return jnp.abs(qpos - kpos) <= hw


def _attn_block(q, k3, v3, sk, valid, *, has_sink):
    s = lax.dot_general(q.astype(BF16), k3.astype(BF16), (((1,), (1,)), ((), ())),
                        preferred_element_type=F32) * (HEAD_DIM ** -0.5)
    s = jnp.where(valid, s, NEG_INF)
    m = jnp.max(s, axis=-1, keepdims=True)
    if has_sink:
        m = jnp.maximum(m, sk)
    m = lax.stop_gradient(m)
    e = jnp.exp(s - m)
    l = jnp.sum(e, axis=-1, keepdims=True)
    if has_sink:
        l = l + jnp.exp(sk - m)
    o = jnp.dot(e.astype(BF16), v3.astype(BF16), preferred_element_type=F32) / l
    return o, m + jnp.log(l)


def _ssd_chunk(state, xs, bm, cm, dtr, dtr_t, bias, bias_t, alog, alog_t, *, reverse):
    t = xs.shape[0]
    hg = dtr.shape[1]
    hp = xs.shape[1]
    p = hp // hg
    dt = jax.nn.softplus(dtr + bias)
    dt_t = jax.nn.softplus(dtr_t + bias_t)
    dta = dt * (-jnp.exp(alog))
    dta_t = dt_t * (-jnp.exp(alog_t))
    li = lax.broadcasted_iota(jnp.int32, (t, t), 0)
    si = lax.broadcasted_iota(jnp.int32, (t, t), 1)
    tri = (li <= si) if reverse else (li >= si)
    trif = tri.astype(F32)
    cs = jnp.dot(trif, dta, precision=HI, preferred_element_type=F32)
    cs_t = lax.dot_general(dta_t, trif, (((1,), (1,)), ((), ())), precision=HI,
                           preferred_element_type=F32)
    total = jnp.sum(dta, axis=0, keepdims=True)
    cb = lax.dot_general(cm.astype(BF16), bm.astype(BF16), (((1,), (1,)), ((), ())),
                         preferred_element_type=F32)
    lane_h = lax.broadcasted_iota(jnp.int32, (1, hp), 1) // p
    col_h = lax.broadcasted_iota(jnp.int32, (1, hg), 1)
    row_h = lax.broadcasted_iota(jnp.int32, (hg, 1), 0)
    dt_x = jnp.zeros((t, hp), F32)
    ecs_x = jnp.zeros((t, hp), F32)
    ds_x = jnp.zeros((t, hp), F32)
    etot_x = jnp.zeros((1, hp), F32)
    decays, masks = [], []
    for h in range(hg):
        oh = (col_h == h).astype(F32)
        oh_t = (row_h == h).astype(F32)
        mk = (lane_h == h).astype(F32)
        dt_h = jnp.sum(dt * oh, axis=1, keepdims=True)
        cs_h = jnp.sum(cs * oh, axis=1, keepdims=True)
        cst_h = jnp.sum(cs_t * oh_t, axis=0, keepdims=True)
        tot_h = jnp.sum(total * oh, axis=1, keepdims=True)
        dt_x = dt_x + dt_h * mk
        ecs_x = ecs_x + jnp.exp(cs_h) * mk
        ds_x = ds_x + jnp.exp(tot_h - cs_h) * mk
        etot_x = etot_x + jnp.exp(tot_h) * mk
        decays.append(jnp.exp(jnp.where(tri, cs_h - cst_h, -jnp.inf)))
        masks.append(mk)
    xdt = xs * dt_x
    y = jnp.dot(cm.astype(BF16), state.astype(BF16), preferred_element_type=F32) * ecs_x
    for h in range(hg):
        y = y + jnp.dot((cb * decays[h]).astype(BF16), (xdt * masks[h]).astype(BF16),
                        preferred_element_type=F32)
    st_new = lax.dot_general(bm.astype(BF16), (xdt * ds_x).astype(BF16), (((0,), (0,)), ((), ())),
                             preferred_element_type=F32)
    return y, state * etot_x + st_new


def _rope_tables(seq):
    half = ROPE_DIM // 2
    inv = ROPE_THETA ** (-jnp.arange(0, ROPE_DIM, 2, dtype=F32) / ROPE_DIM)
    ang = jnp.arange(seq, dtype=F32)[:, None] * inv[None, :]
    cos, sin = jnp.cos(ang), jnp.sin(ang)
    rest = HEAD_DIM - ROPE_DIM
    c = jnp.concatenate([cos, cos, jnp.ones((seq, rest), F32)], axis=1)
    a = jnp.concatenate([-sin, jnp.zeros((seq, HEAD_DIM - half), F32)], axis=1)
    b = jnp.concatenate([jnp.zeros((seq, half), F32), sin, jnp.zeros((seq, rest), F32)], axis=1)
    return c, a, b


def _rope(src, tabs, *, col0, width, seq, group, inverse, out_dtype, name, into=None, out_col0=0):
    t = src.shape[0]
    half = ROPE_DIM // 2
    nhb = 6 if all(v % (6 * HEAD_DIM) == 0 for v in (width, col0, out_col0)) else 3
    cw, tile = nhb * HEAD_DIM, 512
    assert width % cw == 0 and col0 % cw == 0 and out_col0 % cw == 0 and seq % tile == 0 and t % tile == 0
    ns = seq // tile

    def body(x_ref, c_ref, a_ref, b_ref, o_ref):
        jb = pl.program_id(1)
        c, a, b = c_ref[...], a_ref[...], b_ref[...]
        for hh in range(nhb):
            xv = x_ref[:, hh * HEAD_DIM:(hh + 1) * HEAD_DIM].astype(F32)
            if inverse:
                yv = xv * c + pltpu.roll(xv * a, half, 1) + pltpu.roll(xv * b, HEAD_DIM - half, 1)
            else:
                yv = xv * c + pltpu.roll(xv, HEAD_DIM - half, 1) * a + pltpu.roll(xv, half, 1) * b
            if group:
                keep = ((jb * nhb + hh) % group) == (group - 1)
                yv = jnp.where(keep, xv, yv)
            o_ref[:, hh * HEAD_DIM:(hh + 1) * HEAD_DIM] = yv.astype(o_ref.dtype)

    tspec = pl.BlockSpec((tile, HEAD_DIM), lambda i, j: (i % ns, 0))
    return _pcall(
        body, name=name, grid=(t // tile, width // cw),
        in_specs=[pl.BlockSpec((tile, cw), lambda i, j: (i, col0 // cw + j)), tspec, tspec, tspec],
        out_specs=pl.BlockSpec((tile, cw), lambda i, j: (i, out_col0 // cw + j)),
        out_shape=jax.ShapeDtypeStruct((t, width), out_dtype),
        sem=("parallel", "parallel"), into=None if into is None else {0: into},
    )(src, *tabs)


def _shift_rows(x, d, tpos):
    if d == 0:
        return x
    s = x.shape[0]
    y = pltpu.roll(x, (-d) % s, 0)
    ok = (tpos + d >= 0) & (tpos + d < s)
    return jnp.where(ok, y, 0.0)


def _conv_fwd(p, w8, bias, *, col0, chans, batch, seq, name):
    cb = 256
    assert chans % cb == 0 and col0 % cb == 0
    pad = (CONV_WIDTH - 1) // 2

    def body(x_ref, w_ref, b_ref, o_ref):
        x = x_ref[...]
        tpos = lax.broadcasted_iota(jnp.int32, x.shape, 0)
        acc = jnp.broadcast_to(b_ref[...], x.shape)
        for k in range(CONV_WIDTH):
            acc = acc + w_ref[k:k + 1, :] * _shift_rows(x, k - pad, tpos)
        o_ref[...] = _silu(acc)

    return _pcall(
        body, name=name, grid=(chans // cb, batch),
        in_specs=[pl.BlockSpec((seq, cb), lambda j, b: (b, col0 // cb + j)),
                  pl.BlockSpec((8, cb), lambda j, b: (0, j)), pl.BlockSpec((1, cb), lambda j, b: (0, j))],
        out_specs=pl.BlockSpec((seq, cb), lambda j, b: (b, j)),
        out_shape=jax.ShapeDtypeStruct((batch * seq, chans), F32),
        sem=("parallel", "arbitrary"),
    )(p, w8, bias)


def _conv_bwd(p, w8, bias, dus, *, col0, ch0, chans, batch, seq, into, out_col0, name):
    cb = 256 if chans % 256 == 0 and ch0 % 256 == 0 else 128
    assert chans % cb == 0 and ch0 % cb == 0 and col0 % cb == 0 and out_col0 % cb == 0
    pad = (CONV_WIDTH - 1) // 2
    ndu = len(dus)

    def body(*refs):
        x_ref, w_ref, b_ref = refs[:3]
        du_refs = refs[3:3 + ndu]
        dx_ref, dw_ref, db_ref = refs[3 + ndu:]
        du = du_refs[0][...]
        for r in du_refs[1:]:
            du = du + r[...]
        _conv_bwd_block(x_ref, w_ref, b_ref, du, dx_ref, dw_ref, db_ref)

    c0 = (col0 + ch0) // cb
    return _pcall(
        body, name=name, grid=(chans // cb, batch),
        in_specs=[pl.BlockSpec((seq, cb), lambda j, b: (b, c0 + j)),
                  pl.BlockSpec((8, cb), lambda j, b: (0, ch0 // cb + j)),
                  pl.BlockSpec((1, cb), lambda j, b: (0, ch0 // cb + j))]
        + [pl.BlockSpec((seq, cb), lambda j, b: (b, j))] * ndu,
        out_specs=[pl.BlockSpec((seq, cb), lambda j, b: (b, (out_col0 + ch0) // cb + j)),
                   pl.BlockSpec((8, cb), lambda j, b: (0, j)), pl.BlockSpec((1, cb), lambda j, b: (0, j))],
        out_shape=[jax.ShapeDtypeStruct(into.shape, into.dtype), jax.ShapeDtypeStruct((8, chans), F32),
                   jax.ShapeDtypeStruct((1, chans), F32)],
        sem=("parallel", "arbitrary"), into={0: into},
    )(p, w8, bias, *dus)


def _conv_bwd_block(x_ref, w_ref, b_ref, du, dx_ref, dw_ref, db_ref):
    pad = (CONV_WIDTH - 1) // 2
    x = x_ref[...]
    tpos = lax.broadcasted_iota(jnp.int32, x.shape, 0)
    acc = jnp.broadcast_to(b_ref[...], x.shape)
    xs = []
    for k in range(CONV_WIDTH):
        xs.append(_shift_rows(x, k - pad, tpos))
        acc = acc + w_ref[k:k + 1, :] * xs[k]
    sg = jax.nn.sigmoid(acc)
    dacc = du * (sg * (1.0 + acc * (1.0 - sg)))
    dx = jnp.zeros_like(x)
    for k in range(CONV_WIDTH):
        dx = dx + w_ref[k:k + 1, :] * _shift_rows(dacc, pad - k, tpos)
    dx_ref[...] = dx.astype(dx_ref.dtype)

    @pl.when(pl.program_id(1) == 0)
    def _():
        dw_ref[...] = jnp.zeros_like(dw_ref)
        db_ref[...] = jnp.zeros_like(db_ref)

    for k in range(CONV_WIDTH):
        dw_ref[k:k + 1, :] += jnp.sum(dacc * xs[k], axis=0, keepdims=True)
    db_ref[...] += jnp.sum(dacc, axis=0, keepdims=True)


def _scan_gpb(groups):
    return 8 if groups % 8 == 0 else 4 if groups % 4 == 0 else 2 if groups % 2 == 0 else 1


def _scan_specs(batch, nc, groups, hg, inner, reverse_order):
    gpb = _scan_gpb(groups)
    t, n, hp = SSD_CHUNK, SSD_STATE, hg * SSD_HEAD_DIM
    ncb = inner // (gpb * n)
    ngb = groups // gpb

    def row(b, c):
        return b * nc + ((nc - 1 - c) if reverse_order else c)

    return dict(
        xs=pl.BlockSpec((t, gpb * hp), lambda g, b, c: (row(b, c), g)),
        bm=pl.BlockSpec((t, gpb * n), lambda g, b, c: (row(b, c), ncb + g)),
        cm=pl.BlockSpec((t, gpb * n), lambda g, b, c: (row(b, c), ncb + ngb + g)),
        dtr=pl.BlockSpec((gpb, t, hg), lambda g, b, c: (g, row(b, c), 0)),
        dtr_t=pl.BlockSpec((gpb, hg, t), lambda g, b, c: (g, 0, row(b, c))),
        par=pl.BlockSpec((gpb, 1, hg), lambda g, b, c: (g, 0, 0)),
        par_t=pl.BlockSpec((gpb, hg, 1), lambda g, b, c: (g, 0, 0)),
        y=pl.BlockSpec((t, gpb * hp), lambda g, b, c: (row(b, c), g)),
        nrow=pl.BlockSpec((t, gpb * n), lambda g, b, c: (row(b, c), g)),
        st=pl.BlockSpec((gpb, None, n, hp), lambda g, b, c: (g, row(b, c), 0, 0)),
    )


def _scan_fwd(u, dtr, dtr_t, bias, bias_t, alog, alog_t, *, batch, seq, inner, reverse, name):
    groups, hg = dtr.shape[0], dtr.shape[2]
    nc = seq // SSD_CHUNK
    hp = hg * SSD_HEAD_DIM
    sp = _scan_specs(batch, nc, groups, hg, inner, reverse)
    gpb, n = _scan_gpb(groups), SSD_STATE

    def body(xs_ref, bm_ref, cm_ref, dtr_ref, dtrt_ref, b_ref, bt_ref, a_ref, at_ref, y_ref, st_ref, state):
        @pl.when(pl.program_id(2) == 0)
        def _():
            state[...] = jnp.zeros_like(state)

        for k in range(gpb):
            xc, nc_ = slice(k * hp, (k + 1) * hp), slice(k * n, (k + 1) * n)
            st_in = state[k]
            st_ref[k] = st_in
            y, st_out = _ssd_chunk(st_in, xs_ref[:, xc], bm_ref[:, nc_], cm_ref[:, nc_], dtr_ref[k], dtrt_ref[k],
                                   b_ref[k], bt_ref[k], a_ref[k], at_ref[k], reverse=reverse)
            y_ref[:, xc] = y
            state[k] = st_out

    return _pcall(
        body, name=name, grid=(groups // gpb, batch, nc),
        in_specs=[sp["xs"], sp["bm"], sp["cm"], sp["dtr"], sp["dtr_t"], sp["par"], sp["par_t"], sp["par"], sp["par_t"]],
        out_specs=[sp["y"], sp["st"]],
        out_shape=[jax.ShapeDtypeStruct((batch * seq, inner), F32),
                   jax.ShapeDtypeStruct((groups, batch * nc, SSD_STATE, hp), F32)],
        scratch_shapes=[pltpu.VMEM((gpb, SSD_STATE, hp), F32)],
        sem=("parallel", "arbitrary", "arbitrary"),
    )(u, u, u, dtr, dtr_t, bias, bias_t, alog, alog_t)


def _scan_bwd(u, dtr, dtr_t, bias, bias_t, alog, alog_t, st, dy, *, batch, seq, inner, reverse, name):
    groups, hg = dtr.shape[0], dtr.shape[2]
    nc = seq // SSD_CHUNK
    hp = hg * SSD_HEAD_DIM
    t = batch * seq
    sp = _scan_specs(batch, nc, groups, hg, inner, not reverse)
    f = functools.partial(_ssd_chunk, reverse=reverse)
    gpb, n = _scan_gpb(groups), SSD_STATE

    def body(xs_ref, bm_ref, cm_ref, dtr_ref, dtrt_ref, b_ref, bt_ref, a_ref, at_ref, st_ref, dy_ref,
             dxs_ref, dbm_ref, dcm_ref, ddtr_ref, ddtrt_ref, db_ref, dbt_ref, da_ref, dat_ref, dstate):
        first = (pl.program_id(1) == 0) & (pl.program_id(2) == 0)

        @pl.when(pl.program_id(2) == 0)
        def _():
            dstate[...] = jnp.zeros_like(dstate)

        @pl.when(first)
        def _():
            for r in (db_ref, dbt_ref, da_ref, dat_ref):
                r[...] = jnp.zeros_like(r)

        for k in range(gpb):
            xc, nc_ = slice(k * hp, (k + 1) * hp), slice(k * n, (k + 1) * n)
            _, vjp = jax.vjp(f, st_ref[k], xs_ref[:, xc], bm_ref[:, nc_], cm_ref[:, nc_], dtr_ref[k], dtrt_ref[k],
                             b_ref[k], bt_ref[k], a_ref[k], at_ref[k])
            dst, dxs, dbm, dcm, ddtr, ddtrt, db, dbt, da, dat = vjp((dy_ref[:, xc], dstate[k]))
            dstate[k] = dst
            dxs_ref[:, xc] = dxs
            dbm_ref[:, nc_] = dbm
            dcm_ref[:, nc_] = dcm
            ddtr_ref[k] = ddtr
            ddtrt_ref[k] = ddtrt
            db_ref[k] += db
            dbt_ref[k] += dbt
            da_ref[k] += da
            dat_ref[k] += dat

    gn = groups * SSD_STATE
    return _pcall(
        body, name=name, grid=(groups // gpb, batch, nc),
        in_specs=[sp["xs"], sp["bm"], sp["cm"], sp["dtr"], sp["dtr_t"], sp["par"], sp["par_t"], sp["par"], sp["par_t"],
                  sp["st"], sp["y"]],
        out_specs=[sp["y"], sp["nrow"], sp["nrow"], sp["dtr"], sp["dtr_t"], sp["par"], sp["par_t"], sp["par"], sp["par_t"]],
        out_shape=[jax.ShapeDtypeStruct((t, inner), F32), jax.ShapeDtypeStruct((t, gn), F32),
                   jax.ShapeDtypeStruct((t, gn), F32), jax.ShapeDtypeStruct(dtr.shape, F32),
                   jax.ShapeDtypeStruct(dtr_t.shape, F32), jax.ShapeDtypeStruct(bias.shape, F32),
                   jax.ShapeDtypeStruct(bias_t.shape, F32), jax.ShapeDtypeStruct(alog.shape, F32),
                   jax.ShapeDtypeStruct(alog_t.shape, F32)],
        scratch_shapes=[pltpu.VMEM((gpb, SSD_STATE, hp), F32)],
        sem=("parallel", "arbitrary", "arbitrary"),
    )(u, u, u, dtr, dtr_t, bias, bias_t, alog, alog_t, st, dy)


def _heads_per_step(nh, nb, nbw, cb0):
    hps = min(nh, max(1, 16 // nb))
    while nh % hps or nbw % hps or cb0 % hps:
        hps -= 1
    return hps


def _attn_load(ref, col, blk):
    return ref[pl.ds(pl.multiple_of(blk * QBLK, QBLK), QBLK), col * HEAD_DIM:(col + 1) * HEAD_DIM].astype(F32)


def _lane0(row):
    lane = lax.broadcasted_iota(jnp.int32, row.shape, 1)
    return jnp.sum(jnp.where(lane == 0, row, 0.0), axis=1, keepdims=True)


def _attn_fwd(rq, sinkx, *, batch, seq, dil, nbw, cb0, nh, rep, hw, want_lse, out_dtype, name):
    t, w = rq.shape
    ln = seq // dil
    nb = ln // QBLK
    bw = (rep + 2) * HEAD_DIM
    ow = nh * rep * HEAD_DIM
    has_sink = sinkx is not None
    rq3 = rq.reshape(batch, ln, dil * w)
    kwin = _key_window(ln, hw)
    hps = _heads_per_step(nh, nb, nbw, cb0)
    f = functools.partial(_attn_block, has_sink=has_sink)

    def body(*refs):
        if has_sink:
            blk_ref, sink_ref = refs[:2]
            outs = refs[2:]
        else:
            blk_ref, sink_ref = refs[0], None
            outs = refs[1:]
        o_ref = outs[0]
        lse_ref = outs[1] if want_lse else None
        g = pl.program_id(2)

        def qblock(i, carry):
            ks = _key_start(i, ln, hw)
            valid = _band_mask(i * QBLK, ks, kwin, hw)
            rows = pl.ds(pl.multiple_of(i * QBLK, QBLK), QBLK)
            for hh in range(hps):
                hb = hh * (rep + 2)
                k3 = blk_ref[pl.ds(ks, kwin), (hb + rep) * HEAD_DIM:(hb + rep + 1) * HEAD_DIM].astype(F32)
                v3 = blk_ref[pl.ds(ks, kwin), (hb + rep + 1) * HEAD_DIM:(hb + rep + 2) * HEAD_DIM].astype(F32)
                for r in range(rep):
                    sk = _lane0(sink_ref[pl.ds((g * hps + hh) * rep + r, 1), :]) if has_sink else None
                    o, lse = f(_attn_load(blk_ref, hb + r, i), k3, v3, sk, valid)
                    oc = slice((hh * rep + r) * HEAD_DIM, (hh * rep + r + 1) * HEAD_DIM)
                    o_ref[rows, oc] = o.astype(o_ref.dtype)
                    if want_lse:
                        lse_ref[rows, oc] = jnp.broadcast_to(lse, o.shape)
            return carry

        if nb == 1:
            qblock(0, 0)
        else:
            lax.fori_loop(0, nb, qblock, 0, unroll=2 if nb % 2 == 0 else 1)

    in_specs = [pl.BlockSpec((None, ln, hps * bw), lambda b, r, h: (b, 0, (r * nbw + cb0) // hps + h))]
    ins = [rq3]
    if has_sink:
        in_specs.append(pl.BlockSpec(sinkx.shape, lambda b, r, h: (0, 0)))
        ins.append(sinkx)
    ospec = pl.BlockSpec((None, ln, hps * rep * HEAD_DIM), lambda b, r, h: (b, 0, r * (nh // hps) + h))
    out_shape = [jax.ShapeDtypeStruct((batch, ln, dil * ow), out_dtype)]
    out_specs = [ospec]
    if want_lse:
        out_shape.append(jax.ShapeDtypeStruct((batch, ln, dil * ow), F32))
        out_specs.append(ospec)
    res = _pcall(
        body, name=name, grid=(batch, dil, nh // hps), in_specs=in_specs, out_specs=out_specs, out_shape=out_shape,
        sem=("parallel", "parallel", "parallel"),
    )(*ins)
    return [r.reshape(t, ow) for r in res]


def _attn_bwd(rq, sinkx, do, dlse, *, batch, seq, dil, nbw, cb0, nh, rep, hw, name):
    t, w = rq.shape
    ln = seq // dil
    nb = ln // QBLK
    bw = (rep + 2) * HEAD_DIM
    ow = nh * rep * HEAD_DIM
    has_sink = sinkx is not None
    has_lse = dlse is not None
    kwin = _key_window(ln, hw)
    hps = _heads_per_step(nh, nb, nbw, cb0)
    f = functools.partial(_attn_block, has_sink=has_sink)

    def body(*refs):
        refs = list(refs)
        blk_ref = refs.pop(0)
        sink_ref = refs.pop(0) if has_sink else None
        do_ref = refs.pop(0)
        dlse_ref = refs.pop(0) if has_lse else None
        d_ref = refs.pop(0)
        dsink_ref = refs.pop(0) if has_sink else None
        g = pl.program_id(2)
        for hh in range(hps):
            kv0 = (hh * (rep + 2) + rep) * HEAD_DIM
            d_ref[:, kv0:kv0 + 2 * HEAD_DIM] = jnp.zeros((ln, 2 * HEAD_DIM), F32)
        if has_sink:
            @pl.when((pl.program_id(0) == 0) & (pl.program_id(1) == 0) & (g == 0))
            def _():
                dsink_ref[...] = jnp.zeros_like(dsink_ref)

        def qblock(i, carry):
            ks = _key_start(i, ln, hw)
            valid = _band_mask(i * QBLK, ks, kwin, hw)
            krows = pl.ds(ks, kwin)
            rows = pl.ds(pl.multiple_of(i * QBLK, QBLK), QBLK)
            for hh in range(hps):
                hb = hh * (rep + 2)
                kc = slice((hb + rep) * HEAD_DIM, (hb + rep + 1) * HEAD_DIM)
                vc = slice((hb + rep + 1) * HEAD_DIM, (hb + rep + 2) * HEAD_DIM)
                k3 = blk_ref[krows, kc].astype(F32)
                v3 = blk_ref[krows, vc].astype(F32)
                dk3 = jnp.zeros_like(k3)
                dv3 = jnp.zeros_like(v3)
                for r in range(rep):
                    oc = slice((hh * rep + r) * HEAD_DIM, (hh * rep + r + 1) * HEAD_DIM)
                    q = _attn_load(blk_ref, hb + r, i)
                    dov = do_ref[rows, oc]
                    dl = dlse_ref[rows, oc] if has_lse else jnp.zeros_like(dov)
                    if has_sink:
                        srow_i = (g * hps + hh) * rep + r
                        srow = sink_ref[pl.ds(srow_i, 1), :]
                        _, vjp = jax.vjp(lambda q_, k_, v_, s_: f(q_, k_, v_, _lane0(s_), valid), q, k3, v3, srow)
                        dq, dk, dv, ds = vjp((dov, jnp.sum(dl, axis=1, keepdims=True)))
                        dsink_ref[pl.ds(srow_i, 1), :] += ds
                    else:
                        _, vjp = jax.vjp(lambda q_, k_, v_: f(q_, k_, v_, None, valid), q, k3, v3)
                        dq, dk, dv = vjp((dov, jnp.sum(dl, axis=1, keepdims=True)))
                    d_ref[rows, (hb + r) * HEAD_DIM:(hb + r + 1) * HEAD_DIM] = dq
                    dk3 = dk3 + dk
                    dv3 = dv3 + dv
                d_ref[krows, kc] += dk3
                d_ref[krows, vc] += dv3
            return carry

        if nb == 1:
            qblock(0, 0)
        else:
            lax.fori_loop(0, nb, qblock, 0, unroll=2 if nb % 2 == 0 else 1)

    ospec = pl.BlockSpec((None, ln, hps * rep * HEAD_DIM), lambda b, r, h: (b, 0, r * (nh // hps) + h))
    in_specs = [pl.BlockSpec((None, ln, hps * bw), lambda b, r, h: (b, 0, (r * nbw + cb0) // hps + h))]
    ins = [rq.reshape(batch, ln, dil * w)]
    if has_sink:
        in_specs.append(pl.BlockSpec(sinkx.shape, lambda b, r, h: (0, 0)))
        ins.append(sinkx)
    in_specs.append(ospec)
    ins.append(do.reshape(batch, ln, dil * ow))
    if has_lse:
        in_specs.append(ospec)
        ins.append(dlse.reshape(batch, ln, dil * ow))
    dw = nh * bw
    out_specs = [pl.BlockSpec((None, ln, hps * bw), lambda b, r, h: (b, 0, r * (nh // hps) + h))]
    out_shape = [jax.ShapeDtypeStruct((batch, ln, dil * dw), F32)]
    if has_sink:
        out_specs.append(pl.BlockSpec(sinkx.shape, lambda b, r, h: (0, 0)))
        out_shape.append(jax.ShapeDtypeStruct(sinkx.shape, F32))
    res = _pcall(
        body, name=name, grid=(batch, dil, nh // hps), in_specs=in_specs, out_specs=out_specs, out_shape=out_shape,
        sem=("arbitrary", "arbitrary", "arbitrary"),
    )(*ins)
    return [res[0].reshape(t, dw)] + list(res[1:])


def _final_loss(x, g, target, *, name):
    d = x.shape[1]

    def fn(rv, pv):
        xv, tg = rv
        y, vjp = jax.vjp(_rms, xv, pv[0])
        err = y - tg
        dx, dg = vjp(err * (1.0 / d))
        loss = 0.5 * jnp.sum(err * err) * (1.0 / d)
        return [dx], [dg, jnp.full((1, LANE), loss, F32)]

    (dx,), (dg, loss) = _rowcall(fn, [(x, d, 0), (target, d, 0)], [g], [(d, F32)], [(1, d), (1, LANE)],
                                 tile=256, name=name)
    return dx, dg, loss


class _Layout:
    def __init__(self, d_model):
        self.d = d_model
        self.inner = SSD_HEADS * SSD_HEAD_DIM
        self.gn = SSD_GROUPS * SSD_STATE
        self.xbc = self.inner + 2 * self.gn
        self.ndt = 2 * SSD_HEADS
        self.ngrp = len(DIL_PATTERNS)
        self.dilw = DIL_HEADS * HEAD_DIM
        self.rqd = 3 * self.ngrp * self.dilw
        self.rep = WIN_Q_HEADS // WIN_KV_HEADS
        self.rqw = WIN_KV_HEADS * (self.rep + 2) * HEAD_DIM
        self.qw = WIN_Q_HEADS * HEAD_DIM
        self.kw = WIN_KV_HEADS * HEAD_DIM
        self.gates = N_BRANCH * d_model
        self.n_in = self.inner + self.xbc + self.ndt + self.rqd + self.qw + 2 * self.kw + self.gates
        self.o_gates = 0
        self.o_z = self.gates
        self.o_xbc = self.o_z + self.inner
        self.o_rqd = self.o_xbc + self.xbc
        self.o_rqw = self.o_rqd + self.rqd
        self.o_dt = self.o_rqw + self.rqw
        self.dtw = -(-(self.o_dt + self.ndt) // PAD_TO) * PAD_TO - self.o_dt
        self.width = self.o_dt + self.dtw
        assert self.o_z % self.inner == 0 and self.o_xbc % 256 == 0
        assert self.o_rqd % (3 * HEAD_DIM) == 0 and self.o_rqw % (3 * HEAD_DIM) == 0 and self.dtw % LANE == 0
        assert self.o_dt % self.dtw == 0

    def split_points(self):
        sizes = (self.inner, self.xbc, self.ndt, self.rqd, self.qw, self.kw, self.kw, self.gates)
        pts, acc = [], 0
        for s in sizes:
            pts.append((acc, acc + s))
            acc += s
        return pts

    def permute_w(self, w):
        d = w.shape[0]
        z, xbc, dt, qkvd, qw, kw, vw, gates = [w[:, a:b] for a, b in self.split_points()]
        nhd = self.ngrp * DIL_HEADS
        qkvd = qkvd.reshape(d, 3, nhd, HEAD_DIM).transpose(0, 2, 1, 3).reshape(d, self.rqd)
        win = jnp.concatenate([qw.reshape(d, WIN_KV_HEADS, self.rep, HEAD_DIM),
                               kw.reshape(d, WIN_KV_HEADS, 1, HEAD_DIM),
                               vw.reshape(d, WIN_KV_HEADS, 1, HEAD_DIM)], axis=2).reshape(d, self.rqw)
        pad = jnp.zeros((d, self.dtw - self.ndt), w.dtype)
        return jnp.concatenate([gates, z, xbc, qkvd, win, dt, pad], axis=1)

    def unpermute_w(self, wp):
        d = wp.shape[0]
        gates = wp[:, :self.o_z]
        z = wp[:, self.o_z:self.o_xbc]
        xbc = wp[:, self.o_xbc:self.o_rqd]
        qkvd = wp[:, self.o_rqd:self.o_rqw]
        win = wp[:, self.o_rqw:self.o_dt].reshape(d, WIN_KV_HEADS, self.rep + 2, HEAD_DIM)
        dt = wp[:, self.o_dt:self.o_dt + self.ndt]
        nhd = self.ngrp * DIL_HEADS
        qkvd = qkvd.reshape(d, nhd, 3, HEAD_DIM).transpose(0, 2, 1, 3).reshape(d, self.rqd)
        qw = win[:, :, :self.rep].reshape(d, self.qw)
        kw = win[:, :, self.rep].reshape(d, self.kw)
        vw = win[:, :, self.rep + 1].reshape(d, self.kw)
        return jnp.concatenate([z, xbc, dt, qkvd, qw, kw, vw, gates], axis=1)


def _dt_layouts(pdt, dirn, batch_seq):
    hg = SSD_HEADS // SSD_GROUPS
    v = pdt[:, dirn * SSD_HEADS:(dirn + 1) * SSD_HEADS].reshape(batch_seq, SSD_GROUPS, hg)
    return v.transpose(1, 0, 2), v.transpose(1, 2, 0)


def _par_layouts(p):
    hg = SSD_HEADS // SSD_GROUPS
    v = p.reshape(SSD_GROUPS, hg)
    return v[:, None, :], v[:, :, None]


def _layer_fwd(x, lw, lay, tabs, batch, seq):
    d = lay.d
    sv = {"x": x}
    (h,), _ = _rowcall(lambda rv, pv: ([_rms(rv[0], pv[0])], []), [(x, d, 0)], [lw["g_mix"]], [(d, BF16)], [],
                       tile=256, name="norm_mix")
    p = _mm(h, lw["w_in"], name="proj_in")
    sv["h"], sv["p"] = h, p
    u = _conv_fwd(p, lw["conv_w8"], lw["conv_b"], col0=lay.o_xbc, chans=lay.xbc, batch=batch, seq=seq, name="conv_fwd")
    sv["u"] = u
    pdt = p[:, lay.o_dt:lay.o_dt + lay.ndt]
    ys, sv["st"], sv["dtl"] = [], [], []
    for dirn in range(2):
        dtr, dtr_t = _dt_layouts(pdt, dirn, batch * seq)
        bias, bias_t = _par_layouts(lw["dt_bias"][dirn])
        alog, alog_t = _par_layouts(lw["a_log"][dirn])
        y, st = _scan_fwd(u, dtr, dtr_t, bias, bias_t, alog, alog_t, batch=batch, seq=seq, inner=lay.inner,
                          reverse=bool(dirn), name="scan_fwd%d" % dirn)
        ys.append(y)
        sv["st"].append(st)
        sv["dtl"].append((dtr, dtr_t, bias, bias_t, alog, alog_t))
    sv["ys"] = ys
    inner = lay.inner
    (ya,), _ = _rowcall(lambda rv, pv: ([_ssd_post_f(*rv, *pv)], []),
                        [(ys[0], inner, 0), (ys[1], inner, 0), (u, inner, 0), (p, inner, lay.o_z // inner)],
                        [lw["d_skip_x"], lw["ssd_norm"]], [(inner, BF16)], [], tile=128, name="ssd_post")
    sv["ya"] = ya
    gw = 3 * lay.dilw
    rqd = [_rope(p, tabs, col0=lay.o_rqd + gi * gw, width=gw, seq=seq, group=0, inverse=False, out_dtype=BF16,
                 name="rope_dil") for gi in range(lay.ngrp)]
    rqw = _rope(p, tabs, col0=lay.o_rqw, width=lay.rqw, seq=seq, group=lay.rep + 2, inverse=False, out_dtype=BF16,
                name="rope_win")
    sv["rqd"], sv["rqw"] = rqd, rqw
    os_, ls_ = [], []
    for gi, (window, dil) in enumerate(DIL_PATTERNS):
        o, l = _attn_fwd(rqd[gi], None, batch=batch, seq=seq, dil=dil, nbw=DIL_HEADS, cb0=0,
                         nh=DIL_HEADS, rep=1, hw=window // (2 * dil), want_lse=True, out_dtype=F32,
                         name="dil_fwd%d" % gi)
        os_.append(o)
        ls_.append(l)
    sv["os"], sv["ls"] = os_, ls_
    dw = lay.dilw
    (yb,), _ = _rowcall(lambda rv, pv: ([_combine_f(*rv)], []), [(a, dw, 0) for a in os_ + ls_], [], [(dw, BF16)], [],
                        tile=256, name="dil_combine")
    sv["yb"] = yb
    (yc,) = _attn_fwd(rqw, lw["sink_x"], batch=batch, seq=seq, dil=1, nbw=WIN_KV_HEADS, cb0=0, nh=WIN_KV_HEADS,
                      rep=lay.rep, hw=WIN_HALF, want_lse=False, out_dtype=BF16, name="win_fwd")
    sv["yc"] = yc
    ma = _mm(ya, lw["w_a"], name="proj_a")
    mb = _mm(yb, lw["w_b"], name="proj_b")
    mc = _mm(yc, lw["w_c"], name="proj_c")
    sv["mabc"] = (ma, mb, mc)
    (mg,), _ = _rowcall(lambda rv, pv: ([_merge_f(*rv)], []),
                        [(ma, d, 0), (mb, d, 0), (mc, d, 0), (p, d, 0), (p, d, 1), (p, d, 2)], [], [(d, BF16)], [],
                        tile=256, name="merge")
    sv["mg"] = mg
    x1 = _mm(mg, lw["w_out"], add=x, name="proj_out")
    sv["x1"] = x1
    (hm,), _ = _rowcall(lambda rv, pv: ([_rms(rv[0], pv[0])], []), [(x1, d, 0)], [lw["g_mlp"]], [(d, BF16)], [],
                        tile=256, name="norm_mlp")
    up, act = _mm(hm, lw["w_up"], post=(lambda r: (r, jnp.square(jnp.maximum(r, 0.0))), [], [F32, BF16]),
                  name="mlp_up")
    sv["hm"], sv["up"], sv["act"] = hm, up, act
    x2 = _mm(act, lw["w_down"], add=x1, name="mlp_down")
    return x2, sv


def _layer_bwd(dxo, sv, lw, lay, tabs, batch, seq, post):
    d = lay.d
    inner = lay.inner
    gs = {}
    (dup,) = _mm(dxo, lw["w_down"], tb=True, name="mlp_down_dx",
                 post=(lambda r, a: (r * (2.0 * jnp.maximum(a, 0.0)),), [sv["up"]], [BF16]))
    post("w_down", _mm(sv["act"], dxo, ta=True, out_dtype=BF16, name="mlp_down_dw"))
    dhm = _mm(dup, lw["w_up"], tb=True, name="mlp_up_dx")
    post("w_up", _mm(sv["hm"], dup, ta=True, out_dtype=BF16, name="mlp_up_dw"))

    def norm_bwd(rv, pv):
        xv, dh, dres = rv
        _, vjp = jax.vjp(_rms, xv, pv[0])
        dx, dg = vjp(dh)
        return [dx + dres], [dg]

    (dx1,), (gs["g_mlp"],) = _rowcall(norm_bwd, [(sv["x1"], d, 0), (dhm, d, 0), (dxo, d, 0)], [lw["g_mlp"]],
                                      [(d, F32)], [(1, d)], tile=128, name="norm_mlp_bwd")
    dmg = _mm(dx1, lw["w_out"], tb=True, name="proj_out_dx")
    post("w_out", _mm(sv["mg"], dx1, ta=True, out_dtype=BF16, name="proj_out_dw"))
    ma, mb, mc = sv["mabc"]
    p = sv["p"]

    def merge_bwd(rv, pv):
        _, vjp = jax.vjp(_merge_f, *rv[:6])
        da, db, dc, d0, d1, d2 = vjp(rv[6])
        return [da, db, dc, jnp.concatenate([d0, d1, d2], axis=1)], []

    dp = lax.empty((batch * seq, lay.width), BF16)
    (dma, dmb, dmc, dp), _ = _rowcall(
        merge_bwd, [(ma, d, 0), (mb, d, 0), (mc, d, 0), (p, d, 0), (p, d, 1), (p, d, 2), (dmg, d, 0)], [],
        [(d, BF16), (d, BF16), (d, BF16), (lay.gates, BF16, dp, 0)], [], tile=128, name="merge_bwd")
    dya = _mm(dma, lw["w_a"], tb=True, name="proj_a_dx")
    post("w_a", _mm(sv["ya"], dma, ta=True, out_dtype=BF16, name="proj_a_dw"))
    dyb = _mm(dmb, lw["w_b"], tb=True, name="proj_b_dx")
    post("w_b", _mm(sv["yb"], dmb, ta=True, out_dtype=BF16, name="proj_b_dw"))
    dyc = _mm(dmc, lw["w_c"], tb=True, name="proj_c_dx")
    post("w_c", _mm(sv["yc"], dmc, ta=True, out_dtype=BF16, name="proj_c_dw"))
    drqw, dsink = _attn_bwd(sv["rqw"], lw["sink_x"], dyc, None, batch=batch, seq=seq, dil=1, nbw=WIN_KV_HEADS, cb0=0,
                            nh=WIN_KV_HEADS, rep=lay.rep, hw=WIN_HALF, name="win_bwd")
    gs["sink"] = jnp.sum(dsink, axis=1)
    dw = lay.dilw

    def combine_bwd(rv, pv):
        _, vjp = jax.vjp(_combine_f, *rv[:6])
        return list(vjp(rv[6])), []

    dol, _ = _rowcall(combine_bwd, [(a, dw, 0) for a in sv["os"] + sv["ls"]] + [(dyb, dw, 0)], [],
                      [(dw, F32)] * 6, [], tile=256, name="dil_combine_bwd")
    for gi, (window, dil) in enumerate(DIL_PATTERNS):
        (dg_,) = _attn_bwd(sv["rqd"][gi], None, dol[gi], dol[3 + gi], batch=batch, seq=seq, dil=dil,
                           nbw=DIL_HEADS, cb0=0, nh=DIL_HEADS, rep=1,
                           hw=window // (2 * dil), name="dil_bwd%d" % gi)
        dp = _rope(dg_, tabs, col0=0, width=dg_.shape[1], seq=seq, group=0, inverse=True, out_dtype=BF16,
                   name="rope_dil_bwd%d" % gi, into=dp, out_col0=lay.o_rqd + gi * dg_.shape[1])
    dp = _rope(drqw, tabs, col0=0, width=lay.rqw, seq=seq, group=lay.rep + 2, inverse=True, out_dtype=BF16,
               name="rope_win_bwd", into=dp, out_col0=lay.o_rqw)
    u, ys = sv["u"], sv["ys"]

    def post_bwd(rv, pv):
        _, vjp = jax.vjp(_ssd_post_f, *rv[:4], *pv)
        dyf, _, dxs, dz, dsk, dgn = vjp(rv[4])
        return [dyf, dxs, dz], [dsk, dgn]

    (dy, dxs_post, dp), (dsk, gs["ssd_norm"]) = _rowcall(
        post_bwd, [(ys[0], inner, 0), (ys[1], inner, 0), (u, inner, 0), (p, inner, lay.o_z // inner), (dya, inner, 0)],
        [lw["d_skip_x"], lw["ssd_norm"]], [(inner, F32), (inner, F32), (inner, BF16, dp, lay.o_z // inner)],
        [(1, inner), (1, inner)], tile=128, name="ssd_post_bwd")
    gs["d_skip"] = jnp.sum(dsk.reshape(SSD_HEADS, SSD_HEAD_DIM), axis=1)
    rs = []
    ddt, gdb, gda = [], [], []
    for dirn in range(2):
        dtr, dtr_t, bias, bias_t, alog, alog_t = sv["dtl"][dirn]
        r = _scan_bwd(u, dtr, dtr_t, bias, bias_t, alog, alog_t, sv["st"][dirn], dy, batch=batch, seq=seq,
                      inner=inner, reverse=bool(dirn), name="scan_bwd%d" % dirn)
        rs.append(r)
        ddt.append((r[3] + r[4].transpose(0, 2, 1)).transpose(1, 0, 2).reshape(batch * seq, SSD_HEADS))
        gdb.append((r[5][:, 0, :] + r[6][:, :, 0]).reshape(SSD_HEADS))
        gda.append((r[7][:, 0, :] + r[8][:, :, 0]).reshape(SSD_HEADS))
    gs["dt_bias"] = jnp.stack(gdb)
    gs["a_log"] = jnp.stack(gda)
    dcws, dcbs = [], []
    for tag, ch0, chans, dus in (("x", 0, inner, [dxs_post, rs[0][0], rs[1][0]]),
                                 ("b", inner, lay.gn, [rs[0][1], rs[1][1]]),
                                 ("c", inner + lay.gn, lay.gn, [rs[0][2], rs[1][2]])):
        dp, dcw, dcb = _conv_bwd(p, lw["conv_w8"], lw["conv_b"], dus, col0=lay.o_xbc, ch0=ch0, chans=chans,
                                 batch=batch, seq=seq, into=dp, out_col0=lay.o_xbc, name="conv_bwd_" + tag)
        dcws.append(dcw)
        dcbs.append(dcb)
    gs["conv_w"] = jnp.concatenate(dcws, axis=1)[:CONV_WIDTH]
    gs["conv_b"] = jnp.concatenate(dcbs, axis=1)[0]
    ddtp = jnp.concatenate(ddt + [jnp.zeros((batch * seq, lay.dtw - lay.ndt), F32)], axis=1)
    (dp,), _ = _rowcall(lambda rv, pv: ([rv[0]], []), [(ddtp, lay.dtw, 0)], [],
                        [(lay.dtw, BF16, dp, lay.o_dt // lay.dtw)], [], tile=512, name="ddt_store")
    hd = d // W_IN_PARTS
    for part in range(W_IN_PARTS):
        post(("w_in", part), _mm(sv["h"][:, part * hd:(part + 1) * hd], dp, ta=True, out_dtype=BF16,
                                 name="proj_in_dw"))
    dh = _mm(dp, lw["w_in"], tb=True, name="proj_in_dx")
    (dx,), (gs["g_mix"],) = _rowcall(norm_bwd, [(sv["x"], d, 0), (dh, d, 0), (dx1, d, 0)], [lw["g_mix"]],
                                     [(d, F32)], [(1, d)], tile=128, name="norm_mix_bwd")
    return dx, gs


_SHARDED = ("w_in", "w_a", "w_b", "w_c", "w_out", "w_up", "w_down")
_COL_SHARDED = ("w_in", "w_b", "w_up")
_SMALL = ("g_mix", "conv_b", "dt_bias", "a_log", "d_skip", "ssd_norm", "sink", "g_mlp")


def _gathered_to_full(name, g):
    n, r, c = g.shape
    if name in _COL_SHARDED:
        return g.transpose(1, 0, 2).reshape(r, n * c)
    return g.reshape(n * r, c)


def _full_to_slots(name, w):
    r, c = w.shape
    if name in _COL_SHARDED:
        return w.reshape(r, N_DEV, c // N_DEV).transpose(1, 0, 2)
    return w.reshape(N_DEV, r // N_DEV, c)


class _LayerWeights:
    def __init__(self, sched, layer, lay, small):
        self.sched, self.layer, self.lay, self.vals = sched, layer, lay, dict(small)

    def __getitem__(self, name):
        if name not in self.vals:
            full = _gathered_to_full(name, self.sched.get(("w", name, self.layer)))
            self.vals[name] = self.lay.permute_w(full) if name == "w_in" else full
        return self.vals[name]


def _pack(parts):
    flat = jnp.concatenate([p.reshape(-1).astype(F32) for p in parts])
    n = flat.shape[0]
    rows = -(-n // (8 * LANE)) * 8
    return jnp.pad(flat, (0, rows * LANE - n)).reshape(rows, LANE)


def _unpack(buf, shapes):
    flat = buf.reshape(-1)
    out, off = [], 0
    for s in shapes:
        n = math.prod(s)
        out.append(flat[off:off + n].reshape(s))
        off += n
    return out


def kernel(x, g_mix, w_in, conv_w, conv_b, dt_bias, a_log, d_skip, ssd_norm, w_a, w_b, w_c, sink, w_out, g_mlp, w_up, w_down, g_final, loss_target, m_g_mix, m_w_in, m_conv_w, m_conv_b, m_dt_bias, m_a_log, m_d_skip, m_ssd_norm, m_w_a, m_w_b, m_w_c, m_sink, m_w_out, m_g_mlp, m_w_up, m_w_down, m_g_final, v_g_mix, v_w_in, v_conv_w, v_conv_b, v_dt_bias, v_a_log, v_d_skip, v_ssd_norm, v_w_a, v_w_b, v_w_c, v_sink, v_w_out, v_g_mlp, v_w_up, v_w_down, v_g_final):
    batch, seq, d = x.shape
    depth = g_mix.shape[0]
    lay = _Layout(d)
    assert lay.n_in == w_in.shape[2] * N_DEV
    wts = dict(g_mix=g_mix, w_in=w_in, conv_w=conv_w, conv_b=conv_b, dt_bias=dt_bias, a_log=a_log, d_skip=d_skip,
               ssd_norm=ssd_norm, w_a=w_a, w_b=w_b, w_c=w_c, sink=sink, w_out=w_out, g_mlp=g_mlp, w_up=w_up,
               w_down=w_down, g_final=g_final)
    mom = dict(g_mix=m_g_mix, w_in=m_w_in, conv_w=m_conv_w, conv_b=m_conv_b, dt_bias=m_dt_bias, a_log=m_a_log,
               d_skip=m_d_skip, ssd_norm=m_ssd_norm, w_a=m_w_a, w_b=m_w_b, w_c=m_w_c, sink=m_sink, w_out=m_w_out,
               g_mlp=m_g_mlp, w_up=m_w_up, w_down=m_w_down, g_final=m_g_final)
    var = dict(g_mix=v_g_mix, w_in=v_w_in, conv_w=v_conv_w, conv_b=v_conv_b, dt_bias=v_dt_bias, a_log=v_a_log,
               d_skip=v_d_skip, ssd_norm=v_ssd_norm, w_a=v_w_a, w_b=v_w_b, w_c=v_w_c, sink=v_sink, w_out=v_w_out,
               g_mlp=v_g_mlp, w_up=v_w_up, w_down=v_w_down, g_final=v_g_final)
    me = 4 * lax.axis_index("x") + 2 * lax.axis_index("y") + lax.axis_index("c")

    global _SCHED
    sched = _SCHED = _Sched()
    (gconv,) = _exchange([conv_w], scatter=False, name="gather_conv_w")
    conv_full = gconv.transpose(1, 2, 0, 3).reshape(depth, CONV_WIDTH, -1)
    for l in range(depth):
        for n in _SHARDED:
            sched.post(("w", n, l), wts[n][l].astype(BF16), scatter=False)

    tabs = _rope_tables(seq)
    t = batch * seq
    xf = x.reshape(t, d)
    layers = []
    for l in range(depth):
        small = dict(
            g_mix=g_mix[l][None], g_mlp=g_mlp[l][None], ssd_norm=ssd_norm[l][None], conv_b=conv_b[l][None],
            dt_bias=dt_bias[l], a_log=a_log[l],
            d_skip_x=jnp.repeat(d_skip[l], SSD_HEAD_DIM)[None],
            sink_x=jnp.broadcast_to(sink[l][:, None], (WIN_Q_HEADS, LANE)),
            conv_w8=jnp.pad(conv_full[l], ((0, 8 - CONV_WIDTH), (0, 0))))
        layers.append(_LayerWeights(sched, l, lay, small))

    saves = []
    h = xf
    for l in range(depth):
        h, sv = _layer_fwd(h, layers[l], lay, tabs, batch, seq)
        saves.append(sv)
    dx, dgf, loss = _final_loss(h, g_final[None], loss_target.reshape(t, d), name="final_loss")

    gss = [None] * depth
    for l in reversed(range(depth)):
        def post(n, g, l=l):
            if isinstance(n, tuple):
                key, n, g = ("g", "w_in", l, n[1]), "w_in", lay.unpermute_w(g)
            else:
                key = ("g", n, l)
            sched.post(key, _full_to_slots(n, g).astype(BF16), scatter=True)

        dx, gss[l] = _layer_bwd(dx, saves[l], layers[l], lay, tabs, batch, seq, post)
    grad_x = dx.reshape(batch, seq, d)

    small_parts = [jnp.stack([gss[l][n] for l in range(depth)]) for n in _SMALL]
    small_parts += [dgf, jnp.stack([gss[l]["conv_w"] for l in range(depth)]), loss[0, :1]]
    small_shapes = [p.shape for p in small_parts]
    (rs,) = _exchange([_pack(small_parts)], scatter=False, name="gather_small")
    red = _unpack(_sum_slots(rs, name="sum_small"), small_shapes)
    gsmall = dict(zip(list(_SMALL) + ["g_final"], red[:len(_SMALL) + 1]))
    gconv_full, loss_sum = red[-2], red[-1]
    cshard = conv_w.shape[2]
    gsmall["conv_w"] = lax.dynamic_slice_in_dim(gconv_full, me * cshard, cshard, axis=2)

    out = {}
    rep_names = list(_SMALL) + ["g_final"]
    rep_shapes = [wts[n].shape for n in rep_names]
    res = _adamw(_pack([wts[n] for n in rep_names]), [_pack([gsmall[n] for n in rep_names])[None]],
                 _pack([mom[n] for n in rep_names]), _pack([var[n] for n in rep_names]), name="adamw_small")
    unp = [_unpack(a, rep_shapes) for a in res]
    for i, n in enumerate(rep_names):
        out[n] = [unp[k][i] for k in range(4)]
    cs2 = (depth * CONV_WIDTH, cshard)
    res = _adamw(conv_w.reshape(cs2), [gsmall["conv_w"].reshape((1,) + cs2)], m_conv_w.reshape(cs2),
                 v_conv_w.reshape(cs2), name="adamw_conv_w")
    out["conv_w"] = [a.reshape(conv_w.shape) for a in res]
    for n in ("w_down", "w_up", "w_out", "w_a", "w_b", "w_c", "w_in"):
        shp = wts[n].shape
        r2 = (shp[0] * shp[1], shp[2])
        if n == "w_in":
            recvs = [sched.get(("g", n, l, part)) for l in range(depth) for part in range(W_IN_PARTS)]
        else:
            recvs = [sched.get(("g", n, l)) for l in range(depth)]
        res = _adamw(wts[n].reshape(r2), recvs, mom[n].reshape(r2), var[n].reshape(r2), name="adamw_" + n)
        out[n] = [a.reshape(shp) for a in res]

    order = ["g_mix", "w_in", "conv_w", "conv_b", "dt_bias", "a_log", "d_skip", "ssd_norm", "w_a", "w_b", "w_c",
             "sink", "w_out", "g_mlp", "w_up", "w_down", "g_final"]
    outs = [loss_sum.reshape(()), grad_x]
    for k in range(4):
        outs += [out[n][k] for n in order]
    return tuple(outs)
```

```python
import functools
import math

import jax
import jax.numpy as jnp
from jax import lax
from jax.experimental import pallas as pl
from jax.experimental.pallas import tpu as pltpu

F32 = jnp.float32
BF16 = jnp.bfloat16
HI = lax.Precision.HIGHEST
MESH = pl.DeviceIdType.MESH
N_DEV = 8

SSD_HEADS = 32
SSD_HEAD_DIM = 64
SSD_GROUPS = 8
SSD_STATE = 128
SSD_CHUNK = 128
CONV_WIDTH = 5
HEAD_DIM = 128
ROPE_DIM = 32
ROPE_THETA = 500000.0
DIL_PATTERNS = ((128, 1), (512, 4), (2048, 16))
DIL_HEADS = 8
WIN_Q_HEADS = 16
WIN_KV_HEADS = 4
WIN_HALF = 128
N_BRANCH = 3
EPS = 1e-6
NEG_INF = -1e30
ADAM_LR = 0.001
ADAM_B1 = 0.9
ADAM_B2 = 0.999
ADAM_EPS = 1e-08
ADAM_WD = 0.01
ADAM_STEP = 10

LANE = 128
QBLK = 128
VMEM_LIMIT = 56 * 1024 * 1024
PAD_TO = 512
MM_VMEM_BUDGET = 44 * 1024 * 1024
W_IN_PARTS = 4


def _cparams(sem=None):
    return pltpu.CompilerParams(dimension_semantics=sem, vmem_limit_bytes=VMEM_LIMIT)


PIECE_BYTES = 400 * 1024
US_PER_PIECE_BYTE = 8.8e-5
MAX_PIECES = 8
CARRIER_US = {
    "proj_in": 450, "proj_in_dx": 560, "proj_in_dw": 140, "scan_fwd0": 200, "scan_fwd1": 200, "scan_bwd0": 480,
    "scan_bwd1": 480, "win_fwd": 150, "win_bwd": 390, "dil_fwd0": 85, "dil_fwd1": 50, "dil_fwd2": 65,
    "dil_bwd0": 165, "dil_bwd1": 155, "dil_bwd2": 110, "mlp_up": 155, "mlp_down": 170, "mlp_up_dx": 190,
    "mlp_up_dw": 190, "mlp_down_dx": 165, "mlp_down_dw": 188, "conv_bwd_x": 75, "rope_dil": 48,
    "adamw_w_up": 63, "adamw_w_down": 62, "proj_a": 42, "proj_c": 42, "proj_out": 42, "proj_a_dx": 42,
    "proj_c_dx": 42, "proj_out_dx": 42, "proj_a_dw": 42, "proj_c_dw": 42, "proj_out_dw": 42, "conv_fwd": 65,
    "merge": 65, "merge_bwd": 100, "ssd_post": 50, "ssd_post_bwd": 90, "rope_win": 90, "norm_mix_bwd": 50,
    "norm_mlp_bwd": 50, "dil_combine_bwd": 70, "norm_mix": 40, "norm_mlp": 40, "rope_dil_bwd0": 45,
    "rope_dil_bwd1": 45, "rope_dil_bwd2": 45, "rope_win_bwd": 85, "conv_bwd_b": 35, "conv_bwd_c": 35,
}


class _Piece:
    def __init__(self, key, row0, rows, scatter, est):
        self.key, self.row0, self.rows, self.scatter, self.est = key, row0, rows, scatter, est


def _coalesce(pieces):
    out = []
    for p in pieces:
        q = out[-1] if out else None
        if q is not None and q.key == p.key and q.scatter == p.scatter and q.row0 + q.rows == p.row0:
            out[-1] = _Piece(q.key, q.row0, q.rows + p.rows, q.scatter, q.est + p.est)
        else:
            out.append(p)
    return out


class _Sched:
    def __init__(self):
        self.queue, self.src, self.dst = [], {}, {}

    def post(self, key, src, scatter):
        r, c = src.shape[-2:]
        self.src[key] = src
        self.dst[key] = lax.empty((N_DEV, r, c), src.dtype)
        row_bytes = c * src.dtype.itemsize
        pr = r
        while pr * row_bytes > PIECE_BYTES and pr % 32 == 0:
            pr //= 2
        for row0 in range(0, r, pr):
            self.queue.append(_Piece(key, row0, pr, scatter, pr * row_bytes * US_PER_PIECE_BYTE))

    def take(self, name):
        budget = CARRIER_US.get(name)
        out, used = [], 0.0
        while budget and self.queue and used + self.queue[0].est <= 1.1 * budget:
            used += self.queue[0].est
            out.append(self.queue.pop(0))
        return _coalesce(out)

    def get(self, key):
        last = max([i for i, p in enumerate(self.queue) if p.key == key], default=-1)
        if last >= 0:
            pieces, self.queue = _coalesce(self.queue[:last + 1]), self.queue[last + 1:]
            for i in range(0, len(pieces), MAX_PIECES):
                _exchange_pieces(self, pieces[i:i + MAX_PIECES], name="exchange_flush")
        return self.dst[key]


_SCHED = None


def _piece_copies(pieces, keys, src_refs, dst_refs, send_sems, recv_sems, loc_sems):
    x, y, c = lax.axis_index("x"), lax.axis_index("y"), lax.axis_index("c")
    me = 4 * x + 2 * y + c
    cps = []
    for t, p in enumerate(pieces):
        ki = keys.index(p.key)
        rows = pl.ds(p.row0, p.rows)
        for j in range(1, N_DEV):
            px = (1 - x) if (j >> 2) & 1 else x
            py = (1 - y) if (j >> 1) & 1 else y
            pc = (1 - c) if j & 1 else c
            src = src_refs[ki].at[4 * px + 2 * py + pc, rows] if p.scatter else src_refs[ki].at[rows]
            cps.append(pltpu.make_async_remote_copy(
                src_ref=src, dst_ref=dst_refs[ki].at[me, rows], send_sem=send_sems.at[t * 7 + j - 1],
                recv_sem=recv_sems.at[t * 7 + j - 1], device_id=(px, py, pc), device_id_type=MESH))
        src = src_refs[ki].at[me, rows] if p.scatter else src_refs[ki].at[rows]
        cps.append(pltpu.make_async_copy(src, dst_refs[ki].at[me, rows], loc_sems.at[t]))
    return cps


def _two_level_gather(pieces, keys, src_refs, dst_refs, send_sems, recv_sems, loc_sems):
    x, y, c = lax.axis_index("x"), lax.axis_index("y"), lax.axis_index("c")
    me = 4 * x + 2 * y + c
    sib = (x, y, 1 - c)
    chips = [(1 - x, y), (x, 1 - y), (1 - x, 1 - y)]

    def copy(t, k, src, dst, to):
        return pltpu.make_async_remote_copy(src_ref=src, dst_ref=dst, send_sem=send_sems.at[7 * t + k],
                                            recv_sem=recv_sems.at[7 * t + k], device_id=to, device_id_type=MESH)

    def landing(p, px, py, pc):
        return dst_refs[keys.index(p.key)].at[4 * px + 2 * py + pc, pl.ds(p.row0, p.rows)]

    sends, local = [], []
    for t, p in enumerate(pieces):
        src = src_refs[keys.index(p.key)].at[pl.ds(p.row0, p.rows)]
        mine = landing(p, x, y, c)
        sends.append(copy(t, 0, src, mine, sib))
        sends += [copy(t, 1 + j, src, mine, (px, py, c)) for j, (px, py) in enumerate(chips)]
        local.append(pltpu.make_async_copy(src, mine, loc_sems.at[t]))
    for cp in sends + local:
        cp.start()
    passed = []
    for t, p in enumerate(pieces):
        for j, (px, py) in enumerate(chips):
            blk = landing(p, px, py, c)
            copy(t, 1 + j, blk, blk, (x, y, c)).wait_recv()
            fwd = copy(t, 4 + j, blk, blk, sib)
            fwd.start()
            passed.append(fwd)
    for t, p in enumerate(pieces):
        blk = landing(p, x, y, 1 - c)
        copy(t, 0, blk, blk, (x, y, c)).wait_recv()
        for j, (px, py) in enumerate(chips):
            blk = landing(p, px, py, 1 - c)
            copy(t, 4 + j, blk, blk, (x, y, c)).wait_recv()
    for cp in sends + passed:
        cp.wait_send()
    for cp in local:
        cp.wait()


def _carry_call(sched, pieces, body, *, name, grid, in_specs, out_specs, out_shape, scratch_shapes, ins, aliases=None):
    keys = []
    for p in pieces:
        if p.key not in keys:
            keys.append(p.key)
    n_in, n_out, nk, npc = len(ins), len(out_shape), len(keys), len(pieces)
    n_scr = len(scratch_shapes)

    def wrapped(*refs):
        in_refs = refs[:n_in]
        src_refs = refs[n_in:n_in + nk]
        out_refs = refs[n_in + 2 * nk:n_in + 2 * nk + n_out]
        dst_refs = refs[n_in + 2 * nk + n_out:n_in + 3 * nk + n_out]
        scr = refs[n_in + 3 * nk + n_out:]
        inner_scr, sems = scr[:n_scr], scr[n_scr:]
        if grid:
            pids = [pl.program_id(a) for a in range(len(grid))]
            first = functools.reduce(lambda u, v: u & v, [q == 0 for q in pids])
            last = functools.reduce(lambda u, v: u & v, [q == g - 1 for q, g in zip(pids, grid)])

            @pl.when(first)
            def _():
                for cp in _piece_copies(pieces, keys, src_refs, dst_refs, *sems):
                    cp.start()

            body(*in_refs, *out_refs, *inner_scr)

            @pl.when(last)
            def _():
                for cp in _piece_copies(pieces, keys, src_refs, dst_refs, *sems):
                    cp.wait()
        elif all(not p.scatter for p in pieces):
            _two_level_gather(pieces, keys, src_refs, dst_refs, *sems)
        else:
            cps = _piece_copies(pieces, keys, src_refs, dst_refs, *sems)
            for cp in cps:
                cp.start()
            for cp in cps:
                cp.wait()

    anyspec = pl.BlockSpec(memory_space=pl.ANY)
    dsts = [sched.dst[k] for k in keys]
    kwargs = dict(grid=grid) if grid else {}
    res = pl.pallas_call(
        wrapped, name=name, in_specs=list(in_specs) + [anyspec] * (2 * nk), out_specs=list(out_specs) + [anyspec] * nk,
        out_shape=list(out_shape) + [jax.ShapeDtypeStruct(d.shape, d.dtype) for d in dsts],
        input_output_aliases={**(aliases or {}), **{n_in + nk + i: n_out + i for i in range(nk)}},
        scratch_shapes=list(scratch_shapes) + [pltpu.SemaphoreType.DMA((7 * npc,)), pltpu.SemaphoreType.DMA((7 * npc,)),
                                               pltpu.SemaphoreType.DMA((npc,))],
        compiler_params=pltpu.CompilerParams(dimension_semantics=("arbitrary",) * len(grid) if grid else None,
                                             vmem_limit_bytes=VMEM_LIMIT, has_side_effects=True),
        **kwargs,
    )(*ins, *[sched.src[k] for k in keys], *dsts)
    for i, k in enumerate(keys):
        sched.dst[k] = res[n_out + i]
    return list(res[:n_out])


def _exchange_pieces(sched, pieces, *, name):
    _carry_call(sched, pieces, None, name=name, grid=(), in_specs=[], out_specs=[], out_shape=[], scratch_shapes=[],
                ins=[])


def _pcall(body, *, name, grid, in_specs, out_specs, out_shape, scratch_shapes=(), sem=None, into=None):
    single = not isinstance(out_shape, (list, tuple))
    out_shape_l = [out_shape] if single else list(out_shape)
    out_specs_l = [out_specs] if single else list(out_specs)
    into = into or {}

    def run(*ins):
        n0, nb = len(ins), len(into)
        specs = list(in_specs) + [pl.BlockSpec(memory_space=pl.ANY)] * nb
        aliases = {n0 + k: oi for k, oi in enumerate(into)}
        for oi, buf in into.items():
            out_shape_l[oi] = jax.ShapeDtypeStruct(buf.shape, buf.dtype)
        kbody = (lambda *refs: body(*refs[:n0], *refs[n0 + nb:])) if nb else body
        args = list(ins) + list(into.values())
        pieces = _SCHED.take(name) if _SCHED is not None else []
        if pieces:
            res = _carry_call(_SCHED, pieces, kbody, name=name, grid=grid, in_specs=specs, out_specs=out_specs_l,
                              out_shape=out_shape_l, scratch_shapes=list(scratch_shapes), ins=args, aliases=aliases)
        else:
            res = pl.pallas_call(kbody, name=name, grid=grid, in_specs=specs, out_specs=out_specs_l,
                                 out_shape=out_shape_l, scratch_shapes=list(scratch_shapes),
                                 input_output_aliases=aliases, compiler_params=_cparams(sem))(*args)
        return res[0] if single else list(res)

    return run


def _pick(dim, cands):
    for c in cands:
        if dim % c == 0:
            return c
    return dim


def _mm_tiles(m, n, k, a_bytes, b_bytes, o_bytes, has_add):
    tm = _pick(m, (1024, 512, 256, 128))
    tn = _pick(n, (1024, 1792, 512, 256, 128))
    for tk in (3584, 2048, 1792, 1024, 896, 512, 256, 128):
        if k % tk:
            continue
        need = 2 * (tm * tk * a_bytes + tk * tn * b_bytes + tm * tn * (o_bytes + (4 if has_add else 0)))
        need += tm * tn * 4 if k // tk > 1 else 0
        if need <= MM_VMEM_BUDGET:
            return tm, tn, tk
    return tm, tn, _pick(k, (128,))


def _mm(a, b, *, ta=False, tb=False, out_dtype=F32, add=None, post=None, name):
    m, k = (a.shape[1], a.shape[0]) if ta else a.shape
    k2, n = (b.shape[1], b.shape[0]) if tb else b.shape
    assert k == k2, (a.shape, b.shape, ta, tb)
    pfn, pins, pdts = post if post is not None else (None, [], [out_dtype])
    o_bytes = sum(jnp.dtype(dt).itemsize for dt in pdts) + sum(e.dtype.itemsize for e in pins)
    tm, tn, tk = _mm_tiles(m, n, k, a.dtype.itemsize, b.dtype.itemsize, o_bytes, add is not None)
    assert m % tm == 0 and n % tn == 0 and k % tk == 0, (m, n, k, tm, tn, tk)
    nk = k // tk
    a_spec = pl.BlockSpec((tk, tm), lambda i, j, kk: (kk, i)) if ta else pl.BlockSpec((tm, tk), lambda i, j, kk: (i, kk))
    b_spec = pl.BlockSpec((tn, tk), lambda i, j, kk: (j, kk)) if tb else pl.BlockSpec((tk, tn), lambda i, j, kk: (kk, j))
    o_spec = pl.BlockSpec((tm, tn), lambda i, j, kk: (i, j))
    dims = (((0 if ta else 1,), (1 if tb else 0,)), ((), ()))
    has_add = add is not None

    nx, no = (1 if has_add else 0) + len(pins), len(pdts)

    def body(*refs):
        a_ref, b_ref = refs[:2]
        x_refs = refs[2:2 + nx]
        o_refs = refs[2 + nx:2 + nx + no]
        part = lax.dot_general(a_ref[...].astype(BF16), b_ref[...].astype(BF16), dims, preferred_element_type=F32)

        def finish(r):
            if has_add:
                r = r + x_refs[0][...]
            outs = pfn(r, *[x[...] for x in x_refs[1 if has_add else 0:]]) if pfn else (r,)
            for o_ref, v in zip(o_refs, outs):
                o_ref[...] = v.astype(o_ref.dtype)

        if nk == 1:
            finish(part)
            return
        acc_ref = refs[-1]
        kk = pl.program_id(2)

        @pl.when(kk == 0)
        def _():
            acc_ref[...] = part

        @pl.when(kk > 0)
        def _():
            acc_ref[...] += part

        @pl.when(kk == nk - 1)
        def _():
            finish(acc_ref[...])

    ins = [a, b] + ([add] if has_add else []) + list(pins)
    specs = [a_spec, b_spec] + [o_spec] * nx
    res = _pcall(
        body, name=name, grid=(m // tm, n // tn, nk), in_specs=specs, out_specs=[o_spec] * no,
        out_shape=[jax.ShapeDtypeStruct((m, n), dt) for dt in pdts],
        scratch_shapes=[pltpu.VMEM((tm, tn), F32)] if nk > 1 else [],
        sem=("parallel", "parallel", "arbitrary"),
    )(*ins)
    return res if post is not None else res[0]


def _rowcall(fn, rows, pars, row_outs, par_outs, *, tile, name):
    t = rows[0][0].shape[0]
    tile = min(tile, t)
    assert t % tile == 0
    nr, npar, nro, npo = len(rows), len(pars), len(row_outs), len(par_outs)
    in_specs = [pl.BlockSpec((tile, c), functools.partial(lambda i, cb: (i, cb), cb=cb)) for (_, c, cb) in rows]
    in_specs += [pl.BlockSpec(p.shape, lambda i: (0, 0)) for p in pars]
    into = {k: ro[2] for k, ro in enumerate(row_outs) if len(ro) == 4}
    out_specs = [pl.BlockSpec((tile, ro[0]), functools.partial(lambda i, cb: (i, cb), cb=ro[3] if len(ro) == 4 else 0))
                 for ro in row_outs]
    out_specs += [pl.BlockSpec(s, lambda i: (0, 0)) for s in par_outs]
    out_shape = [jax.ShapeDtypeStruct((t, ro[0]), ro[1]) for ro in row_outs]
    out_shape += [jax.ShapeDtypeStruct(s, F32) for s in par_outs]

    def body(*refs):
        rv = [r[...] for r in refs[:nr]]
        pv = [r[...] for r in refs[nr:nr + npar]]
        ro_refs = refs[nr + npar:nr + npar + nro]
        po_refs = refs[nr + npar + nro:]
        ro, po = fn(rv, pv)
        for ref, v in zip(ro_refs, ro):
            ref[...] = v.astype(ref.dtype)
        if npo:
            @pl.when(pl.program_id(0) == 0)
            def _():
                for ref in po_refs:
                    ref[...] = jnp.zeros_like(ref)
            for ref, v in zip(po_refs, po):
                ref[...] += v

    res = _pcall(
        body, name=name, grid=(t // tile,), in_specs=in_specs, out_specs=out_specs, out_shape=out_shape,
        sem=("arbitrary",), into=into,
    )(*[r[0] for r in rows], *pars)
    return list(res[:nro]), list(res[nro:])


def _map2d(fn, ins, out_dtype, *, name, tile=256, cw=2048):
    t, w = ins[0].shape
    tile, cw = min(tile, t), min(cw, w)
    assert t % tile == 0 and w % cw == 0

    def body(*refs):
        refs[-1][...] = fn(*[r[...] for r in refs[:-1]]).astype(out_dtype)

    spec = pl.BlockSpec((tile, cw), lambda i, j: (i, j))
    return pl.pallas_call(
        body, name=name, grid=(t // tile, w // cw), in_specs=[spec] * len(ins), out_specs=spec,
        out_shape=jax.ShapeDtypeStruct((t, w), out_dtype), compiler_params=_cparams(("parallel", "parallel")),
    )(*ins)


def _exchange(srcs, *, scatter, name):
    n = len(srcs)
    out_shape = [jax.ShapeDtypeStruct(s.shape if scatter else (N_DEV,) + s.shape, s.dtype) for s in srcs]

    def body(*refs):
        src_refs, out_refs = refs[:n], refs[n:2 * n]
        send_sems, recv_sems, loc_sems = refs[2 * n:]
        x, y, c = lax.axis_index("x"), lax.axis_index("y"), lax.axis_index("c")
        me = 4 * x + 2 * y + c
        copies = []
        for a in range(n):
            for j in range(1, N_DEV):
                px = (1 - x) if (j >> 2) & 1 else x
                py = (1 - y) if (j >> 1) & 1 else y
                pc = (1 - c) if j & 1 else c
                src = src_refs[a].at[4 * px + 2 * py + pc] if scatter else src_refs[a]
                cp = pltpu.make_async_remote_copy(
                    src_ref=src, dst_ref=out_refs[a].at[me], send_sem=send_sems.at[a * 7 + j - 1],
                    recv_sem=recv_sems.at[a * 7 + j - 1], device_id=(px, py, pc), device_id_type=MESH)
                cp.start()
                copies.append(cp)
            src = src_refs[a].at[me] if scatter else src_refs[a]
            cp = pltpu.make_async_copy(src, out_refs[a].at[me], loc_sems.at[a])
            cp.start()
            copies.append(cp)
        for cp in copies:
            cp.wait()

    anyspec = pl.BlockSpec(memory_space=pl.ANY)
    return pl.pallas_call(
        body, name=name, in_specs=[anyspec] * n, out_specs=[anyspec] * n, out_shape=out_shape,
        scratch_shapes=[pltpu.SemaphoreType.DMA((7 * n,)), pltpu.SemaphoreType.DMA((7 * n,)),
                        pltpu.SemaphoreType.DMA((n,))],
        compiler_params=pltpu.CompilerParams(has_side_effects=True),
    )(*srcs)


def _row_tile(r, c, budget_elems=256 * 1024):
    tr = r
    while tr * c > budget_elems and tr % 16 == 0:
        tr //= 2
    return tr


def _adamw(w, recvs, m, v, *, name):
    r, c = w.shape
    nl = len(recvs)
    ns, rl = recvs[0].shape[:2]
    assert rl * nl == r
    tr = _row_tile(rl, c, budget_elems=(1024 * 1024) // max(nl, 4))
    nt = rl // tr
    bc1 = 1.0 / (1.0 - ADAM_B1 ** ADAM_STEP)
    bc2 = 1.0 / (1.0 - ADAM_B2 ** ADAM_STEP)

    def body(*refs):
        w_ref, m_ref, v_ref = refs[:3]
        r_refs = refs[3:3 + nl]
        g_ref, d_ref, mo_ref, vo_ref = refs[3 + nl:]
        i = pl.program_id(0)
        for k in range(nl):
            @pl.when(i // nt == k)
            def _(k=k):
                g = r_refs[k][0].astype(F32)
                for s in range(1, ns):
                    g = g + r_refs[k][s].astype(F32)
                g_ref[...] = g
        g = g_ref[...]
        mn = ADAM_B1 * m_ref[...] + (1.0 - ADAM_B1) * g
        vn = ADAM_B2 * v_ref[...] + (1.0 - ADAM_B2) * (g * g)
        mo_ref[...] = mn
        vo_ref[...] = vn
        d_ref[...] = -ADAM_LR * ((mn * bc1) / (jnp.sqrt(vn * bc2) + ADAM_EPS) + ADAM_WD * w_ref[...])

    spec = pl.BlockSpec((tr, c), lambda i: (i, 0))
    rspecs = [pl.BlockSpec((ns, tr, c), functools.partial(lambda i, k: (0, jnp.clip(i - k * nt, 0, nt - 1), 0), k=k))
              for k in range(nl)]
    return _pcall(
        body, name=name, grid=(r // tr,), in_specs=[spec, spec, spec] + rspecs,
        out_specs=[spec] * 4, out_shape=[jax.ShapeDtypeStruct((r, c), F32)] * 4, sem=("arbitrary",),
    )(w, m, v, *recvs)


def _sum_slots(recv, *, name):
    ns, r, c = recv.shape

    def body(r_ref, o_ref):
        g = r_ref[0]
        for s in range(1, ns):
            g = g + r_ref[s]
        o_ref[...] = g

    return pl.pallas_call(body, name=name, out_shape=jax.ShapeDtypeStruct((r, c), F32))(recv)


def _rms(x, g):
    return x * lax.rsqrt(jnp.mean(x * x, axis=-1, keepdims=True) + EPS) * g


def _silu(x):
    return x * jax.nn.sigmoid(x)


def _merge_f(a, b, c, g0, g1, g2):
    return jax.nn.sigmoid(g0) * a + jax.nn.sigmoid(g1) * b + jax.nn.sigmoid(g2) * c


def _ssd_post_f(yf, yb, xs, z, dskip, gnorm):
    y = (yf + yb + dskip * xs) * _silu(z)
    return _rms(y, gnorm)


def _combine_f(o0, o1, o2, l0, l1, l2):
    m = jnp.maximum(jnp.maximum(l0, l1), l2)
    e0, e1, e2 = jnp.exp(l0 - m), jnp.exp(l1 - m), jnp.exp(l2 - m)
    return (e0 * o0 + e1 * o1 + e2 * o2) / (e0 + e1 + e2)


def _key_window(ln, hw):
    return min(ln, QBLK + 2 * hw)


def _key_start(i, ln, hw):
    return pl.multiple_of(jnp.clip(i * QBLK - hw, 0, ln - _key_window(ln, hw)), 64)


def _band_mask(qs, ks, kwin, hw):
    qpos = qs + lax.broadcasted_iota(jnp.int32, (QBLK, kwin), 0)
    kpos = ks + lax.broadcasted_iota(jnp.int32, (QBLK, kwin), 1)
    return jnp.abs(qpos - kpos) <= hw


def _attn_block(q, k3, v3, sk, valid, *, has_sink):
    s = lax.dot_general(q.astype(BF16), k3.astype(BF16), (((1,), (1,)), ((), ())),
                        preferred_element_type=F32) * (HEAD_DIM ** -0.5)
    s = jnp.where(valid, s, NEG_INF)
    m = jnp.max(s, axis=-1, keepdims=True)
    if has_sink:
        m = jnp.maximum(m, sk)
    m = lax.stop_gradient(m)
    e = jnp.exp(s - m)
    l = jnp.sum(e, axis=-1, keepdims=True)
    if has_sink:
        l = l + jnp.exp(sk - m)
    o = jnp.dot(e.astype(BF16), v3.astype(BF16), preferred_element_type=F32) / l
    return o, m + jnp.log(l)


def _ssd_chunk(state, xs, bm, cm, dtr, dtr_t, bias, bias_t, alog, alog_t, *, reverse):
    t = xs.shape[0]
    hg = dtr.shape[1]
    hp = xs.shape[1]
    p = hp // hg
    dt = jax.nn.softplus(dtr + bias)
    dt_t = jax.nn.softplus(dtr_t + bias_t)
    dta = dt * (-jnp.exp(alog))
    dta_t = dt_t * (-jnp.exp(alog_t))
    li = lax.broadcasted_iota(jnp.int32, (t, t), 0)
    si = lax.broadcasted_iota(jnp.int32, (t, t), 1)
    tri = (li <= si) if reverse else (li >= si)
    trif = tri.astype(F32)
    cs = jnp.dot(trif, dta, precision=HI, preferred_element_type=F32)
    cs_t = lax.dot_general(dta_t, trif, (((1,), (1,)), ((), ())), precision=HI,
                           preferred_element_type=F32)
    total = jnp.sum(dta, axis=0, keepdims=True)
    cb = lax.dot_general(cm.astype(BF16), bm.astype(BF16), (((1,), (1,)), ((), ())),
                         preferred_element_type=F32)
    lane_h = lax.broadcasted_iota(jnp.int32, (1, hp), 1) // p
    col_h = lax.broadcasted_iota(jnp.int32, (1, hg), 1)
    row_h = lax.broadcasted_iota(jnp.int32, (hg, 1), 0)
    dt_x = jnp.zeros((t, hp), F32)
    ecs_x = jnp.zeros((t, hp), F32)
    ds_x = jnp.zeros((t, hp), F32)
    etot_x = jnp.zeros((1, hp), F32)
    decays, masks = [], []
    for h in range(hg):
        oh = (col_h == h).astype(F32)
        oh_t = (row_h == h).astype(F32)
        mk = (lane_h == h).astype(F32)
        dt_h = jnp.sum(dt * oh, axis=1, keepdims=True)
        cs_h = jnp.sum(cs * oh, axis=1, keepdims=True)
        cst_h = jnp.sum(cs_t * oh_t, axis=0, keepdims=True)
        tot_h = jnp.sum(total * oh, axis=1, keepdims=True)
        dt_x = dt_x + dt_h * mk
        ecs_x = ecs_x + jnp.exp(cs_h) * mk
        ds_x = ds_x + jnp.exp(tot_h - cs_h) * mk
        etot_x = etot_x + jnp.exp(tot_h) * mk
        decays.append(jnp.exp(jnp.where(tri, cs_h - cst_h, -jnp.inf)))
        masks.append(mk)
    xdt = xs * dt_x
    y = jnp.dot(cm.astype(BF16), state.astype(BF16), preferred_element_type=F32) * ecs_x
    for h in range(hg):
        y = y + jnp.dot((cb * decays[h]).astype(BF16), (xdt * masks[h]).astype(BF16),
                        preferred_element_type=F32)
    st_new = lax.dot_general(bm.astype(BF16), (xdt * ds_x).astype(BF16), (((0,), (0,)), ((), ())),
                             preferred_element_type=F32)
    return y, state * etot_x + st_new


def _rope_tables(seq):
    half = ROPE_DIM // 2
    inv = ROPE_THETA ** (-jnp.arange(0, ROPE_DIM, 2, dtype=F32) / ROPE_DIM)
    ang = jnp.arange(seq, dtype=F32)[:, None] * inv[None, :]
    cos, sin = jnp.cos(ang), jnp.sin(ang)
    rest = HEAD_DIM - ROPE_DIM
    c = jnp.concatenate([cos, cos, jnp.ones((seq, rest), F32)], axis=1)
    a = jnp.concatenate([-sin, jnp.zeros((seq, HEAD_DIM - half), F32)], axis=1)
    b = jnp.concatenate([jnp.zeros((seq, half), F32), sin, jnp.zeros((seq, rest), F32)], axis=1)
    return c, a, b


def _rope(src, tabs, *, col0, width, seq, group, inverse, out_dtype, name, into=None, out_col0=0):
    t = src.shape[0]
    half = ROPE_DIM // 2
    nhb = 6 if all(v % (6 * HEAD_DIM) == 0 for v in (width, col0, out_col0)) else 3
    cw, tile = nhb * HEAD_DIM, 512
    assert width % cw == 0 and col0 % cw == 0 and out_col0 % cw == 0 and seq % tile == 0 and t % tile == 0
    ns = seq // tile

    def body(x_ref, c_ref, a_ref, b_ref, o_ref):
        jb = pl.program_id(1)
        c, a, b = c_ref[...], a_ref[...], b_ref[...]
        for hh in range(nhb):
            xv = x_ref[:, hh * HEAD_DIM:(hh + 1) * HEAD_DIM].astype(F32)
            if inverse:
                yv = xv * c + pltpu.roll(xv * a, half, 1) + pltpu.roll(xv * b, HEAD_DIM - half, 1)
            else:
                yv = xv * c + pltpu.roll(xv, HEAD_DIM - half, 1) * a + pltpu.roll(xv, half, 1) * b
            if group:
                keep = ((jb * nhb + hh) % group) == (group - 1)
                yv = jnp.where(keep, xv, yv)
            o_ref[:, hh * HEAD_DIM:(hh + 1) * HEAD_DIM] = yv.astype(o_ref.dtype)

    tspec = pl.BlockSpec((tile, HEAD_DIM), lambda i, j: (i % ns, 0))
    return _pcall(
        body, name=name, grid=(t // tile, width // cw),
        in_specs=[pl.BlockSpec((tile, cw), lambda i, j: (i, col0 // cw + j)), tspec, tspec, tspec],
        out_specs=pl.BlockSpec((tile, cw), lambda i, j: (i, out_col0 // cw + j)),
        out_shape=jax.ShapeDtypeStruct((t, width), out_dtype),
        sem=("parallel", "parallel"), into=None if into is None else {0: into},
    )(src, *tabs)


def _shift_rows(x, d, tpos):
    if d == 0:
        return x
    s = x.shape[0]
    y = pltpu.roll(x, (-d) % s, 0)
    ok = (tpos + d >= 0) & (tpos + d < s)
    return jnp.where(ok, y, 0.0)


def _conv_fwd(p, w8, bias, *, col0, chans, batch, seq, name):
    cb = 256
    assert chans % cb == 0 and col0 % cb == 0
    pad = (CONV_WIDTH - 1) // 2

    def body(x_ref, w_ref, b_ref, o_ref):
        x = x_ref[...]
        tpos = lax.broadcasted_iota(jnp.int32, x.shape, 0)
        acc = jnp.broadcast_to(b_ref[...], x.shape)
        for k in range(CONV_WIDTH):
            acc = acc + w_ref[k:k + 1, :] * _shift_rows(x, k - pad, tpos)
        o_ref[...] = _silu(acc)

    return _pcall(
        body, name=name, grid=(chans // cb, batch),
        in_specs=[pl.BlockSpec((seq, cb), lambda j, b: (b, col0 // cb + j)),
                  pl.BlockSpec((8, cb), lambda j, b: (0, j)), pl.BlockSpec((1, cb), lambda j, b: (0, j))],
        out_specs=pl.BlockSpec((seq, cb), lambda j, b: (b, j)),
        out_shape=jax.ShapeDtypeStruct((batch * seq, chans), F32),
        sem=("parallel", "arbitrary"),
    )(p, w8, bias)


def _conv_bwd(p, w8, bias, dus, *, col0, ch0, chans, batch, seq, into, out_col0, name):
    cb = 256 if chans % 256 == 0 and ch0 % 256 == 0 else 128
    assert chans % cb == 0 and ch0 % cb == 0 and col0 % cb == 0 and out_col0 % cb == 0
    pad = (CONV_WIDTH - 1) // 2
    ndu = len(dus)

    def body(*refs):
        x_ref, w_ref, b_ref = refs[:3]
        du_refs = refs[3:3 + ndu]
        dx_ref, dw_ref, db_ref = refs[3 + ndu:]
        du = du_refs[0][...]
        for r in du_refs[1:]:
            du = du + r[...]
        _conv_bwd_block(x_ref, w_ref, b_ref, du, dx_ref, dw_ref, db_ref)

    c0 = (col0 + ch0) // cb
    return _pcall(
        body, name=name, grid=(chans // cb, batch),
        in_specs=[pl.BlockSpec((seq, cb), lambda j, b: (b, c0 + j)),
                  pl.BlockSpec((8, cb), lambda j, b: (0, ch0 // cb + j)),
                  pl.BlockSpec((1, cb), lambda j, b: (0, ch0 // cb + j))]
        + [pl.BlockSpec((seq, cb), lambda j, b: (b, j))] * ndu,
        out_specs=[pl.BlockSpec((seq, cb), lambda j, b: (b, (out_col0 + ch0) // cb + j)),
                   pl.BlockSpec((8, cb), lambda j, b: (0, j)), pl.BlockSpec((1, cb), lambda j, b: (0, j))],
        out_shape=[jax.ShapeDtypeStruct(into.shape, into.dtype), jax.ShapeDtypeStruct((8, chans), F32),
                   jax.ShapeDtypeStruct((1, chans), F32)],
        sem=("parallel", "arbitrary"), into={0: into},
    )(p, w8, bias, *dus)


def _conv_bwd_block(x_ref, w_ref, b_ref, du, dx_ref, dw_ref, db_ref):
    pad = (CONV_WIDTH - 1) // 2
    x = x_ref[...]
    tpos = lax.broadcasted_iota(jnp.int32, x.shape, 0)
    acc = jnp.broadcast_to(b_ref[...], x.shape)
    xs = []
    for k in range(CONV_WIDTH):
        xs.append(_shift_rows(x, k - pad, tpos))
        acc = acc + w_ref[k:k + 1, :] * xs[k]
    sg = jax.nn.sigmoid(acc)
    dacc = du * (sg * (1.0 + acc * (1.0 - sg)))
    dx = jnp.zeros_like(x)
    for k in range(CONV_WIDTH):
        dx = dx + w_ref[k:k + 1, :] * _shift_rows(dacc, pad - k, tpos)
    dx_ref[...] = dx.astype(dx_ref.dtype)

    @pl.when(pl.program_id(1) == 0)
    def _():
        dw_ref[...] = jnp.zeros_like(dw_ref)
        db_ref[...] = jnp.zeros_like(db_ref)

    for k in range(CONV_WIDTH):
        dw_ref[k:k + 1, :] += jnp.sum(dacc * xs[k], axis=0, keepdims=True)
    db_ref[...] += jnp.sum(dacc, axis=0, keepdims=True)


def _scan_gpb(groups):
    return 8 if groups % 8 == 0 else 4 if groups % 4 == 0 else 2 if groups % 2 == 0 else 1


def _scan_specs(batch, nc, groups, hg, inner, reverse_order):
    gpb = _scan_gpb(groups)
    t, n, hp = SSD_CHUNK, SSD_STATE, hg * SSD_HEAD_DIM
    ncb = inner // (gpb * n)
    ngb = groups // gpb

    def row(b, c):
        return b * nc + ((nc - 1 - c) if reverse_order else c)

    return dict(
        xs=pl.BlockSpec((t, gpb * hp), lambda g, b, c: (row(b, c), g)),
        bm=pl.BlockSpec((t, gpb * n), lambda g, b, c: (row(b, c), ncb + g)),
        cm=pl.BlockSpec((t, gpb * n), lambda g, b, c: (row(b, c), ncb + ngb + g)),
        dtr=pl.BlockSpec((gpb, t, hg), lambda g, b, c: (g, row(b, c), 0)),
        dtr_t=pl.BlockSpec((gpb, hg, t), lambda g, b, c: (g, 0, row(b, c))),
        par=pl.BlockSpec((gpb, 1, hg), lambda g, b, c: (g, 0, 0)),
        par_t=pl.BlockSpec((gpb, hg, 1), lambda g, b, c: (g, 0, 0)),
        y=pl.BlockSpec((t, gpb * hp), lambda g, b, c: (row(b, c), g)),
        nrow=pl.BlockSpec((t, gpb * n), lambda g, b, c: (row(b, c), g)),
        st=pl.BlockSpec((gpb, None, n, hp), lambda g, b, c: (g, row(b, c), 0, 0)),
    )


def _scan_fwd(u, dtr, dtr_t, bias, bias_t, alog, alog_t, *, batch, seq, inner, reverse, name):
    groups, hg = dtr.shape[0], dtr.shape[2]
    nc = seq // SSD_CHUNK
    hp = hg * SSD_HEAD_DIM
    sp = _scan_specs(batch, nc, groups, hg, inner, reverse)
    gpb, n = _scan_gpb(groups), SSD_STATE

    def body(xs_ref, bm_ref, cm_ref, dtr_ref, dtrt_ref, b_ref, bt_ref, a_ref, at_ref, y_ref, st_ref, state):
        @pl.when(pl.program_id(2) == 0)
        def _():
            state[...] = jnp.zeros_like(state)

        for k in range(gpb):
            xc, nc_ = slice(k * hp, (k + 1) * hp), slice(k * n, (k + 1) * n)
            st_in = state[k]
            st_ref[k] = st_in
            y, st_out = _ssd_chunk(st_in, xs_ref[:, xc], bm_ref[:, nc_], cm_ref[:, nc_], dtr_ref[k], dtrt_ref[k],
                                   b_ref[k], bt_ref[k], a_ref[k], at_ref[k], reverse=reverse)
            y_ref[:, xc] = y
            state[k] = st_out

    return _pcall(
        body, name=name, grid=(groups // gpb, batch, nc),
        in_specs=[sp["xs"], sp["bm"], sp["cm"], sp["dtr"], sp["dtr_t"], sp["par"], sp["par_t"], sp["par"], sp["par_t"]],
        out_specs=[sp["y"], sp["st"]],
        out_shape=[jax.ShapeDtypeStruct((batch * seq, inner), F32),
                   jax.ShapeDtypeStruct((groups, batch * nc, SSD_STATE, hp), F32)],
        scratch_shapes=[pltpu.VMEM((gpb, SSD_STATE, hp), F32)],
        sem=("parallel", "arbitrary", "arbitrary"),
    )(u, u, u, dtr, dtr_t, bias, bias_t, alog, alog_t)


def _scan_bwd(u, dtr, dtr_t, bias, bias_t, alog, alog_t, st, dy, *, batch, seq, inner, reverse, name):
    groups, hg = dtr.shape[0], dtr.shape[2]
    nc = seq // SSD_CHUNK
    hp = hg * SSD_HEAD_DIM
    t = batch * seq
    sp = _scan_specs(batch, nc, groups, hg, inner, not reverse)
    f = functools.partial(_ssd_chunk, reverse=reverse)
    gpb, n = _scan_gpb(groups), SSD_STATE

    def body(xs_ref, bm_ref, cm_ref, dtr_ref, dtrt_ref, b_ref, bt_ref, a_ref, at_ref, st_ref, dy_ref,
             dxs_ref, dbm_ref, dcm_ref, ddtr_ref, ddtrt_ref, db_ref, dbt_ref, da_ref, dat_ref, dstate):
        first = (pl.program_id(1) == 0) & (pl.program_id(2) == 0)

        @pl.when(pl.program_id(2) == 0)
        def _():
            dstate[...] = jnp.zeros_like(dstate)

        @pl.when(first)
        def _():
            for r in (db_ref, dbt_ref, da_ref, dat_ref):
                r[...] = jnp.zeros_like(r)

        for k in range(gpb):
            xc, nc_ = slice(k * hp, (k + 1) * hp), slice(k * n, (k + 1) * n)
            _, vjp = jax.vjp(f, st_ref[k], xs_ref[:, xc], bm_ref[:, nc_], cm_ref[:, nc_], dtr_ref[k], dtrt_ref[k],
                             b_ref[k], bt_ref[k], a_ref[k], at_ref[k])
            dst, dxs, dbm, dcm, ddtr, ddtrt, db, dbt, da, dat = vjp((dy_ref[:, xc], dstate[k]))
            dstate[k] = dst
            dxs_ref[:, xc] = dxs
            dbm_ref[:, nc_] = dbm
            dcm_ref[:, nc_] = dcm
            ddtr_ref[k] = ddtr
            ddtrt_ref[k] = ddtrt
            db_ref[k] += db
            dbt_ref[k] += dbt
            da_ref[k] += da
            dat_ref[k] += dat

    gn = groups * SSD_STATE
    return _pcall(
        body, name=name, grid=(groups // gpb, batch, nc),
        in_specs=[sp["xs"], sp["bm"], sp["cm"], sp["dtr"], sp["dtr_t"], sp["par"], sp["par_t"], sp["par"], sp["par_t"],
                  sp["st"], sp["y"]],
        out_specs=[sp["y"], sp["nrow"], sp["nrow"], sp["dtr"], sp["dtr_t"], sp["par"], sp["par_t"], sp["par"], sp["par_t"]],
        out_shape=[jax.ShapeDtypeStruct((t, inner), F32), jax.ShapeDtypeStruct((t, gn), F32),
                   jax.ShapeDtypeStruct((t, gn), F32), jax.ShapeDtypeStruct(dtr.shape, F32),
                   jax.ShapeDtypeStruct(dtr_t.shape, F32), jax.ShapeDtypeStruct(bias.shape, F32),
                   jax.ShapeDtypeStruct(bias_t.shape, F32), jax.ShapeDtypeStruct(alog.shape, F32),
                   jax.ShapeDtypeStruct(alog_t.shape, F32)],
        scratch_shapes=[pltpu.VMEM((gpb, SSD_STATE, hp), F32)],
        sem=("parallel", "arbitrary", "arbitrary"),
    )(u, u, u, dtr, dtr_t, bias, bias_t, alog, alog_t, st, dy)


def _heads_per_step(nh, nb, nbw, cb0):
    hps = min(nh, max(1, 16 // nb))
    while nh % hps or nbw % hps or cb0 % hps:
        hps -= 1
    return hps


def _attn_load(ref, col, blk):
    return ref[pl.ds(pl.multiple_of(blk * QBLK, QBLK), QBLK), col * HEAD_DIM:(col + 1) * HEAD_DIM].astype(F32)


def _lane0(row):
    lane = lax.broadcasted_iota(jnp.int32, row.shape, 1)
    return jnp.sum(jnp.where(lane == 0, row, 0.0), axis=1, keepdims=True)


def _attn_fwd(rq, sinkx, *, batch, seq, dil, nbw, cb0, nh, rep, hw, want_lse, out_dtype, name):
    t, w = rq.shape
    ln = seq // dil
    nb = ln // QBLK
    bw = (rep + 2) * HEAD_DIM
    ow = nh * rep * HEAD_DIM
    has_sink = sinkx is not None
    rq3 = rq.reshape(batch, ln, dil * w)
    kwin = _key_window(ln, hw)
    hps = _heads_per_step(nh, nb, nbw, cb0)
    f = functools.partial(_attn_block, has_sink=has_sink)

    def body(*refs):
        if has_sink:
            blk_ref, sink_ref = refs[:2]
            outs = refs[2:]
        else:
            blk_ref, sink_ref = refs[0], None
            outs = refs[1:]
        o_ref = outs[0]
        lse_ref = outs[1] if want_lse else None
        g = pl.program_id(2)

        def qblock(i, carry):
            ks = _key_start(i, ln, hw)
            valid = _band_mask(i * QBLK, ks, kwin, hw)
            rows = pl.ds(pl.multiple_of(i * QBLK, QBLK), QBLK)
            for hh in range(hps):
                hb = hh * (rep + 2)
                k3 = blk_ref[pl.ds(ks, kwin), (hb + rep) * HEAD_DIM:(hb + rep + 1) * HEAD_DIM].astype(F32)
                v3 = blk_ref[pl.ds(ks, kwin), (hb + rep + 1) * HEAD_DIM:(hb + rep + 2) * HEAD_DIM].astype(F32)
                for r in range(rep):
                    sk = _lane0(sink_ref[pl.ds((g * hps + hh) * rep + r, 1), :]) if has_sink else None
                    o, lse = f(_attn_load(blk_ref, hb + r, i), k3, v3, sk, valid)
                    oc = slice((hh * rep + r) * HEAD_DIM, (hh * rep + r + 1) * HEAD_DIM)
                    o_ref[rows, oc] = o.astype(o_ref.dtype)
                    if want_lse:
                        lse_ref[rows, oc] = jnp.broadcast_to(lse, o.shape)
            return carry

        if nb == 1:
            qblock(0, 0)
        else:
            lax.fori_loop(0, nb, qblock, 0, unroll=2 if nb % 2 == 0 else 1)

    in_specs = [pl.BlockSpec((None, ln, hps * bw), lambda b, r, h: (b, 0, (r * nbw + cb0) // hps + h))]
    ins = [rq3]
    if has_sink:
        in_specs.append(pl.BlockSpec(sinkx.shape, lambda b, r, h: (0, 0)))
        ins.append(sinkx)
    ospec = pl.BlockSpec((None, ln, hps * rep * HEAD_DIM), lambda b, r, h: (b, 0, r * (nh // hps) + h))
    out_shape = [jax.ShapeDtypeStruct((batch, ln, dil * ow), out_dtype)]
    out_specs = [ospec]
    if want_lse:
        out_shape.append(jax.ShapeDtypeStruct((batch, ln, dil * ow), F32))
        out_specs.append(ospec)
    res = _pcall(
        body, name=name, grid=(batch, dil, nh // hps), in_specs=in_specs, out_specs=out_specs, out_shape=out_shape,
        sem=("parallel", "parallel", "parallel"),
    )(*ins)
    return [r.reshape(t, ow) for r in res]


def _attn_bwd(rq, sinkx, do, dlse, *, batch, seq, dil, nbw, cb0, nh, rep, hw, name):
    t, w = rq.shape
    ln = seq // dil
    nb = ln // QBLK
    bw = (rep + 2) * HEAD_DIM
    ow = nh * rep * HEAD_DIM
    has_sink = sinkx is not None
    has_lse = dlse is not None
    kwin = _key_window(ln, hw)
    hps = _heads_per_step(nh, nb, nbw, cb0)
    f = functools.partial(_attn_block, has_sink=has_sink)

    def body(*refs):
        refs = list(refs)
        blk_ref = refs.pop(0)
        sink_ref = refs.pop(0) if has_sink else None
        do_ref = refs.pop(0)
        dlse_ref = refs.pop(0) if has_lse else None
        d_ref = refs.pop(0)
        dsink_ref = refs.pop(0) if has_sink else None
        g = pl.program_id(2)
        for hh in range(hps):
            kv0 = (hh * (rep + 2) + rep) * HEAD_DIM
            d_ref[:, kv0:kv0 + 2 * HEAD_DIM] = jnp.zeros((ln, 2 * HEAD_DIM), F32)
        if has_sink:
            @pl.when((pl.program_id(0) == 0) & (pl.program_id(1) == 0) & (g == 0))
            def _():
                dsink_ref[...] = jnp.zeros_like(dsink_ref)

        def qblock(i, carry):
            ks = _key_start(i, ln, hw)
            valid = _band_mask(i * QBLK, ks, kwin, hw)
            krows = pl.ds(ks, kwin)
            rows = pl.ds(pl.multiple_of(i * QBLK, QBLK), QBLK)
            for hh in range(hps):
                hb = hh * (rep + 2)
                kc = slice((hb + rep) * HEAD_DIM, (hb + rep + 1) * HEAD_DIM)
                vc = slice((hb + rep + 1) * HEAD_DIM, (hb + rep + 2) * HEAD_DIM)
                k3 = blk_ref[krows, kc].astype(F32)
                v3 = blk_ref[krows, vc].astype(F32)
                dk3 = jnp.zeros_like(k3)
                dv3 = jnp.zeros_like(v3)
                for r in range(rep):
                    oc = slice((hh * rep + r) * HEAD_DIM, (hh * rep + r + 1) * HEAD_DIM)
                    q = _attn_load(blk_ref, hb + r, i)
                    dov = do_ref[rows, oc]
                    dl = dlse_ref[rows, oc] if has_lse else jnp.zeros_like(dov)
                    if has_sink:
                        srow_i = (g * hps + hh) * rep + r
                        srow = sink_ref[pl.ds(srow_i, 1), :]
                        _, vjp = jax.vjp(lambda q_, k_, v_, s_: f(q_, k_, v_, _lane0(s_), valid), q, k3, v3, srow)
                        dq, dk, dv, ds = vjp((dov, jnp.sum(dl, axis=1, keepdims=True)))
                        dsink_ref[pl.ds(srow_i, 1), :] += ds
                    else:
                        _, vjp = jax.vjp(lambda q_, k_, v_: f(q_, k_, v_, None, valid), q, k3, v3)
                        dq, dk, dv = vjp((dov, jnp.sum(dl, axis=1, keepdims=True)))
                    d_ref[rows, (hb + r) * HEAD_DIM:(hb + r + 1) * HEAD_DIM] = dq
                    dk3 = dk3 + dk
                    dv3 = dv3 + dv
                d_ref[krows, kc] += dk3
                d_ref[krows, vc] += dv3
            return carry

        if nb == 1:
            qblock(0, 0)
        else:
            lax.fori_loop(0, nb, qblock, 0, unroll=2 if nb % 2 == 0 else 1)

    ospec = pl.BlockSpec((None, ln, hps * rep * HEAD_DIM), lambda b, r, h: (b, 0, r * (nh // hps) + h))
    in_specs = [pl.BlockSpec((None, ln, hps * bw), lambda b, r, h: (b, 0, (r * nbw + cb0) // hps + h))]
    ins = [rq.reshape(batch, ln, dil * w)]
    if has_sink:
        in_specs.append(pl.BlockSpec(sinkx.shape, lambda b, r, h: (0, 0)))
        ins.append(sinkx)
    in_specs.append(ospec)
    ins.append(do.reshape(batch, ln, dil * ow))
    if has_lse:
        in_specs.append(ospec)
        ins.append(dlse.reshape(batch, ln, dil * ow))
    dw = nh * bw
    out_specs = [pl.BlockSpec((None, ln, hps * bw), lambda b, r, h: (b, 0, r * (nh // hps) + h))]
    out_shape = [jax.ShapeDtypeStruct((batch, ln, dil * dw), F32)]
    if has_sink:
        out_specs.append(pl.BlockSpec(sinkx.shape, lambda b, r, h: (0, 0)))
        out_shape.append(jax.ShapeDtypeStruct(sinkx.shape, F32))
    res = _pcall(
        body, name=name, grid=(batch, dil, nh // hps), in_specs=in_specs, out_specs=out_specs, out_shape=out_shape,
        sem=("arbitrary", "arbitrary", "arbitrary"),
    )(*ins)
    return [res[0].reshape(t, dw)] + list(res[1:])


def _final_loss(x, g, target, *, name):
    d = x.shape[1]

    def fn(rv, pv):
        xv, tg = rv
        y, vjp = jax.vjp(_rms, xv, pv[0])
        err = y - tg
        dx, dg = vjp(err * (1.0 / d))
        loss = 0.5 * jnp.sum(err * err) * (1.0 / d)
        return [dx, dx], [dg, jnp.full((1, LANE), loss, F32)]

    (dx, dxb), (dg, loss) = _rowcall(fn, [(x, d, 0), (target, d, 0)], [g], [(d, F32), (d, BF16)],
                                     [(1, d), (1, LANE)], tile=256, name=name)
    return dx, dxb, dg, loss


class _Layout:
    def __init__(self, d_model):
        self.d = d_model
        self.inner = SSD_HEADS * SSD_HEAD_DIM
        self.gn = SSD_GROUPS * SSD_STATE
        self.xbc = self.inner + 2 * self.gn
        self.ndt = 2 * SSD_HEADS
        self.ngrp = len(DIL_PATTERNS)
        self.dilw = DIL_HEADS * HEAD_DIM
        self.rqd = 3 * self.ngrp * self.dilw
        self.rep = WIN_Q_HEADS // WIN_KV_HEADS
        self.rqw = WIN_KV_HEADS * (self.rep + 2) * HEAD_DIM
        self.qw = WIN_Q_HEADS * HEAD_DIM
        self.kw = WIN_KV_HEADS * HEAD_DIM
        self.gates = N_BRANCH * d_model
        self.n_in = self.inner + self.xbc + self.ndt + self.rqd + self.qw + 2 * self.kw + self.gates
        self.o_gates = 0
        self.o_z = self.gates
        self.o_xbc = self.o_z + self.inner
        self.o_rqd = self.o_xbc + self.xbc
        self.o_rqw = self.o_rqd + self.rqd
        self.o_dt = self.o_rqw + self.rqw
        self.dtw = -(-(self.o_dt + self.ndt) // PAD_TO) * PAD_TO - self.o_dt
        self.width = self.o_dt + self.dtw
        assert self.o_z % self.inner == 0 and self.o_xbc % 256 == 0
        assert self.o_rqd % (3 * HEAD_DIM) == 0 and self.o_rqw % (3 * HEAD_DIM) == 0 and self.dtw % LANE == 0
        assert self.o_dt % self.dtw == 0

    def split_points(self):
        sizes = (self.inner, self.xbc, self.ndt, self.rqd, self.qw, self.kw, self.kw, self.gates)
        pts, acc = [], 0
        for s in sizes:
            pts.append((acc, acc + s))
            acc += s
        return pts

    def permute_w(self, w):
        d = w.shape[0]
        z, xbc, dt, qkvd, qw, kw, vw, gates = [w[:, a:b] for a, b in self.split_points()]
        nhd = self.ngrp * DIL_HEADS
        qkvd = qkvd.reshape(d, 3, nhd, HEAD_DIM).transpose(0, 2, 1, 3).reshape(d, self.rqd)
        win = jnp.concatenate([qw.reshape(d, WIN_KV_HEADS, self.rep, HEAD_DIM),
                               kw.reshape(d, WIN_KV_HEADS, 1, HEAD_DIM),
                               vw.reshape(d, WIN_KV_HEADS, 1, HEAD_DIM)], axis=2).reshape(d, self.rqw)
        pad = jnp.zeros((d, self.dtw - self.ndt), w.dtype)
        return jnp.concatenate([gates, z, xbc, qkvd, win, dt, pad], axis=1)

    def unpermute_w(self, wp):
        d = wp.shape[0]
        gates = wp[:, :self.o_z]
        z = wp[:, self.o_z:self.o_xbc]
        xbc = wp[:, self.o_xbc:self.o_rqd]
        qkvd = wp[:, self.o_rqd:self.o_rqw]
        win = wp[:, self.o_rqw:self.o_dt].reshape(d, WIN_KV_HEADS, self.rep + 2, HEAD_DIM)
        dt = wp[:, self.o_dt:self.o_dt + self.ndt]
        nhd = self.ngrp * DIL_HEADS
        qkvd = qkvd.reshape(d, nhd, 3, HEAD_DIM).transpose(0, 2, 1, 3).reshape(d, self.rqd)
        qw = win[:, :, :self.rep].reshape(d, self.qw)
        kw = win[:, :, self.rep].reshape(d, self.kw)
        vw = win[:, :, self.rep + 1].reshape(d, self.kw)
        return jnp.concatenate([z, xbc, dt, qkvd, qw, kw, vw, gates], axis=1)


def _dt_layouts(pdt, dirn, batch_seq):
    hg = SSD_HEADS // SSD_GROUPS
    v = pdt[:, dirn * SSD_HEADS:(dirn + 1) * SSD_HEADS].reshape(batch_seq, SSD_GROUPS, hg)
    return v.transpose(1, 0, 2), v.transpose(1, 2, 0)


def _par_layouts(p):
    hg = SSD_HEADS // SSD_GROUPS
    v = p.reshape(SSD_GROUPS, hg)
    return v[:, None, :], v[:, :, None]


def _layer_fwd(x, lw, lay, tabs, batch, seq):
    d = lay.d
    sv = {"x": x}
    (h,), _ = _rowcall(lambda rv, pv: ([_rms(rv[0], pv[0])], []), [(x, d, 0)], [lw["g_mix"]], [(d, BF16)], [],
                       tile=256, name="norm_mix")
    p = _mm(h, lw["w_in"], name="proj_in")
    sv["h"], sv["p"] = h, p
    u = _conv_fwd(p, lw["conv_w8"], lw["conv_b"], col0=lay.o_xbc, chans=lay.xbc, batch=batch, seq=seq, name="conv_fwd")
    sv["u"] = u
    pdt = p[:, lay.o_dt:lay.o_dt + lay.ndt]
    ys, sv["st"], sv["dtl"] = [], [], []
    for dirn in range(2):
        dtr, dtr_t = _dt_layouts(pdt, dirn, batch * seq)
        bias, bias_t = _par_layouts(lw["dt_bias"][dirn])
        alog, alog_t = _par_layouts(lw["a_log"][dirn])
        y, st = _scan_fwd(u, dtr, dtr_t, bias, bias_t, alog, alog_t, batch=batch, seq=seq, inner=lay.inner,
                          reverse=bool(dirn), name="scan_fwd%d" % dirn)
        ys.append(y)
        sv["st"].append(st)
        sv["dtl"].append((dtr, dtr_t, bias, bias_t, alog, alog_t))
    sv["ys"] = ys
    inner = lay.inner
    (ya,), _ = _rowcall(lambda rv, pv: ([_ssd_post_f(*rv, *pv)], []),
                        [(ys[0], inner, 0), (ys[1], inner, 0), (u, inner, 0), (p, inner, lay.o_z // inner)],
                        [lw["d_skip_x"], lw["ssd_norm"]], [(inner, BF16)], [], tile=128, name="ssd_post")
    sv["ya"] = ya
    gw = 3 * lay.dilw
    rqd = [_rope(p, tabs, col0=lay.o_rqd + gi * gw, width=gw, seq=seq, group=0, inverse=False, out_dtype=BF16,
                 name="rope_dil") for gi in range(lay.ngrp)]
    rqw = _rope(p, tabs, col0=lay.o_rqw, width=lay.rqw, seq=seq, group=lay.rep + 2, inverse=False, out_dtype=BF16,
                name="rope_win")
    sv["rqd"], sv["rqw"] = rqd, rqw
    os_, ls_ = [], []
    for gi, (window, dil) in enumerate(DIL_PATTERNS):
        o, l = _attn_fwd(rqd[gi], None, batch=batch, seq=seq, dil=dil, nbw=DIL_HEADS, cb0=0,
                         nh=DIL_HEADS, rep=1, hw=window // (2 * dil), want_lse=True, out_dtype=F32,
                         name="dil_fwd%d" % gi)
        os_.append(o)
        ls_.append(l)
    sv["os"], sv["ls"] = os_, ls_
    dw = lay.dilw
    (yb,), _ = _rowcall(lambda rv, pv: ([_combine_f(*rv)], []), [(a, dw, 0) for a in os_ + ls_], [], [(dw, BF16)], [],
                        tile=256, name="dil_combine")
    sv["yb"] = yb
    (yc,) = _attn_fwd(rqw, lw["sink_x"], batch=batch, seq=seq, dil=1, nbw=WIN_KV_HEADS, cb0=0, nh=WIN_KV_HEADS,
                      rep=lay.rep, hw=WIN_HALF, want_lse=False, out_dtype=BF16, name="win_fwd")
    sv["yc"] = yc
    ma = _mm(ya, lw["w_a"], name="proj_a")
    mb = _mm(yb, lw["w_b"], name="proj_b")
    mc = _mm(yc, lw["w_c"], name="proj_c")
    sv["mabc"] = (ma, mb, mc)
    (mg,), _ = _rowcall(lambda rv, pv: ([_merge_f(*rv)], []),
                        [(ma, d, 0), (mb, d, 0), (mc, d, 0), (p, d, 0), (p, d, 1), (p, d, 2)], [], [(d, BF16)], [],
                        tile=256, name="merge")
    sv["mg"] = mg
    x1 = _mm(mg, lw["w_out"], add=x, name="proj_out")
    sv["x1"] = x1
    (hm,), _ = _rowcall(lambda rv, pv: ([_rms(rv[0], pv[0])], []), [(x1, d, 0)], [lw["g_mlp"]], [(d, BF16)], [],
                        tile=256, name="norm_mlp")
    up, act = _mm(hm, lw["w_up"], post=(lambda r: (r, jnp.square(jnp.maximum(r, 0.0))), [], [F32, BF16]),
                  name="mlp_up")
    sv["hm"], sv["up"], sv["act"] = hm, up, act
    x2 = _mm(act, lw["w_down"], add=x1, name="mlp_down")
    return x2, sv


def _layer_bwd(dxo, dxo_b, sv, lw, lay, tabs, batch, seq, post):
    d = lay.d
    inner = lay.inner
    gs = {}
    (dup,) = _mm(dxo_b, lw["w_down"], tb=True, name="mlp_down_dx",
                 post=(lambda r, a: (r * (2.0 * jnp.maximum(a, 0.0)),), [sv["up"]], [BF16]))
    post("w_down", _mm(sv["act"], dxo_b, ta=True, out_dtype=BF16, name="mlp_down_dw"))
    dhm = _mm(dup, lw["w_up"], tb=True, name="mlp_up_dx")
    post("w_up", _mm(sv["hm"], dup, ta=True, out_dtype=BF16, name="mlp_up_dw"))

    def norm_bwd(rv, pv):
        xv, dh, dres = rv
        _, vjp = jax.vjp(_rms, xv, pv[0])
        dx, dg = vjp(dh)
        return [dx + dres, dx + dres], [dg]

    (dx1, dx1_b), (gs["g_mlp"],) = _rowcall(norm_bwd, [(sv["x1"], d, 0), (dhm, d, 0), (dxo, d, 0)], [lw["g_mlp"]],
                                            [(d, F32), (d, BF16)], [(1, d)], tile=128, name="norm_mlp_bwd")
    dmg = _mm(dx1_b, lw["w_out"], tb=True, name="proj_out_dx")
    post("w_out", _mm(sv["mg"], dx1_b, ta=True, out_dtype=BF16, name="proj_out_dw"))
    ma, mb, mc = sv["mabc"]
    p = sv["p"]

    def merge_bwd(rv, pv):
        _, vjp = jax.vjp(_merge_f, *rv[:6])
        da, db, dc, d0, d1, d2 = vjp(rv[6])
        return [da, db, dc, jnp.concatenate([d0, d1, d2], axis=1)], []

    dp = lax.empty((batch * seq, lay.width), BF16)
    (dma, dmb, dmc, dp), _ = _rowcall(
        merge_bwd, [(ma, d, 0), (mb, d, 0), (mc, d, 0), (p, d, 0), (p, d, 1), (p, d, 2), (dmg, d, 0)], [],
        [(d, BF16), (d, BF16), (d, BF16), (lay.gates, BF16, dp, 0)], [], tile=128, name="merge_bwd")
    dya = _mm(dma, lw["w_a"], tb=True, name="proj_a_dx")
    post("w_a", _mm(sv["ya"], dma, ta=True, out_dtype=BF16, name="proj_a_dw"))
    dyb = _mm(dmb, lw["w_b"], tb=True, name="proj_b_dx")
    post("w_b", _mm(sv["yb"], dmb, ta=True, out_dtype=BF16, name="proj_b_dw"))
    dyc = _mm(dmc, lw["w_c"], tb=True, name="proj_c_dx")
    post("w_c", _mm(sv["yc"], dmc, ta=True, out_dtype=BF16, name="proj_c_dw"))
    drqw, dsink = _attn_bwd(sv["rqw"], lw["sink_x"], dyc, None, batch=batch, seq=seq, dil=1, nbw=WIN_KV_HEADS, cb0=0,
                            nh=WIN_KV_HEADS, rep=lay.rep, hw=WIN_HALF, name="win_bwd")
    gs["sink"] = jnp.sum(dsink, axis=1)
    dw = lay.dilw

    def combine_bwd(rv, pv):
        _, vjp = jax.vjp(_combine_f, *rv[:6])
        return list(vjp(rv[6])), []

    dol, _ = _rowcall(combine_bwd, [(a, dw, 0) for a in sv["os"] + sv["ls"]] + [(dyb, dw, 0)], [],
                      [(dw, F32)] * 6, [], tile=256, name="dil_combine_bwd")
    for gi, (window, dil) in enumerate(DIL_PATTERNS):
        (dg_,) = _attn_bwd(sv["rqd"][gi], None, dol[gi], dol[3 + gi], batch=batch, seq=seq, dil=dil,
                           nbw=DIL_HEADS, cb0=0, nh=DIL_HEADS, rep=1,
                           hw=window // (2 * dil), name="dil_bwd%d" % gi)
        dp = _rope(dg_, tabs, col0=0, width=dg_.shape[1], seq=seq, group=0, inverse=True, out_dtype=BF16,
                   name="rope_dil_bwd%d" % gi, into=dp, out_col0=lay.o_rqd + gi * dg_.shape[1])
    dp = _rope(drqw, tabs, col0=0, width=lay.rqw, seq=seq, group=lay.rep + 2, inverse=True, out_dtype=BF16,
               name="rope_win_bwd", into=dp, out_col0=lay.o_rqw)
    u, ys = sv["u"], sv["ys"]

    def post_bwd(rv, pv):
        _, vjp = jax.vjp(_ssd_post_f, *rv[:4], *pv)
        dyf, _, dxs, dz, dsk, dgn = vjp(rv[4])
        return [dyf, dxs, dz], [dsk, dgn]

    (dy, dxs_post, dp), (dsk, gs["ssd_norm"]) = _rowcall(
        post_bwd, [(ys[0], inner, 0), (ys[1], inner, 0), (u, inner, 0), (p, inner, lay.o_z // inner), (dya, inner, 0)],
        [lw["d_skip_x"], lw["ssd_norm"]], [(inner, F32), (inner, F32), (inner, BF16, dp, lay.o_z // inner)],
        [(1, inner), (1, inner)], tile=128, name="ssd_post_bwd")
    gs["d_skip"] = jnp.sum(dsk.reshape(SSD_HEADS, SSD_HEAD_DIM), axis=1)
    rs = []
    ddt, gdb, gda = [], [], []
    for dirn in range(2):
        dtr, dtr_t, bias, bias_t, alog, alog_t = sv["dtl"][dirn]
        r = _scan_bwd(u, dtr, dtr_t, bias, bias_t, alog, alog_t, sv["st"][dirn], dy, batch=batch, seq=seq,
                      inner=inner, reverse=bool(dirn), name="scan_bwd%d" % dirn)
        rs.append(r)
        ddt.append((r[3] + r[4].transpose(0, 2, 1)).transpose(1, 0, 2).reshape(batch * seq, SSD_HEADS))
        gdb.append((r[5][:, 0, :] + r[6][:, :, 0]).reshape(SSD_HEADS))
        gda.append((r[7][:, 0, :] + r[8][:, :, 0]).reshape(SSD_HEADS))
    gs["dt_bias"] = jnp.stack(gdb)
    gs["a_log"] = jnp.stack(gda)
    dcws, dcbs = [], []
    for tag, ch0, chans, dus in (("x", 0, inner, [dxs_post, rs[0][0], rs[1][0]]),
                                 ("b", inner, lay.gn, [rs[0][1], rs[1][1]]),
                                 ("c", inner + lay.gn, lay.gn, [rs[0][2], rs[1][2]])):
        dp, dcw, dcb = _conv_bwd(p, lw["conv_w8"], lw["conv_b"], dus, col0=lay.o_xbc, ch0=ch0, chans=chans,
                                 batch=batch, seq=seq, into=dp, out_col0=lay.o_xbc, name="conv_bwd_" + tag)
        dcws.append(dcw)
        dcbs.append(dcb)
    gs["conv_w"] = jnp.concatenate(dcws, axis=1)[:CONV_WIDTH]
    gs["conv_b"] = jnp.concatenate(dcbs, axis=1)[0]
    ddtp = jnp.concatenate(ddt + [jnp.zeros((batch * seq, lay.dtw - lay.ndt), F32)], axis=1)
    (dp,), _ = _rowcall(lambda rv, pv: ([rv[0]], []), [(ddtp, lay.dtw, 0)], [],
                        [(lay.dtw, BF16, dp, lay.o_dt // lay.dtw)], [], tile=512, name="ddt_store")
    hd = d // W_IN_PARTS
    for part in range(W_IN_PARTS):
        post(("w_in", part), _mm(sv["h"][:, part * hd:(part + 1) * hd], dp, ta=True, out_dtype=BF16,
                                 name="proj_in_dw"))
    dh = _mm(dp, lw["w_in"], tb=True, name="proj_in_dx")
    (dx, dx_b), (gs["g_mix"],) = _rowcall(norm_bwd, [(sv["x"], d, 0), (dh, d, 0), (dx1, d, 0)], [lw["g_mix"]],
                                          [(d, F32), (d, BF16)], [(1, d)], tile=128, name="norm_mix_bwd")
    return dx, dx_b, gs


_SHARDED = ("w_in", "w_a", "w_b", "w_c", "w_out", "w_up", "w_down")
_COL_SHARDED = ("w_in", "w_b", "w_up")
_SMALL = ("g_mix", "conv_b", "dt_bias", "a_log", "d_skip", "ssd_norm", "sink", "g_mlp")


def _gathered_to_full(name, g):
    n, r, c = g.shape
    if name in _COL_SHARDED:
        return g.transpose(1, 0, 2).reshape(r, n * c)
    return g.reshape(n * r, c)


def _full_to_slots(name, w):
    r, c = w.shape
    if name in _COL_SHARDED:
        return w.reshape(r, N_DEV, c // N_DEV).transpose(1, 0, 2)
    return w.reshape(N_DEV, r // N_DEV, c)


class _LayerWeights:
    def __init__(self, sched, layer, lay, small):
        self.sched, self.layer, self.lay, self.vals = sched, layer, lay, dict(small)

    def __getitem__(self, name):
        if name not in self.vals:
            full = _gathered_to_full(name, self.sched.get(("w", name, self.layer)))
            self.vals[name] = self.lay.permute_w(full) if name == "w_in" else full
        return self.vals[name]


def _pack(parts):
    flat = jnp.concatenate([p.reshape(-1).astype(F32) for p in parts])
    n = flat.shape[0]
    rows = -(-n // (8 * LANE)) * 8
    return jnp.pad(flat, (0, rows * LANE - n)).reshape(rows, LANE)


def _unpack(buf, shapes):
    flat = buf.reshape(-1)
    out, off = [], 0
    for s in shapes:
        n = math.prod(s)
        out.append(flat[off:off + n].reshape(s))
        off += n
    return out


def kernel(x, g_mix, w_in, conv_w, conv_b, dt_bias, a_log, d_skip, ssd_norm, w_a, w_b, w_c, sink, w_out, g_mlp, w_up, w_down, g_final, loss_target, m_g_mix, m_w_in, m_conv_w, m_conv_b, m_dt_bias, m_a_log, m_d_skip, m_ssd_norm, m_w_a, m_w_b, m_w_c, m_sink, m_w_out, m_g_mlp, m_w_up, m_w_down, m_g_final, v_g_mix, v_w_in, v_conv_w, v_conv_b, v_dt_bias, v_a_log, v_d_skip, v_ssd_norm, v_w_a, v_w_b, v_w_c, v_sink, v_w_out, v_g_mlp, v_w_up, v_w_down, v_g_final):
    batch, seq, d = x.shape
    depth = g_mix.shape[0]
    lay = _Layout(d)
    assert lay.n_in == w_in.shape[2] * N_DEV
    wts = dict(g_mix=g_mix, w_in=w_in, conv_w=conv_w, conv_b=conv_b, dt_bias=dt_bias, a_log=a_log, d_skip=d_skip,
               ssd_norm=ssd_norm, w_a=w_a, w_b=w_b, w_c=w_c, sink=sink, w_out=w_out, g_mlp=g_mlp, w_up=w_up,
               w_down=w_down, g_final=g_final)
    mom = dict(g_mix=m_g_mix, w_in=m_w_in, conv_w=m_conv_w, conv_b=m_conv_b, dt_bias=m_dt_bias, a_log=m_a_log,
               d_skip=m_d_skip, ssd_norm=m_ssd_norm, w_a=m_w_a, w_b=m_w_b, w_c=m_w_c, sink=m_sink, w_out=m_w_out,
               g_mlp=m_g_mlp, w_up=m_w_up, w_down=m_w_down, g_final=m_g_final)
    var = dict(g_mix=v_g_mix, w_in=v_w_in, conv_w=v_conv_w, conv_b=v_conv_b, dt_bias=v_dt_bias, a_log=v_a_log,
               d_skip=v_d_skip, ssd_norm=v_ssd_norm, w_a=v_w_a, w_b=v_w_b, w_c=v_w_c, sink=v_sink, w_out=v_w_out,
               g_mlp=v_g_mlp, w_up=v_w_up, w_down=v_w_down, g_final=v_g_final)
    me = 4 * lax.axis_index("x") + 2 * lax.axis_index("y") + lax.axis_index("c")

    global _SCHED
    sched = _SCHED = _Sched()
    (gconv,) = _exchange([conv_w], scatter=False, name="gather_conv_w")
    conv_full = gconv.transpose(1, 2, 0, 3).reshape(depth, CONV_WIDTH, -1)
    for l in range(depth):
        for n in _SHARDED:
            sched.post(("w", n, l), wts[n][l].astype(BF16), scatter=False)

    tabs = _rope_tables(seq)
    t = batch * seq
    xf = x.reshape(t, d)
    layers = []
    for l in range(depth):
        small = dict(
            g_mix=g_mix[l][None], g_mlp=g_mlp[l][None], ssd_norm=ssd_norm[l][None], conv_b=conv_b[l][None],
            dt_bias=dt_bias[l], a_log=a_log[l],
            d_skip_x=jnp.repeat(d_skip[l], SSD_HEAD_DIM)[None],
            sink_x=jnp.broadcast_to(sink[l][:, None], (WIN_Q_HEADS, LANE)),
            conv_w8=jnp.pad(conv_full[l], ((0, 8 - CONV_WIDTH), (0, 0))))
        layers.append(_LayerWeights(sched, l, lay, small))

    saves = []
    h = xf
    for l in range(depth):
        h, sv = _layer_fwd(h, layers[l], lay, tabs, batch, seq)
        saves.append(sv)
    dx, dx_b, dgf, loss = _final_loss(h, g_final[None], loss_target.reshape(t, d), name="final_loss")

    gss = [None] * depth
    for l in reversed(range(depth)):
        def post(n, g, l=l):
            if isinstance(n, tuple):
                key, n, g = ("g", "w_in", l, n[1]), "w_in", lay.unpermute_w(g)
            else:
                key = ("g", n, l)
            sched.post(key, _full_to_slots(n, g).astype(BF16), scatter=True)

        dx, dx_b, gss[l] = _layer_bwd(dx, dx_b, saves[l], layers[l], lay, tabs, batch, seq, post)
    grad_x = dx.reshape(batch, seq, d)

    small_parts = [jnp.stack([gss[l][n] for l in range(depth)]) for n in _SMALL]
    small_parts += [dgf, jnp.stack([gss[l]["conv_w"] for l in range(depth)]), loss[0, :1]]
    small_shapes = [p.shape for p in small_parts]
    (rs,) = _exchange([_pack(small_parts)], scatter=False, name="gather_small")
    red = _unpack(_sum_slots(rs, name="sum_small"), small_shapes)
    gsmall = dict(zip(list(_SMALL) + ["g_final"], red[:len(_SMALL) + 1]))
    gconv_full, loss_sum = red[-2], red[-1]
    cshard = conv_w.shape[2]
    gsmall["conv_w"] = lax.dynamic_slice_in_dim(gconv_full, me * cshard, cshard, axis=2)

    out = {}
    rep_names = list(_SMALL) + ["g_final"]
    rep_shapes = [wts[n].shape for n in rep_names]
    res = _adamw(_pack([wts[n] for n in rep_names]), [_pack([gsmall[n] for n in rep_names])[None]],
                 _pack([mom[n] for n in rep_names]), _pack([var[n] for n in rep_names]), name="adamw_small")
    unp = [_unpack(a, rep_shapes) for a in res]
    for i, n in enumerate(rep_names):
        out[n] = [unp[k][i] for k in range(4)]
    cs2 = (depth * CONV_WIDTH, cshard)
    res = _adamw(conv_w.reshape(cs2), [gsmall["conv_w"].reshape((1,) + cs2)], m_conv_w.reshape(cs2),
                 v_conv_w.reshape(cs2), name="adamw_conv_w")
    out["conv_w"] = [a.reshape(conv_w.shape) for a in res]
    for n in ("w_down", "w_up", "w_out", "w_a", "w_b", "w_c", "w_in"):
        shp = wts[n].shape
        r2 = (shp[0] * shp[1], shp[2])
        if n == "w_in":
            recvs = [sched.get(("g", n, l, part)) for l in range(depth) for part in range(W_IN_PARTS)]
        else:
            recvs = [sched.get(("g", n, l)) for l in range(depth)]
        res = _adamw(wts[n].reshape(r2), recvs, mom[n].reshape(r2), var[n].reshape(r2), name="adamw_" + n)
        out[n] = [a.reshape(shp) for a in res]

    order = ["g_mix", "w_in", "conv_w", "conv_b", "dt_bias", "a_log", "d_skip", "ssd_norm", "w_a", "w_b", "w_c",
             "sink", "w_out", "g_mlp", "w_up", "w_down", "g_final"]
    outs = [loss_sum.reshape(()), grad_x]
    for k in range(4):
        outs += [out[n][k] for n in order]
    return tuple(outs)
```
